```python
import math
import jax, jax.numpy as jnp
from jax import lax
import numpy as np

D_MODEL = 1024
BATCH = 8
SEQ = 4096
DEPTH = 2

MEM_LEN = 256
EPS = 1e-6
POS_OFFSET_MAX = 1024

N_GROUPS_MIX = 4
GROUP_W = 256
D_MIX = N_GROUPS_MIX * GROUP_W

MLA_HEADS = 4
MLA_Q_RANK = 256
MLA_KV_RANK = 128
MLA_NOPE = 64
MLA_ROPE = 32
MLA_QK_DIM = MLA_NOPE + MLA_ROPE
MLA_V = GROUP_W // MLA_HEADS
ROPE_THETA = 10000.0
Q_BLOCK = 128

POOL_WINDOWS = (2, 4, 8, 16)
POOL_GROUP = GROUP_W // len(POOL_WINDOWS)

SWA_HEADS = 4
SWA_KV_HEADS = 2
SWA_HEAD_DIM = GROUP_W // SWA_HEADS
SWA_WINDOW = 128

CONV_W = 3
CONV_CH = GROUP_W

A_COLS = MLA_Q_RANK + MLA_KV_RANK + MLA_ROPE
B_COLS = GROUP_W
C_COLS = (SWA_HEADS + 2 * SWA_KV_HEADS) * SWA_HEAD_DIM
D_COLS = 3 * CONV_CH
IN_COLS = A_COLS + B_COLS + C_COLS + D_COLS

XA_HEADS = 4
XA_HEAD_DIM = 64

N_EXPERT_GROUPS = 4
EXPERTS_PER_GROUP = 8
N_EXPERTS = N_EXPERT_GROUPS * EXPERTS_PER_GROUP
TOP_K = 2
D_EXPERT = 256
MOE_BLOCK = 256

kernel_name = "hybrid_parallel_heads_mla_pool_swa_conv_hmoe"


def rms_norm(x, gain):
    xf = x.astype(jnp.float32)
    y = xf * lax.rsqrt(jnp.mean(xf * xf, axis=-1, keepdims=True) + EPS)
    return (y * gain.astype(jnp.float32)).astype(x.dtype)


def rotary(x, positions):
    half = x.shape[-1] // 2
    inv_freq = ROPE_THETA ** (-jnp.arange(half, dtype=jnp.float32) / half)
    ang = positions.astype(jnp.float32)[..., None] * inv_freq
    cos = jnp.cos(ang)[:, :, None, :]
    sin = jnp.sin(ang)[:, :, None, :]
    xf = x.astype(jnp.float32)
    x1, x2 = xf[..., :half], xf[..., half:]
    return jnp.concatenate([x1 * cos - x2 * sin, x2 * cos + x1 * sin], axis=-1).astype(x.dtype)


def causal_attention_qblocks(q, k, v, scale):
    B, S, H, dq = q.shape
    nb = S // Q_BLOCK
    qb = q.reshape(B, nb, Q_BLOCK, H, dq).transpose(1, 0, 2, 3, 4)
    key_pos = jnp.arange(S)

    def one_block(args):
        qi, bi = args
        s = jnp.einsum('bqhd,bkhd->bhqk', qi, k, preferred_element_type=jnp.float32) * scale
        q_pos = bi * Q_BLOCK + jnp.arange(Q_BLOCK)
        mask = key_pos[None, :] <= q_pos[:, None]
        s = jnp.where(mask[None, None], s, -jnp.inf)
        p = jax.nn.softmax(s, axis=-1)
        return jnp.einsum('bhqk,bkhd->bqhd', p.astype(v.dtype), v)

    out = lax.map(one_block, (qb, jnp.arange(nb)))
    return out.transpose(1, 0, 2, 3, 4).reshape(B, S, H, v.shape[-1])


def mla_mixer(u, positions, g_cq, w_uq, g_ckv, w_ukv, g_q, g_k):
    B, S, _ = u.shape
    c_q = u[..., :MLA_Q_RANK]
    c_kv = u[..., MLA_Q_RANK:MLA_Q_RANK + MLA_KV_RANK]
    k_rope = u[..., MLA_Q_RANK + MLA_KV_RANK:]
    q = (rms_norm(c_q, g_cq) @ w_uq).reshape(B, S, MLA_HEADS, MLA_QK_DIM)
    kv = (rms_norm(c_kv, g_ckv) @ w_ukv).reshape(B, S, MLA_HEADS, MLA_NOPE + MLA_V)
    k = jnp.concatenate(
        [kv[..., :MLA_NOPE], jnp.broadcast_to(k_rope[:, :, None, :], (B, S, MLA_HEADS, MLA_ROPE))], axis=-1)
    v = kv[..., MLA_NOPE:]
    q = rms_norm(q, g_q)
    k = rms_norm(k, g_k)
    q = jnp.concatenate([q[..., :MLA_NOPE], rotary(q[..., MLA_NOPE:], positions)], axis=-1)
    k = jnp.concatenate([k[..., :MLA_NOPE], rotary(k[..., MLA_NOPE:], positions)], axis=-1)
    o = causal_attention_qblocks(q, k, v, MLA_QK_DIM ** -0.5)
    return o.reshape(B, S, MLA_HEADS * MLA_V)


def pool_mixer(u, w_pool, pool_scale):
    B, S, C = u.shape
    uf = u.astype(jnp.float32)
    cs = jnp.pad(jnp.cumsum(uf, axis=1), ((0, 0), (1, 0), (0, 0)))
    t = jnp.arange(S)
    outs = []
    for gi, w in enumerate(POOL_WINDOWS):
        lo, hi = gi * POOL_GROUP, (gi + 1) * POOL_GROUP
        csg = cs[..., lo:hi]
        start = jnp.maximum(t + 1 - w, 0)
        win_sum = csg[:, 1:] - csg[:, start]
        count = jnp.minimum(t + 1, w).astype(jnp.float32)
        outs.append(win_sum / count[None, :, None] - uf[..., lo:hi])
    pooled = jnp.concatenate(outs, axis=-1).astype(u.dtype).reshape(B, S, len(POOL_WINDOWS), POOL_GROUP)
    y = jnp.einsum('bsgc,gcd->bsgd', pooled, w_pool).reshape(B, S, C)
    return y * pool_scale


def swa_mixer(u, g_q, g_k, sinks):
    B, S, _ = u.shape
    H, KV, dh, W = SWA_HEADS, SWA_KV_HEADS, SWA_HEAD_DIM, SWA_WINDOW
    G = H // KV
    nb = S // W
    q = rms_norm(u[..., :H * dh].reshape(B, S, KV, G, dh), g_q)
    k = rms_norm(u[..., H * dh:(H + KV) * dh].reshape(B, S, KV, dh), g_k)
    v = u[..., (H + KV) * dh:].reshape(B, S, KV, dh)
    qb = q.reshape(B, nb, W, KV, G, dh)

    def band(z):
        zb = z.reshape(B, nb, W, KV, dh)
        prev = jnp.pad(zb, ((0, 0), (1, 0), (0, 0), (0, 0), (0, 0)))[:, :-1]
        return jnp.concatenate([prev, zb], axis=2)

    kb, vb = band(k), band(v)
    s = jnp.einsum('bnqkgd,bnjkd->bnkgqj', qb, kb, preferred_element_type=jnp.float32) * (dh ** -0.5)
    qi = jnp.arange(W)[:, None]
    kj = jnp.arange(2 * W)[None, :]
    diff = W + qi - kj
    in_win = (diff >= 0) & (diff < W)
    key_ok = (jnp.arange(nb)[:, None] * W + kj - W) >= 0
    mask = in_win[None] & key_ok[:, None, :]
    s = jnp.where(mask[None, :, None, None], s, -jnp.inf)
    sink = jnp.broadcast_to(sinks.reshape(KV, G).astype(jnp.float32)[None, None, :, :, None, None],
                            s.shape[:-1] + (1,))
    p = jax.nn.softmax(jnp.concatenate([s, sink], axis=-1), axis=-1)[..., :2 * W]
    o = jnp.einsum('bnkgqj,bnjkd->bnqkgd', p.astype(vb.dtype), vb)
    return o.reshape(B, S, H * dh)


def conv_mixer(u, conv_w, conv_b):
    b_gate = u[..., :CONV_CH]
    c_gate = u[..., CONV_CH:2 * CONV_CH]
    h = u[..., 2 * CONV_CH:]
    z = c_gate * h
    conv = lax.conv_general_dilated(
        z, conv_w.reshape(CONV_W, 1, CONV_CH), window_strides=(1,), padding=[(CONV_W - 1, 0)],
        dimension_numbers=('NWC', 'WIO', 'NWC'), feature_group_count=CONV_CH)
    return b_gate * (conv + conv_b)


def cross_attention(h, m, w_q, w_kv, g_q, g_k, w_o):
    B, S, _ = h.shape
    M = m.shape[1]
    q = rms_norm((h @ w_q).reshape(B, S, XA_HEADS, XA_HEAD_DIM), g_q)
    kv = (m @ w_kv).reshape(B, M, 2, XA_HEADS, XA_HEAD_DIM)
    k = rms_norm(kv[:, :, 0], g_k)
    v = kv[:, :, 1]
    s = jnp.einsum('bshd,bmhd->bhsm', q, k, preferred_element_type=jnp.float32) * (XA_HEAD_DIM ** -0.5)
    p = jax.nn.softmax(s, axis=-1)
    o = jnp.einsum('bhsm,bmhd->bshd', p.astype(v.dtype), v).reshape(B, S, XA_HEADS * XA_HEAD_DIM)
    return o @ w_o


def hier_moe(h, w_group, b_group, w_expert, b_expert, w_gate, w_up, w_down):
    B, S, D = h.shape
    N = B * S
    t = h.reshape(N, D)
    rows = jnp.arange(N)
    g_logits = (t @ w_group).astype(jnp.float32) + b_group.astype(jnp.float32)
    g_prob = jax.nn.softmax(g_logits, axis=-1)
    g_idx = jnp.argmax(g_logits, axis=-1)
    g_w = g_prob[rows, g_idx]
    e_logits = ((t @ w_expert).astype(jnp.float32) + b_expert.astype(jnp.float32)).reshape(
        N, N_EXPERT_GROUPS, EXPERTS_PER_GROUP)
    e_prob = jax.nn.softmax(e_logits[rows, g_idx], axis=-1)
    top_p, top_i = lax.top_k(e_prob, TOP_K)
    top_p = top_p / jnp.sum(top_p, axis=-1, keepdims=True)
    weights = g_w[:, None] * top_p
    expert_id = g_idx[:, None] * EXPERTS_PER_GROUP + top_i
    A = N * TOP_K
    flat_e = expert_id.reshape(A)
    flat_w = weights.reshape(A)
    flat_tok = jnp.repeat(rows, TOP_K)
    order = jnp.argsort(flat_e)
    se, stok, sw = flat_e[order], flat_tok[order], flat_w[order]
    counts = jnp.bincount(flat_e, length=N_EXPERTS)
    starts = jnp.cumsum(counts) - counts
    padded = ((counts + MOE_BLOCK - 1) // MOE_BLOCK) * MOE_BLOCK
    pends = jnp.cumsum(padded)
    pstarts = pends - padded
    dest = pstarts[se] + (jnp.arange(A) - starts[se])
    P = ((A + N_EXPERTS * (MOE_BLOCK - 1) + MOE_BLOCK - 1) // MOE_BLOCK) * MOE_BLOCK
    n_blocks = P // MOE_BLOCK
    tok_buf = jnp.zeros((P,), jnp.int32).at[dest].set(stok.astype(jnp.int32))
    w_buf = jnp.zeros((P,), jnp.float32).at[dest].set(sw)
    blk_expert = jnp.minimum(
        jnp.searchsorted(pends, jnp.arange(n_blocks) * MOE_BLOCK, side='right'), N_EXPERTS - 1)

    def expert_block(args):
        tok, wt, e = args
        xb = t[tok]
        y = (jax.nn.silu(xb @ w_gate[e]) * (xb @ w_up[e])) @ w_down[e]
        return y * wt[:, None].astype(y.dtype)

    ys = lax.map(expert_block, (tok_buf.reshape(n_blocks, MOE_BLOCK),
                                w_buf.reshape(n_blocks, MOE_BLOCK), blk_expert))
    out = jax.ops.segment_sum(ys.reshape(P, D), tok_buf, num_segments=N)
    return out.reshape(B, S, D)


def setup_inputs(seed: int = 0) -> dict:
    key = jax.random.key(seed)
    ks = iter(jax.random.split(key, 64))
    f32 = jnp.float32
    L = DEPTH

    def nrm(shape, scale):
        return jax.random.normal(next(ks), shape, f32) * scale

    def gain(shape):
        return 1.0 + 0.05 * jax.random.normal(next(ks), shape, f32)

    x = nrm((BATCH, SEQ, D_MODEL), 1.0)
    mem = nrm((BATCH, MEM_LEN, D_MODEL), 1.0)
    positions = jnp.arange(SEQ, dtype=jnp.int32)[None, :] + jax.random.randint(
        next(ks), (BATCH, 1), 0, POS_OFFSET_MAX, dtype=jnp.int32)
    return {
        "x": x,
        "mem": mem,
        "positions": positions,
        "norm_mix": gain((L, D_MODEL)),
        "w_in": nrm((L, D_MODEL, IN_COLS), D_MODEL ** -0.5),
        "mla_g_cq": gain((L, MLA_Q_RANK)),
        "mla_w_uq": nrm((L, MLA_Q_RANK, MLA_HEADS * MLA_QK_DIM), MLA_Q_RANK ** -0.5),
        "mla_g_ckv": gain((L, MLA_KV_RANK)),
        "mla_w_ukv": nrm((L, MLA_KV_RANK, MLA_HEADS * (MLA_NOPE + MLA_V)), MLA_KV_RANK ** -0.5),
        "mla_g_q": gain((L, MLA_QK_DIM)),
        "mla_g_k": gain((L, MLA_QK_DIM)),
        "pool_w": nrm((L, len(POOL_WINDOWS), POOL_GROUP, POOL_GROUP), POOL_GROUP ** -0.5),
        "pool_scale": gain((L, GROUP_W)),
        "swa_g_q": gain((L, SWA_HEAD_DIM)),
        "swa_g_k": gain((L, SWA_HEAD_DIM)),
        "swa_sinks": nrm((L, SWA_HEADS), 0.5),
        "conv_w": nrm((L, CONV_W, CONV_CH), CONV_W ** -0.5),
        "conv_b": nrm((L, CONV_CH), 0.02),
        "mix_out_norm": gain((L, D_MIX)),
        "w_mix_out": nrm((L, D_MIX, D_MODEL), D_MIX ** -0.5),
        "norm_xa": gain((L, D_MODEL)),
        "norm_mem": gain((L, D_MODEL)),
        "xa_w_q": nrm((L, D_MODEL, XA_HEADS * XA_HEAD_DIM), D_MODEL ** -0.5),
        "xa_w_kv": nrm((L, D_MODEL, 2 * XA_HEADS * XA_HEAD_DIM), D_MODEL ** -0.5),
        "xa_g_q": gain((L, XA_HEAD_DIM)),
        "xa_g_k": gain((L, XA_HEAD_DIM)),
        "xa_w_o": nrm((L, XA_HEADS * XA_HEAD_DIM, D_MODEL), (XA_HEADS * XA_HEAD_DIM) ** -0.5),
        "norm_ffn": gain((L, D_MODEL)),
        "w_group": nrm((L, D_MODEL, N_EXPERT_GROUPS), D_MODEL ** -0.5),
        "b_group": nrm((L, N_EXPERT_GROUPS), 0.01),
        "w_expert": nrm((L, D_MODEL, N_EXPERTS), D_MODEL ** -0.5),
        "b_expert": nrm((L, N_EXPERTS), 0.01),
        "w_gate": nrm((L, N_EXPERTS, D_MODEL, D_EXPERT), D_MODEL ** -0.5),
        "w_up": nrm((L, N_EXPERTS, D_MODEL, D_EXPERT), D_MODEL ** -0.5),
        "w_down": nrm((L, N_EXPERTS, D_EXPERT, D_MODEL), D_EXPERT ** -0.5),
    }


def reference(x, mem, positions, norm_mix, w_in, mla_g_cq, mla_w_uq, mla_g_ckv, mla_w_ukv,
              mla_g_q, mla_g_k, pool_w, pool_scale, swa_g_q, swa_g_k, swa_sinks, conv_w, conv_b,
              mix_out_norm, w_mix_out, norm_xa, norm_mem, xa_w_q, xa_w_kv, xa_g_q, xa_g_k, xa_w_o,
              norm_ffn, w_group, b_group, w_expert, b_expert, w_gate, w_up, w_down):
    B, S, _ = x.shape
    for l in range(DEPTH):
        h = rms_norm(x, norm_mix[l])
        u = h @ w_in[l]
        c0 = A_COLS
        c1 = c0 + B_COLS
        c2 = c1 + C_COLS
        y_a = mla_mixer(u[..., :c0], positions, mla_g_cq[l], mla_w_uq[l], mla_g_ckv[l],
                        mla_w_ukv[l], mla_g_q[l], mla_g_k[l])
        y_b = pool_mixer(u[..., c0:c1], pool_w[l], pool_scale[l])
        y_c = swa_mixer(u[..., c1:c2], swa_g_q[l], swa_g_k[l], swa_sinks[l])
        y_d = conv_mixer(u[..., c2:], conv_w[l], conv_b[l])
        y = jnp.stack([y_a, y_b, y_c, y_d], axis=2)
        y = rms_norm(y, mix_out_norm[l].reshape(N_GROUPS_MIX, GROUP_W)).reshape(B, S, D_MIX)
        x = x + y @ w_mix_out[l]
        h = rms_norm(x, norm_xa[l])
        m = rms_norm(mem, norm_mem[l])
        x = x + cross_attention(h, m, xa_w_q[l], xa_w_kv[l], xa_g_q[l], xa_g_k[l], xa_w_o[l])
        h = rms_norm(x, norm_ffn[l])
        x = x + hier_moe(h, w_group[l], b_group[l], w_expert[l], b_expert[l],
                         w_gate[l], w_up[l], w_down[l])
    return x
```

```python
import functools

import jax
import jax.numpy as jnp
import numpy as np
from jax import lax
from jax.experimental import pallas as pl
from jax.experimental.pallas import tpu as pltpu

EPS = 1e-6
NEG_BIG = -1e30
ROPE_THETA = 10000.0

D_MODEL = 1024
MEM_LEN = 256
GROUP_W = 256

MLA_HEADS = 4
MLA_Q_RANK = 256
MLA_KV_RANK = 128
MLA_NOPE = 64
MLA_ROPE = 32
MLA_QK_DIM = MLA_NOPE + MLA_ROPE
MLA_V = 64
HEAD_LANES = 128

POOL_WINDOWS = (2, 4, 8, 16)
POOL_GROUP = 64
POOL_HALO = 16

SWA_HEADS = 4
SWA_KV_HEADS = 2
SWA_HEAD_DIM = 64
SWA_WINDOW = 128

CONV_CH = 256

XA_HEADS = 4
XA_HEAD_DIM = 64

N_EXPERT_GROUPS = 4
EXPERTS_PER_GROUP = 8
N_EXPERTS = 32
D_EXPERT = 256
MOE_BLOCK = 256

A_COLS = MLA_Q_RANK + MLA_KV_RANK + MLA_ROPE
B_COLS = GROUP_W
C_COLS = (SWA_HEADS + 2 * SWA_KV_HEADS) * SWA_HEAD_DIM
IN_COLS_PADDED = 2048

TOKEN_TILE = 512
ATTN_TILE = 512
RUN_ALIGN = 8
RUN_BITS = 7
SORT_ROWS = 2 * TOKEN_TILE + N_EXPERTS * RUN_ALIGN
XS_COLS = D_MODEL + 128
VMEM_LIMIT = 56 * 1024 * 1024

_F32 = jnp.float32
_BF16 = jnp.bfloat16


def _params(n_axes):
    return pltpu.CompilerParams(dimension_semantics=("arbitrary",) * n_axes,
                                vmem_limit_bytes=VMEM_LIMIT)


def _dot(a, b):
    return jnp.dot(a, b, preferred_element_type=_F32)


def _dot_nt(a, b):
    return lax.dot_general(a, b, (((1,), (1,)), ((), ())), preferred_element_type=_F32)


def _rms_scale(x, width):
    ss = jnp.sum(x * x, axis=-1, keepdims=True)
    return x * lax.rsqrt(ss * (1.0 / width) + EPS)


def _pair_norm64(x, lane_lo):
    x2 = x * x
    s_all = jnp.sum(x2, axis=-1, keepdims=True)
    s_lo = jnp.sum(jnp.where(lane_lo, x2, 0.0), axis=-1, keepdims=True)
    ss = jnp.where(lane_lo, s_lo, s_all - s_lo)
    return x * lax.rsqrt(ss * (1.0 / 64.0) + EPS)


def _rope_kernel(pos_ref, freq_ref, m1_ref, m2_ref, c_ref, s1_ref, s2_ref):
    ang = pos_ref[...] * freq_ref[...]
    s = jnp.sin(ang)
    c_ref[...] = jnp.cos(ang)
    s1_ref[...] = s * m1_ref[...]
    s2_ref[...] = s * m2_ref[...]


def _rope_tables(posf, n_rows):
    half = MLA_ROPE // 2
    inv_freq = ROPE_THETA ** (-jnp.arange(half, dtype=_F32) / half)
    zeros = jnp.zeros((HEAD_LANES,), _F32)
    freq = zeros.at[MLA_NOPE:MLA_NOPE + half].set(inv_freq).at[MLA_NOPE + half:MLA_QK_DIM].set(inv_freq)
    m1 = zeros.at[MLA_NOPE:MLA_NOPE + half].set(-1.0)
    m2 = zeros.at[MLA_NOPE + half:MLA_QK_DIM].set(1.0)
    tile = 1024
    row = pl.BlockSpec((1, HEAD_LANES), lambda i: (0, 0))
    tab = pl.BlockSpec((tile, HEAD_LANES), lambda i: (i, 0))
    shp = jax.ShapeDtypeStruct((n_rows, HEAD_LANES), _F32)
    return pl.pallas_call(
        _rope_kernel,
        grid=(n_rows // tile,),
        in_specs=[pl.BlockSpec((tile, 1), lambda i: (i, 0)), row, row, row],
        out_specs=[tab, tab, tab],
        out_shape=[shp, shp, shp],
        compiler_params=_params(1),
        name="rope_tables",
    )(posf, freq[None], m1[None], m2[None])


def _rotate(xn, c, s1, s2):
    return xn * c + pltpu.roll(xn, HEAD_LANES - 16, 1) * s1 + pltpu.roll(xn, 16, 1) * s2


def _proj_kernel(x_ref, gmix_ref, win_ref, gcq_ref, wuq_ref, gckv_ref, wukv_ref, gq_ref, gk_ref,
                 c_ref, s1_ref, s2_ref, wpool_ref, pscale_ref, convw_ref, convb_ref, gsq_ref, gsk_ref,
                 qm_ref, km_ref, vm_ref, qs_ref, ks_ref, vs_ref, yb_ref, yd_ref, halo_ref,
                 *, tiles_per_seq):
    tm = TOKEN_TILE
    i = pl.program_id(0)
    seq_tile = i % tiles_per_seq

    x = x_ref[...]
    h = _rms_scale(x, D_MODEL) * gmix_ref[...]
    u = _dot(h.astype(_BF16), win_ref[...])

    c = c_ref[...]
    s1 = s1_ref[...]
    s2 = s2_ref[...]

    cq = _rms_scale(u[:, 0:256], MLA_Q_RANK) * gcq_ref[...]
    q = _dot(cq.astype(_BF16), wuq_ref[...])
    gq = gq_ref[...]
    for hd in range(MLA_HEADS):
        sl = slice(hd * HEAD_LANES, (hd + 1) * HEAD_LANES)
        xn = _rms_scale(q[:, sl], MLA_QK_DIM) * gq
        qm_ref[:, sl] = (_rotate(xn, c, s1, s2) * (MLA_QK_DIM ** -0.5)).astype(_BF16)

    ckv = _rms_scale(u[:, 256:384], MLA_KV_RANK) * gckv_ref[...]
    kv = _dot(ckv.astype(_BF16), wukv_ref[...])
    krope = u[:, 384:512]
    gk = gk_ref[...]
    for hd in range(MLA_HEADS):
        sl = slice(hd * HEAD_LANES, (hd + 1) * HEAD_LANES)
        xn = _rms_scale(kv[:, sl] + krope, MLA_QK_DIM) * gk
        km_ref[:, sl] = _rotate(xn, c, s1, s2).astype(_BF16)
    vm_ref[...] = kv[:, 512:768].astype(_BF16)

    lane_lo = lax.broadcasted_iota(jnp.int32, (1, HEAD_LANES), 1) < 64
    gsq = gsq_ref[...]
    for blk in range(2):
        xq = u[:, 768 + blk * 128:768 + (blk + 1) * 128]
        qs_ref[:, blk * 128:(blk + 1) * 128] = (
            _pair_norm64(xq, lane_lo) * gsq * (SWA_HEAD_DIM ** -0.5)).astype(_BF16)
    ks_ref[...] = (_pair_norm64(u[:, 1024:1152], lane_lo) * gsk_ref[...]).astype(_BF16)
    vs_ref[...] = u[:, 1152:1280].astype(_BF16)

    @pl.when(seq_tile == 0)
    def _():
        halo_ref[...] = jnp.zeros_like(halo_ref)

    halo = halo_ref[...]
    up = u[:, 512:768]
    ud = u[:, 1280:2048]

    b = jnp.concatenate([halo[:, 0:256], up], axis=0)
    w2 = b + pltpu.roll(b, 1, 0)
    w4 = w2 + pltpu.roll(w2, 2, 0)
    w8 = w4 + pltpu.roll(w4, 4, 0)
    w16 = w8 + pltpu.roll(w8, 8, 0)
    lane = lax.broadcasted_iota(jnp.int32, (1, 256), 1)
    win = jnp.where(lane < 64, w2, jnp.where(lane < 128, w4, jnp.where(lane < 192, w8, w16)))
    win = win[POOL_HALO:, :]
    width = jnp.where(lane < 64, 2, jnp.where(lane < 128, 4, jnp.where(lane < 192, 8, 16)))
    t = seq_tile * tm + lax.broadcasted_iota(jnp.int32, (tm, 1), 0)
    count = jnp.minimum(t + 1, width).astype(_F32)
    pooled = win / count - up
    yb_ref[...] = (_dot(pooled.astype(_BF16), wpool_ref[...]) * pscale_ref[...]).astype(_BF16)

    z = ud[:, 256:512] * ud[:, 512:768]
    zh = halo[:, 512:768] * halo[:, 768:1024]
    zb = jnp.concatenate([zh, z], axis=0)
    cw = convw_ref[...]
    conv = (pltpu.roll(zb, 2, 0)[POOL_HALO:, :] * cw[0:1, :]
            + pltpu.roll(zb, 1, 0)[POOL_HALO:, :] * cw[1:2, :]
            + z * cw[2:3, :])
    yd_ref[...] = (ud[:, 0:256] * (conv + convb_ref[...])).astype(_BF16)

    halo_ref[:, 0:256] = up[tm - POOL_HALO:, :]
    halo_ref[:, 256:1024] = ud[tm - POOL_HALO:, :]


def _proj_call(x, lw, tabs, seq_len):
    n = x.shape[0]
    tm = TOKEN_TILE
    tiles_per_seq = seq_len // tm

    def full(a):
        nd = a.ndim
        return pl.BlockSpec(a.shape, lambda i, _nd=nd: (0,) * _nd)

    def rows(width):
        return pl.BlockSpec((tm, width), lambda i: (i, 0))

    consts = [lw["g_mix"], lw["w_in"], lw["g_cq"], lw["w_uq"], lw["g_ckv"], lw["w_ukv"],
              lw["g_q"], lw["g_k"]]
    consts2 = [lw["w_pool"], lw["pool_scale"], lw["conv_w"], lw["conv_b"], lw["g_sq"], lw["g_sk"]]
    out_widths = [512, 512, 256, 256, 128, 128, 256, 256]
    return pl.pallas_call(
        functools.partial(_proj_kernel, tiles_per_seq=tiles_per_seq),
        grid=(n // tm,),
        in_specs=[rows(D_MODEL)] + [full(a) for a in consts] + [rows(HEAD_LANES)] * 3
                 + [full(a) for a in consts2],
        out_specs=[rows(w) for w in out_widths],
        out_shape=[jax.ShapeDtypeStruct((n, w), _BF16) for w in out_widths],
        scratch_shapes=[pltpu.VMEM((POOL_HALO, 1024), _F32)],
        compiler_params=_params(1),
        name="proj_in",
    )(x, *consts, *tabs, *consts2)


def _mla_attn_kernel(qi_ref, ki_ref, q_ref, k_ref, v_ref, o_ref, m_ref, l_ref, acc_ref):
    tq = tk = ATTN_TILE
    p_id = pl.program_id(1)
    qi = qi_ref[p_id]
    ki = ki_ref[p_id]

    @pl.when(ki == 0)
    def _():
        m_ref[...] = jnp.full_like(m_ref, NEG_BIG)
        l_ref[...] = jnp.zeros_like(l_ref)
        acc_ref[...] = jnp.zeros_like(acc_ref)

    rows = qi * tq + lax.broadcasted_iota(jnp.int32, (tq, tk), 0)
    cols = ki * tk + lax.broadcasted_iota(jnp.int32, (tq, tk), 1)
    visible = cols <= rows
    lane_lo = lax.broadcasted_iota(jnp.int32, (1, HEAD_LANES), 1) < 64

    for pair in range(MLA_HEADS // 2):
        vblk = v_ref[:, pair * 128:(pair + 1) * 128]
        alphas = []
        pvs = []
        for sub in range(2):
            hd = 2 * pair + sub
            sl = slice(hd * HEAD_LANES, (hd + 1) * HEAD_LANES)
            s = _dot_nt(q_ref[:, sl], k_ref[:, sl])
            s = jnp.where(visible, s, NEG_BIG)
            m_prev = m_ref[hd]
            m_new = jnp.maximum(m_prev, jnp.max(s, axis=-1, keepdims=True))
            alpha = jnp.exp(m_prev - m_new)
            p = jnp.exp(s - m_new[:, 0:1])
            l_ref[hd] = alpha * l_ref[hd] + jnp.sum(p, axis=-1, keepdims=True)
            m_ref[hd] = m_new
            alphas.append(alpha)
            pvs.append(_dot(p.astype(_BF16), vblk))
        psl = slice(pair * 128, (pair + 1) * 128)
        acc_ref[:, psl] = (acc_ref[:, psl] * jnp.where(lane_lo, alphas[0], alphas[1])
                           + jnp.where(lane_lo, pvs[0], pvs[1]))

    @pl.when(ki == qi)
    def _():
        for pair in range(MLA_HEADS // 2):
            psl = slice(pair * 128, (pair + 1) * 128)
            denom = jnp.where(lane_lo, l_ref[2 * pair], l_ref[2 * pair + 1])
            o_ref[:, psl] = (acc_ref[:, psl] / denom).astype(o_ref.dtype)


def _mla_attn_call(qm, km, vm, batch, seq_len):
    tq = ATTN_TILE
    nq = seq_len // tq
    pairs = [(q, k) for q in range(nq) for k in range(q + 1)]
    qi = jnp.asarray([p[0] for p in pairs], jnp.int32)
    ki = jnp.asarray([p[1] for p in pairs], jnp.int32)
    grid_spec = pltpu.PrefetchScalarGridSpec(
        num_scalar_prefetch=2,
        grid=(batch, len(pairs)),
        in_specs=[
            pl.BlockSpec((tq, 512), lambda b, p, qi, ki: (b * nq + qi[p], 0)),
            pl.BlockSpec((tq, 512), lambda b, p, qi, ki: (b * nq + ki[p], 0)),
            pl.BlockSpec((tq, 256), lambda b, p, qi, ki: (b * nq + ki[p], 0)),
        ],
        out_specs=pl.BlockSpec((tq, 256), lambda b, p, qi, ki: (b * nq + qi[p], 0)),
        scratch_shapes=[pltpu.VMEM((MLA_HEADS, tq, HEAD_LANES), _F32),
                        pltpu.VMEM((MLA_HEADS, tq, HEAD_LANES), _F32),
                        pltpu.VMEM((tq, 256), _F32)],
    )
    return pl.pallas_call(
        _mla_attn_kernel,
        grid_spec=grid_spec,
        out_shape=jax.ShapeDtypeStruct((batch * seq_len, 256), _BF16),
        compiler_params=_params(2),
        name="mla_attention",
    )(qi, ki, qm, km, vm)


def _swa_kernel(sink_ref, q_ref, kc_ref, kp_ref, vc_ref, vp_ref, o_ref, *, tiles_per_seq):
    w = SWA_WINDOW
    i = pl.program_id(0)
    oldest = jnp.where((i % tiles_per_seq) == 0, 0, -w)
    lane_lo = lax.broadcasted_iota(jnp.int32, (1, 128), 1) < 64
    qpos = lax.broadcasted_iota(jnp.int32, (w, 2 * w), 0)
    kpos = lax.broadcasted_iota(jnp.int32, (w, 2 * w), 1) - w
    band = jnp.logical_and(kpos <= qpos, kpos > qpos - w)

    for jb in range(TOKEN_TILE // w):
        rs = slice(jb * w, (jb + 1) * w)
        if jb == 0:
            kprev, vprev = kp_ref[...], vp_ref[...]
            visible = jnp.logical_and(band, kpos >= oldest)
        else:
            ps = slice((jb - 1) * w, jb * w)
            kprev, vprev = kc_ref[ps, :], vc_ref[ps, :]
            visible = band
        kk = jnp.concatenate([kprev, kc_ref[rs, :]], axis=0)
        vv = jnp.concatenate([vprev, vc_ref[rs, :]], axis=0)
        zero = jnp.zeros_like(kk)
        k_half = (jnp.where(lane_lo, kk, zero), jnp.where(lane_lo, zero, kk))
        for blk in range(2):
            qblk = q_ref[rs, blk * 128:(blk + 1) * 128]
            outs = []
            for half in range(2):
                sink = sink_ref[2 * blk + half]
                s = _dot_nt(qblk, k_half[half])
                s = jnp.where(visible, s, NEG_BIG)
                m = jnp.maximum(jnp.max(s, axis=-1, keepdims=True), sink)
                p = jnp.exp(s - m)
                denom = jnp.sum(p, axis=-1, keepdims=True) + jnp.exp(sink - m)
                outs.append(_dot(p.astype(_BF16), vv) / denom)
            o_ref[rs, blk * 128:(blk + 1) * 128] = jnp.where(lane_lo, outs[0], outs[1]).astype(o_ref.dtype)


def _swa_call(qs, ks, vs, sinks, seq_len):
    n = qs.shape[0]
    tm = TOKEN_TILE
    per = tm // SWA_WINDOW
    prev = lambda i, s: (jnp.maximum(i * per - 1, 0), 0)
    grid_spec = pltpu.PrefetchScalarGridSpec(
        num_scalar_prefetch=1,
        grid=(n // tm,),
        in_specs=[
            pl.BlockSpec((tm, 256), lambda i, s: (i, 0)),
            pl.BlockSpec((tm, 128), lambda i, s: (i, 0)),
            pl.BlockSpec((SWA_WINDOW, 128), prev),
            pl.BlockSpec((tm, 128), lambda i, s: (i, 0)),
            pl.BlockSpec((SWA_WINDOW, 128), prev),
        ],
        out_specs=pl.BlockSpec((tm, 256), lambda i, s: (i, 0)),
    )
    return pl.pallas_call(
        functools.partial(_swa_kernel, tiles_per_seq=seq_len // tm),
        grid_spec=grid_spec,
        out_shape=jax.ShapeDtypeStruct((n, 256), _BF16),
        compiler_params=_params(1),
        name="swa_attention",
    )(sinks, qs, ks, ks, vs, vs)


def _memkv_kernel(mem_ref, gmem_ref, wkv_ref, gk_ref, k_ref, v_ref):
    m = _rms_scale(mem_ref[...], D_MODEL) * gmem_ref[...]
    kv = _dot(m.astype(_BF16), wkv_ref[...])
    lane_lo = lax.broadcasted_iota(jnp.int32, (1, 128), 1) < 64
    lane = lax.broadcasted_iota(jnp.int32, (1, 256), 1)
    k = jnp.concatenate([_pair_norm64(kv[:, 0:128], lane_lo), _pair_norm64(kv[:, 128:256], lane_lo)],
                        axis=1) * gk_ref[...]
    v = kv[:, 256:512]
    for hd in range(XA_HEADS):
        own = jnp.logical_and(lane >= hd * XA_HEAD_DIM, lane < (hd + 1) * XA_HEAD_DIM)
        k_ref[hd] = jnp.where(own, k, 0.0).astype(_BF16)
        v_ref[hd] = jnp.where(own, v, 0.0).astype(_BF16)


def _memkv_call(mem2d, g_mem, w_kv, g_k4, depth, batch):
    out = jax.ShapeDtypeStruct((depth, batch, XA_HEADS, MEM_LEN, 256), _BF16)
    ospec = pl.BlockSpec((None, None, XA_HEADS, MEM_LEN, 256), lambda l, b: (l, b, 0, 0, 0))
    return pl.pallas_call(
        _memkv_kernel,
        grid=(depth, batch),
        in_specs=[pl.BlockSpec((MEM_LEN, D_MODEL), lambda l, b: (b, 0)),
                  pl.BlockSpec((None, 1, D_MODEL), lambda l, b: (l, 0, 0)),
                  pl.BlockSpec((None, D_MODEL, 512), lambda l, b: (l, 0, 0)),
                  pl.BlockSpec((None, 1, 256), lambda l, b: (l, 0, 0))],
        out_specs=[ospec, ospec],
        out_shape=[out, out],
        compiler_params=_params(2),
        name="memory_kv",
    )(mem2d, g_mem, w_kv, g_k4)


def _mix_kernel(ya_ref, yb_ref, yc_ref, yd_ref, x_ref, gmo_ref, wmo_ref, gxa_ref, wq_ref, gxq_ref,
                kx_ref, vx_ref, wo_ref, gffn_ref, wr_ref, br_ref, ltri_ref, utri_ref,
                x2_ref, h3_ref, meta_ref, cnt_ref):
    tm = TOKEN_TILE
    gmo = gmo_ref[...]
    parts = []
    for g, ref in enumerate((ya_ref, yb_ref, yc_ref, yd_ref)):
        yg = ref[...].astype(_F32)
        parts.append((_rms_scale(yg, GROUP_W) * gmo[:, g * 256:(g + 1) * 256]).astype(_BF16))
    y = jnp.concatenate(parts, axis=1)
    x1 = x_ref[...] + _dot(y, wmo_ref[...])

    h = (_rms_scale(x1, D_MODEL) * gxa_ref[...]).astype(_BF16)
    q = _dot(h, wq_ref[...])
    lane_lo = lax.broadcasted_iota(jnp.int32, (1, 128), 1) < 64
    qn = jnp.concatenate([_pair_norm64(q[:, 0:128], lane_lo), _pair_norm64(q[:, 128:256], lane_lo)], axis=1)
    qn = (qn * gxq_ref[...] * (XA_HEAD_DIM ** -0.5)).astype(_BF16)
    o = jnp.zeros((tm, 256), _F32)
    for hd in range(XA_HEADS):
        s = _dot_nt(qn, kx_ref[hd])
        m = jnp.max(s, axis=-1, keepdims=True)
        p = jnp.exp(s - m)
        denom = jnp.sum(p, axis=-1, keepdims=True)
        o = o + _dot(p.astype(_BF16), vx_ref[hd]) / denom
    x2 = x1 + _dot(o.astype(_BF16), wo_ref[...])
    x2_ref[...] = x2

    h3 = _rms_scale(x2, D_MODEL) * gffn_ref[...]
    h3_ref[...] = h3
    logits = jnp.dot(h3, wr_ref[...], preferred_element_type=_F32,
                     precision=lax.Precision.HIGHEST) + br_ref[...]
    lane = lax.broadcasted_iota(jnp.int32, (tm, 128), 1)
    far = jnp.int32(1 << 20)
    is_group = jnp.logical_and(lane >= N_EXPERTS, lane < N_EXPERTS + N_EXPERT_GROUPS)
    gl = jnp.where(is_group, logits, -jnp.inf)
    gmax = jnp.max(gl, axis=-1, keepdims=True)
    gidx = jnp.min(jnp.where(gl == gmax, lane, far), axis=-1, keepdims=True) - N_EXPERTS
    g_w = 1.0 / jnp.sum(jnp.where(is_group, jnp.exp(logits - gmax), 0.0), axis=-1, keepdims=True)
    in_group = jnp.logical_and(lane < N_EXPERTS, jnp.right_shift(lane, 3) == gidx)
    el = jnp.where(in_group, logits, -jnp.inf)
    emax = jnp.max(el, axis=-1, keepdims=True)
    ep = jnp.where(in_group, jnp.exp(logits - emax), 0.0)
    prob = jnp.where(in_group, ep / jnp.sum(ep, axis=-1, keepdims=True), -1.0)
    p1 = jnp.max(prob, axis=-1, keepdims=True)
    e0 = jnp.min(jnp.where(prob == p1, lane, far), axis=-1, keepdims=True)
    prob2 = jnp.where(lane == e0, -1.0, prob)
    p2 = jnp.max(prob2, axis=-1, keepdims=True)
    e1 = jnp.min(jnp.where(prob2 == p2, lane, far), axis=-1, keepdims=True)
    w0 = g_w * (p1 / (p1 + p2))
    w1 = g_w * (p2 / (p1 + p2))

    onehot = jnp.where(jnp.logical_or(lane == e0, lane == e1), 1.0, 0.0)
    prefix = _dot(ltri_ref[...], onehot.astype(_BF16))
    counts = jnp.sum(onehot, axis=0, keepdims=True)
    units = jnp.floor((counts + (RUN_ALIGN - 1)) * (1.0 / RUN_ALIGN))
    offs = _dot(jnp.broadcast_to(units, (8, 128)).astype(_BF16), utri_ref[...])[0:1, :] * RUN_ALIGN
    where_to = prefix + offs
    pos0 = jnp.sum(jnp.where(lane == e0, where_to, 0.0), axis=-1, keepdims=True)
    pos1 = jnp.sum(jnp.where(lane == e1, where_to, 0.0), axis=-1, keepdims=True)

    meta = jnp.where(lane == 0, e0.astype(_F32),
           jnp.where(lane == 1, e1.astype(_F32),
           jnp.where(lane == 2, w0,
           jnp.where(lane == 3, w1,
           jnp.where(lane == 4, pos0,
           jnp.where(lane == 5, pos1, 0.0))))))
    meta_ref[...] = meta
    cnt_ref[...] = jnp.broadcast_to(counts, (8, 128))


def _mix_call(ya, yb, yc, yd, x, lw, kx, vx, seq_len):
    n = x.shape[0]
    tm = TOKEN_TILE
    tiles_per_seq = seq_len // tm

    def full(a):
        nd = a.ndim
        return pl.BlockSpec(a.shape, lambda i, _nd=nd: (0,) * _nd)

    def rows(width):
        return pl.BlockSpec((tm, width), lambda i: (i, 0))

    kvspec = pl.BlockSpec((None, XA_HEADS, MEM_LEN, 256), lambda i: (i // tiles_per_seq, 0, 0, 0))
    consts_a = [lw["g_mo"], lw["w_mo"], lw["g_xa"], lw["xa_w_q"], lw["g_xq"]]
    consts_b = [lw["xa_w_o"], lw["g_ffn"], lw["w_router"], lw["b_router"], lw["ltri"], lw["utri"]]
    ntiles = n // tm
    return pl.pallas_call(
        _mix_kernel,
        grid=(ntiles,),
        in_specs=[rows(256)] * 4 + [rows(D_MODEL)] + [full(a) for a in consts_a]
                 + [kvspec, kvspec] + [full(a) for a in consts_b],
        out_specs=[rows(D_MODEL), rows(D_MODEL), rows(128), pl.BlockSpec((8, 128), lambda i: (i, 0))],
        out_shape=[jax.ShapeDtypeStruct((n, D_MODEL), _F32), jax.ShapeDtypeStruct((n, D_MODEL), _F32),
                   jax.ShapeDtypeStruct((n, 128), _F32), jax.ShapeDtypeStruct((ntiles * 8, 128), _F32)],
        compiler_params=_params(1),
        name="mix_xattn_router",
    )(ya, yb, yc, yd, x, *consts_a, kx, vx, *consts_b)


def _run_copies(tab_ref, t, local_ref, remote_ref, sems, to_remote):
    for e in range(N_EXPERTS):
        base = (t * N_EXPERTS + e) * 3
        units = tab_ref[base]
        loc = tab_ref[base + 1]
        rem = tab_ref[base + 2]
        for bit in range(RUN_BITS):
            rows = RUN_ALIGN << bit
            done = (units & ((1 << bit) - 1)) * RUN_ALIGN
            lsl = local_ref.at[pl.ds(pl.multiple_of(loc + done, RUN_ALIGN), rows)]
            rsl = remote_ref.at[pl.ds(pl.multiple_of(rem + done, RUN_ALIGN), rows)]
            src, dst = (lsl, rsl) if to_remote else (rsl, lsl)
            yield (units & (1 << bit)) != 0, pltpu.make_async_copy(src, dst, sems.at[e, bit])


def _selection(meta, rows):
    r = lax.broadcasted_iota(jnp.int32, (TOKEN_TILE, rows), 1)
    pos0 = meta[:, 4:5].astype(jnp.int32)
    pos1 = meta[:, 5:6].astype(jnp.int32)
    return jnp.where(r == pos0, 1.0, 0.0).astype(_BF16), jnp.where(r == pos1, 1.0, 0.0).astype(_BF16)


def _dispatch_kernel(tab_ref, meta_ref, h3_ref, xs_in_ref, xs_ref, buf_ref, sems):
    del xs_in_ref
    t = pl.program_id(0)
    meta = meta_ref[...]
    sel0, sel1 = _selection(meta, SORT_ROWS)
    both = sel0 + sel1
    tn = (((0,), (0,)), ((), ()))
    buf_ref[:, 0:D_MODEL] = lax.dot_general(both, h3_ref[...].astype(_BF16), tn,
                                            preferred_element_type=_F32)
    lane = lax.broadcasted_iota(jnp.int32, (TOKEN_TILE, 128), 1)
    cols = []
    for k in range(2):
        w = meta[:, 2 + k:3 + k]
        hi = w.astype(_BF16).astype(_F32)
        mid = (w - hi).astype(_BF16).astype(_F32)
        lo = (w - hi) - mid
        cols.append(jnp.where(lane == 0, hi, jnp.where(lane == 1, mid, jnp.where(lane == 2, lo, 0.0)))
                    .astype(_BF16))
    buf_ref[:, D_MODEL:XS_COLS] = (
        lax.dot_general(sel0, cols[0], tn, preferred_element_type=_F32)
        + lax.dot_general(sel1, cols[1], tn, preferred_element_type=_F32))

    copies = list(_run_copies(tab_ref, t, buf_ref, xs_ref, sems, True))
    for cond, cp in copies:
        @pl.when(cond)
        def _(cp=cp):
            cp.start()
    for cond, cp in copies:
        @pl.when(cond)
        def _(cp=cp):
            cp.wait()


def _dispatch_call(tab, meta, h3, xs_zero):
    n = h3.shape[0]
    tm = TOKEN_TILE
    grid_spec = pltpu.PrefetchScalarGridSpec(
        num_scalar_prefetch=1,
        grid=(n // tm,),
        in_specs=[pl.BlockSpec((tm, 128), lambda i, tab: (i, 0)),
                  pl.BlockSpec((tm, D_MODEL), lambda i, tab: (i, 0)),
                  pl.BlockSpec(memory_space=pl.ANY)],
        out_specs=pl.BlockSpec(memory_space=pl.ANY),
        scratch_shapes=[pltpu.VMEM((SORT_ROWS, XS_COLS), _F32),
                        pltpu.SemaphoreType.DMA((N_EXPERTS, RUN_BITS))],
    )
    return pl.pallas_call(
        _dispatch_kernel,
        grid_spec=grid_spec,
        out_shape=jax.ShapeDtypeStruct(xs_zero.shape, _F32),
        input_output_aliases={3: 0},
        compiler_params=_params(1),
        name="moe_dispatch",
    )(tab, meta, h3, xs_zero)


def _ffn_kernel(be_ref, nv_ref, xs_ref, wg_ref, wu_ref, wd_ref, ys_ref):
    i = pl.program_id(0)

    @pl.when(i < nv_ref[0])
    def _():
        blk = xs_ref[...]
        xb = blk[:, 0:D_MODEL].astype(_BF16)
        wt = blk[:, D_MODEL:D_MODEL + 1] + blk[:, D_MODEL + 1:D_MODEL + 2] + blk[:, D_MODEL + 2:D_MODEL + 3]
        g = _dot(xb, wg_ref[...].astype(_BF16))
        u = _dot(xb, wu_ref[...].astype(_BF16))
        a = (g * jax.nn.sigmoid(g)) * u
        y = _dot(a.astype(_BF16), wd_ref[...].astype(_BF16))
        ys_ref[...] = y * wt

    @pl.when(i >= nv_ref[0])
    def _():
        ys_ref[...] = jnp.zeros_like(ys_ref)


def _ffn_call(blk_expert, n_valid, xs, w_gate, w_up, w_down, layer):
    n_blocks = xs.shape[0] // MOE_BLOCK
    wspec = lambda shape: pl.BlockSpec((None, None) + shape, lambda i, be, nv: (layer, be[i], 0, 0))
    grid_spec = pltpu.PrefetchScalarGridSpec(
        num_scalar_prefetch=2,
        grid=(n_blocks,),
        in_specs=[pl.BlockSpec((MOE_BLOCK, XS_COLS), lambda i, be, nv: (i, 0)),
                  wspec((D_MODEL, D_EXPERT)), wspec((D_MODEL, D_EXPERT)), wspec((D_EXPERT, D_MODEL))],
        out_specs=pl.BlockSpec((MOE_BLOCK, D_MODEL), lambda i, be, nv: (i, 0)),
    )
    return pl.pallas_call(
        _ffn_kernel,
        grid_spec=grid_spec,
        out_shape=jax.ShapeDtypeStruct((xs.shape[0], D_MODEL), _F32),
        compiler_params=_params(1),
        name="moe_experts",
    )(blk_expert, n_valid, xs, w_gate, w_up, w_down)


def _combine_kernel(tab_ref, meta_ref, x2_ref, ys_ref, out_ref, buf_ref, sems):
    t = pl.program_id(0)

    @pl.when(t == 0)
    def _():
        buf_ref[...] = jnp.zeros_like(buf_ref)

    copies = list(_run_copies(tab_ref, t, buf_ref, ys_ref, sems, False))
    for cond, cp in copies:
        @pl.when(cond)
        def _(cp=cp):
            cp.start()
    for cond, cp in copies:
        @pl.when(cond)
        def _(cp=cp):
            cp.wait()

    sel0, sel1 = _selection(meta_ref[...], SORT_ROWS)
    out_ref[...] = x2_ref[...] + _dot(sel0 + sel1, buf_ref[...].astype(_BF16))


def _combine_call(tab, meta, x2, ys):
    n = x2.shape[0]
    tm = TOKEN_TILE
    grid_spec = pltpu.PrefetchScalarGridSpec(
        num_scalar_prefetch=1,
        grid=(n // tm,),
        in_specs=[pl.BlockSpec((tm, 128), lambda i, tab: (i, 0)),
                  pl.BlockSpec((tm, D_MODEL), lambda i, tab: (i, 0)),
                  pl.BlockSpec(memory_space=pl.ANY)],
        out_specs=pl.BlockSpec((tm, D_MODEL), lambda i, tab: (i, 0)),
        scratch_shapes=[pltpu.VMEM((SORT_ROWS, D_MODEL), _F32),
                        pltpu.SemaphoreType.DMA((N_EXPERTS, RUN_BITS))],
    )
    return pl.pallas_call(
        _combine_kernel,
        grid_spec=grid_spec,
        out_shape=jax.ShapeDtypeStruct((n, D_MODEL), _F32),
        compiler_params=_params(1),
        name="moe_combine",
    )(tab, meta, x2, ys)


def _moe_tables(cnt, n_tokens):
    ntiles = n_tokens // TOKEN_TILE
    counts = cnt.reshape(ntiles, 8, 128)[:, 0, :N_EXPERTS].astype(jnp.int32)
    units = (counts + RUN_ALIGN - 1) // RUN_ALIGN
    padded = units * RUN_ALIGN
    local = jnp.cumsum(padded, axis=1) - padded
    total = jnp.sum(padded, axis=0)
    total_blk = ((total + MOE_BLOCK - 1) // MOE_BLOCK) * MOE_BLOCK
    ends = jnp.cumsum(total_blk)
    starts = ends - total_blk
    remote = starts[None, :] + jnp.cumsum(padded, axis=0) - padded
    tab = jnp.stack([units, local, remote], axis=-1).reshape(-1).astype(jnp.int32)
    n_blocks = _xs_rows(n_tokens) // MOE_BLOCK
    blk_expert = jnp.minimum(
        jnp.searchsorted(ends, jnp.arange(n_blocks, dtype=jnp.int32) * MOE_BLOCK, side="right"),
        N_EXPERTS - 1).astype(jnp.int32)
    n_valid = (ends[-1] // MOE_BLOCK).astype(jnp.int32).reshape(1)
    return tab, blk_expert, n_valid


def _xs_rows(n_tokens):
    ntiles = n_tokens // TOKEN_TILE
    worst = 2 * n_tokens + ntiles * N_EXPERTS * (RUN_ALIGN - 1) + N_EXPERTS * (MOE_BLOCK - 1)
    return ((worst + MOE_BLOCK - 1) // MOE_BLOCK) * MOE_BLOCK


def _pad_last(a, width):
    return jnp.pad(a, [(0, 0)] * (a.ndim - 1) + [(0, width - a.shape[-1])])


def _swap_mid_heads(a, axis):
    shape = a.shape
    a = a.reshape(shape[:axis] + (4, 64) + shape[axis + 1:])
    a = jnp.take(a, jnp.asarray([0, 2, 1, 3]), axis=axis)
    return a.reshape(shape)


def _layer_weights(l, p):
    w_in = p["w_in"][l]
    a, b = w_in[:, :A_COLS], w_in[:, A_COLS:A_COLS + B_COLS]
    c = w_in[:, A_COLS + B_COLS:A_COLS + B_COLS + C_COLS]
    d = w_in[:, A_COLS + B_COLS + C_COLS:]
    zeros = lambda w: jnp.zeros((D_MODEL, w), _F32)
    w_in_p = jnp.concatenate([
        a[:, :384], zeros(64), a[:, 384:416], zeros(32), b,
        _swap_mid_heads(c[:, :256], 1), c[:, 256:384], c[:, 384:512], d], axis=1).astype(_BF16)
    assert w_in_p.shape[1] == IN_COLS_PADDED

    w_uq = _pad_last(p["mla_w_uq"][l].reshape(MLA_Q_RANK, MLA_HEADS, MLA_QK_DIM), HEAD_LANES)
    w_uq = w_uq.reshape(MLA_Q_RANK, MLA_HEADS * HEAD_LANES).astype(_BF16)
    w_ukv = p["mla_w_ukv"][l].reshape(MLA_KV_RANK, MLA_HEADS, MLA_NOPE + MLA_V)
    k_nope = _pad_last(w_ukv[..., :MLA_NOPE], HEAD_LANES).reshape(MLA_KV_RANK, MLA_HEADS * HEAD_LANES)
    v_part = w_ukv[..., MLA_NOPE:].reshape(MLA_KV_RANK, MLA_HEADS * MLA_V)
    w_ukv_p = jnp.concatenate([k_nope, v_part], axis=1).astype(_BF16)

    eye = jnp.eye(len(POOL_WINDOWS), dtype=_F32)
    w_pool = jnp.einsum("gcd,gh->gchd", p["pool_w"][l], eye).reshape(256, 256).astype(_BF16)

    g_mo = p["mix_out_norm"][l]
    g_mo = jnp.concatenate([g_mo[:512], _swap_mid_heads(g_mo[512:768], 0), g_mo[768:]])
    w_mo = p["w_mix_out"][l]
    w_mo = jnp.concatenate([w_mo[:512], _swap_mid_heads(w_mo[512:768], 0), w_mo[768:]], axis=0)

    w_router = jnp.concatenate([p["w_expert"][l], p["w_group"][l]], axis=1)
    b_router = jnp.concatenate([p["b_expert"][l], p["b_group"][l]])
    tm = TOKEN_TILE
    ltri = (jnp.arange(tm)[None, :] < jnp.arange(tm)[:, None]).astype(_BF16)
    utri = (jnp.arange(128)[:, None] < jnp.arange(128)[None, :]).astype(_BF16)
    tile2 = lambda g: jnp.concatenate([g, g])[None]
    return dict(
        g_mix=p["norm_mix"][l][None], w_in=w_in_p,
        g_cq=p["mla_g_cq"][l][None], w_uq=w_uq, g_ckv=p["mla_g_ckv"][l][None], w_ukv=w_ukv_p,
        g_q=_pad_last(p["mla_g_q"][l], HEAD_LANES)[None], g_k=_pad_last(p["mla_g_k"][l], HEAD_LANES)[None],
        w_pool=w_pool, pool_scale=p["pool_scale"][l][None], conv_w=p["conv_w"][l],
        conv_b=p["conv_b"][l][None], g_sq=tile2(p["swa_g_q"][l]), g_sk=tile2(p["swa_g_k"][l]),
        sinks=jnp.take(p["swa_sinks"][l], jnp.asarray([0, 2, 1, 3])),
        g_mo=g_mo[None], w_mo=w_mo.astype(_BF16), g_xa=p["norm_xa"][l][None],
        xa_w_q=p["xa_w_q"][l].astype(_BF16), g_xq=jnp.tile(p["xa_g_q"][l], 4)[None],
        xa_w_o=p["xa_w_o"][l].astype(_BF16), g_ffn=p["norm_ffn"][l][None],
        w_router=_pad_last(w_router, 128), b_router=_pad_last(b_router, 128)[None],
        ltri=ltri, utri=utri,
    )


def kernel(x, mem, positions, norm_mix, w_in, mla_g_cq, mla_w_uq, mla_g_ckv, mla_w_ukv, mla_g_q, mla_g_k, pool_w, pool_scale, swa_g_q, swa_g_k, swa_sinks, conv_w, conv_b, mix_out_norm, w_mix_out, norm_xa, norm_mem, xa_w_q, xa_w_kv, xa_g_q, xa_g_k, xa_w_o, norm_ffn, w_group, b_group, w_expert, b_expert, w_gate, w_up, w_down):
    p = dict(norm_mix=norm_mix, w_in=w_in, mla_g_cq=mla_g_cq, mla_w_uq=mla_w_uq, mla_g_ckv=mla_g_ckv,
             mla_w_ukv=mla_w_ukv, mla_g_q=mla_g_q, mla_g_k=mla_g_k, pool_w=pool_w, pool_scale=pool_scale,
             swa_g_q=swa_g_q, swa_g_k=swa_g_k, swa_sinks=swa_sinks, conv_w=conv_w, conv_b=conv_b,
             mix_out_norm=mix_out_norm, w_mix_out=w_mix_out, norm_xa=norm_xa, xa_w_q=xa_w_q,
             xa_g_q=xa_g_q, xa_w_o=xa_w_o, norm_ffn=norm_ffn, w_group=w_group, b_group=b_group,
             w_expert=w_expert, b_expert=b_expert)
    batch, seq_len, _ = x.shape
    depth = w_in.shape[0]
    n = batch * seq_len
    assert seq_len % TOKEN_TILE == 0 and seq_len % ATTN_TILE == 0 and n % 1024 == 0
    assert mem.shape[1] == MEM_LEN

    xf = x.reshape(n, D_MODEL)
    tabs = _rope_tables(positions.astype(_F32).reshape(n, 1), n)
    kx, vx = _memkv_call(mem.reshape(batch * MEM_LEN, D_MODEL), norm_mem[:, None, :],
                         xa_w_kv.astype(_BF16), jnp.tile(xa_g_k, (1, 4))[:, None, :], depth, batch)
    for l in range(depth):
        xs_zero = jnp.zeros((_xs_rows(n), XS_COLS), _F32)
        lw = _layer_weights(l, p)
        qm, km, vm, qs, ks, vs, yb, yd = _proj_call(xf, lw, tabs, seq_len)
        ya = _mla_attn_call(qm, km, vm, batch, seq_len)
        yc = _swa_call(qs, ks, vs, lw["sinks"], seq_len)
        x2, h3, meta, cnt = _mix_call(ya, yb, yc, yd, xf, lw, kx[l], vx[l], seq_len)
        tab, blk_expert, n_valid = _moe_tables(cnt, n)
        xs = _dispatch_call(tab, meta, h3, xs_zero)
        ys = _ffn_call(blk_expert, n_valid, xs, w_gate, w_up, w_down, l)
        xf = _combine_call(tab, meta, x2, ys)
    return xf.reshape(batch, seq_len, D_MODEL)
```

```python
import functools

import jax
import jax.numpy as jnp
import numpy as np
from jax import lax
from jax.experimental import pallas as pl
from jax.experimental.pallas import tpu as pltpu

EPS = 1e-6
NEG_BIG = -1e30
ROPE_THETA = 10000.0

D_MODEL = 1024
MEM_LEN = 256
GROUP_W = 256

MLA_HEADS = 4
MLA_Q_RANK = 256
MLA_KV_RANK = 128
MLA_NOPE = 64
MLA_ROPE = 32
MLA_QK_DIM = MLA_NOPE + MLA_ROPE
MLA_V = 64
HEAD_LANES = 128

POOL_WINDOWS = (2, 4, 8, 16)
POOL_GROUP = 64
POOL_HALO = 16

SWA_HEADS = 4
SWA_KV_HEADS = 2
SWA_HEAD_DIM = 64
SWA_WINDOW = 128

CONV_CH = 256

XA_HEADS = 4
XA_HEAD_DIM = 64

N_EXPERT_GROUPS = 4
EXPERTS_PER_GROUP = 8
N_EXPERTS = 32
D_EXPERT = 256
MOE_BLOCK = 256

A_COLS = MLA_Q_RANK + MLA_KV_RANK + MLA_ROPE
B_COLS = GROUP_W
C_COLS = (SWA_HEADS + 2 * SWA_KV_HEADS) * SWA_HEAD_DIM
COL_CQ, COL_CKV, COL_KROPE, COL_KROPE_SWAP = 0, 256, 384, 512
COL_POOL, COL_SWA_Q, COL_SWA_K, COL_SWA_V, COL_CONV = 640, 896, 1152, 1280, 1408
IN_COLS_PADDED = COL_CONV + 3 * CONV_CH

TOKEN_TILE = 512
ATTN_TILE = 512
RUN_ALIGN = 8
RUN_BITS = 7
SORT_ROWS = 2 * TOKEN_TILE + N_EXPERTS * RUN_ALIGN
XS_COLS = D_MODEL + 128
VMEM_LIMIT = 56 * 1024 * 1024

_F32 = jnp.float32
_BF16 = jnp.bfloat16


def _params(n_axes):
    return pltpu.CompilerParams(dimension_semantics=("arbitrary",) * n_axes,
                                vmem_limit_bytes=VMEM_LIMIT)


def _dot(a, b):
    return jnp.dot(a, b, preferred_element_type=_F32)


def _dot_nt(a, b):
    return lax.dot_general(a, b, (((1,), (1,)), ((), ())), preferred_element_type=_F32)


def _rms_factor(x, width):
    ss = jnp.sum(x * x, axis=-1, keepdims=True)
    return lax.rsqrt(ss * (1.0 / width) + EPS)


def _rms_scale(x, width):
    return x * _rms_factor(x, width)


def _pair_norm64(x, lane_lo):
    x2 = x * x
    s_all = jnp.sum(x2, axis=-1, keepdims=True)
    s_lo = jnp.sum(jnp.where(lane_lo, x2, 0.0), axis=-1, keepdims=True)
    ss = jnp.where(lane_lo, s_lo, s_all - s_lo)
    return x * lax.rsqrt(ss * (1.0 / 64.0) + EPS)


def _rope_kernel(pos_ref, freq_ref, sign_ref, c_ref, s_ref):
    ang = pos_ref[...] * freq_ref[...]
    c_ref[...] = jnp.cos(ang)
    s_ref[...] = jnp.sin(ang) * sign_ref[...]


def _rope_tables(posf, n_rows):
    half = MLA_ROPE // 2
    inv_freq = ROPE_THETA ** (-jnp.arange(half, dtype=_F32) / half)
    zeros = jnp.zeros((HEAD_LANES,), _F32)
    freq = zeros.at[MLA_NOPE:MLA_NOPE + half].set(inv_freq).at[MLA_NOPE + half:MLA_QK_DIM].set(inv_freq)
    sign = zeros.at[MLA_NOPE:MLA_NOPE + half].set(-1.0).at[MLA_NOPE + half:MLA_QK_DIM].set(1.0)
    tile = 1024
    row = pl.BlockSpec((1, HEAD_LANES), lambda i: (0, 0))
    tab = pl.BlockSpec((tile, HEAD_LANES), lambda i: (i, 0))
    shp = jax.ShapeDtypeStruct((n_rows, HEAD_LANES), _F32)
    return pl.pallas_call(
        _rope_kernel,
        grid=(n_rows // tile,),
        in_specs=[pl.BlockSpec((tile, 1), lambda i: (i, 0)), row, row],
        out_specs=[tab, tab],
        out_shape=[shp, shp],
        compiler_params=_params(1),
        name="rope_tables",
    )(posf, freq[None], sign[None])


def _proj_kernel(x_ref, gmix_ref, win_ref, gcq_ref, wuq_ref, gckv_ref, wukv_ref, gq_ref, gqs_ref,
                 gk_ref, gks_ref, c_ref, s_ref, wpool_ref, pscale_ref, convw_ref, convb_ref, gsq_ref,
                 gsk_ref, qm_ref, km_ref, vm_ref, qs_ref, ks_ref, vs_ref, yb_ref, yd_ref, halo_ref,
                 *, tiles_per_seq):
    tm = TOKEN_TILE
    i = pl.program_id(0)
    seq_tile = i % tiles_per_seq

    x = x_ref[...]
    h = _rms_scale(x, D_MODEL) * gmix_ref[...]
    u = _dot(h.astype(_BF16), win_ref[...])

    c = c_ref[...]
    s = s_ref[...]
    hw = MLA_HEADS * HEAD_LANES

    cq = _rms_scale(u[:, COL_CQ:COL_CQ + MLA_Q_RANK], MLA_Q_RANK) * gcq_ref[...]
    q = _dot(cq.astype(_BF16), wuq_ref[...])
    gq = gq_ref[...] * (MLA_QK_DIM ** -0.5)
    gqs = gqs_ref[...] * (MLA_QK_DIM ** -0.5)
    for hd in range(MLA_HEADS):
        sl = slice(hd * HEAD_LANES, (hd + 1) * HEAD_LANES)
        xq = q[:, sl]
        r = _rms_factor(xq, MLA_QK_DIM)
        qm_ref[:, sl] = ((xq * r) * (gq * c) + (q[:, hw + sl.start:hw + sl.stop] * r) * (gqs * s)).astype(_BF16)

    ckv = _rms_scale(u[:, COL_CKV:COL_CKV + MLA_KV_RANK], MLA_KV_RANK) * gckv_ref[...]
    kv = _dot(ckv.astype(_BF16), wukv_ref[...])
    krope = u[:, COL_KROPE:COL_KROPE + HEAD_LANES]
    krope_swap = u[:, COL_KROPE_SWAP:COL_KROPE_SWAP + HEAD_LANES]
    gk = gk_ref[...]
    gks = gks_ref[...]
    for hd in range(MLA_HEADS):
        sl = slice(hd * HEAD_LANES, (hd + 1) * HEAD_LANES)
        xk = kv[:, sl] + krope
        r = _rms_factor(xk, MLA_QK_DIM)
        km_ref[:, sl] = ((xk * r) * (gk * c) + (krope_swap * r) * (gks * s)).astype(_BF16)
    vm_ref[...] = kv[:, hw:hw + MLA_HEADS * MLA_V].astype(_BF16)

    lane_lo = lax.broadcasted_iota(jnp.int32, (1, HEAD_LANES), 1) < 64
    gsq = gsq_ref[...]
    for blk in range(2):
        xq = u[:, COL_SWA_Q + blk * 128:COL_SWA_Q + (blk + 1) * 128]
        qs_ref[:, blk * 128:(blk + 1) * 128] = (
            _pair_norm64(xq, lane_lo) * gsq * (SWA_HEAD_DIM ** -0.5)).astype(_BF16)
    ks_ref[...] = (_pair_norm64(u[:, COL_SWA_K:COL_SWA_K + 128], lane_lo) * gsk_ref[...]).astype(_BF16)
    vs_ref[...] = u[:, COL_SWA_V:COL_SWA_V + 128].astype(_BF16)

    @pl.when(seq_tile == 0)
    def _():
        halo_ref[...] = jnp.zeros_like(halo_ref)

    halo = halo_ref[...]
    up = u[:, COL_POOL:COL_POOL + GROUP_W]
    ud = u[:, COL_CONV:COL_CONV + 3 * CONV_CH]

    b = jnp.concatenate([halo[:, 0:256], up], axis=0)
    w2 = b + pltpu.roll(b, 1, 0)
    w4 = w2 + pltpu.roll(w2, 2, 0)
    w8 = w4 + pltpu.roll(w4, 4, 0)
    w16 = w8 + pltpu.roll(w8, 8, 0)
    lane = lax.broadcasted_iota(jnp.int32, (1, 256), 1)
    win = jnp.where(lane < 64, w2, jnp.where(lane < 128, w4, jnp.where(lane < 192, w8, w16)))
    win = win[POOL_HALO:, :]
    width = jnp.where(lane < 64, 2, jnp.where(lane < 128, 4, jnp.where(lane < 192, 8, 16)))
    t = seq_tile * tm + lax.broadcasted_iota(jnp.int32, (tm, 1), 0)
    count = jnp.minimum(t + 1, width).astype(_F32)
    pooled = win / count - up
    yb_ref[...] = (_dot(pooled.astype(_BF16), wpool_ref[...]) * pscale_ref[...]).astype(_BF16)

    z = ud[:, 256:512] * ud[:, 512:768]
    zh = halo[:, 512:768] * halo[:, 768:1024]
    zb = jnp.concatenate([zh, z], axis=0)
    cw = convw_ref[...]
    conv = (pltpu.roll(zb, 2, 0)[POOL_HALO:, :] * cw[0:1, :]
            + pltpu.roll(zb, 1, 0)[POOL_HALO:, :] * cw[1:2, :]
            + z * cw[2:3, :])
    yd_ref[...] = (ud[:, 0:256] * (conv + convb_ref[...])).astype(_BF16)

    halo_ref[:, 0:256] = up[tm - POOL_HALO:, :]
    halo_ref[:, 256:1024] = ud[tm - POOL_HALO:, :]


def _proj_call(x, lw, tabs, seq_len):
    n = x.shape[0]
    tm = TOKEN_TILE
    tiles_per_seq = seq_len // tm

    def full(a):
        nd = a.ndim
        return pl.BlockSpec(a.shape, lambda i, _nd=nd: (0,) * _nd)

    def rows(width):
        return pl.BlockSpec((tm, width), lambda i: (i, 0))

    consts = [lw["g_mix"], lw["w_in"], lw["g_cq"], lw["w_uq"], lw["g_ckv"], lw["w_ukv"],
              lw["g_q"], lw["g_q_swap"], lw["g_k"], lw["g_k_swap"]]
    consts2 = [lw["w_pool"], lw["pool_scale"], lw["conv_w"], lw["conv_b"], lw["g_sq"], lw["g_sk"]]
    out_widths = [512, 512, 256, 256, 128, 128, 256, 256]
    return pl.pallas_call(
        functools.partial(_proj_kernel, tiles_per_seq=tiles_per_seq),
        grid=(n // tm,),
        in_specs=[rows(D_MODEL)] + [full(a) for a in consts] + [rows(HEAD_LANES)] * 2
                 + [full(a) for a in consts2],
        out_specs=[rows(w) for w in out_widths],
        out_shape=[jax.ShapeDtypeStruct((n, w), _BF16) for w in out_widths],
        scratch_shapes=[pltpu.VMEM((POOL_HALO, 1024), _F32)],
        compiler_params=_params(1),
        name="proj_in",
    )(x, *consts, *tabs, *consts2)


def _mla_attn_kernel(qi_ref, ki_ref, q_ref, k_ref, v_ref, o_ref, m_ref, l_ref, acc_ref):
    tq = tk = ATTN_TILE
    p_id = pl.program_id(1)
    qi = qi_ref[p_id]
    ki = ki_ref[p_id]

    @pl.when(ki == 0)
    def _():
        m_ref[...] = jnp.full_like(m_ref, NEG_BIG)
        l_ref[...] = jnp.zeros_like(l_ref)
        acc_ref[...] = jnp.zeros_like(acc_ref)

    lane_lo = lax.broadcasted_iota(jnp.int32, (1, HEAD_LANES), 1) < 64

    def accumulate(on_diagonal):
        if on_diagonal:
            visible = (lax.broadcasted_iota(jnp.int32, (tq, tk), 1)
                       <= lax.broadcasted_iota(jnp.int32, (tq, tk), 0))
        for pair in range(MLA_HEADS // 2):
            vblk = v_ref[:, pair * 128:(pair + 1) * 128]
            alphas = []
            pvs = []
            for sub in range(2):
                hd = 2 * pair + sub
                sl = slice(hd * HEAD_LANES, (hd + 1) * HEAD_LANES)
                s = _dot_nt(q_ref[:, sl], k_ref[:, sl])
                if on_diagonal:
                    s = jnp.where(visible, s, NEG_BIG)
                m_prev = m_ref[hd]
                m_new = jnp.maximum(m_prev, jnp.max(s, axis=-1, keepdims=True))
                alpha = jnp.exp(m_prev - m_new)
                p = jnp.exp(s - m_new[:, 0:1])
                l_ref[hd] = alpha * l_ref[hd] + jnp.sum(p, axis=-1, keepdims=True)
                m_ref[hd] = m_new
                alphas.append(alpha)
                pvs.append(_dot(p.astype(_BF16), vblk))
            psl = slice(pair * 128, (pair + 1) * 128)
            acc_ref[:, psl] = (acc_ref[:, psl] * jnp.where(lane_lo, alphas[0], alphas[1])
                               + jnp.where(lane_lo, pvs[0], pvs[1]))

    @pl.when(ki < qi)
    def _():
        accumulate(False)

    @pl.when(ki == qi)
    def _():
        accumulate(True)
        for pair in range(MLA_HEADS // 2):
            psl = slice(pair * 128, (pair + 1) * 128)
            denom = jnp.where(lane_lo, l_ref[2 * pair], l_ref[2 * pair + 1])
            o_ref[:, psl] = (acc_ref[:, psl] / denom).astype(o_ref.dtype)


def _mla_attn_call(qm, km, vm, batch, seq_len):
    tq = ATTN_TILE
    nq = seq_len // tq
    pairs = [(q, k) for q in range(nq) for k in range(q + 1)]
    qi = jnp.asarray([p[0] for p in pairs], jnp.int32)
    ki = jnp.asarray([p[1] for p in pairs], jnp.int32)
    grid_spec = pltpu.PrefetchScalarGridSpec(
        num_scalar_prefetch=2,
        grid=(batch, len(pairs)),
        in_specs=[
            pl.BlockSpec((tq, 512), lambda b, p, qi, ki: (b * nq + qi[p], 0)),
            pl.BlockSpec((tq, 512), lambda b, p, qi, ki: (b * nq + ki[p], 0)),
            pl.BlockSpec((tq, 256), lambda b, p, qi, ki: (b * nq + ki[p], 0)),
        ],
        out_specs=pl.BlockSpec((tq, 256), lambda b, p, qi, ki: (b * nq + qi[p], 0)),
        scratch_shapes=[pltpu.VMEM((MLA_HEADS, tq, HEAD_LANES), _F32),
                        pltpu.VMEM((MLA_HEADS, tq, HEAD_LANES), _F32),
                        pltpu.VMEM((tq, 256), _F32)],
    )
    return pl.pallas_call(
        _mla_attn_kernel,
        grid_spec=grid_spec,
        out_shape=jax.ShapeDtypeStruct((batch * seq_len, 256), _BF16),
        compiler_params=_params(2),
        name="mla_attention",
    )(qi, ki, qm, km, vm)


def _swa_kernel(sink_ref, q_ref, kc_ref, kp_ref, vc_ref, vp_ref, o_ref, *, tiles_per_seq):
    w = SWA_WINDOW
    i = pl.program_id(0)
    oldest = jnp.where((i % tiles_per_seq) == 0, 0, -w)
    lane_lo = lax.broadcasted_iota(jnp.int32, (1, 128), 1) < 64
    qpos = lax.broadcasted_iota(jnp.int32, (w, 2 * w), 0)
    kpos = lax.broadcasted_iota(jnp.int32, (w, 2 * w), 1) - w
    band = jnp.logical_and(kpos <= qpos, kpos > qpos - w)

    for jb in range(TOKEN_TILE // w):
        rs = slice(jb * w, (jb + 1) * w)
        if jb == 0:
            kprev, vprev = kp_ref[...], vp_ref[...]
            visible = jnp.logical_and(band, kpos >= oldest)
        else:
            ps = slice((jb - 1) * w, jb * w)
            kprev, vprev = kc_ref[ps, :], vc_ref[ps, :]
            visible = band
        kk = jnp.concatenate([kprev, kc_ref[rs, :]], axis=0)
        vv = jnp.concatenate([vprev, vc_ref[rs, :]], axis=0)
        zero = jnp.zeros_like(kk)
        k_half = (jnp.where(lane_lo, kk, zero), jnp.where(lane_lo, zero, kk))
        for blk in range(2):
            qblk = q_ref[rs, blk * 128:(blk + 1) * 128]
            outs = []
            for half in range(2):
                sink = sink_ref[2 * blk + half]
                s = _dot_nt(qblk, k_half[half])
                s = jnp.where(visible, s, NEG_BIG)
                m = jnp.maximum(jnp.max(s, axis=-1, keepdims=True), sink)
                p = jnp.exp(s - m)
                denom = jnp.sum(p, axis=-1, keepdims=True) + jnp.exp(sink - m)
                outs.append(_dot(p.astype(_BF16), vv) / denom)
            o_ref[rs, blk * 128:(blk + 1) * 128] = jnp.where(lane_lo, outs[0], outs[1]).astype(o_ref.dtype)


def _swa_call(qs, ks, vs, sinks, seq_len):
    n = qs.shape[0]
    tm = TOKEN_TILE
    per = tm // SWA_WINDOW
    prev = lambda i, s: (jnp.maximum(i * per - 1, 0), 0)
    grid_spec = pltpu.PrefetchScalarGridSpec(
        num_scalar_prefetch=1,
        grid=(n // tm,),
        in_specs=[
            pl.BlockSpec((tm, 256), lambda i, s: (i, 0)),
            pl.BlockSpec((tm, 128), lambda i, s: (i, 0)),
            pl.BlockSpec((SWA_WINDOW, 128), prev),
            pl.BlockSpec((tm, 128), lambda i, s: (i, 0)),
            pl.BlockSpec((SWA_WINDOW, 128), prev),
        ],
        out_specs=pl.BlockSpec((tm, 256), lambda i, s: (i, 0)),
    )
    return pl.pallas_call(
        functools.partial(_swa_kernel, tiles_per_seq=seq_len // tm),
        grid_spec=grid_spec,
        out_shape=jax.ShapeDtypeStruct((n, 256), _BF16),
        compiler_params=_params(1),
        name="swa_attention",
    )(sinks, qs, ks, ks, vs, vs)


def _memkv_kernel(mem_ref, gmem_ref, wkv_ref, gk_ref, k_ref, v_ref):
    m = _rms_scale(mem_ref[...], D_MODEL) * gmem_ref[...]
    kv = _dot(m.astype(_BF16), wkv_ref[...])
    lane_lo = lax.broadcasted_iota(jnp.int32, (1, 128), 1) < 64
    lane = lax.broadcasted_iota(jnp.int32, (1, 256), 1)
    k = jnp.concatenate([_pair_norm64(kv[:, 0:128], lane_lo), _pair_norm64(kv[:, 128:256], lane_lo)],
                        axis=1) * gk_ref[...]
    v = kv[:, 256:512]
    for hd in range(XA_HEADS):
        own = jnp.logical_and(lane >= hd * XA_HEAD_DIM, lane < (hd + 1) * XA_HEAD_DIM)
        k_ref[hd] = jnp.where(own, k, 0.0).astype(_BF16)
        v_ref[hd] = jnp.where(own, v, 0.0).astype(_BF16)


def _memkv_call(mem2d, g_mem, w_kv, g_k4, depth, batch):
    out = jax.ShapeDtypeStruct((depth, batch, XA_HEADS, MEM_LEN, 256), _BF16)
    ospec = pl.BlockSpec((None, None, XA_HEADS, MEM_LEN, 256), lambda l, b: (l, b, 0, 0, 0))
    return pl.pallas_call(
        _memkv_kernel,
        grid=(depth, batch),
        in_specs=[pl.BlockSpec((MEM_LEN, D_MODEL), lambda l, b: (b, 0)),
                  pl.BlockSpec((None, 1, D_MODEL), lambda l, b: (l, 0, 0)),
                  pl.BlockSpec((None, D_MODEL, 512), lambda l, b: (l, 0, 0)),
                  pl.BlockSpec((None, 1, 256), lambda l, b: (l, 0, 0))],
        out_specs=[ospec, ospec],
        out_shape=[out, out],
        compiler_params=_params(2),
        name="memory_kv",
    )(mem2d, g_mem, w_kv, g_k4)


def _mix_kernel(ya_ref, yb_ref, yc_ref, yd_ref, x_ref, gmo_ref, wmo_ref, gxa_ref, wq_ref, gxq_ref,
                kx_ref, vx_ref, wo_ref, gffn_ref, wrh_ref, wrl_ref, br_ref, ltri_ref, utri_ref,
                x2_ref, h3_ref, meta_ref, cnt_ref):
    tm = TOKEN_TILE
    gmo = gmo_ref[...]
    parts = []
    for g, ref in enumerate((ya_ref, yb_ref, yc_ref, yd_ref)):
        yg = ref[...].astype(_F32)
        parts.append((_rms_scale(yg, GROUP_W) * gmo[:, g * 256:(g + 1) * 256]).astype(_BF16))
    y = jnp.concatenate(parts, axis=1)
    x1 = x_ref[...] + _dot(y, wmo_ref[...])

    h = (_rms_scale(x1, D_MODEL) * gxa_ref[...]).astype(_BF16)
    q = _dot(h, wq_ref[...])
    lane_lo = lax.broadcasted_iota(jnp.int32, (1, 128), 1) < 64
    qn = jnp.concatenate([_pair_norm64(q[:, 0:128], lane_lo), _pair_norm64(q[:, 128:256], lane_lo)], axis=1)
    qn = (qn * gxq_ref[...] * (XA_HEAD_DIM ** -0.5)).astype(_BF16)
    o = jnp.zeros((tm, 256), _F32)
    for hd in range(XA_HEADS):
        s = _dot_nt(qn, kx_ref[hd])
        m = jnp.max(s, axis=-1, keepdims=True)
        p = jnp.exp(s - m)
        denom = jnp.sum(p, axis=-1, keepdims=True)
        o = o + _dot(p.astype(_BF16), vx_ref[hd]) / denom
    x2 = x1 + _dot(o.astype(_BF16), wo_ref[...])
    x2_ref[...] = x2

    h3 = _rms_scale(x2, D_MODEL) * gffn_ref[...]
    h3_hi = h3.astype(_BF16)
    h3_lo = (h3 - h3_hi.astype(_F32)).astype(_BF16)
    h3_ref[...] = h3_hi
    logits = (_dot(h3_hi, wrh_ref[...]) + (_dot(h3_hi, wrl_ref[...]) + _dot(h3_lo, wrh_ref[...]))
              + br_ref[...])
    lane = lax.broadcasted_iota(jnp.int32, (tm, 128), 1)
    far = jnp.int32(1 << 20)
    is_group = jnp.logical_and(lane >= N_EXPERTS, lane < N_EXPERTS + N_EXPERT_GROUPS)
    gl = jnp.where(is_group, logits, -jnp.inf)
    gmax = jnp.max(gl, axis=-1, keepdims=True)
    gidx = jnp.min(jnp.where(gl == gmax, lane, far), axis=-1, keepdims=True) - N_EXPERTS
    g_w = 1.0 / jnp.sum(jnp.where(is_group, jnp.exp(logits - gmax), 0.0), axis=-1, keepdims=True)
    in_group = jnp.logical_and(lane < N_EXPERTS, jnp.right_shift(lane, 3) == gidx)
    el = jnp.where(in_group, logits, -jnp.inf)
    emax = jnp.max(el, axis=-1, keepdims=True)
    ep = jnp.where(in_group, jnp.exp(logits - emax), 0.0)
    prob = jnp.where(in_group, ep / jnp.sum(ep, axis=-1, keepdims=True), -1.0)
    p1 = jnp.max(prob, axis=-1, keepdims=True)
    e0 = jnp.min(jnp.where(prob == p1, lane, far), axis=-1, keepdims=True)
    prob2 = jnp.where(lane == e0, -1.0, prob)
    p2 = jnp.max(prob2, axis=-1, keepdims=True)
    e1 = jnp.min(jnp.where(prob2 == p2, lane, far), axis=-1, keepdims=True)
    w0 = g_w * (p1 / (p1 + p2))
    w1 = g_w * (p2 / (p1 + p2))

    onehot = jnp.where(jnp.logical_or(lane == e0, lane == e1), 1.0, 0.0)
    prefix = _dot(ltri_ref[...], onehot.astype(_BF16))
    counts = jnp.sum(onehot, axis=0, keepdims=True)
    units = jnp.floor((counts + (RUN_ALIGN - 1)) * (1.0 / RUN_ALIGN))
    offs = _dot(jnp.broadcast_to(units, (8, 128)).astype(_BF16), utri_ref[...])[0:1, :] * RUN_ALIGN
    where_to = prefix + offs
    pos0 = jnp.sum(jnp.where(lane == e0, where_to, 0.0), axis=-1, keepdims=True)
    pos1 = jnp.sum(jnp.where(lane == e1, where_to, 0.0), axis=-1, keepdims=True)

    meta = jnp.where(lane == 0, e0.astype(_F32),
           jnp.where(lane == 1, e1.astype(_F32),
           jnp.where(lane == 2, w0,
           jnp.where(lane == 3, w1,
           jnp.where(lane == 4, pos0,
           jnp.where(lane == 5, pos1, 0.0))))))
    meta_ref[...] = meta
    cnt_ref[...] = jnp.broadcast_to(counts, (8, 128))


def _mix_call(ya, yb, yc, yd, x, lw, kx, vx, seq_len):
    n = x.shape[0]
    tm = TOKEN_TILE
    tiles_per_seq = seq_len // tm

    def full(a):
        nd = a.ndim
        return pl.BlockSpec(a.shape, lambda i, _nd=nd: (0,) * _nd)

    def rows(width):
        return pl.BlockSpec((tm, width), lambda i: (i, 0))

    kvspec = pl.BlockSpec((None, XA_HEADS, MEM_LEN, 256), lambda i: (i // tiles_per_seq, 0, 0, 0))
    consts_a = [lw["g_mo"], lw["w_mo"], lw["g_xa"], lw["xa_w_q"], lw["g_xq"]]
    consts_b = [lw["xa_w_o"], lw["g_ffn"], lw["w_router_hi"], lw["w_router_lo"], lw["b_router"],
                lw["ltri"], lw["utri"]]
    ntiles = n // tm
    return pl.pallas_call(
        _mix_kernel,
        grid=(ntiles,),
        in_specs=[rows(256)] * 4 + [rows(D_MODEL)] + [full(a) for a in consts_a]
                 + [kvspec, kvspec] + [full(a) for a in consts_b],
        out_specs=[rows(D_MODEL), rows(D_MODEL), rows(128), pl.BlockSpec((8, 128), lambda i: (i, 0))],
        out_shape=[jax.ShapeDtypeStruct((n, D_MODEL), _F32), jax.ShapeDtypeStruct((n, D_MODEL), _BF16),
                   jax.ShapeDtypeStruct((n, 128), _F32), jax.ShapeDtypeStruct((ntiles * 8, 128), _F32)],
        compiler_params=_params(1),
        name="mix_xattn_router",
    )(ya, yb, yc, yd, x, *consts_a, kx, vx, *consts_b)


def _run_copies(tab_ref, t, local_ref, remote_ref, sems, to_remote):
    for e in range(N_EXPERTS):
        base = (t * N_EXPERTS + e) * 3
        units = tab_ref[base]
        loc = tab_ref[base + 1]
        rem = tab_ref[base + 2]
        for bit in range(RUN_BITS):
            rows = RUN_ALIGN << bit
            done = (units & ((1 << bit) - 1)) * RUN_ALIGN
            lsl = local_ref.at[pl.ds(pl.multiple_of(loc + done, RUN_ALIGN), rows)]
            rsl = remote_ref.at[pl.ds(pl.multiple_of(rem + done, RUN_ALIGN), rows)]
            src, dst = (lsl, rsl) if to_remote else (rsl, lsl)
            yield (units & (1 << bit)) != 0, pltpu.make_async_copy(src, dst, sems.at[e, bit])


def _selection(meta, rows):
    r = lax.broadcasted_iota(jnp.int32, (TOKEN_TILE, rows), 1)
    pos0 = meta[:, 4:5].astype(jnp.int32)
    pos1 = meta[:, 5:6].astype(jnp.int32)
    return jnp.where(r == pos0, 1.0, 0.0).astype(_BF16), jnp.where(r == pos1, 1.0, 0.0).astype(_BF16)


def _dispatch_kernel(tab_ref, meta_ref, h3_ref, xs_in_ref, xs_ref, buf_ref, sems):
    del xs_in_ref
    t = pl.program_id(0)
    meta = meta_ref[...]
    sel0, sel1 = _selection(meta, SORT_ROWS)
    both = sel0 + sel1
    tn = (((0,), (0,)), ((), ()))
    buf_ref[:, 0:D_MODEL] = lax.dot_general(both, h3_ref[...], tn, preferred_element_type=_F32)
    lane = lax.broadcasted_iota(jnp.int32, (TOKEN_TILE, 128), 1)
    cols = []
    for k in range(2):
        w = meta[:, 2 + k:3 + k]
        hi = w.astype(_BF16).astype(_F32)
        mid = (w - hi).astype(_BF16).astype(_F32)
        lo = (w - hi) - mid
        cols.append(jnp.where(lane == 0, hi, jnp.where(lane == 1, mid, jnp.where(lane == 2, lo, 0.0)))
                    .astype(_BF16))
    buf_ref[:, D_MODEL:XS_COLS] = (
        lax.dot_general(sel0, cols[0], tn, preferred_element_type=_F32)
        + lax.dot_general(sel1, cols[1], tn, preferred_element_type=_F32))

    copies = list(_run_copies(tab_ref, t, buf_ref, xs_ref, sems, True))
    for cond, cp in copies:
        @pl.when(cond)
        def _(cp=cp):
            cp.start()
    for cond, cp in copies:
        @pl.when(cond)
        def _(cp=cp):
            cp.wait()


def _dispatch_call(tab, meta, h3, xs_zero):
    n = h3.shape[0]
    tm = TOKEN_TILE
    grid_spec = pltpu.PrefetchScalarGridSpec(
        num_scalar_prefetch=1,
        grid=(n // tm,),
        in_specs=[pl.BlockSpec((tm, 128), lambda i, tab: (i, 0)),
                  pl.BlockSpec((tm, D_MODEL), lambda i, tab: (i, 0)),
                  pl.BlockSpec(memory_space=pl.ANY)],
        out_specs=pl.BlockSpec(memory_space=pl.ANY),
        scratch_shapes=[pltpu.VMEM((SORT_ROWS, XS_COLS), _F32),
                        pltpu.SemaphoreType.DMA((N_EXPERTS, RUN_BITS))],
    )
    return pl.pallas_call(
        _dispatch_kernel,
        grid_spec=grid_spec,
        out_shape=jax.ShapeDtypeStruct(xs_zero.shape, xs_zero.dtype),
        input_output_aliases={3: 0},
        compiler_params=_params(1),
        name="moe_dispatch",
    )(tab, meta, h3, xs_zero)


def _ffn_kernel(be_ref, nv_ref, xs_ref, wg_ref, wu_ref, wd_ref, ys_ref):
    i = pl.program_id(0)

    @pl.when(i < nv_ref[0])
    def _():
        blk = xs_ref[...]
        xb = blk[:, 0:D_MODEL].astype(_BF16)
        wt = blk[:, D_MODEL:D_MODEL + 1] + blk[:, D_MODEL + 1:D_MODEL + 2] + blk[:, D_MODEL + 2:D_MODEL + 3]
        g = _dot(xb, wg_ref[...].astype(_BF16))
        u = _dot(xb, wu_ref[...].astype(_BF16))
        a = (g * jax.nn.sigmoid(g)) * u
        y = _dot(a.astype(_BF16), wd_ref[...].astype(_BF16))
        ys_ref[...] = y * wt

    @pl.when(i >= nv_ref[0])
    def _():
        ys_ref[...] = jnp.zeros_like(ys_ref)


def _ffn_call(blk_expert, n_valid, xs, w_gate, w_up, w_down, layer):
    n_blocks = xs.shape[0] // MOE_BLOCK
    wspec = lambda shape: pl.BlockSpec((None, None) + shape, lambda i, be, nv: (layer, be[i], 0, 0))
    last_live = lambda i, be, nv: (jnp.minimum(i, nv[0] - 1), 0)
    grid_spec = pltpu.PrefetchScalarGridSpec(
        num_scalar_prefetch=2,
        grid=(n_blocks,),
        in_specs=[pl.BlockSpec((MOE_BLOCK, XS_COLS), last_live),
                  wspec((D_MODEL, D_EXPERT)), wspec((D_MODEL, D_EXPERT)), wspec((D_EXPERT, D_MODEL))],
        out_specs=pl.BlockSpec((MOE_BLOCK, D_MODEL), lambda i, be, nv: (i, 0)),
    )
    return pl.pallas_call(
        _ffn_kernel,
        grid_spec=grid_spec,
        out_shape=jax.ShapeDtypeStruct((xs.shape[0], D_MODEL), _F32),
        compiler_params=_params(1),
        name="moe_experts",
    )(blk_expert, n_valid, xs, w_gate, w_up, w_down)


def _combine_kernel(tab_ref, meta_ref, x2_ref, ys_ref, out_ref, buf_ref, sems):
    t = pl.program_id(0)

    @pl.when(t == 0)
    def _():
        buf_ref[...] = jnp.zeros_like(buf_ref)

    copies = list(_run_copies(tab_ref, t, buf_ref, ys_ref, sems, False))
    for cond, cp in copies:
        @pl.when(cond)
        def _(cp=cp):
            cp.start()
    for cond, cp in copies:
        @pl.when(cond)
        def _(cp=cp):
            cp.wait()

    sel0, sel1 = _selection(meta_ref[...], SORT_ROWS)
    out_ref[...] = x2_ref[...] + _dot(sel0 + sel1, buf_ref[...].astype(_BF16))


def _combine_call(tab, meta, x2, ys):
    n = x2.shape[0]
    tm = TOKEN_TILE
    grid_spec = pltpu.PrefetchScalarGridSpec(
        num_scalar_prefetch=1,
        grid=(n // tm,),
        in_specs=[pl.BlockSpec((tm, 128), lambda i, tab: (i, 0)),
                  pl.BlockSpec((tm, D_MODEL), lambda i, tab: (i, 0)),
                  pl.BlockSpec(memory_space=pl.ANY)],
        out_specs=pl.BlockSpec((tm, D_MODEL), lambda i, tab: (i, 0)),
        scratch_shapes=[pltpu.VMEM((SORT_ROWS, D_MODEL), _F32),
                        pltpu.SemaphoreType.DMA((N_EXPERTS, RUN_BITS))],
    )
    return pl.pallas_call(
        _combine_kernel,
        grid_spec=grid_spec,
        out_shape=jax.ShapeDtypeStruct((n, D_MODEL), _F32),
        compiler_params=_params(1),
        name="moe_combine",
    )(tab, meta, x2, ys)


def _moe_tables(cnt, n_tokens):
    ntiles = n_tokens // TOKEN_TILE
    counts = cnt.reshape(ntiles, 8, 128)[:, 0, :N_EXPERTS].astype(jnp.int32)
    units = (counts + RUN_ALIGN - 1) // RUN_ALIGN
    padded = units * RUN_ALIGN
    local = jnp.cumsum(padded, axis=1) - padded
    total = jnp.sum(padded, axis=0)
    total_blk = ((total + MOE_BLOCK - 1) // MOE_BLOCK) * MOE_BLOCK
    ends = jnp.cumsum(total_blk)
    starts = ends - total_blk
    remote = starts[None, :] + jnp.cumsum(padded, axis=0) - padded
    tab = jnp.stack([units, local, remote], axis=-1).reshape(-1).astype(jnp.int32)
    n_blocks = _xs_rows(n_tokens) // MOE_BLOCK
    blk_start = jnp.arange(n_blocks, dtype=jnp.int32) * MOE_BLOCK
    blk_expert = jnp.minimum(jnp.sum((ends[None, :] <= blk_start[:, None]).astype(jnp.int32), axis=1),
                             N_EXPERTS - 1)
    n_valid = (ends[-1] // MOE_BLOCK).astype(jnp.int32).reshape(1)
    return tab, blk_expert, n_valid


def _xs_rows(n_tokens):
    ntiles = n_tokens // TOKEN_TILE
    worst = 2 * n_tokens + ntiles * N_EXPERTS * (RUN_ALIGN - 1) + N_EXPERTS * (MOE_BLOCK - 1)
    return ((worst + MOE_BLOCK - 1) // MOE_BLOCK) * MOE_BLOCK


def _pad_last(a, width):
    return jnp.pad(a, [(0, 0)] * (a.ndim - 1) + [(0, width - a.shape[-1])])


def _swap_mid_heads(a, axis):
    shape = a.shape
    a = a.reshape(shape[:axis] + (4, 64) + shape[axis + 1:])
    a = jnp.take(a, jnp.asarray([0, 2, 1, 3]), axis=axis)
    return a.reshape(shape)


def _layer_weights(l, p):
    w_in = p["w_in"][l]
    a, b = w_in[:, :A_COLS], w_in[:, A_COLS:A_COLS + B_COLS]
    c = w_in[:, A_COLS + B_COLS:A_COLS + B_COLS + C_COLS]
    d = w_in[:, A_COLS + B_COLS + C_COLS:]
    zeros = lambda w: jnp.zeros((D_MODEL, w), _F32)
    half = MLA_ROPE // 2
    k_rope = a[:, 384:416]
    k_rope_swap = jnp.concatenate([k_rope[:, half:], k_rope[:, :half]], axis=1)
    w_in_p = jnp.concatenate([
        a[:, :384], zeros(64), k_rope, zeros(32), zeros(64), k_rope_swap, zeros(32), b,
        _swap_mid_heads(c[:, :256], 1), c[:, 256:384], c[:, 384:512], d], axis=1).astype(_BF16)
    assert w_in_p.shape[1] == IN_COLS_PADDED

    w_uq = p["mla_w_uq"][l].reshape(MLA_Q_RANK, MLA_HEADS, MLA_QK_DIM)
    w_uq_swap = jnp.concatenate([jnp.zeros_like(w_uq[..., :MLA_NOPE]), w_uq[..., MLA_NOPE + half:],
                                 w_uq[..., MLA_NOPE:MLA_NOPE + half]], axis=-1)
    w_uq = jnp.concatenate([_pad_last(w_uq, HEAD_LANES).reshape(MLA_Q_RANK, -1),
                            _pad_last(w_uq_swap, HEAD_LANES).reshape(MLA_Q_RANK, -1)], axis=1).astype(_BF16)

    def swap_gain(g):
        return _pad_last(jnp.concatenate([jnp.zeros((MLA_NOPE,), _F32), g[MLA_NOPE + half:],
                                          g[MLA_NOPE:MLA_NOPE + half]]), HEAD_LANES)[None]
    w_ukv = p["mla_w_ukv"][l].reshape(MLA_KV_RANK, MLA_HEADS, MLA_NOPE + MLA_V)
    k_nope = _pad_last(w_ukv[..., :MLA_NOPE], HEAD_LANES).reshape(MLA_KV_RANK, MLA_HEADS * HEAD_LANES)
    v_part = w_ukv[..., MLA_NOPE:].reshape(MLA_KV_RANK, MLA_HEADS * MLA_V)
    w_ukv_p = jnp.concatenate([k_nope, v_part], axis=1).astype(_BF16)

    eye = jnp.eye(len(POOL_WINDOWS), dtype=_F32)
    w_pool = jnp.einsum("gcd,gh->gchd", p["pool_w"][l], eye).reshape(256, 256).astype(_BF16)

    g_mo = p["mix_out_norm"][l]
    g_mo = jnp.concatenate([g_mo[:512], _swap_mid_heads(g_mo[512:768], 0), g_mo[768:]])
    w_mo = p["w_mix_out"][l]
    w_mo = jnp.concatenate([w_mo[:512], _swap_mid_heads(w_mo[512:768], 0), w_mo[768:]], axis=0)

    w_router = _pad_last(jnp.concatenate([p["w_expert"][l], p["w_group"][l]], axis=1), 128)
    w_router_hi = w_router.astype(_BF16)
    w_router_lo = (w_router - w_router_hi.astype(_F32)).astype(_BF16)
    b_router = jnp.concatenate([p["b_expert"][l], p["b_group"][l]])
    tm = TOKEN_TILE
    ltri = (jnp.arange(tm)[None, :] < jnp.arange(tm)[:, None]).astype(_BF16)
    utri = (jnp.arange(128)[:, None] < jnp.arange(128)[None, :]).astype(_BF16)
    tile2 = lambda g: jnp.concatenate([g, g])[None]
    return dict(
        g_mix=p["norm_mix"][l][None], w_in=w_in_p,
        g_cq=p["mla_g_cq"][l][None], w_uq=w_uq, g_ckv=p["mla_g_ckv"][l][None], w_ukv=w_ukv_p,
        g_q=_pad_last(p["mla_g_q"][l], HEAD_LANES)[None], g_k=_pad_last(p["mla_g_k"][l], HEAD_LANES)[None],
        g_q_swap=swap_gain(p["mla_g_q"][l]), g_k_swap=swap_gain(p["mla_g_k"][l]),
        w_pool=w_pool, pool_scale=p["pool_scale"][l][None], conv_w=p["conv_w"][l],
        conv_b=p["conv_b"][l][None], g_sq=tile2(p["swa_g_q"][l]), g_sk=tile2(p["swa_g_k"][l]),
        sinks=jnp.take(p["swa_sinks"][l], jnp.asarray([0, 2, 1, 3])),
        g_mo=g_mo[None], w_mo=w_mo.astype(_BF16), g_xa=p["norm_xa"][l][None],
        xa_w_q=p["xa_w_q"][l].astype(_BF16), g_xq=jnp.tile(p["xa_g_q"][l], 4)[None],
        xa_w_o=p["xa_w_o"][l].astype(_BF16), g_ffn=p["norm_ffn"][l][None],
        w_router_hi=w_router_hi, w_router_lo=w_router_lo, b_router=_pad_last(b_router, 128)[None],
        ltri=ltri, utri=utri,
    )


def kernel(x, mem, positions, norm_mix, w_in, mla_g_cq, mla_w_uq, mla_g_ckv, mla_w_ukv, mla_g_q, mla_g_k, pool_w, pool_scale, swa_g_q, swa_g_k, swa_sinks, conv_w, conv_b, mix_out_norm, w_mix_out, norm_xa, norm_mem, xa_w_q, xa_w_kv, xa_g_q, xa_g_k, xa_w_o, norm_ffn, w_group, b_group, w_expert, b_expert, w_gate, w_up, w_down):
    p = dict(norm_mix=norm_mix, w_in=w_in, mla_g_cq=mla_g_cq, mla_w_uq=mla_w_uq, mla_g_ckv=mla_g_ckv,
             mla_w_ukv=mla_w_ukv, mla_g_q=mla_g_q, mla_g_k=mla_g_k, pool_w=pool_w, pool_scale=pool_scale,
             swa_g_q=swa_g_q, swa_g_k=swa_g_k, swa_sinks=swa_sinks, conv_w=conv_w, conv_b=conv_b,
             mix_out_norm=mix_out_norm, w_mix_out=w_mix_out, norm_xa=norm_xa, xa_w_q=xa_w_q,
             xa_g_q=xa_g_q, xa_w_o=xa_w_o, norm_ffn=norm_ffn, w_group=w_group, b_group=b_group,
             w_expert=w_expert, b_expert=b_expert)
    batch, seq_len, _ = x.shape
    depth = w_in.shape[0]
    n = batch * seq_len
    assert seq_len % TOKEN_TILE == 0 and seq_len % ATTN_TILE == 0 and n % 1024 == 0
    assert mem.shape[1] == MEM_LEN

    xf = x.reshape(n, D_MODEL)
    tabs = _rope_tables(positions.astype(_F32).reshape(n, 1), n)
    kx, vx = _memkv_call(mem.reshape(batch * MEM_LEN, D_MODEL), norm_mem[:, None, :],
                         xa_w_kv.astype(_BF16), jnp.tile(xa_g_k, (1, 4))[:, None, :], depth, batch)
    for l in range(depth):
        lw = _layer_weights(l, p)
        qm, km, vm, qs, ks, vs, yb, yd = _proj_call(xf, lw, tabs, seq_len)
        ya = _mla_attn_call(qm, km, vm, batch, seq_len)
        yc = _swa_call(qs, ks, vs, lw["sinks"], seq_len)
        x2, h3, meta, cnt = _mix_call(ya, yb, yc, yd, xf, lw, kx[l], vx[l], seq_len)
        tab, blk_expert, n_valid = _moe_tables(cnt, n)
        xs = _dispatch_call(tab, meta, h3, jnp.zeros((_xs_rows(n), XS_COLS), _F32))
        ys = _ffn_call(blk_expert, n_valid, xs, w_gate, w_up, w_down, l)
        xf = _combine_call(tab, meta, x2, ys)
    return xf.reshape(batch, seq_len, D_MODEL)
```

```python
import functools

import jax
import jax.numpy as jnp
import numpy as np
from jax import lax
from jax.experimental import pallas as pl
from jax.experimental.pallas import tpu as pltpu

EPS = 1e-6
NEG_BIG = -1e30
ROPE_THETA = 10000.0

D_MODEL = 1024
MEM_LEN = 256
GROUP_W = 256

MLA_HEADS = 4
MLA_Q_RANK = 256
MLA_KV_RANK = 128
MLA_NOPE = 64
MLA_ROPE = 32
MLA_QK_DIM = MLA_NOPE + MLA_ROPE
MLA_V = 64
HEAD_LANES = 128

POOL_WINDOWS = (2, 4, 8, 16)
POOL_GROUP = 64
POOL_HALO = 16

SWA_HEADS = 4
SWA_KV_HEADS = 2
SWA_HEAD_DIM = 64
SWA_WINDOW = 128

CONV_CH = 256

XA_HEADS = 4
XA_HEAD_DIM = 64

N_EXPERT_GROUPS = 4
EXPERTS_PER_GROUP = 8
N_EXPERTS = 32
D_EXPERT = 256
MOE_BLOCK = 256

A_COLS = MLA_Q_RANK + MLA_KV_RANK + MLA_ROPE
B_COLS = GROUP_W
C_COLS = (SWA_HEADS + 2 * SWA_KV_HEADS) * SWA_HEAD_DIM
COL_CQ, COL_CKV, COL_KROPE, COL_KROPE_SWAP = 0, 256, 384, 512
COL_POOL, COL_SWA_Q, COL_SWA_K, COL_SWA_V, COL_CONV = 640, 896, 1152, 1280, 1408
IN_COLS_PADDED = COL_CONV + 3 * CONV_CH

TOKEN_TILE = 512
ATTN_TILE = 512
RUN_ALIGN = 16
RUN_BITS = 6
SORT_ROWS = 2 * TOKEN_TILE + N_EXPERTS * RUN_ALIGN
PIECE_COLS = 1 + 2 * N_EXPERTS
XS_COLS = D_MODEL + 128
VMEM_LIMIT = 56 * 1024 * 1024

_F32 = jnp.float32
_BF16 = jnp.bfloat16


def _params(n_axes):
    return pltpu.CompilerParams(dimension_semantics=("arbitrary",) * n_axes,
                                vmem_limit_bytes=VMEM_LIMIT)


def _dot(a, b):
    return jnp.dot(a, b, preferred_element_type=_F32)


def _dot_nt(a, b):
    return lax.dot_general(a, b, (((1,), (1,)), ((), ())), preferred_element_type=_F32)


def _rms_factor(x, width):
    ss = jnp.sum(x * x, axis=-1, keepdims=True)
    return lax.rsqrt(ss * (1.0 / width) + EPS)


def _rms_scale(x, width):
    return x * _rms_factor(x, width)


def _pair_norm64(x, lane_lo):
    x2 = x * x
    s_all = jnp.sum(x2, axis=-1, keepdims=True)
    s_lo = jnp.sum(jnp.where(lane_lo, x2, 0.0), axis=-1, keepdims=True)
    ss = jnp.where(lane_lo, s_lo, s_all - s_lo)
    return x * lax.rsqrt(ss * (1.0 / 64.0) + EPS)


def _rope_kernel(pos_ref, freq_ref, sign_ref, c_ref, s_ref):
    ang = pos_ref[...] * freq_ref[...]
    c_ref[...] = jnp.cos(ang)
    s_ref[...] = jnp.sin(ang) * sign_ref[...]


def _rope_tables(posf, n_rows):
    half = MLA_ROPE // 2
    inv_freq = ROPE_THETA ** (-jnp.arange(half, dtype=_F32) / half)
    zeros = jnp.zeros((HEAD_LANES,), _F32)
    freq = zeros.at[MLA_NOPE:MLA_NOPE + half].set(inv_freq).at[MLA_NOPE + half:MLA_QK_DIM].set(inv_freq)
    sign = zeros.at[MLA_NOPE:MLA_NOPE + half].set(-1.0).at[MLA_NOPE + half:MLA_QK_DIM].set(1.0)
    tile = 1024
    row = pl.BlockSpec((1, HEAD_LANES), lambda i: (0, 0))
    tab = pl.BlockSpec((tile, HEAD_LANES), lambda i: (i, 0))
    shp = jax.ShapeDtypeStruct((n_rows, HEAD_LANES), _F32)
    return pl.pallas_call(
        _rope_kernel,
        grid=(n_rows // tile,),
        in_specs=[pl.BlockSpec((tile, 1), lambda i: (i, 0)), row, row],
        out_specs=[tab, tab],
        out_shape=[shp, shp],
        compiler_params=_params(1),
        name="rope_tables",
    )(posf, freq[None], sign[None])


def _proj_kernel(x_ref, gmix_ref, win_ref, gcq_ref, wuq_ref, gckv_ref, wukv_ref, gq_ref, gqs_ref,
                 gk_ref, gks_ref, c_ref, s_ref, wpool_ref, pscale_ref, convw_ref, convb_ref, gsq_ref,
                 gsk_ref, qm_ref, km_ref, vm_ref, qs_ref, ks_ref, vs_ref, yb_ref, yd_ref, halo_ref,
                 *, tiles_per_seq):
    tm = TOKEN_TILE
    i = pl.program_id(0)
    seq_tile = i % tiles_per_seq

    x = x_ref[...]
    h = _rms_scale(x, D_MODEL) * gmix_ref[...]
    u = _dot(h.astype(_BF16), win_ref[...])

    c = c_ref[...]
    s = s_ref[...]
    hw = MLA_HEADS * HEAD_LANES

    cq = _rms_scale(u[:, COL_CQ:COL_CQ + MLA_Q_RANK], MLA_Q_RANK) * gcq_ref[...]
    q = _dot(cq.astype(_BF16), wuq_ref[...])
    gq = gq_ref[...] * (MLA_QK_DIM ** -0.5)
    gqs = gqs_ref[...] * (MLA_QK_DIM ** -0.5)
    for hd in range(MLA_HEADS):
        sl = slice(hd * HEAD_LANES, (hd + 1) * HEAD_LANES)
        xq = q[:, sl]
        r = _rms_factor(xq, MLA_QK_DIM)
        qm_ref[:, sl] = ((xq * r) * (gq * c) + (q[:, hw + sl.start:hw + sl.stop] * r) * (gqs * s)).astype(_BF16)

    ckv = _rms_scale(u[:, COL_CKV:COL_CKV + MLA_KV_RANK], MLA_KV_RANK) * gckv_ref[...]
    kv = _dot(ckv.astype(_BF16), wukv_ref[...])
    krope = u[:, COL_KROPE:COL_KROPE + HEAD_LANES]
    krope_swap = u[:, COL_KROPE_SWAP:COL_KROPE_SWAP + HEAD_LANES]
    gk = gk_ref[...]
    gks = gks_ref[...]
    for hd in range(MLA_HEADS):
        sl = slice(hd * HEAD_LANES, (hd + 1) * HEAD_LANES)
        xk = kv[:, sl] + krope
        r = _rms_factor(xk, MLA_QK_DIM)
        km_ref[:, sl] = ((xk * r) * (gk * c) + (krope_swap * r) * (gks * s)).astype(_BF16)
    vm_ref[...] = kv[:, hw:hw + MLA_HEADS * MLA_V].astype(_BF16)

    lane_lo = lax.broadcasted_iota(jnp.int32, (1, HEAD_LANES), 1) < 64
    gsq = gsq_ref[...]
    for blk in range(2):
        xq = u[:, COL_SWA_Q + blk * 128:COL_SWA_Q + (blk + 1) * 128]
        qs_ref[:, blk * 128:(blk + 1) * 128] = (
            _pair_norm64(xq, lane_lo) * gsq * (SWA_HEAD_DIM ** -0.5)).astype(_BF16)
    ks_ref[...] = (_pair_norm64(u[:, COL_SWA_K:COL_SWA_K + 128], lane_lo) * gsk_ref[...]).astype(_BF16)
    vs_ref[...] = u[:, COL_SWA_V:COL_SWA_V + 128].astype(_BF16)

    @pl.when(seq_tile == 0)
    def _():
        halo_ref[...] = jnp.zeros_like(halo_ref)

    halo = halo_ref[...]
    up = u[:, COL_POOL:COL_POOL + GROUP_W]
    ud = u[:, COL_CONV:COL_CONV + 3 * CONV_CH]

    b = jnp.concatenate([halo[:, 0:256], up], axis=0)
    w2 = b + pltpu.roll(b, 1, 0)
    w4 = w2 + pltpu.roll(w2, 2, 0)
    w8 = w4 + pltpu.roll(w4, 4, 0)
    w16 = w8 + pltpu.roll(w8, 8, 0)
    lane = lax.broadcasted_iota(jnp.int32, (1, 256), 1)
    win = jnp.where(lane < 64, w2, jnp.where(lane < 128, w4, jnp.where(lane < 192, w8, w16)))
    win = win[POOL_HALO:, :]
    width = jnp.where(lane < 64, 2, jnp.where(lane < 128, 4, jnp.where(lane < 192, 8, 16)))
    t = seq_tile * tm + lax.broadcasted_iota(jnp.int32, (tm, 1), 0)
    count = jnp.minimum(t + 1, width).astype(_F32)
    pooled = win / count - up
    yb_ref[...] = (_dot(pooled.astype(_BF16), wpool_ref[...]) * pscale_ref[...]).astype(_BF16)

    z = ud[:, 256:512] * ud[:, 512:768]
    zh = halo[:, 512:768] * halo[:, 768:1024]
    zb = jnp.concatenate([zh, z], axis=0)
    cw = convw_ref[...]
    conv = (pltpu.roll(zb, 2, 0)[POOL_HALO:, :] * cw[0:1, :]
            + pltpu.roll(zb, 1, 0)[POOL_HALO:, :] * cw[1:2, :]
            + z * cw[2:3, :])
    yd_ref[...] = (ud[:, 0:256] * (conv + convb_ref[...])).astype(_BF16)

    halo_ref[:, 0:256] = up[tm - POOL_HALO:, :]
    halo_ref[:, 256:1024] = ud[tm - POOL_HALO:, :]


def _proj_call(x, lw, tabs, seq_len):
    n = x.shape[0]
    tm = TOKEN_TILE
    tiles_per_seq = seq_len // tm

    def full(a):
        nd = a.ndim
        return pl.BlockSpec(a.shape, lambda i, _nd=nd: (0,) * _nd)

    def rows(width):
        return pl.BlockSpec((tm, width), lambda i: (i, 0))

    consts = [lw["g_mix"], lw["w_in"], lw["g_cq"], lw["w_uq"], lw["g_ckv"], lw["w_ukv"],
              lw["g_q"], lw["g_q_swap"], lw["g_k"], lw["g_k_swap"]]
    consts2 = [lw["w_pool"], lw["pool_scale"], lw["conv_w"], lw["conv_b"], lw["g_sq"], lw["g_sk"]]
    out_widths = [512, 512, 256, 256, 128, 128, 256, 256]
    return pl.pallas_call(
        functools.partial(_proj_kernel, tiles_per_seq=tiles_per_seq),
        grid=(n // tm,),
        in_specs=[rows(D_MODEL)] + [full(a) for a in consts] + [rows(HEAD_LANES)] * 2
                 + [full(a) for a in consts2],
        out_specs=[rows(w) for w in out_widths],
        out_shape=[jax.ShapeDtypeStruct((n, w), _BF16) for w in out_widths],
        scratch_shapes=[pltpu.VMEM((POOL_HALO, 1024), _F32)],
        compiler_params=_params(1),
        name="proj_in",
    )(x, *consts, *tabs, *consts2)


def _mla_attn_kernel(qi_ref, ki_ref, q_ref, k_ref, v_ref, o_ref, m_ref, l_ref, acc_ref):
    tq = tk = ATTN_TILE
    p_id = pl.program_id(1)
    qi = qi_ref[p_id]
    ki = ki_ref[p_id]

    @pl.when(ki == 0)
    def _():
        m_ref[...] = jnp.full_like(m_ref, NEG_BIG)
        l_ref[...] = jnp.zeros_like(l_ref)
        acc_ref[...] = jnp.zeros_like(acc_ref)

    lane_lo = lax.broadcasted_iota(jnp.int32, (1, HEAD_LANES), 1) < 64

    def accumulate(on_diagonal):
        if on_diagonal:
            visible = (lax.broadcasted_iota(jnp.int32, (tq, tk), 1)
                       <= lax.broadcasted_iota(jnp.int32, (tq, tk), 0))
        for pair in range(MLA_HEADS // 2):
            vblk = v_ref[:, pair * 128:(pair + 1) * 128]
            alphas = []
            pvs = []
            for sub in range(2):
                hd = 2 * pair + sub
                sl = slice(hd * HEAD_LANES, (hd + 1) * HEAD_LANES)
                s = _dot_nt(q_ref[:, sl], k_ref[:, sl])
                if on_diagonal:
                    s = jnp.where(visible, s, NEG_BIG)
                m_prev = m_ref[hd]
                m_new = jnp.maximum(m_prev, jnp.max(s, axis=-1, keepdims=True))
                alpha = jnp.exp(m_prev - m_new)
                p = jnp.exp(s - m_new[:, 0:1])
                l_ref[hd] = alpha * l_ref[hd] + jnp.sum(p, axis=-1, keepdims=True)
                m_ref[hd] = m_new
                alphas.append(alpha)
                pvs.append(_dot(p.astype(_BF16), vblk))
            psl = slice(pair * 128, (pair + 1) * 128)
            acc_ref[:, psl] = (acc_ref[:, psl] * jnp.where(lane_lo, alphas[0], alphas[1])
                               + jnp.where(lane_lo, pvs[0], pvs[1]))

    @pl.when(ki < qi)
    def _():
        accumulate(False)

    @pl.when(ki == qi)
    def _():
        accumulate(True)
        for pair in range(MLA_HEADS // 2):
            psl = slice(pair * 128, (pair + 1) * 128)
            denom = jnp.where(lane_lo, l_ref[2 * pair], l_ref[2 * pair + 1])
            o_ref[:, psl] = (acc_ref[:, psl] / denom).astype(o_ref.dtype)


def _mla_attn_call(qm, km, vm, batch, seq_len):
    tq = ATTN_TILE
    nq = seq_len // tq
    pairs = [(q, k) for q in range(nq) for k in range(q + 1)]
    qi = jnp.asarray([p[0] for p in pairs], jnp.int32)
    ki = jnp.asarray([p[1] for p in pairs], jnp.int32)
    grid_spec = pltpu.PrefetchScalarGridSpec(
        num_scalar_prefetch=2,
        grid=(batch, len(pairs)),
        in_specs=[
            pl.BlockSpec((tq, 512), lambda b, p, qi, ki: (b * nq + qi[p], 0)),
            pl.BlockSpec((tq, 512), lambda b, p, qi, ki: (b * nq + ki[p], 0)),
            pl.BlockSpec((tq, 256), lambda b, p, qi, ki: (b * nq + ki[p], 0)),
        ],
        out_specs=pl.BlockSpec((tq, 256), lambda b, p, qi, ki: (b * nq + qi[p], 0)),
        scratch_shapes=[pltpu.VMEM((MLA_HEADS, tq, HEAD_LANES), _F32),
                        pltpu.VMEM((MLA_HEADS, tq, HEAD_LANES), _F32),
                        pltpu.VMEM((tq, 256), _F32)],
    )
    return pl.pallas_call(
        _mla_attn_kernel,
        grid_spec=grid_spec,
        out_shape=jax.ShapeDtypeStruct((batch * seq_len, 256), _BF16),
        compiler_params=_params(2),
        name="mla_attention",
    )(qi, ki, qm, km, vm)


def _swa_kernel(sink_ref, q_ref, kc_ref, kp_ref, vc_ref, vp_ref, o_ref, *, tiles_per_seq):
    w = SWA_WINDOW
    i = pl.program_id(0)
    oldest = jnp.where((i % tiles_per_seq) == 0, 0, -w)
    lane_lo = lax.broadcasted_iota(jnp.int32, (1, 128), 1) < 64
    qpos = lax.broadcasted_iota(jnp.int32, (w, 2 * w), 0)
    kpos = lax.broadcasted_iota(jnp.int32, (w, 2 * w), 1) - w
    band = jnp.logical_and(kpos <= qpos, kpos > qpos - w)

    for jb in range(TOKEN_TILE // w):
        rs = slice(jb * w, (jb + 1) * w)
        if jb == 0:
            kprev, vprev = kp_ref[...], vp_ref[...]
            visible = jnp.logical_and(band, kpos >= oldest)
        else:
            ps = slice((jb - 1) * w, jb * w)
            kprev, vprev = kc_ref[ps, :], vc_ref[ps, :]
            visible = band
        kk = jnp.concatenate([kprev, kc_ref[rs, :]], axis=0)
        vv = jnp.concatenate([vprev, vc_ref[rs, :]], axis=0)
        zero = jnp.zeros_like(kk)
        k_half = (jnp.where(lane_lo, kk, zero), jnp.where(lane_lo, zero, kk))
        for blk in range(2):
            qblk = q_ref[rs, blk * 128:(blk + 1) * 128]
            outs = []
            for half in range(2):
                sink = sink_ref[2 * blk + half]
                s = _dot_nt(qblk, k_half[half])
                s = jnp.where(visible, s, NEG_BIG)
                m = jnp.maximum(jnp.max(s, axis=-1, keepdims=True), sink)
                p = jnp.exp(s - m)
                denom = jnp.sum(p, axis=-1, keepdims=True) + jnp.exp(sink - m)
                outs.append(_dot(p.astype(_BF16), vv) / denom)
            o_ref[rs, blk * 128:(blk + 1) * 128] = jnp.where(lane_lo, outs[0], outs[1]).astype(o_ref.dtype)


def _swa_call(qs, ks, vs, sinks, seq_len):
    n = qs.shape[0]
    tm = TOKEN_TILE
    per = tm // SWA_WINDOW
    prev = lambda i, s: (jnp.maximum(i * per - 1, 0), 0)
    grid_spec = pltpu.PrefetchScalarGridSpec(
        num_scalar_prefetch=1,
        grid=(n // tm,),
        in_specs=[
            pl.BlockSpec((tm, 256), lambda i, s: (i, 0)),
            pl.BlockSpec((tm, 128), lambda i, s: (i, 0)),
            pl.BlockSpec((SWA_WINDOW, 128), prev),
            pl.BlockSpec((tm, 128), lambda i, s: (i, 0)),
            pl.BlockSpec((SWA_WINDOW, 128), prev),
        ],
        out_specs=pl.BlockSpec((tm, 256), lambda i, s: (i, 0)),
    )
    return pl.pallas_call(
        functools.partial(_swa_kernel, tiles_per_seq=seq_len // tm),
        grid_spec=grid_spec,
        out_shape=jax.ShapeDtypeStruct((n, 256), _BF16),
        compiler_params=_params(1),
        name="swa_attention",
    )(sinks, qs, ks, ks, vs, vs)


def _memkv_kernel(mem_ref, gmem_ref, wkv_ref, gk_ref, k_ref, v_ref):
    m = _rms_scale(mem_ref[...], D_MODEL) * gmem_ref[...]
    kv = _dot(m.astype(_BF16), wkv_ref[...])
    lane_lo = lax.broadcasted_iota(jnp.int32, (1, 128), 1) < 64
    lane = lax.broadcasted_iota(jnp.int32, (1, 256), 1)
    k = jnp.concatenate([_pair_norm64(kv[:, 0:128], lane_lo), _pair_norm64(kv[:, 128:256], lane_lo)],
                        axis=1) * gk_ref[...]
    v = kv[:, 256:512]
    for hd in range(XA_HEADS):
        own = jnp.logical_and(lane >= hd * XA_HEAD_DIM, lane < (hd + 1) * XA_HEAD_DIM)
        k_ref[hd] = jnp.where(own, k, 0.0).astype(_BF16)
        v_ref[hd] = jnp.where(own, v, 0.0).astype(_BF16)


def _memkv_call(mem2d, g_mem, w_kv, g_k4, depth, batch):
    out = jax.ShapeDtypeStruct((depth, batch, XA_HEADS, MEM_LEN, 256), _BF16)
    ospec = pl.BlockSpec((None, None, XA_HEADS, MEM_LEN, 256), lambda l, b: (l, b, 0, 0, 0))
    return pl.pallas_call(
        _memkv_kernel,
        grid=(depth, batch),
        in_specs=[pl.BlockSpec((MEM_LEN, D_MODEL), lambda l, b: (b, 0)),
                  pl.BlockSpec((None, 1, D_MODEL), lambda l, b: (l, 0, 0)),
                  pl.BlockSpec((None, D_MODEL, 512), lambda l, b: (l, 0, 0)),
                  pl.BlockSpec((None, 1, 256), lambda l, b: (l, 0, 0))],
        out_specs=[ospec, ospec],
        out_shape=[out, out],
        compiler_params=_params(2),
        name="memory_kv",
    )(mem2d, g_mem, w_kv, g_k4)


def _mix_kernel(ya_ref, yb_ref, yc_ref, yd_ref, x_ref, gmo_ref, wmo_ref, gxa_ref, wq_ref, gxq_ref,
                kx_ref, vx_ref, wo_ref, gffn_ref, wrh_ref, wrl_ref, br_ref, ltri_ref, utri_ref,
                x2_ref, h3_ref, meta_ref, cnt_ref):
    tm = TOKEN_TILE
    gmo = gmo_ref[...]
    parts = []
    for g, ref in enumerate((ya_ref, yb_ref, yc_ref, yd_ref)):
        yg = ref[...].astype(_F32)
        parts.append((_rms_scale(yg, GROUP_W) * gmo[:, g * 256:(g + 1) * 256]).astype(_BF16))
    y = jnp.concatenate(parts, axis=1)
    x1 = x_ref[...] + _dot(y, wmo_ref[...])

    h = (_rms_scale(x1, D_MODEL) * gxa_ref[...]).astype(_BF16)
    q = _dot(h, wq_ref[...])
    lane_lo = lax.broadcasted_iota(jnp.int32, (1, 128), 1) < 64
    qn = jnp.concatenate([_pair_norm64(q[:, 0:128], lane_lo), _pair_norm64(q[:, 128:256], lane_lo)], axis=1)
    qn = (qn * gxq_ref[...] * (XA_HEAD_DIM ** -0.5)).astype(_BF16)
    o = jnp.zeros((tm, 256), _F32)
    for hd in range(XA_HEADS):
        s = _dot_nt(qn, kx_ref[hd])
        m = jnp.max(s, axis=-1, keepdims=True)
        p = jnp.exp(s - m)
        denom = jnp.sum(p, axis=-1, keepdims=True)
        o = o + _dot(p.astype(_BF16), vx_ref[hd]) / denom
    x2 = x1 + _dot(o.astype(_BF16), wo_ref[...])
    x2_ref[...] = x2

    h3 = _rms_scale(x2, D_MODEL) * gffn_ref[...]
    h3_hi = h3.astype(_BF16)
    h3_lo = (h3 - h3_hi.astype(_F32)).astype(_BF16)
    h3_ref[...] = h3_hi
    logits = (_dot(h3_hi, wrh_ref[...]) + (_dot(h3_hi, wrl_ref[...]) + _dot(h3_lo, wrh_ref[...]))
              + br_ref[...])
    lane = lax.broadcasted_iota(jnp.int32, (tm, 128), 1)
    far = jnp.int32(1 << 20)
    is_group = jnp.logical_and(lane >= N_EXPERTS, lane < N_EXPERTS + N_EXPERT_GROUPS)
    gl = jnp.where(is_group, logits, -jnp.inf)
    gmax = jnp.max(gl, axis=-1, keepdims=True)
    gidx = jnp.min(jnp.where(gl == gmax, lane, far), axis=-1, keepdims=True) - N_EXPERTS
    g_w = 1.0 / jnp.sum(jnp.where(is_group, jnp.exp(logits - gmax), 0.0), axis=-1, keepdims=True)
    in_group = jnp.logical_and(lane < N_EXPERTS, jnp.right_shift(lane, 3) == gidx)
    el = jnp.where(in_group, logits, -jnp.inf)
    emax = jnp.max(el, axis=-1, keepdims=True)
    ep = jnp.where(in_group, jnp.exp(logits - emax), 0.0)
    prob = jnp.where(in_group, ep / jnp.sum(ep, axis=-1, keepdims=True), -1.0)
    p1 = jnp.max(prob, axis=-1, keepdims=True)
    e0 = jnp.min(jnp.where(prob == p1, lane, far), axis=-1, keepdims=True)
    prob2 = jnp.where(lane == e0, -1.0, prob)
    p2 = jnp.max(prob2, axis=-1, keepdims=True)
    e1 = jnp.min(jnp.where(prob2 == p2, lane, far), axis=-1, keepdims=True)
    w0 = g_w * (p1 / (p1 + p2))
    w1 = g_w * (p2 / (p1 + p2))

    onehot = jnp.where(jnp.logical_or(lane == e0, lane == e1), 1.0, 0.0)
    prefix = _dot(ltri_ref[...], onehot.astype(_BF16))
    counts = jnp.sum(onehot, axis=0, keepdims=True)
    units = jnp.floor((counts + (RUN_ALIGN - 1)) * (1.0 / RUN_ALIGN))
    offs = _dot(jnp.broadcast_to(units, (8, 128)).astype(_BF16), utri_ref[...])[0:1, :] * RUN_ALIGN
    where_to = prefix + offs
    pos0 = jnp.sum(jnp.where(lane == e0, where_to, 0.0), axis=-1, keepdims=True)
    pos1 = jnp.sum(jnp.where(lane == e1, where_to, 0.0), axis=-1, keepdims=True)

    meta = jnp.where(lane == 0, e0.astype(_F32),
           jnp.where(lane == 1, e1.astype(_F32),
           jnp.where(lane == 2, w0,
           jnp.where(lane == 3, w1,
           jnp.where(lane == 4, pos0,
           jnp.where(lane == 5, pos1, 0.0))))))
    meta_ref[...] = meta
    cnt_ref[...] = jnp.broadcast_to(counts, (8, 128))


def _mix_call(ya, yb, yc, yd, x, lw, kx, vx, seq_len):
    n = x.shape[0]
    tm = TOKEN_TILE
    tiles_per_seq = seq_len // tm

    def full(a):
        nd = a.ndim
        return pl.BlockSpec(a.shape, lambda i, _nd=nd: (0,) * _nd)

    def rows(width):
        return pl.BlockSpec((tm, width), lambda i: (i, 0))

    kvspec = pl.BlockSpec((None, XA_HEADS, MEM_LEN, 256), lambda i: (i // tiles_per_seq, 0, 0, 0))
    consts_a = [lw["g_mo"], lw["w_mo"], lw["g_xa"], lw["xa_w_q"], lw["g_xq"]]
    consts_b = [lw["xa_w_o"], lw["g_ffn"], lw["w_router_hi"], lw["w_router_lo"], lw["b_router"],
                lw["ltri"], lw["utri"]]
    ntiles = n // tm
    return pl.pallas_call(
        _mix_kernel,
        grid=(ntiles,),
        in_specs=[rows(256)] * 4 + [rows(D_MODEL)] + [full(a) for a in consts_a]
                 + [kvspec, kvspec] + [full(a) for a in consts_b],
        out_specs=[rows(D_MODEL), rows(D_MODEL), rows(128), pl.BlockSpec((8, 128), lambda i: (i, 0))],
        out_shape=[jax.ShapeDtypeStruct((n, D_MODEL), _F32), jax.ShapeDtypeStruct((n, D_MODEL), _BF16),
                   jax.ShapeDtypeStruct((n, 128), _F32), jax.ShapeDtypeStruct((ntiles * 8, 128), _F32)],
        compiler_params=_params(1),
        name="mix_xattn_router",
    )(ya, yb, yc, yd, x, *consts_a, kx, vx, *consts_b)


def _move_runs(tab_ref, t, live, local_ref, remote_ref, sems, *, to_remote, wait):
    for bit in range(RUN_BITS):
        rows = RUN_ALIGN << bit
        base = (t * RUN_BITS + bit) * PIECE_COLS
        count = jnp.where(live, tab_ref[base], 0)

        def one(p, carry, base=base, rows=rows, bit=bit):
            loc = pl.multiple_of(tab_ref[base + 1 + p], RUN_ALIGN)
            rem = pl.multiple_of(tab_ref[base + 1 + N_EXPERTS + p], RUN_ALIGN)
            lsl = local_ref.at[pl.ds(loc, rows)]
            rsl = remote_ref.at[pl.ds(rem, rows)]
            src, dst = (lsl, rsl) if to_remote else (rsl, lsl)
            copy = pltpu.make_async_copy(src, dst, sems.at[bit, p])
            if wait:
                copy.wait()
            else:
                copy.start()
            return carry

        lax.fori_loop(0, count, one, 0)


def _selection(meta, rows):
    r = lax.broadcasted_iota(jnp.int32, (TOKEN_TILE, rows), 1)
    pos0 = meta[:, 4:5].astype(jnp.int32)
    pos1 = meta[:, 5:6].astype(jnp.int32)
    return jnp.where(r == pos0, 1.0, 0.0).astype(_BF16), jnp.where(r == pos1, 1.0, 0.0).astype(_BF16)


def _dispatch_kernel(tab_ref, meta_ref, h3_ref, xs_in_ref, xs_ref, buf0_ref, buf1_ref, sems, *, ntiles):
    del xs_in_ref
    t = pl.program_id(0)
    last = ntiles - 1
    bufs = (buf0_ref, buf1_ref)

    def step(slot):
        mine, other = bufs[slot], bufs[1 - slot]
        _move_runs(tab_ref, jnp.clip(t - 2, 0, last), t >= 2, mine, xs_ref, sems.at[slot],
                   to_remote=True, wait=True)
        _move_runs(tab_ref, jnp.clip(t - 1, 0, last), jnp.logical_and(t >= 1, t <= ntiles), other, xs_ref,
                   sems.at[1 - slot], to_remote=True, wait=False)

        meta = meta_ref[...]
        sel0, sel1 = _selection(meta, SORT_ROWS)
        tn = (((0,), (0,)), ((), ()))
        mine[:, 0:D_MODEL] = lax.dot_general(
            sel0 + sel1, h3_ref[...], tn, preferred_element_type=_F32).astype(_BF16)
        lane = lax.broadcasted_iota(jnp.int32, (TOKEN_TILE, 128), 1)
        cols = []
        for k in range(2):
            w = meta[:, 2 + k:3 + k]
            hi = w.astype(_BF16).astype(_F32)
            mid = (w - hi).astype(_BF16).astype(_F32)
            lo = (w - hi) - mid
            cols.append(jnp.where(lane == 0, hi, jnp.where(lane == 1, mid, jnp.where(lane == 2, lo, 0.0)))
                        .astype(_BF16))
        mine[:, D_MODEL:XS_COLS] = (
            lax.dot_general(sel0, cols[0], tn, preferred_element_type=_F32)
            + lax.dot_general(sel1, cols[1], tn, preferred_element_type=_F32)).astype(_BF16)

    for slot in range(2):
        pl.when(t % 2 == slot)(functools.partial(step, slot))


def _dispatch_call(tab, meta, h3, xs_zero):
    n = h3.shape[0]
    tm = TOKEN_TILE
    ntiles = n // tm
    tile = lambda i, tab: (jnp.minimum(i, ntiles - 1), 0)
    grid_spec = pltpu.PrefetchScalarGridSpec(
        num_scalar_prefetch=1,
        grid=(ntiles + 2,),
        in_specs=[pl.BlockSpec((tm, 128), tile),
                  pl.BlockSpec((tm, D_MODEL), tile),
                  pl.BlockSpec(memory_space=pl.ANY)],
        out_specs=pl.BlockSpec(memory_space=pl.ANY),
        scratch_shapes=[pltpu.VMEM((SORT_ROWS, XS_COLS), _BF16), pltpu.VMEM((SORT_ROWS, XS_COLS), _BF16),
                        pltpu.SemaphoreType.DMA((2, RUN_BITS, N_EXPERTS))],
    )
    return pl.pallas_call(
        functools.partial(_dispatch_kernel, ntiles=ntiles),
        grid_spec=grid_spec,
        out_shape=jax.ShapeDtypeStruct(xs_zero.shape, xs_zero.dtype),
        input_output_aliases={3: 0},
        compiler_params=_params(1),
        name="moe_dispatch",
    )(tab, meta, h3, xs_zero)


def _ffn_kernel(be_ref, nv_ref, xs_ref, wg_ref, wu_ref, wd_ref, ys_ref):
    i = pl.program_id(0)

    @pl.when(i < nv_ref[0])
    def _():
        xb = xs_ref[:, 0:D_MODEL]
        wcols = xs_ref[:, D_MODEL:XS_COLS].astype(_F32)
        wt = wcols[:, 0:1] + wcols[:, 1:2] + wcols[:, 2:3]
        g = _dot(xb, wg_ref[...].astype(_BF16))
        u = _dot(xb, wu_ref[...].astype(_BF16))
        a = (g * jax.nn.sigmoid(g)) * u
        y = _dot(a.astype(_BF16), wd_ref[...].astype(_BF16))
        ys_ref[...] = (y * wt).astype(ys_ref.dtype)

    @pl.when(i >= nv_ref[0])
    def _():
        ys_ref[...] = jnp.zeros_like(ys_ref)


def _ffn_call(blk_expert, n_valid, xs, w_gate, w_up, w_down, layer):
    n_blocks = xs.shape[0] // MOE_BLOCK
    wspec = lambda shape: pl.BlockSpec((None, None) + shape, lambda i, be, nv: (layer, be[i], 0, 0))
    last_live = lambda i, be, nv: (jnp.minimum(i, nv[0] - 1), 0)
    grid_spec = pltpu.PrefetchScalarGridSpec(
        num_scalar_prefetch=2,
        grid=(n_blocks,),
        in_specs=[pl.BlockSpec((MOE_BLOCK, XS_COLS), last_live),
                  wspec((D_MODEL, D_EXPERT)), wspec((D_MODEL, D_EXPERT)), wspec((D_EXPERT, D_MODEL))],
        out_specs=pl.BlockSpec((MOE_BLOCK, D_MODEL), lambda i, be, nv: (i, 0)),
    )
    return pl.pallas_call(
        _ffn_kernel,
        grid_spec=grid_spec,
        out_shape=jax.ShapeDtypeStruct((xs.shape[0], D_MODEL), _BF16),
        compiler_params=_params(1),
        name="moe_experts",
    )(blk_expert, n_valid, xs, w_gate, w_up, w_down)


def _combine_kernel(tab_ref, meta_ref, x2_ref, ys_ref, out_ref, buf0_ref, buf1_ref, sems, *, ntiles):
    t = pl.program_id(0)
    last = ntiles - 1
    bufs = (buf0_ref, buf1_ref)

    @pl.when(t == 0)
    def _():
        buf0_ref[...] = jnp.zeros_like(buf0_ref)
        buf1_ref[...] = jnp.zeros_like(buf1_ref)

    def step(slot):
        mine, other = bufs[slot], bufs[1 - slot]
        _move_runs(tab_ref, jnp.clip(t - 1, 0, last), t >= 1, other, ys_ref, sems.at[1 - slot],
                   to_remote=False, wait=True)
        _move_runs(tab_ref, jnp.minimum(t, last), t <= last, mine, ys_ref, sems.at[slot],
                   to_remote=False, wait=False)
        sel0, sel1 = _selection(meta_ref[...], SORT_ROWS)
        out_ref[...] = x2_ref[...] + _dot(sel0 + sel1, other[...])

    for slot in range(2):
        pl.when(t % 2 == slot)(functools.partial(step, slot))


def _combine_call(tab, meta, x2, ys):
    n = x2.shape[0]
    tm = TOKEN_TILE
    ntiles = n // tm
    tile = lambda i, tab: (jnp.maximum(i - 1, 0), 0)
    grid_spec = pltpu.PrefetchScalarGridSpec(
        num_scalar_prefetch=1,
        grid=(ntiles + 1,),
        in_specs=[pl.BlockSpec((tm, 128), tile),
                  pl.BlockSpec((tm, D_MODEL), tile),
                  pl.BlockSpec(memory_space=pl.ANY)],
        out_specs=pl.BlockSpec((tm, D_MODEL), tile),
        scratch_shapes=[pltpu.VMEM((SORT_ROWS, D_MODEL), _BF16), pltpu.VMEM((SORT_ROWS, D_MODEL), _BF16),
                        pltpu.SemaphoreType.DMA((2, RUN_BITS, N_EXPERTS))],
    )
    return pl.pallas_call(
        functools.partial(_combine_kernel, ntiles=ntiles),
        grid_spec=grid_spec,
        out_shape=jax.ShapeDtypeStruct((n, D_MODEL), _F32),
        compiler_params=_params(1),
        name="moe_combine",
    )(tab, meta, x2, ys)


def _moe_tables(cnt, n_tokens):
    ntiles = n_tokens // TOKEN_TILE
    counts = cnt.reshape(ntiles, 8, 128)[:, 0, :N_EXPERTS].astype(jnp.int32)
    units = (counts + RUN_ALIGN - 1) // RUN_ALIGN
    padded = units * RUN_ALIGN
    local = jnp.cumsum(padded, axis=1) - padded
    total = jnp.sum(padded, axis=0)
    total_blk = ((total + MOE_BLOCK - 1) // MOE_BLOCK) * MOE_BLOCK
    ends = jnp.cumsum(total_blk)
    starts = ends - total_blk
    remote = starts[None, :] + jnp.cumsum(padded, axis=0) - padded
    bits = jnp.arange(RUN_BITS, dtype=jnp.int32)[None, :, None]
    has = (units[:, None, :] >> bits) & 1
    done = (units[:, None, :] & ((1 << bits) - 1)) * RUN_ALIGN
    rank = jnp.cumsum(has, axis=2) - 1
    place = jnp.logical_and(rank[..., None] == jnp.arange(N_EXPERTS, dtype=jnp.int32), has[..., None] == 1)
    compact = lambda rows: jnp.sum(jnp.where(place, rows[..., None], 0), axis=2)
    tab = jnp.concatenate([jnp.sum(has, axis=2, keepdims=True), compact(local[:, None, :] + done),
                           compact(remote[:, None, :] + done)], axis=-1).reshape(-1).astype(jnp.int32)
    n_blocks = _xs_rows(n_tokens) // MOE_BLOCK
    blk_start = jnp.arange(n_blocks, dtype=jnp.int32) * MOE_BLOCK
    blk_expert = jnp.minimum(jnp.sum((ends[None, :] <= blk_start[:, None]).astype(jnp.int32), axis=1),
                             N_EXPERTS - 1)
    n_valid = (ends[-1] // MOE_BLOCK).astype(jnp.int32).reshape(1)
    return tab, blk_expert, n_valid


def _xs_rows(n_tokens):
    ntiles = n_tokens // TOKEN_TILE
    worst = 2 * n_tokens + ntiles * N_EXPERTS * (RUN_ALIGN - 1) + N_EXPERTS * (MOE_BLOCK - 1)
    return ((worst + MOE_BLOCK - 1) // MOE_BLOCK) * MOE_BLOCK


def _pad_last(a, width):
    return jnp.pad(a, [(0, 0)] * (a.ndim - 1) + [(0, width - a.shape[-1])])


def _swap_mid_heads(a, axis):
    shape = a.shape
    a = a.reshape(shape[:axis] + (4, 64) + shape[axis + 1:])
    a = jnp.take(a, jnp.asarray([0, 2, 1, 3]), axis=axis)
    return a.reshape(shape)


def _layer_weights(l, p):
    w_in = p["w_in"][l]
    a, b = w_in[:, :A_COLS], w_in[:, A_COLS:A_COLS + B_COLS]
    c = w_in[:, A_COLS + B_COLS:A_COLS + B_COLS + C_COLS]
    d = w_in[:, A_COLS + B_COLS + C_COLS:]
    zeros = lambda w: jnp.zeros((D_MODEL, w), _F32)
    half = MLA_ROPE // 2
    k_rope = a[:, 384:416]
    k_rope_swap = jnp.concatenate([k_rope[:, half:], k_rope[:, :half]], axis=1)
    w_in_p = jnp.concatenate([
        a[:, :384], zeros(64), k_rope, zeros(32), zeros(64), k_rope_swap, zeros(32), b,
        _swap_mid_heads(c[:, :256], 1), c[:, 256:384], c[:, 384:512], d], axis=1).astype(_BF16)
    assert w_in_p.shape[1] == IN_COLS_PADDED

    w_uq = p["mla_w_uq"][l].reshape(MLA_Q_RANK, MLA_HEADS, MLA_QK_DIM)
    w_uq_swap = jnp.concatenate([jnp.zeros_like(w_uq[..., :MLA_NOPE]), w_uq[..., MLA_NOPE + half:],
                                 w_uq[..., MLA_NOPE:MLA_NOPE + half]], axis=-1)
    w_uq = jnp.concatenate([_pad_last(w_uq, HEAD_LANES).reshape(MLA_Q_RANK, -1),
                            _pad_last(w_uq_swap, HEAD_LANES).reshape(MLA_Q_RANK, -1)], axis=1).astype(_BF16)

    def swap_gain(g):
        return _pad_last(jnp.concatenate([jnp.zeros((MLA_NOPE,), _F32), g[MLA_NOPE + half:],
                                          g[MLA_NOPE:MLA_NOPE + half]]), HEAD_LANES)[None]
    w_ukv = p["mla_w_ukv"][l].reshape(MLA_KV_RANK, MLA_HEADS, MLA_NOPE + MLA_V)
    k_nope = _pad_last(w_ukv[..., :MLA_NOPE], HEAD_LANES).reshape(MLA_KV_RANK, MLA_HEADS * HEAD_LANES)
    v_part = w_ukv[..., MLA_NOPE:].reshape(MLA_KV_RANK, MLA_HEADS * MLA_V)
    w_ukv_p = jnp.concatenate([k_nope, v_part], axis=1).astype(_BF16)

    eye = jnp.eye(len(POOL_WINDOWS), dtype=_F32)
    w_pool = jnp.einsum("gcd,gh->gchd", p["pool_w"][l], eye).reshape(256, 256).astype(_BF16)

    g_mo = p["mix_out_norm"][l]
    g_mo = jnp.concatenate([g_mo[:512], _swap_mid_heads(g_mo[512:768], 0), g_mo[768:]])
    w_mo = p["w_mix_out"][l]
    w_mo = jnp.concatenate([w_mo[:512], _swap_mid_heads(w_mo[512:768], 0), w_mo[768:]], axis=0)

    w_router = _pad_last(jnp.concatenate([p["w_expert"][l], p["w_group"][l]], axis=1), 128)
    w_router_hi = w_router.astype(_BF16)
    w_router_lo = (w_router - w_router_hi.astype(_F32)).astype(_BF16)
    b_router = jnp.concatenate([p["b_expert"][l], p["b_group"][l]])
    tm = TOKEN_TILE
    ltri = (jnp.arange(tm)[None, :] < jnp.arange(tm)[:, None]).astype(_BF16)
    utri = (jnp.arange(128)[:, None] < jnp.arange(128)[None, :]).astype(_BF16)
    tile2 = lambda g: jnp.concatenate([g, g])[None]
    return dict(
        g_mix=p["norm_mix"][l][None], w_in=w_in_p,
        g_cq=p["mla_g_cq"][l][None], w_uq=w_uq, g_ckv=p["mla_g_ckv"][l][None], w_ukv=w_ukv_p,
        g_q=_pad_last(p["mla_g_q"][l], HEAD_LANES)[None], g_k=_pad_last(p["mla_g_k"][l], HEAD_LANES)[None],
        g_q_swap=swap_gain(p["mla_g_q"][l]), g_k_swap=swap_gain(p["mla_g_k"][l]),
        w_pool=w_pool, pool_scale=p["pool_scale"][l][None], conv_w=p["conv_w"][l],
        conv_b=p["conv_b"][l][None], g_sq=tile2(p["swa_g_q"][l]), g_sk=tile2(p["swa_g_k"][l]),
        sinks=jnp.take(p["swa_sinks"][l], jnp.asarray([0, 2, 1, 3])),
        g_mo=g_mo[None], w_mo=w_mo.astype(_BF16), g_xa=p["norm_xa"][l][None],
        xa_w_q=p["xa_w_q"][l].astype(_BF16), g_xq=jnp.tile(p["xa_g_q"][l], 4)[None],
        xa_w_o=p["xa_w_o"][l].astype(_BF16), g_ffn=p["norm_ffn"][l][None],
        w_router_hi=w_router_hi, w_router_lo=w_router_lo, b_router=_pad_last(b_router, 128)[None],
        ltri=ltri, utri=utri,
    )


def kernel(x, mem, positions, norm_mix, w_in, mla_g_cq, mla_w_uq, mla_g_ckv, mla_w_ukv, mla_g_q, mla_g_k, pool_w, pool_scale, swa_g_q, swa_g_k, swa_sinks, conv_w, conv_b, mix_out_norm, w_mix_out, norm_xa, norm_mem, xa_w_q, xa_w_kv, xa_g_q, xa_g_k, xa_w_o, norm_ffn, w_group, b_group, w_expert, b_expert, w_gate, w_up, w_down):
    p = dict(norm_mix=norm_mix, w_in=w_in, mla_g_cq=mla_g_cq, mla_w_uq=mla_w_uq, mla_g_ckv=mla_g_ckv,
             mla_w_ukv=mla_w_ukv, mla_g_q=mla_g_q, mla_g_k=mla_g_k, pool_w=pool_w, pool_scale=pool_scale,
             swa_g_q=swa_g_q, swa_g_k=swa_g_k, swa_sinks=swa_sinks, conv_w=conv_w, conv_b=conv_b,
             mix_out_norm=mix_out_norm, w_mix_out=w_mix_out, norm_xa=norm_xa, xa_w_q=xa_w_q,
             xa_g_q=xa_g_q, xa_w_o=xa_w_o, norm_ffn=norm_ffn, w_group=w_group, b_group=b_group,
             w_expert=w_expert, b_expert=b_expert)
    batch, seq_len, _ = x.shape
    depth = w_in.shape[0]
    n = batch * seq_len
    assert seq_len % TOKEN_TILE == 0 and seq_len % ATTN_TILE == 0 and n % 1024 == 0
    assert mem.shape[1] == MEM_LEN

    xf = x.reshape(n, D_MODEL)
    tabs = _rope_tables(positions.astype(_F32).reshape(n, 1), n)
    kx, vx = _memkv_call(mem.reshape(batch * MEM_LEN, D_MODEL), norm_mem[:, None, :],
                         xa_w_kv.astype(_BF16), jnp.tile(xa_g_k, (1, 4))[:, None, :], depth, batch)
    for l in range(depth):
        lw = _layer_weights(l, p)
        qm, km, vm, qs, ks, vs, yb, yd = _proj_call(xf, lw, tabs, seq_len)
        ya = _mla_attn_call(qm, km, vm, batch, seq_len)
        yc = _swa_call(qs, ks, vs, lw["sinks"], seq_len)
        x2, h3, meta, cnt = _mix_call(ya, yb, yc, yd, xf, lw, kx[l], vx[l], seq_len)
        tab, blk_expert, n_valid = _moe_tables(cnt, n)
        xs = _dispatch_call(tab, meta, h3, jnp.zeros((_xs_rows(n), XS_COLS), _BF16))
        ys = _ffn_call(blk_expert, n_valid, xs, w_gate, w_up, w_down, l)
        xf = _combine_call(tab, meta, x2, ys)
    return xf.reshape(batch, seq_len, D_MODEL)
```

```python
import functools

import jax
import jax.numpy as jnp
import numpy as np
from jax import lax
from jax.experimental import pallas as pl
from jax.experimental.pallas import tpu as pltpu

EPS = 1e-6
NEG_BIG = -1e30
LOG2_E = 1.4426950408889634
ROPE_THETA = 10000.0

D_MODEL = 1024
MEM_LEN = 256
GROUP_W = 256

MLA_HEADS = 4
MLA_Q_RANK = 256
MLA_KV_RANK = 128
MLA_NOPE = 64
MLA_ROPE = 32
MLA_QK_DIM = MLA_NOPE + MLA_ROPE
MLA_V = 64
HEAD_LANES = 128

POOL_WINDOWS = (2, 4, 8, 16)
POOL_GROUP = 64
POOL_HALO = 16

SWA_HEADS = 4
SWA_KV_HEADS = 2
SWA_HEAD_DIM = 64
SWA_WINDOW = 128

CONV_CH = 256

XA_HEADS = 4
XA_HEAD_DIM = 64

N_EXPERT_GROUPS = 4
EXPERTS_PER_GROUP = 8
N_EXPERTS = 32
D_EXPERT = 256
MOE_BLOCK = 512

A_COLS = MLA_Q_RANK + MLA_KV_RANK + MLA_ROPE
B_COLS = GROUP_W
C_COLS = (SWA_HEADS + 2 * SWA_KV_HEADS) * SWA_HEAD_DIM
COL_CQ, COL_CKV, COL_KROPE, COL_KROPE_SWAP = 0, 256, 384, 512
COL_POOL, COL_SWA_Q, COL_SWA_K, COL_SWA_V, COL_CONV = 640, 896, 1152, 1280, 1408
IN_COLS_PADDED = COL_CONV + 3 * CONV_CH

TOKEN_TILE = 512
ATTN_TILE = 512
RUN_ALIGN = 16
RUN_BITS = 6
SORT_ROWS = 2 * TOKEN_TILE + N_EXPERTS * RUN_ALIGN
PIECE_COLS = 1 + 2 * N_EXPERTS
XS_COLS = D_MODEL + 128
VMEM_LIMIT = 56 * 1024 * 1024

_F32 = jnp.float32
_BF16 = jnp.bfloat16


def _params(n_axes):
    return pltpu.CompilerParams(dimension_semantics=("arbitrary",) * n_axes,
                                vmem_limit_bytes=VMEM_LIMIT)


def _dot(a, b):
    return jnp.dot(a, b, preferred_element_type=_F32)


def _dot_nt(a, b):
    return lax.dot_general(a, b, (((1,), (1,)), ((), ())), preferred_element_type=_F32)


def _rms_factor(x, width):
    ss = jnp.sum(x * x, axis=-1, keepdims=True)
    return lax.rsqrt(ss * (1.0 / width) + EPS)


def _rms_scale(x, width):
    return x * _rms_factor(x, width)


def _pair_norm64(x, lane_lo):
    x2 = x * x
    s_all = jnp.sum(x2, axis=-1, keepdims=True)
    s_lo = jnp.sum(jnp.where(lane_lo, x2, 0.0), axis=-1, keepdims=True)
    ss = jnp.where(lane_lo, s_lo, s_all - s_lo)
    return x * lax.rsqrt(ss * (1.0 / 64.0) + EPS)


def _rope_kernel(pos_ref, freq_ref, sign_ref, c_ref, s_ref):
    ang = pos_ref[...] * freq_ref[...]
    c_ref[...] = jnp.cos(ang)
    s_ref[...] = jnp.sin(ang) * sign_ref[...]


def _rope_tables(posf, n_rows):
    half = MLA_ROPE // 2
    inv_freq = ROPE_THETA ** (-jnp.arange(half, dtype=_F32) / half)
    zeros = jnp.zeros((HEAD_LANES,), _F32)
    freq = zeros.at[MLA_NOPE:MLA_NOPE + half].set(inv_freq).at[MLA_NOPE + half:MLA_QK_DIM].set(inv_freq)
    sign = zeros.at[MLA_NOPE:MLA_NOPE + half].set(-1.0).at[MLA_NOPE + half:MLA_QK_DIM].set(1.0)
    tile = 1024
    row = pl.BlockSpec((1, HEAD_LANES), lambda i: (0, 0))
    tab = pl.BlockSpec((tile, HEAD_LANES), lambda i: (i, 0))
    shp = jax.ShapeDtypeStruct((n_rows, HEAD_LANES), _F32)
    return pl.pallas_call(
        _rope_kernel,
        grid=(n_rows // tile,),
        in_specs=[pl.BlockSpec((tile, 1), lambda i: (i, 0)), row, row],
        out_specs=[tab, tab],
        out_shape=[shp, shp],
        compiler_params=_params(1),
        name="rope_tables",
    )(posf, freq[None], sign[None])


def _proj_kernel(x_ref, gmix_ref, win_ref, gcq_ref, wuq_ref, gckv_ref, wukv_ref, gq_ref, gqs_ref,
                 gk_ref, gks_ref, c_ref, s_ref, wpool_ref, pscale_ref, convw_ref, convb_ref, gsq_ref,
                 gsk_ref, qm_ref, km_ref, vm_ref, qs_ref, ks_ref, vs_ref, yb_ref, yd_ref, halo_ref,
                 *, tiles_per_seq):
    tm = TOKEN_TILE
    i = pl.program_id(0)
    seq_tile = i % tiles_per_seq

    x = x_ref[...]
    h = _rms_scale(x, D_MODEL) * gmix_ref[...]
    u = _dot(h.astype(_BF16), win_ref[...])

    c = c_ref[...]
    s = s_ref[...]
    hw = MLA_HEADS * HEAD_LANES

    cq = _rms_scale(u[:, COL_CQ:COL_CQ + MLA_Q_RANK], MLA_Q_RANK) * gcq_ref[...]
    q = _dot(cq.astype(_BF16), wuq_ref[...])
    gq = gq_ref[...] * (MLA_QK_DIM ** -0.5 * LOG2_E)
    gqs = gqs_ref[...] * (MLA_QK_DIM ** -0.5 * LOG2_E)
    for hd in range(MLA_HEADS):
        sl = slice(hd * HEAD_LANES, (hd + 1) * HEAD_LANES)
        xq = q[:, sl]
        r = _rms_factor(xq, MLA_QK_DIM)
        qm_ref[:, sl] = ((xq * r) * (gq * c) + (q[:, hw + sl.start:hw + sl.stop] * r) * (gqs * s)).astype(_BF16)

    ckv = _rms_scale(u[:, COL_CKV:COL_CKV + MLA_KV_RANK], MLA_KV_RANK) * gckv_ref[...]
    kv = _dot(ckv.astype(_BF16), wukv_ref[...])
    krope = u[:, COL_KROPE:COL_KROPE + HEAD_LANES]
    krope_swap = u[:, COL_KROPE_SWAP:COL_KROPE_SWAP + HEAD_LANES]
    gk = gk_ref[...]
    gks = gks_ref[...]
    for hd in range(MLA_HEADS):
        sl = slice(hd * HEAD_LANES, (hd + 1) * HEAD_LANES)
        xk = kv[:, sl] + krope
        r = _rms_factor(xk, MLA_QK_DIM)
        km_ref[:, sl] = ((xk * r) * (gk * c) + (krope_swap * r) * (gks * s)).astype(_BF16)
    vm_ref[...] = kv[:, hw:hw + MLA_HEADS * MLA_V].astype(_BF16)

    lane_lo = lax.broadcasted_iota(jnp.int32, (1, HEAD_LANES), 1) < 64
    gsq = gsq_ref[...]
    for blk in range(2):
        xq = u[:, COL_SWA_Q + blk * 128:COL_SWA_Q + (blk + 1) * 128]
        qs_ref[:, blk * 128:(blk + 1) * 128] = (
            _pair_norm64(xq, lane_lo) * gsq * (SWA_HEAD_DIM ** -0.5)).astype(_BF16)
    ks_ref[...] = (_pair_norm64(u[:, COL_SWA_K:COL_SWA_K + 128], lane_lo) * gsk_ref[...]).astype(_BF16)
    vs_ref[...] = u[:, COL_SWA_V:COL_SWA_V + 128].astype(_BF16)

    @pl.when(seq_tile == 0)
    def _():
        halo_ref[...] = jnp.zeros_like(halo_ref)

    halo = halo_ref[...]
    up = u[:, COL_POOL:COL_POOL + GROUP_W]
    ud = u[:, COL_CONV:COL_CONV + 3 * CONV_CH]

    b = jnp.concatenate([halo[:, 0:256], up], axis=0)
    w2 = b + pltpu.roll(b, 1, 0)
    w4 = w2 + pltpu.roll(w2, 2, 0)
    w8 = w4 + pltpu.roll(w4, 4, 0)
    w16 = w8 + pltpu.roll(w8, 8, 0)
    lane = lax.broadcasted_iota(jnp.int32, (1, 256), 1)
    win = jnp.where(lane < 64, w2, jnp.where(lane < 128, w4, jnp.where(lane < 192, w8, w16)))
    win = win[POOL_HALO:, :]
    width = jnp.where(lane < 64, 2, jnp.where(lane < 128, 4, jnp.where(lane < 192, 8, 16)))
    t = seq_tile * tm + lax.broadcasted_iota(jnp.int32, (tm, 1), 0)
    count = jnp.minimum(t + 1, width).astype(_F32)
    pooled = win / count - up
    yb_ref[...] = (_dot(pooled.astype(_BF16), wpool_ref[...]) * pscale_ref[...]).astype(_BF16)

    z = ud[:, 256:512] * ud[:, 512:768]
    zh = halo[:, 512:768] * halo[:, 768:1024]
    zb = jnp.concatenate([zh, z], axis=0)
    cw = convw_ref[...]
    conv = (pltpu.roll(zb, 2, 0)[POOL_HALO:, :] * cw[0:1, :]
            + pltpu.roll(zb, 1, 0)[POOL_HALO:, :] * cw[1:2, :]
            + z * cw[2:3, :])
    yd_ref[...] = (ud[:, 0:256] * (conv + convb_ref[...])).astype(_BF16)

    halo_ref[:, 0:256] = up[tm - POOL_HALO:, :]
    halo_ref[:, 256:1024] = ud[tm - POOL_HALO:, :]


def _proj_call(x, lw, tabs, seq_len):
    n = x.shape[0]
    tm = TOKEN_TILE
    tiles_per_seq = seq_len // tm

    def full(a):
        nd = a.ndim
        return pl.BlockSpec(a.shape, lambda i, _nd=nd: (0,) * _nd)

    def rows(width):
        return pl.BlockSpec((tm, width), lambda i: (i, 0))

    consts = [lw["g_mix"], lw["w_in"], lw["g_cq"], lw["w_uq"], lw["g_ckv"], lw["w_ukv"],
              lw["g_q"], lw["g_q_swap"], lw["g_k"], lw["g_k_swap"]]
    consts2 = [lw["w_pool"], lw["pool_scale"], lw["conv_w"], lw["conv_b"], lw["g_sq"], lw["g_sk"]]
    out_widths = [512, 512, 256, 256, 128, 128, 256, 256]
    return pl.pallas_call(
        functools.partial(_proj_kernel, tiles_per_seq=tiles_per_seq),
        grid=(n // tm,),
        in_specs=[rows(D_MODEL)] + [full(a) for a in consts] + [rows(HEAD_LANES)] * 2
                 + [full(a) for a in consts2],
        out_specs=[rows(w) for w in out_widths],
        out_shape=[jax.ShapeDtypeStruct((n, w), _BF16) for w in out_widths],
        scratch_shapes=[pltpu.VMEM((POOL_HALO, 1024), _F32)],
        compiler_params=_params(1),
        name="proj_in",
    )(x, *consts, *tabs, *consts2)


def _mla_attn_kernel(qi_ref, ki_ref, q_ref, k_ref, v_ref, o_ref, m_ref, l_ref, acc_ref):
    tq = tk = ATTN_TILE
    p_id = pl.program_id(1)
    qi = qi_ref[p_id]
    ki = ki_ref[p_id]

    @pl.when(ki == 0)
    def _():
        m_ref[...] = jnp.full_like(m_ref, NEG_BIG)
        l_ref[...] = jnp.zeros_like(l_ref)
        acc_ref[...] = jnp.zeros_like(acc_ref)

    lane_lo = lax.broadcasted_iota(jnp.int32, (1, HEAD_LANES), 1) < 64

    def accumulate(on_diagonal):
        if on_diagonal:
            visible = (lax.broadcasted_iota(jnp.int32, (tq, tk), 1)
                       <= lax.broadcasted_iota(jnp.int32, (tq, tk), 0))
        for pair in range(MLA_HEADS // 2):
            vblk = v_ref[:, pair * 128:(pair + 1) * 128]
            alphas = []
            pvs = []
            for sub in range(2):
                hd = 2 * pair + sub
                sl = slice(hd * HEAD_LANES, (hd + 1) * HEAD_LANES)
                s = _dot_nt(q_ref[:, sl], k_ref[:, sl])
                if on_diagonal:
                    s = jnp.where(visible, s, NEG_BIG)
                m_prev = m_ref[hd]
                m_new = jnp.maximum(m_prev, jnp.max(s, axis=-1, keepdims=True))
                alpha = jnp.exp2(m_prev - m_new)
                p = jnp.exp2(s - jnp.tile(m_new, (1, tk // HEAD_LANES)))
                l_ref[hd] = alpha * l_ref[hd] + jnp.sum(p, axis=-1, keepdims=True)
                m_ref[hd] = m_new
                alphas.append(alpha)
                pvs.append(_dot(p.astype(_BF16), vblk))
            psl = slice(pair * 128, (pair + 1) * 128)
            acc_ref[:, psl] = (acc_ref[:, psl] * jnp.where(lane_lo, alphas[0], alphas[1])
                               + jnp.where(lane_lo, pvs[0], pvs[1]))

    @pl.when(ki < qi)
    def _():
        accumulate(False)

    @pl.when(ki == qi)
    def _():
        accumulate(True)
        for pair in range(MLA_HEADS // 2):
            psl = slice(pair * 128, (pair + 1) * 128)
            denom = jnp.where(lane_lo, l_ref[2 * pair], l_ref[2 * pair + 1])
            o_ref[:, psl] = (acc_ref[:, psl] / denom).astype(o_ref.dtype)


def _mla_attn_call(qm, km, vm, batch, seq_len):
    tq = ATTN_TILE
    nq = seq_len // tq
    pairs = [(q, k) for q in range(nq) for k in range(q + 1)]
    qi = jnp.asarray([p[0] for p in pairs], jnp.int32)
    ki = jnp.asarray([p[1] for p in pairs], jnp.int32)
    grid_spec = pltpu.PrefetchScalarGridSpec(
        num_scalar_prefetch=2,
        grid=(batch, len(pairs)),
        in_specs=[
            pl.BlockSpec((tq, 512), lambda b, p, qi, ki: (b * nq + qi[p], 0)),
            pl.BlockSpec((tq, 512), lambda b, p, qi, ki: (b * nq + ki[p], 0)),
            pl.BlockSpec((tq, 256), lambda b, p, qi, ki: (b * nq + ki[p], 0)),
        ],
        out_specs=pl.BlockSpec((tq, 256), lambda b, p, qi, ki: (b * nq + qi[p], 0)),
        scratch_shapes=[pltpu.VMEM((MLA_HEADS, tq, HEAD_LANES), _F32),
                        pltpu.VMEM((MLA_HEADS, tq, HEAD_LANES), _F32),
                        pltpu.VMEM((tq, 256), _F32)],
    )
    return pl.pallas_call(
        _mla_attn_kernel,
        grid_spec=grid_spec,
        out_shape=jax.ShapeDtypeStruct((batch * seq_len, 256), _BF16),
        compiler_params=_params(2),
        name="mla_attention",
    )(qi, ki, qm, km, vm)


def _swa_kernel(sink_ref, q_ref, kc_ref, kp_ref, vc_ref, vp_ref, o_ref, *, tiles_per_seq):
    w = SWA_WINDOW
    i = pl.program_id(0)
    oldest = jnp.where((i % tiles_per_seq) == 0, 0, -w)
    lane_lo = lax.broadcasted_iota(jnp.int32, (1, 128), 1) < 64
    qpos = lax.broadcasted_iota(jnp.int32, (w, 2 * w), 0)
    kpos = lax.broadcasted_iota(jnp.int32, (w, 2 * w), 1) - w
    band = jnp.logical_and(kpos <= qpos, kpos > qpos - w)

    for jb in range(TOKEN_TILE // w):
        rs = slice(jb * w, (jb + 1) * w)
        if jb == 0:
            kprev, vprev = kp_ref[...], vp_ref[...]
            visible = jnp.logical_and(band, kpos >= oldest)
        else:
            ps = slice((jb - 1) * w, jb * w)
            kprev, vprev = kc_ref[ps, :], vc_ref[ps, :]
            visible = band
        kk = jnp.concatenate([kprev, kc_ref[rs, :]], axis=0)
        vv = jnp.concatenate([vprev, vc_ref[rs, :]], axis=0)
        zero = jnp.zeros_like(kk)
        k_half = (jnp.where(lane_lo, kk, zero), jnp.where(lane_lo, zero, kk))
        for blk in range(2):
            qblk = q_ref[rs, blk * 128:(blk + 1) * 128]
            outs = []
            for half in range(2):
                sink = sink_ref[2 * blk + half]
                s = _dot_nt(qblk, k_half[half])
                s = jnp.where(visible, s, NEG_BIG)
                m = jnp.maximum(jnp.max(s, axis=-1, keepdims=True), sink)
                p = jnp.exp(s - m)
                denom = jnp.sum(p, axis=-1, keepdims=True) + jnp.exp(sink - m)
                outs.append(_dot(p.astype(_BF16), vv) / denom)
            o_ref[rs, blk * 128:(blk + 1) * 128] = jnp.where(lane_lo, outs[0], outs[1]).astype(o_ref.dtype)


def _swa_call(qs, ks, vs, sinks, seq_len):
    n = qs.shape[0]
    tm = TOKEN_TILE
    per = tm // SWA_WINDOW
    prev = lambda i, s: (jnp.maximum(i * per - 1, 0), 0)
    grid_spec = pltpu.PrefetchScalarGridSpec(
        num_scalar_prefetch=1,
        grid=(n // tm,),
        in_specs=[
            pl.BlockSpec((tm, 256), lambda i, s: (i, 0)),
            pl.BlockSpec((tm, 128), lambda i, s: (i, 0)),
            pl.BlockSpec((SWA_WINDOW, 128), prev),
            pl.BlockSpec((tm, 128), lambda i, s: (i, 0)),
            pl.BlockSpec((SWA_WINDOW, 128), prev),
        ],
        out_specs=pl.BlockSpec((tm, 256), lambda i, s: (i, 0)),
    )
    return pl.pallas_call(
        functools.partial(_swa_kernel, tiles_per_seq=seq_len // tm),
        grid_spec=grid_spec,
        out_shape=jax.ShapeDtypeStruct((n, 256), _BF16),
        compiler_params=_params(1),
        name="swa_attention",
    )(sinks, qs, ks, ks, vs, vs)


def _memkv_kernel(mem_ref, gmem_ref, wkv_ref, gk_ref, k_ref, v_ref):
    m = _rms_scale(mem_ref[...], D_MODEL) * gmem_ref[...]
    kv = _dot(m.astype(_BF16), wkv_ref[...])
    lane_lo = lax.broadcasted_iota(jnp.int32, (1, 128), 1) < 64
    lane = lax.broadcasted_iota(jnp.int32, (1, 256), 1)
    k = jnp.concatenate([_pair_norm64(kv[:, 0:128], lane_lo), _pair_norm64(kv[:, 128:256], lane_lo)],
                        axis=1) * gk_ref[...]
    v = kv[:, 256:512]
    for hd in range(XA_HEADS):
        own = jnp.logical_and(lane >= hd * XA_HEAD_DIM, lane < (hd + 1) * XA_HEAD_DIM)
        k_ref[hd] = jnp.where(own, k, 0.0).astype(_BF16)
        v_ref[hd] = jnp.where(own, v, 0.0).astype(_BF16)


def _memkv_call(mem2d, g_mem, w_kv, g_k4, depth, batch):
    out = jax.ShapeDtypeStruct((depth, batch, XA_HEADS, MEM_LEN, 256), _BF16)
    ospec = pl.BlockSpec((None, None, XA_HEADS, MEM_LEN, 256), lambda l, b: (l, b, 0, 0, 0))
    return pl.pallas_call(
        _memkv_kernel,
        grid=(depth, batch),
        in_specs=[pl.BlockSpec((MEM_LEN, D_MODEL), lambda l, b: (b, 0)),
                  pl.BlockSpec((None, 1, D_MODEL), lambda l, b: (l, 0, 0)),
                  pl.BlockSpec((None, D_MODEL, 512), lambda l, b: (l, 0, 0)),
                  pl.BlockSpec((None, 1, 256), lambda l, b: (l, 0, 0))],
        out_specs=[ospec, ospec],
        out_shape=[out, out],
        compiler_params=_params(2),
        name="memory_kv",
    )(mem2d, g_mem, w_kv, g_k4)


def _mix_kernel(ya_ref, yb_ref, yc_ref, yd_ref, x_ref, gmo_ref, wmo_ref, gxa_ref, wq_ref, gxq_ref,
                kx_ref, vx_ref, wo_ref, gffn_ref, wrh_ref, wrl_ref, br_ref, ltri_ref, utri_ref,
                x2_ref, h3_ref, meta_ref, cnt_ref):
    tm = TOKEN_TILE
    gmo = gmo_ref[...]
    parts = []
    for g, ref in enumerate((ya_ref, yb_ref, yc_ref, yd_ref)):
        yg = ref[...].astype(_F32)
        parts.append((_rms_scale(yg, GROUP_W) * gmo[:, g * 256:(g + 1) * 256]).astype(_BF16))
    y = jnp.concatenate(parts, axis=1)
    x1 = x_ref[...] + _dot(y, wmo_ref[...])

    h = (_rms_scale(x1, D_MODEL) * gxa_ref[...]).astype(_BF16)
    q = _dot(h, wq_ref[...])
    lane_lo = lax.broadcasted_iota(jnp.int32, (1, 128), 1) < 64
    qn = jnp.concatenate([_pair_norm64(q[:, 0:128], lane_lo), _pair_norm64(q[:, 128:256], lane_lo)], axis=1)
    qn = (qn * gxq_ref[...] * (XA_HEAD_DIM ** -0.5)).astype(_BF16)
    o = jnp.zeros((tm, 256), _F32)
    for hd in range(XA_HEADS):
        s = _dot_nt(qn, kx_ref[hd])
        m = jnp.max(s, axis=-1, keepdims=True)
        p = jnp.exp(s - m)
        denom = jnp.sum(p, axis=-1, keepdims=True)
        o = o + _dot(p.astype(_BF16), vx_ref[hd]) / denom
    x2 = x1 + _dot(o.astype(_BF16), wo_ref[...])
    x2_ref[...] = x2

    h3 = _rms_scale(x2, D_MODEL) * gffn_ref[...]
    h3_hi = h3.astype(_BF16)
    h3_lo = (h3 - h3_hi.astype(_F32)).astype(_BF16)
    h3_ref[...] = h3_hi
    logits = (_dot(h3_hi, wrh_ref[...]) + (_dot(h3_hi, wrl_ref[...]) + _dot(h3_lo, wrh_ref[...]))
              + br_ref[...])
    lane = lax.broadcasted_iota(jnp.int32, (tm, 128), 1)
    far = jnp.int32(1 << 20)
    is_group = jnp.logical_and(lane >= N_EXPERTS, lane < N_EXPERTS + N_EXPERT_GROUPS)
    gl = jnp.where(is_group, logits, -jnp.inf)
    gmax = jnp.max(gl, axis=-1, keepdims=True)
    gidx = jnp.min(jnp.where(gl == gmax, lane, far), axis=-1, keepdims=True) - N_EXPERTS
    g_w = 1.0 / jnp.sum(jnp.where(is_group, jnp.exp(logits - gmax), 0.0), axis=-1, keepdims=True)
    in_group = jnp.logical_and(lane < N_EXPERTS, jnp.right_shift(lane, 3) == gidx)
    el = jnp.where(in_group, logits, -jnp.inf)
    emax = jnp.max(el, axis=-1, keepdims=True)
    ep = jnp.where(in_group, jnp.exp(logits - emax), 0.0)
    prob = jnp.where(in_group, ep / jnp.sum(ep, axis=-1, keepdims=True), -1.0)
    p1 = jnp.max(prob, axis=-1, keepdims=True)
    e0 = jnp.min(jnp.where(prob == p1, lane, far), axis=-1, keepdims=True)
    prob2 = jnp.where(lane == e0, -1.0, prob)
    p2 = jnp.max(prob2, axis=-1, keepdims=True)
    e1 = jnp.min(jnp.where(prob2 == p2, lane, far), axis=-1, keepdims=True)
    w0 = g_w * (p1 / (p1 + p2))
    w1 = g_w * (p2 / (p1 + p2))

    onehot = jnp.where(jnp.logical_or(lane == e0, lane == e1), 1.0, 0.0)
    prefix = _dot(ltri_ref[...], onehot.astype(_BF16))
    counts = jnp.sum(onehot, axis=0, keepdims=True)
    units = jnp.floor((counts + (RUN_ALIGN - 1)) * (1.0 / RUN_ALIGN))
    offs = _dot(jnp.broadcast_to(units, (8, 128)).astype(_BF16), utri_ref[...])[0:1, :] * RUN_ALIGN
    where_to = prefix + offs
    pos0 = jnp.sum(jnp.where(lane == e0, where_to, 0.0), axis=-1, keepdims=True)
    pos1 = jnp.sum(jnp.where(lane == e1, where_to, 0.0), axis=-1, keepdims=True)

    meta = jnp.where(lane == 0, e0.astype(_F32),
           jnp.where(lane == 1, e1.astype(_F32),
           jnp.where(lane == 2, w0,
           jnp.where(lane == 3, w1,
           jnp.where(lane == 4, pos0,
           jnp.where(lane == 5, pos1, 0.0))))))
    meta_ref[...] = meta
    cnt_ref[...] = jnp.broadcast_to(counts, (8, 128))


def _mix_call(ya, yb, yc, yd, x, lw, kx, vx, seq_len):
    n = x.shape[0]
    tm = TOKEN_TILE
    tiles_per_seq = seq_len // tm

    def full(a):
        nd = a.ndim
        return pl.BlockSpec(a.shape, lambda i, _nd=nd: (0,) * _nd)

    def rows(width):
        return pl.BlockSpec((tm, width), lambda i: (i, 0))

    kvspec = pl.BlockSpec((None, XA_HEADS, MEM_LEN, 256), lambda i: (i // tiles_per_seq, 0, 0, 0))
    consts_a = [lw["g_mo"], lw["w_mo"], lw["g_xa"], lw["xa_w_q"], lw["g_xq"]]
    consts_b = [lw["xa_w_o"], lw["g_ffn"], lw["w_router_hi"], lw["w_router_lo"], lw["b_router"],
                lw["ltri"], lw["utri"]]
    ntiles = n // tm
    return pl.pallas_call(
        _mix_kernel,
        grid=(ntiles,),
        in_specs=[rows(256)] * 4 + [rows(D_MODEL)] + [full(a) for a in consts_a]
                 + [kvspec, kvspec] + [full(a) for a in consts_b],
        out_specs=[rows(D_MODEL), rows(D_MODEL), rows(128), pl.BlockSpec((8, 128), lambda i: (i, 0))],
        out_shape=[jax.ShapeDtypeStruct((n, D_MODEL), _F32), jax.ShapeDtypeStruct((n, D_MODEL), _BF16),
                   jax.ShapeDtypeStruct((n, 128), _F32), jax.ShapeDtypeStruct((ntiles * 8, 128), _F32)],
        compiler_params=_params(1),
        name="mix_xattn_router",
    )(ya, yb, yc, yd, x, *consts_a, kx, vx, *consts_b)


def _move_runs(tab_ref, t, live, local_ref, remote_ref, sems, *, to_remote, wait):
    for bit in range(RUN_BITS):
        rows = RUN_ALIGN << bit
        base = (t * RUN_BITS + bit) * PIECE_COLS
        count = jnp.where(live, tab_ref[base], 0)

        def one(p, carry, base=base, rows=rows, bit=bit):
            loc = pl.multiple_of(tab_ref[base + 1 + p], RUN_ALIGN)
            rem = pl.multiple_of(tab_ref[base + 1 + N_EXPERTS + p], RUN_ALIGN)
            lsl = local_ref.at[pl.ds(loc, rows)]
            rsl = remote_ref.at[pl.ds(rem, rows)]
            src, dst = (lsl, rsl) if to_remote else (rsl, lsl)
            copy = pltpu.make_async_copy(src, dst, sems.at[bit, p])
            if wait:
                copy.wait()
            else:
                copy.start()
            return carry

        lax.fori_loop(0, count, one, 0)


def _selection(meta, rows):
    r = lax.broadcasted_iota(jnp.int32, (TOKEN_TILE, rows), 1)
    pos0 = meta[:, 4:5].astype(jnp.int32)
    pos1 = meta[:, 5:6].astype(jnp.int32)
    return jnp.where(r == pos0, 1.0, 0.0).astype(_BF16), jnp.where(r == pos1, 1.0, 0.0).astype(_BF16)


def _dispatch_kernel(tab_ref, meta_ref, h3_ref, xs_in_ref, xs_ref, buf0_ref, buf1_ref, sems, *, ntiles):
    del xs_in_ref
    t = pl.program_id(0)
    last = ntiles - 1
    bufs = (buf0_ref, buf1_ref)

    def step(slot):
        mine, other = bufs[slot], bufs[1 - slot]
        _move_runs(tab_ref, jnp.clip(t - 2, 0, last), t >= 2, mine, xs_ref, sems.at[slot],
                   to_remote=True, wait=True)
        _move_runs(tab_ref, jnp.clip(t - 1, 0, last), jnp.logical_and(t >= 1, t <= ntiles), other, xs_ref,
                   sems.at[1 - slot], to_remote=True, wait=False)

        meta = meta_ref[...]
        sel0, sel1 = _selection(meta, SORT_ROWS)
        tn = (((0,), (0,)), ((), ()))
        mine[:, 0:D_MODEL] = lax.dot_general(
            sel0 + sel1, h3_ref[...], tn, preferred_element_type=_F32).astype(_BF16)
        lane = lax.broadcasted_iota(jnp.int32, (TOKEN_TILE, 128), 1)
        cols = []
        for k in range(2):
            w = meta[:, 2 + k:3 + k]
            hi = w.astype(_BF16).astype(_F32)
            mid = (w - hi).astype(_BF16).astype(_F32)
            lo = (w - hi) - mid
            cols.append(jnp.where(lane == 0, hi, jnp.where(lane == 1, mid, jnp.where(lane == 2, lo, 0.0)))
                        .astype(_BF16))
        mine[:, D_MODEL:XS_COLS] = (
            lax.dot_general(sel0, cols[0], tn, preferred_element_type=_F32)
            + lax.dot_general(sel1, cols[1], tn, preferred_element_type=_F32)).astype(_BF16)

    for slot in range(2):
        pl.when(t % 2 == slot)(functools.partial(step, slot))


def _dispatch_call(tab, meta, h3, xs_zero):
    n = h3.shape[0]
    tm = TOKEN_TILE
    ntiles = n // tm
    tile = lambda i, tab: (jnp.minimum(i, ntiles - 1), 0)
    grid_spec = pltpu.PrefetchScalarGridSpec(
        num_scalar_prefetch=1,
        grid=(ntiles + 2,),
        in_specs=[pl.BlockSpec((tm, 128), tile),
                  pl.BlockSpec((tm, D_MODEL), tile),
                  pl.BlockSpec(memory_space=pl.ANY)],
        out_specs=pl.BlockSpec(memory_space=pl.ANY),
        scratch_shapes=[pltpu.VMEM((SORT_ROWS, XS_COLS), _BF16), pltpu.VMEM((SORT_ROWS, XS_COLS), _BF16),
                        pltpu.SemaphoreType.DMA((2, RUN_BITS, N_EXPERTS))],
    )
    return pl.pallas_call(
        functools.partial(_dispatch_kernel, ntiles=ntiles),
        grid_spec=grid_spec,
        out_shape=jax.ShapeDtypeStruct(xs_zero.shape, xs_zero.dtype),
        input_output_aliases={3: 0},
        compiler_params=_params(1),
        name="moe_dispatch",
    )(tab, meta, h3, xs_zero)


def _ffn_kernel(be_ref, nv_ref, xs_ref, wg_ref, wu_ref, wd_ref, ys_ref, wgu_ref, wdn_ref):
    i = pl.program_id(0)

    @pl.when(jnp.logical_or(i == 0, be_ref[i] != be_ref[jnp.maximum(i - 1, 0)]))
    def _():
        wgu_ref[:, 0:D_EXPERT] = wg_ref[...].astype(_BF16)
        wgu_ref[:, D_EXPERT:2 * D_EXPERT] = wu_ref[...].astype(_BF16)
        wdn_ref[...] = wd_ref[...].astype(_BF16)

    @pl.when(i < nv_ref[0])
    def _():
        xb = xs_ref[:, 0:D_MODEL]
        wcols = xs_ref[:, D_MODEL:XS_COLS].astype(_F32)
        wt = wcols[:, 0:1] + wcols[:, 1:2] + wcols[:, 2:3]
        gu = _dot(xb, wgu_ref[...])
        g = gu[:, 0:D_EXPERT]
        a = (g * jax.nn.sigmoid(g)) * gu[:, D_EXPERT:2 * D_EXPERT]
        y = _dot(a.astype(_BF16), wdn_ref[...])
        ys_ref[...] = (y * wt).astype(ys_ref.dtype)

    @pl.when(i >= nv_ref[0])
    def _():
        ys_ref[...] = jnp.zeros_like(ys_ref)


def _ffn_call(blk_expert, n_valid, xs, w_gate, w_up, w_down, layer):
    n_blocks = xs.shape[0] // MOE_BLOCK
    wspec = lambda shape: pl.BlockSpec((None, None) + shape, lambda i, be, nv: (layer, be[i], 0, 0))
    last_live = lambda i, be, nv: (jnp.minimum(i, nv[0] - 1), 0)
    grid_spec = pltpu.PrefetchScalarGridSpec(
        num_scalar_prefetch=2,
        grid=(n_blocks,),
        in_specs=[pl.BlockSpec((MOE_BLOCK, XS_COLS), last_live),
                  wspec((D_MODEL, D_EXPERT)), wspec((D_MODEL, D_EXPERT)), wspec((D_EXPERT, D_MODEL))],
        out_specs=pl.BlockSpec((MOE_BLOCK, D_MODEL), lambda i, be, nv: (i, 0)),
        scratch_shapes=[pltpu.VMEM((D_MODEL, 2 * D_EXPERT), _BF16), pltpu.VMEM((D_EXPERT, D_MODEL), _BF16)],
    )
    return pl.pallas_call(
        _ffn_kernel,
        grid_spec=grid_spec,
        out_shape=jax.ShapeDtypeStruct((xs.shape[0], D_MODEL), _BF16),
        compiler_params=_params(1),
        name="moe_experts",
    )(blk_expert, n_valid, xs, w_gate, w_up, w_down)


def _combine_kernel(tab_ref, meta_ref, x2_ref, ys_ref, out_ref, buf0_ref, buf1_ref, sems, *, ntiles):
    t = pl.program_id(0)
    last = ntiles - 1
    bufs = (buf0_ref, buf1_ref)

    @pl.when(t == 0)
    def _():
        buf0_ref[...] = jnp.zeros_like(buf0_ref)
        buf1_ref[...] = jnp.zeros_like(buf1_ref)

    def step(slot):
        mine, other = bufs[slot], bufs[1 - slot]
        _move_runs(tab_ref, jnp.clip(t - 1, 0, last), t >= 1, other, ys_ref, sems.at[1 - slot],
                   to_remote=False, wait=True)
        _move_runs(tab_ref, jnp.minimum(t, last), t <= last, mine, ys_ref, sems.at[slot],
                   to_remote=False, wait=False)
        sel0, sel1 = _selection(meta_ref[...], SORT_ROWS)
        out_ref[...] = x2_ref[...] + _dot(sel0 + sel1, other[...])

    for slot in range(2):
        pl.when(t % 2 == slot)(functools.partial(step, slot))


def _combine_call(tab, meta, x2, ys):
    n = x2.shape[0]
    tm = TOKEN_TILE
    ntiles = n // tm
    tile = lambda i, tab: (jnp.maximum(i - 1, 0), 0)
    grid_spec = pltpu.PrefetchScalarGridSpec(
        num_scalar_prefetch=1,
        grid=(ntiles + 1,),
        in_specs=[pl.BlockSpec((tm, 128), tile),
                  pl.BlockSpec((tm, D_MODEL), tile),
                  pl.BlockSpec(memory_space=pl.ANY)],
        out_specs=pl.BlockSpec((tm, D_MODEL), tile),
        scratch_shapes=[pltpu.VMEM((SORT_ROWS, D_MODEL), _BF16), pltpu.VMEM((SORT_ROWS, D_MODEL), _BF16),
                        pltpu.SemaphoreType.DMA((2, RUN_BITS, N_EXPERTS))],
    )
    return pl.pallas_call(
        functools.partial(_combine_kernel, ntiles=ntiles),
        grid_spec=grid_spec,
        out_shape=jax.ShapeDtypeStruct((n, D_MODEL), _F32),
        compiler_params=_params(1),
        name="moe_combine",
    )(tab, meta, x2, ys)


def _moe_tables(cnt, n_tokens):
    ntiles = n_tokens // TOKEN_TILE
    counts = cnt.reshape(ntiles, 8, 128)[:, 0, :N_EXPERTS].astype(jnp.int32)
    units = (counts + RUN_ALIGN - 1) // RUN_ALIGN
    padded = units * RUN_ALIGN
    local = jnp.cumsum(padded, axis=1) - padded
    total = jnp.sum(padded, axis=0)
    total_blk = ((total + MOE_BLOCK - 1) // MOE_BLOCK) * MOE_BLOCK
    ends = jnp.cumsum(total_blk)
    starts = ends - total_blk
    remote = starts[None, :] + jnp.cumsum(padded, axis=0) - padded
    bits = jnp.arange(RUN_BITS, dtype=jnp.int32)[None, :, None]
    has = (units[:, None, :] >> bits) & 1
    done = (units[:, None, :] & ((1 << bits) - 1)) * RUN_ALIGN
    rank = jnp.cumsum(has, axis=2) - 1
    place = jnp.logical_and(rank[..., None] == jnp.arange(N_EXPERTS, dtype=jnp.int32), has[..., None] == 1)
    compact = lambda rows: jnp.sum(jnp.where(place, rows[..., None], 0), axis=2)
    tab = jnp.concatenate([jnp.sum(has, axis=2, keepdims=True), compact(local[:, None, :] + done),
                           compact(remote[:, None, :] + done)], axis=-1).reshape(-1).astype(jnp.int32)
    n_blocks = _xs_rows(n_tokens) // MOE_BLOCK
    blk_start = jnp.arange(n_blocks, dtype=jnp.int32) * MOE_BLOCK
    blk_expert = jnp.minimum(jnp.sum((ends[None, :] <= blk_start[:, None]).astype(jnp.int32), axis=1),
                             N_EXPERTS - 1)
    n_valid = (ends[-1] // MOE_BLOCK).astype(jnp.int32).reshape(1)
    return tab, blk_expert, n_valid


def _xs_rows(n_tokens):
    ntiles = n_tokens // TOKEN_TILE
    worst = 2 * n_tokens + ntiles * N_EXPERTS * (RUN_ALIGN - 1) + N_EXPERTS * (MOE_BLOCK - 1)
    return ((worst + MOE_BLOCK - 1) // MOE_BLOCK) * MOE_BLOCK


def _pad_last(a, width):
    return jnp.pad(a, [(0, 0)] * (a.ndim - 1) + [(0, width - a.shape[-1])])


def _swap_mid_heads(a, axis):
    shape = a.shape
    a = a.reshape(shape[:axis] + (4, 64) + shape[axis + 1:])
    a = jnp.take(a, jnp.asarray([0, 2, 1, 3]), axis=axis)
    return a.reshape(shape)


def _layer_weights(l, p):
    w_in = p["w_in"][l]
    a, b = w_in[:, :A_COLS], w_in[:, A_COLS:A_COLS + B_COLS]
    c = w_in[:, A_COLS + B_COLS:A_COLS + B_COLS + C_COLS]
    d = w_in[:, A_COLS + B_COLS + C_COLS:]
    zeros = lambda w: jnp.zeros((D_MODEL, w), _F32)
    half = MLA_ROPE // 2
    k_rope = a[:, 384:416]
    k_rope_swap = jnp.concatenate([k_rope[:, half:], k_rope[:, :half]], axis=1)
    w_in_p = jnp.concatenate([
        a[:, :384], zeros(64), k_rope, zeros(32), zeros(64), k_rope_swap, zeros(32), b,
        _swap_mid_heads(c[:, :256], 1), c[:, 256:384], c[:, 384:512], d], axis=1).astype(_BF16)
    assert w_in_p.shape[1] == IN_COLS_PADDED

    w_uq = p["mla_w_uq"][l].reshape(MLA_Q_RANK, MLA_HEADS, MLA_QK_DIM)
    w_uq_swap = jnp.concatenate([jnp.zeros_like(w_uq[..., :MLA_NOPE]), w_uq[..., MLA_NOPE + half:],
                                 w_uq[..., MLA_NOPE:MLA_NOPE + half]], axis=-1)
    w_uq = jnp.concatenate([_pad_last(w_uq, HEAD_LANES).reshape(MLA_Q_RANK, -1),
                            _pad_last(w_uq_swap, HEAD_LANES).reshape(MLA_Q_RANK, -1)], axis=1).astype(_BF16)

    def swap_gain(g):
        return _pad_last(jnp.concatenate([jnp.zeros((MLA_NOPE,), _F32), g[MLA_NOPE + half:],
                                          g[MLA_NOPE:MLA_NOPE + half]]), HEAD_LANES)[None]
    w_ukv = p["mla_w_ukv"][l].reshape(MLA_KV_RANK, MLA_HEADS, MLA_NOPE + MLA_V)
    k_nope = _pad_last(w_ukv[..., :MLA_NOPE], HEAD_LANES).reshape(MLA_KV_RANK, MLA_HEADS * HEAD_LANES)
    v_part = w_ukv[..., MLA_NOPE:].reshape(MLA_KV_RANK, MLA_HEADS * MLA_V)
    w_ukv_p = jnp.concatenate([k_nope, v_part], axis=1).astype(_BF16)

    eye = jnp.eye(len(POOL_WINDOWS), dtype=_F32)
    w_pool = jnp.einsum("gcd,gh->gchd", p["pool_w"][l], eye).reshape(256, 256).astype(_BF16)

    g_mo = p["mix_out_norm"][l]
    g_mo = jnp.concatenate([g_mo[:512], _swap_mid_heads(g_mo[512:768], 0), g_mo[768:]])
    w_mo = p["w_mix_out"][l]
    w_mo = jnp.concatenate([w_mo[:512], _swap_mid_heads(w_mo[512:768], 0), w_mo[768:]], axis=0)

    w_router = _pad_last(jnp.concatenate([p["w_expert"][l], p["w_group"][l]], axis=1), 128)
    w_router_hi = w_router.astype(_BF16)
    w_router_lo = (w_router - w_router_hi.astype(_F32)).astype(_BF16)
    b_router = jnp.concatenate([p["b_expert"][l], p["b_group"][l]])
    tm = TOKEN_TILE
    ltri = (jnp.arange(tm)[None, :] < jnp.arange(tm)[:, None]).astype(_BF16)
    utri = (jnp.arange(128)[:, None] < jnp.arange(128)[None, :]).astype(_BF16)
    tile2 = lambda g: jnp.concatenate([g, g])[None]
    return dict(
        g_mix=p["norm_mix"][l][None], w_in=w_in_p,
        g_cq=p["mla_g_cq"][l][None], w_uq=w_uq, g_ckv=p["mla_g_ckv"][l][None], w_ukv=w_ukv_p,
        g_q=_pad_last(p["mla_g_q"][l], HEAD_LANES)[None], g_k=_pad_last(p["mla_g_k"][l], HEAD_LANES)[None],
        g_q_swap=swap_gain(p["mla_g_q"][l]), g_k_swap=swap_gain(p["mla_g_k"][l]),
        w_pool=w_pool, pool_scale=p["pool_scale"][l][None], conv_w=p["conv_w"][l],
        conv_b=p["conv_b"][l][None], g_sq=tile2(p["swa_g_q"][l]), g_sk=tile2(p["swa_g_k"][l]),
        sinks=jnp.take(p["swa_sinks"][l], jnp.asarray([0, 2, 1, 3])),
        g_mo=g_mo[None], w_mo=w_mo.astype(_BF16), g_xa=p["norm_xa"][l][None],
        xa_w_q=p["xa_w_q"][l].astype(_BF16), g_xq=jnp.tile(p["xa_g_q"][l], 4)[None],
        xa_w_o=p["xa_w_o"][l].astype(_BF16), g_ffn=p["norm_ffn"][l][None],
        w_router_hi=w_router_hi, w_router_lo=w_router_lo, b_router=_pad_last(b_router, 128)[None],
        ltri=ltri, utri=utri,
    )


def kernel(x, mem, positions, norm_mix, w_in, mla_g_cq, mla_w_uq, mla_g_ckv, mla_w_ukv, mla_g_q, mla_g_k, pool_w, pool_scale, swa_g_q, swa_g_k, swa_sinks, conv_w, conv_b, mix_out_norm, w_mix_out, norm_xa, norm_mem, xa_w_q, xa_w_kv, xa_g_q, xa_g_k, xa_w_o, norm_ffn, w_group, b_group, w_expert, b_expert, w_gate, w_up, w_down):
    p = dict(norm_mix=norm_mix, w_in=w_in, mla_g_cq=mla_g_cq, mla_w_uq=mla_w_uq, mla_g_ckv=mla_g_ckv,
             mla_w_ukv=mla_w_ukv, mla_g_q=mla_g_q, mla_g_k=mla_g_k, pool_w=pool_w, pool_scale=pool_scale,
             swa_g_q=swa_g_q, swa_g_k=swa_g_k, swa_sinks=swa_sinks, conv_w=conv_w, conv_b=conv_b,
             mix_out_norm=mix_out_norm, w_mix_out=w_mix_out, norm_xa=norm_xa, xa_w_q=xa_w_q,
             xa_g_q=xa_g_q, xa_w_o=xa_w_o, norm_ffn=norm_ffn, w_group=w_group, b_group=b_group,
             w_expert=w_expert, b_expert=b_expert)
    batch, seq_len, _ = x.shape
    depth = w_in.shape[0]
    n = batch * seq_len
    assert seq_len % TOKEN_TILE == 0 and seq_len % ATTN_TILE == 0 and n % 1024 == 0
    assert mem.shape[1] == MEM_LEN

    xf = x.reshape(n, D_MODEL)
    tabs = _rope_tables(positions.astype(_F32).reshape(n, 1), n)
    kx, vx = _memkv_call(mem.reshape(batch * MEM_LEN, D_MODEL), norm_mem[:, None, :],
                         xa_w_kv.astype(_BF16), jnp.tile(xa_g_k, (1, 4))[:, None, :], depth, batch)
    for l in range(depth):
        lw = _layer_weights(l, p)
        qm, km, vm, qs, ks, vs, yb, yd = _proj_call(xf, lw, tabs, seq_len)
        ya = _mla_attn_call(qm, km, vm, batch, seq_len)
        yc = _swa_call(qs, ks, vs, lw["sinks"], seq_len)
        x2, h3, meta, cnt = _mix_call(ya, yb, yc, yd, xf, lw, kx[l], vx[l], seq_len)
        tab, blk_expert, n_valid = _moe_tables(cnt, n)
        xs = _dispatch_call(tab, meta, h3, jnp.zeros((_xs_rows(n), XS_COLS), _BF16))
        ys = _ffn_call(blk_expert, n_valid, xs, w_gate, w_up, w_down, l)
        xf = _combine_call(tab, meta, x2, ys)
    return xf.reshape(batch, seq_len, D_MODEL)
```

```python
import functools

import jax
import jax.numpy as jnp
import numpy as np
from jax import lax
from jax.experimental import pallas as pl
from jax.experimental.pallas import tpu as pltpu

EPS = 1e-6
NEG_BIG = -1e30
LOG2_E = 1.4426950408889634
ROPE_THETA = 10000.0

D_MODEL = 1024
MEM_LEN = 256
GROUP_W = 256

MLA_HEADS = 4
MLA_Q_RANK = 256
MLA_KV_RANK = 128
MLA_NOPE = 64
MLA_ROPE = 32
MLA_QK_DIM = MLA_NOPE + MLA_ROPE
MLA_V = 64
HEAD_LANES = 128

POOL_WINDOWS = (2, 4, 8, 16)
POOL_GROUP = 64
POOL_HALO = 16

SWA_HEADS = 4
SWA_KV_HEADS = 2
SWA_HEAD_DIM = 64
SWA_WINDOW = 128

CONV_CH = 256

XA_HEADS = 4
XA_HEAD_DIM = 64

N_EXPERT_GROUPS = 4
EXPERTS_PER_GROUP = 8
N_EXPERTS = 32
D_EXPERT = 256
MOE_BLOCK = 512

A_COLS = MLA_Q_RANK + MLA_KV_RANK + MLA_ROPE
B_COLS = GROUP_W
C_COLS = (SWA_HEADS + 2 * SWA_KV_HEADS) * SWA_HEAD_DIM
COL_CQ, COL_CKV, COL_KROPE, COL_KROPE_SWAP = 0, 256, 384, 512
COL_POOL, COL_SWA_Q, COL_SWA_K, COL_SWA_V, COL_CONV = 640, 896, 1152, 1280, 1408
IN_COLS_PADDED = COL_CONV + 3 * CONV_CH

TOKEN_TILE = 512
PROJ_TILE = 512
MIX_TILE = 1024
ATTN_TILE = 512
RUN_ALIGN = 16
RUN_BITS = 6
SORT_ROWS = 2 * TOKEN_TILE + N_EXPERTS * RUN_ALIGN
PIECE_COLS = 1 + 2 * N_EXPERTS
XS_COLS = D_MODEL + 128
VMEM_LIMIT = 56 * 1024 * 1024

_F32 = jnp.float32
_BF16 = jnp.bfloat16


def _params(n_axes):
    return pltpu.CompilerParams(dimension_semantics=("arbitrary",) * n_axes,
                                vmem_limit_bytes=VMEM_LIMIT)


def _dot(a, b):
    return jnp.dot(a, b, preferred_element_type=_F32)


def _dot_nt(a, b):
    return lax.dot_general(a, b, (((1,), (1,)), ((), ())), preferred_element_type=_F32)


def _rms_factor(x, width):
    ss = jnp.sum(x * x, axis=-1, keepdims=True)
    return lax.rsqrt(ss * (1.0 / width) + EPS)


def _rms_scale(x, width):
    return x * _rms_factor(x, width)


def _pair_norm64(x, lane_lo):
    x2 = x * x
    s_all = jnp.sum(x2, axis=-1, keepdims=True)
    s_lo = jnp.sum(jnp.where(lane_lo, x2, 0.0), axis=-1, keepdims=True)
    ss = jnp.where(lane_lo, s_lo, s_all - s_lo)
    return x * lax.rsqrt(ss * (1.0 / 64.0) + EPS)


def _rope_kernel(pos_ref, freq_ref, sign_ref, c_ref, s_ref):
    ang = pos_ref[...] * freq_ref[...]
    c_ref[...] = jnp.cos(ang)
    s_ref[...] = jnp.sin(ang) * sign_ref[...]


def _rope_tables(posf, n_rows):
    half = MLA_ROPE // 2
    inv_freq = ROPE_THETA ** (-jnp.arange(half, dtype=_F32) / half)
    zeros = jnp.zeros((HEAD_LANES,), _F32)
    freq = zeros.at[MLA_NOPE:MLA_NOPE + half].set(inv_freq).at[MLA_NOPE + half:MLA_QK_DIM].set(inv_freq)
    sign = zeros.at[MLA_NOPE:MLA_NOPE + half].set(-1.0).at[MLA_NOPE + half:MLA_QK_DIM].set(1.0)
    tile = 1024
    row = pl.BlockSpec((1, HEAD_LANES), lambda i: (0, 0))
    tab = pl.BlockSpec((tile, HEAD_LANES), lambda i: (i, 0))
    shp = jax.ShapeDtypeStruct((n_rows, HEAD_LANES), _F32)
    return pl.pallas_call(
        _rope_kernel,
        grid=(n_rows // tile,),
        in_specs=[pl.BlockSpec((tile, 1), lambda i: (i, 0)), row, row],
        out_specs=[tab, tab],
        out_shape=[shp, shp],
        compiler_params=_params(1),
        name="rope_tables",
    )(posf, freq[None], sign[None])


def _proj_kernel(x_ref, gmix_ref, win_ref, gcq_ref, wuq_ref, gckv_ref, wukv_ref, gq_ref, gqs_ref,
                 gk_ref, gks_ref, c_ref, s_ref, wpool_ref, pscale_ref, convw_ref, convb_ref, gsq_ref,
                 gsk_ref, qm_ref, km_ref, vm_ref, qs_ref, ks_ref, vs_ref, yb_ref, yd_ref, halo_ref,
                 *, tiles_per_seq):
    tm = PROJ_TILE
    i = pl.program_id(0)
    seq_tile = i % tiles_per_seq

    x = x_ref[...]
    h = _rms_scale(x, D_MODEL) * gmix_ref[...]
    u = _dot(h.astype(_BF16), win_ref[...])

    c = c_ref[...]
    s = s_ref[...]
    hw = MLA_HEADS * HEAD_LANES

    cq = _rms_scale(u[:, COL_CQ:COL_CQ + MLA_Q_RANK], MLA_Q_RANK) * gcq_ref[...]
    q = _dot(cq.astype(_BF16), wuq_ref[...])
    gq = gq_ref[...] * (MLA_QK_DIM ** -0.5 * LOG2_E)
    gqs = gqs_ref[...] * (MLA_QK_DIM ** -0.5 * LOG2_E)
    for hd in range(MLA_HEADS):
        sl = slice(hd * HEAD_LANES, (hd + 1) * HEAD_LANES)
        xq = q[:, sl]
        r = _rms_factor(xq, MLA_QK_DIM)
        qm_ref[:, sl] = ((xq * r) * (gq * c) + (q[:, hw + sl.start:hw + sl.stop] * r) * (gqs * s)).astype(_BF16)

    ckv = _rms_scale(u[:, COL_CKV:COL_CKV + MLA_KV_RANK], MLA_KV_RANK) * gckv_ref[...]
    kv = _dot(ckv.astype(_BF16), wukv_ref[...])
    krope = u[:, COL_KROPE:COL_KROPE + HEAD_LANES]
    krope_swap = u[:, COL_KROPE_SWAP:COL_KROPE_SWAP + HEAD_LANES]
    gk = gk_ref[...]
    gks = gks_ref[...]
    for hd in range(MLA_HEADS):
        sl = slice(hd * HEAD_LANES, (hd + 1) * HEAD_LANES)
        xk = kv[:, sl] + krope
        r = _rms_factor(xk, MLA_QK_DIM)
        km_ref[:, sl] = ((xk * r) * (gk * c) + (krope_swap * r) * (gks * s)).astype(_BF16)
    ones_lane = (lax.broadcasted_iota(jnp.int32, (1, hw), 1) % HEAD_LANES) == MLA_V
    vm_ref[...] = jnp.where(ones_lane, 1.0, kv[:, hw:2 * hw]).astype(_BF16)

    lane_lo = lax.broadcasted_iota(jnp.int32, (1, HEAD_LANES), 1) < 64
    gsq = gsq_ref[...]
    for blk in range(2):
        xq = u[:, COL_SWA_Q + blk * 128:COL_SWA_Q + (blk + 1) * 128]
        qs_ref[:, blk * 128:(blk + 1) * 128] = (
            _pair_norm64(xq, lane_lo) * gsq * (SWA_HEAD_DIM ** -0.5)).astype(_BF16)
    ks_ref[...] = (_pair_norm64(u[:, COL_SWA_K:COL_SWA_K + 128], lane_lo) * gsk_ref[...]).astype(_BF16)
    vs_ref[...] = u[:, COL_SWA_V:COL_SWA_V + 128].astype(_BF16)

    @pl.when(seq_tile == 0)
    def _():
        halo_ref[...] = jnp.zeros_like(halo_ref)

    halo = halo_ref[...]
    up = u[:, COL_POOL:COL_POOL + GROUP_W]
    ud = u[:, COL_CONV:COL_CONV + 3 * CONV_CH]

    b = jnp.concatenate([halo[:, 0:256], up], axis=0)
    w2 = b + pltpu.roll(b, 1, 0)
    w4 = w2 + pltpu.roll(w2, 2, 0)
    w8 = w4 + pltpu.roll(w4, 4, 0)
    w16 = w8 + pltpu.roll(w8, 8, 0)
    lane = lax.broadcasted_iota(jnp.int32, (1, 256), 1)
    win = jnp.where(lane < 64, w2, jnp.where(lane < 128, w4, jnp.where(lane < 192, w8, w16)))
    win = win[POOL_HALO:, :]
    width = jnp.where(lane < 64, 2, jnp.where(lane < 128, 4, jnp.where(lane < 192, 8, 16)))
    t = seq_tile * tm + lax.broadcasted_iota(jnp.int32, (tm, 1), 0)
    count = jnp.minimum(t + 1, width).astype(_F32)
    pooled = win / count - up
    yb_ref[...] = (_dot(pooled.astype(_BF16), wpool_ref[...]) * pscale_ref[...]).astype(_BF16)

    z = ud[:, 256:512] * ud[:, 512:768]
    zh = halo[:, 512:768] * halo[:, 768:1024]
    zb = jnp.concatenate([zh, z], axis=0)
    cw = convw_ref[...]
    conv = (pltpu.roll(zb, 2, 0)[POOL_HALO:, :] * cw[0:1, :]
            + pltpu.roll(zb, 1, 0)[POOL_HALO:, :] * cw[1:2, :]
            + z * cw[2:3, :])
    yd_ref[...] = (ud[:, 0:256] * (conv + convb_ref[...])).astype(_BF16)

    halo_ref[:, 0:256] = up[tm - POOL_HALO:, :]
    halo_ref[:, 256:1024] = ud[tm - POOL_HALO:, :]


def _proj_call(x, lw, tabs, seq_len):
    n = x.shape[0]
    tm = PROJ_TILE
    tiles_per_seq = seq_len // tm

    def full(a):
        nd = a.ndim
        return pl.BlockSpec(a.shape, lambda i, _nd=nd: (0,) * _nd)

    def rows(width):
        return pl.BlockSpec((tm, width), lambda i: (i, 0))

    consts = [lw["g_mix"], lw["w_in"], lw["g_cq"], lw["w_uq"], lw["g_ckv"], lw["w_ukv"],
              lw["g_q"], lw["g_q_swap"], lw["g_k"], lw["g_k_swap"]]
    consts2 = [lw["w_pool"], lw["pool_scale"], lw["conv_w"], lw["conv_b"], lw["g_sq"], lw["g_sk"]]
    out_widths = [512, 512, 512, 256, 128, 128, 256, 256]
    return pl.pallas_call(
        functools.partial(_proj_kernel, tiles_per_seq=tiles_per_seq),
        grid=(n // tm,),
        in_specs=[rows(D_MODEL)] + [full(a) for a in consts] + [rows(HEAD_LANES)] * 2
                 + [full(a) for a in consts2],
        out_specs=[rows(w) for w in out_widths],
        out_shape=[jax.ShapeDtypeStruct((n, w), _BF16) for w in out_widths],
        scratch_shapes=[pltpu.VMEM((POOL_HALO, 1024), _F32)],
        compiler_params=_params(1),
        name="proj_in",
    )(x, *consts, *tabs, *consts2)


def _mla_attn_kernel(qi_ref, ki_ref, q_ref, k_ref, v_ref, o_ref, m_ref, acc_ref):
    tq = tk = ATTN_TILE
    p_id = pl.program_id(1)
    qi = qi_ref[p_id]
    ki = ki_ref[p_id]

    @pl.when(ki == 0)
    def _():
        m_ref[...] = jnp.full_like(m_ref, NEG_BIG)
        acc_ref[...] = jnp.zeros_like(acc_ref)

    def accumulate(on_diagonal):
        if on_diagonal:
            visible = (lax.broadcasted_iota(jnp.int32, (tq, tk), 1)
                       <= lax.broadcasted_iota(jnp.int32, (tq, tk), 0))
        for hd in range(MLA_HEADS):
            sl = slice(hd * HEAD_LANES, (hd + 1) * HEAD_LANES)
            s = _dot_nt(q_ref[:, sl], k_ref[:, sl])
            if on_diagonal:
                s = jnp.where(visible, s, NEG_BIG)
            m_prev = m_ref[hd]
            m_new = jnp.maximum(m_prev, jnp.max(s, axis=-1, keepdims=True))
            p = jnp.exp2(s - jnp.tile(m_new, (1, tk // HEAD_LANES)))
            m_ref[hd] = m_new
            acc_ref[:, sl] = acc_ref[:, sl] * jnp.exp2(m_prev - m_new) + _dot(p.astype(_BF16), v_ref[:, sl])

    @pl.when(ki < qi)
    def _():
        accumulate(False)

    @pl.when(ki == qi)
    def _():
        accumulate(True)
        outs = []
        for hd in range(MLA_HEADS):
            a = acc_ref[:, hd * HEAD_LANES:(hd + 1) * HEAD_LANES]
            outs.append(a[:, 0:MLA_V] / a[:, MLA_V:MLA_V + 1])
        o_ref[...] = jnp.concatenate(outs, axis=1).astype(o_ref.dtype)


def _mla_attn_call(qm, km, vm, batch, seq_len):
    tq = ATTN_TILE
    nq = seq_len // tq
    pairs = [(q, k) for q in range(nq) for k in range(q + 1)]
    qi = jnp.asarray([p[0] for p in pairs], jnp.int32)
    ki = jnp.asarray([p[1] for p in pairs], jnp.int32)
    grid_spec = pltpu.PrefetchScalarGridSpec(
        num_scalar_prefetch=2,
        grid=(batch, len(pairs)),
        in_specs=[
            pl.BlockSpec((tq, 512), lambda b, p, qi, ki: (b * nq + qi[p], 0)),
            pl.BlockSpec((tq, 512), lambda b, p, qi, ki: (b * nq + ki[p], 0)),
            pl.BlockSpec((tq, 512), lambda b, p, qi, ki: (b * nq + ki[p], 0)),
        ],
        out_specs=pl.BlockSpec((tq, 256), lambda b, p, qi, ki: (b * nq + qi[p], 0)),
        scratch_shapes=[pltpu.VMEM((MLA_HEADS, tq, HEAD_LANES), _F32),
                        pltpu.VMEM((tq, MLA_HEADS * HEAD_LANES), _F32)],
    )
    return pl.pallas_call(
        _mla_attn_kernel,
        grid_spec=grid_spec,
        out_shape=jax.ShapeDtypeStruct((batch * seq_len, 256), _BF16),
        compiler_params=_params(2),
        name="mla_attention",
    )(qi, ki, qm, km, vm)


def _swa_kernel(sink_ref, q_ref, kc_ref, kp_ref, vc_ref, vp_ref, o_ref, *, tiles_per_seq):
    w = SWA_WINDOW
    i = pl.program_id(0)
    oldest = jnp.where((i % tiles_per_seq) == 0, 0, -w)
    lane_lo = lax.broadcasted_iota(jnp.int32, (1, 128), 1) < 64
    qpos = lax.broadcasted_iota(jnp.int32, (w, 2 * w), 0)
    kpos = lax.broadcasted_iota(jnp.int32, (w, 2 * w), 1) - w
    band = jnp.logical_and(kpos <= qpos, kpos > qpos - w)

    for jb in range(TOKEN_TILE // w):
        rs = slice(jb * w, (jb + 1) * w)
        if jb == 0:
            kprev, vprev = kp_ref[...], vp_ref[...]
            visible = jnp.logical_and(band, kpos >= oldest)
        else:
            ps = slice((jb - 1) * w, jb * w)
            kprev, vprev = kc_ref[ps, :], vc_ref[ps, :]
            visible = band
        kk = jnp.concatenate([kprev, kc_ref[rs, :]], axis=0)
        vv = jnp.concatenate([vprev, vc_ref[rs, :]], axis=0)
        zero = jnp.zeros_like(kk)
        k_half = (jnp.where(lane_lo, kk, zero), jnp.where(lane_lo, zero, kk))
        for blk in range(2):
            qblk = q_ref[rs, blk * 128:(blk + 1) * 128]
            outs = []
            for half in range(2):
                sink = sink_ref[2 * blk + half]
                s = _dot_nt(qblk, k_half[half])
                s = jnp.where(visible, s, NEG_BIG)
                m = jnp.maximum(jnp.max(s, axis=-1, keepdims=True), sink)
                p = jnp.exp(s - m)
                denom = jnp.sum(p, axis=-1, keepdims=True) + jnp.exp(sink - m)
                outs.append(_dot(p.astype(_BF16), vv) / denom)
            o_ref[rs, blk * 128:(blk + 1) * 128] = jnp.where(lane_lo, outs[0], outs[1]).astype(o_ref.dtype)


def _swa_call(qs, ks, vs, sinks, seq_len):
    n = qs.shape[0]
    tm = TOKEN_TILE
    per = tm // SWA_WINDOW
    prev = lambda i, s: (jnp.maximum(i * per - 1, 0), 0)
    grid_spec = pltpu.PrefetchScalarGridSpec(
        num_scalar_prefetch=1,
        grid=(n // tm,),
        in_specs=[
            pl.BlockSpec((tm, 256), lambda i, s: (i, 0)),
            pl.BlockSpec((tm, 128), lambda i, s: (i, 0)),
            pl.BlockSpec((SWA_WINDOW, 128), prev),
            pl.BlockSpec((tm, 128), lambda i, s: (i, 0)),
            pl.BlockSpec((SWA_WINDOW, 128), prev),
        ],
        out_specs=pl.BlockSpec((tm, 256), lambda i, s: (i, 0)),
    )
    return pl.pallas_call(
        functools.partial(_swa_kernel, tiles_per_seq=seq_len // tm),
        grid_spec=grid_spec,
        out_shape=jax.ShapeDtypeStruct((n, 256), _BF16),
        compiler_params=_params(1),
        name="swa_attention",
    )(sinks, qs, ks, ks, vs, vs)


def _memkv_kernel(mem_ref, gmem_ref, wkv_ref, gk_ref, k_ref, v_ref):
    m = _rms_scale(mem_ref[...], D_MODEL) * gmem_ref[...]
    kv = _dot(m.astype(_BF16), wkv_ref[...])
    lane_lo = lax.broadcasted_iota(jnp.int32, (1, 128), 1) < 64
    lane = lax.broadcasted_iota(jnp.int32, (1, 256), 1)
    k = jnp.concatenate([_pair_norm64(kv[:, 0:128], lane_lo), _pair_norm64(kv[:, 128:256], lane_lo)],
                        axis=1) * gk_ref[...]
    v = kv[:, 256:512]
    for hd in range(XA_HEADS):
        own = jnp.logical_and(lane >= hd * XA_HEAD_DIM, lane < (hd + 1) * XA_HEAD_DIM)
        k_ref[hd] = jnp.where(own, k, 0.0).astype(_BF16)
        v_ref[hd] = jnp.where(own, v, 0.0).astype(_BF16)


def _memkv_call(mem2d, g_mem, w_kv, g_k4, depth, batch):
    out = jax.ShapeDtypeStruct((depth, batch, XA_HEADS, MEM_LEN, 256), _BF16)
    ospec = pl.BlockSpec((None, None, XA_HEADS, MEM_LEN, 256), lambda l, b: (l, b, 0, 0, 0))
    return pl.pallas_call(
        _memkv_kernel,
        grid=(depth, batch),
        in_specs=[pl.BlockSpec((MEM_LEN, D_MODEL), lambda l, b: (b, 0)),
                  pl.BlockSpec((None, 1, D_MODEL), lambda l, b: (l, 0, 0)),
                  pl.BlockSpec((None, D_MODEL, 512), lambda l, b: (l, 0, 0)),
                  pl.BlockSpec((None, 1, 256), lambda l, b: (l, 0, 0))],
        out_specs=[ospec, ospec],
        out_shape=[out, out],
        compiler_params=_params(2),
        name="memory_kv",
    )(mem2d, g_mem, w_kv, g_k4)


def _mix_kernel(ya_ref, yb_ref, yc_ref, yd_ref, x_ref, gmo_ref, wmo_ref, gxa_ref, wq_ref, gxq_ref,
                kx_ref, vx_ref, wo_ref, gffn_ref, wrh_ref, wrl_ref, br_ref, ltri_ref, utri_ref,
                x2_ref, h3_ref, meta_ref, cnt_ref):
    tm = MIX_TILE
    gmo = gmo_ref[...]
    lane_lo = lax.broadcasted_iota(jnp.int32, (1, 128), 1) < 64

    def rows_to_logits(rs):
        parts = []
        for g, ref in enumerate((ya_ref, yb_ref, yc_ref, yd_ref)):
            yg = ref[rs, :].astype(_F32)
            parts.append((_rms_scale(yg, GROUP_W) * gmo[:, g * 256:(g + 1) * 256]).astype(_BF16))
        y = jnp.concatenate(parts, axis=1)
        x1 = x_ref[rs, :] + _dot(y, wmo_ref[...])

        h = (_rms_scale(x1, D_MODEL) * gxa_ref[...]).astype(_BF16)
        q = _dot(h, wq_ref[...])
        qn = jnp.concatenate([_pair_norm64(q[:, 0:128], lane_lo), _pair_norm64(q[:, 128:256], lane_lo)],
                             axis=1)
        qn = (qn * gxq_ref[...] * (XA_HEAD_DIM ** -0.5)).astype(_BF16)
        o = jnp.zeros((x1.shape[0], 256), _F32)
        for hd in range(XA_HEADS):
            s = _dot_nt(qn, kx_ref[hd])
            m = jnp.max(s, axis=-1, keepdims=True)
            p = jnp.exp(s - m)
            denom = jnp.sum(p, axis=-1, keepdims=True)
            o = o + _dot(p.astype(_BF16), vx_ref[hd]) / denom
        x2 = x1 + _dot(o.astype(_BF16), wo_ref[...])
        x2_ref[rs, :] = x2

        h3 = _rms_scale(x2, D_MODEL) * gffn_ref[...]
        h3_hi = h3.astype(_BF16)
        h3_lo = (h3 - h3_hi.astype(_F32)).astype(_BF16)
        h3_ref[rs, :] = h3_hi
        return (_dot(h3_hi, wrh_ref[...]) + (_dot(h3_hi, wrl_ref[...]) + _dot(h3_lo, wrh_ref[...]))
                + br_ref[...])

    all_logits = rows_to_logits(slice(0, tm))
    for st in range(tm // TOKEN_TILE):
        meta, counts = _route(all_logits[st * TOKEN_TILE:(st + 1) * TOKEN_TILE, :], ltri_ref, utri_ref)
        meta_ref[st * TOKEN_TILE:(st + 1) * TOKEN_TILE, :] = meta
        cnt_ref[st * 8:(st + 1) * 8, :] = jnp.broadcast_to(counts, (8, 128))


def _route(logits, ltri_ref, utri_ref):
    lane = lax.broadcasted_iota(jnp.int32, (TOKEN_TILE, 128), 1).astype(_F32)
    far = 1e9
    is_group = jnp.logical_and(lane >= N_EXPERTS, lane < N_EXPERTS + N_EXPERT_GROUPS)
    gl = jnp.where(is_group, logits, -jnp.inf)
    gmax = jnp.max(gl, axis=-1, keepdims=True)
    gidx = jnp.min(jnp.where(gl == gmax, lane, far), axis=-1, keepdims=True) - N_EXPERTS
    g_w = 1.0 / jnp.sum(jnp.where(is_group, jnp.exp(logits - gmax), 0.0), axis=-1, keepdims=True)
    in_group = jnp.logical_and(lane < N_EXPERTS, jnp.floor(lane * (1.0 / EXPERTS_PER_GROUP)) == gidx)
    el = jnp.where(in_group, logits, -jnp.inf)
    emax = jnp.max(el, axis=-1, keepdims=True)
    ep = jnp.where(in_group, jnp.exp(logits - emax), 0.0)
    prob = jnp.where(in_group, ep / jnp.sum(ep, axis=-1, keepdims=True), -1.0)
    p1 = jnp.max(prob, axis=-1, keepdims=True)
    e0 = jnp.min(jnp.where(prob == p1, lane, far), axis=-1, keepdims=True)
    prob2 = jnp.where(lane == e0, -1.0, prob)
    p2 = jnp.max(prob2, axis=-1, keepdims=True)
    e1 = jnp.min(jnp.where(prob2 == p2, lane, far), axis=-1, keepdims=True)
    w0 = g_w * (p1 / (p1 + p2))
    w1 = g_w * (p2 / (p1 + p2))

    onehot = jnp.where(jnp.logical_or(lane == e0, lane == e1), 1.0, 0.0)
    prefix = _dot(ltri_ref[...], onehot.astype(_BF16))
    counts = jnp.sum(onehot, axis=0, keepdims=True)
    units = jnp.floor((counts + (RUN_ALIGN - 1)) * (1.0 / RUN_ALIGN))
    offs = _dot(jnp.broadcast_to(units, (8, 128)).astype(_BF16), utri_ref[...])[0:1, :] * RUN_ALIGN
    where_to = prefix + offs
    pos0 = jnp.sum(jnp.where(lane == e0, where_to, 0.0), axis=-1, keepdims=True)
    pos1 = jnp.sum(jnp.where(lane == e1, where_to, 0.0), axis=-1, keepdims=True)

    meta = jnp.where(lane == 0, e0,
           jnp.where(lane == 1, e1,
           jnp.where(lane == 2, w0,
           jnp.where(lane == 3, w1,
           jnp.where(lane == 4, pos0,
           jnp.where(lane == 5, pos1, 0.0))))))
    return meta, counts


def _mix_call(ya, yb, yc, yd, x, lw, kx, vx, seq_len):
    n = x.shape[0]
    tm = MIX_TILE
    tiles_per_seq = seq_len // tm

    def full(a):
        nd = a.ndim
        return pl.BlockSpec(a.shape, lambda i, _nd=nd: (0,) * _nd)

    def rows(width):
        return pl.BlockSpec((tm, width), lambda i: (i, 0))

    kvspec = pl.BlockSpec((None, XA_HEADS, MEM_LEN, 256), lambda i: (i // tiles_per_seq, 0, 0, 0))
    consts_a = [lw["g_mo"], lw["w_mo"], lw["g_xa"], lw["xa_w_q"], lw["g_xq"]]
    consts_b = [lw["xa_w_o"], lw["g_ffn"], lw["w_router_hi"], lw["w_router_lo"], lw["b_router"],
                lw["ltri"], lw["utri"]]
    ntiles = n // tm
    return pl.pallas_call(
        _mix_kernel,
        grid=(ntiles,),
        in_specs=[rows(256)] * 4 + [rows(D_MODEL)] + [full(a) for a in consts_a]
                 + [kvspec, kvspec] + [full(a) for a in consts_b],
        out_specs=[rows(D_MODEL), rows(D_MODEL), rows(128),
                   pl.BlockSpec((8 * (tm // TOKEN_TILE), 128), lambda i: (i, 0))],
        out_shape=[jax.ShapeDtypeStruct((n, D_MODEL), _F32), jax.ShapeDtypeStruct((n, D_MODEL), _BF16),
                   jax.ShapeDtypeStruct((n, 128), _F32),
                   jax.ShapeDtypeStruct((n // TOKEN_TILE * 8, 128), _F32)],
        compiler_params=_params(1),
        name="mix_xattn_router",
    )(ya, yb, yc, yd, x, *consts_a, kx, vx, *consts_b)


def _move_runs(tab_ref, t, live, local_ref, remote_ref, sems, *, to_remote, wait):
    for bit in range(RUN_BITS):
        rows = RUN_ALIGN << bit
        base = (t * RUN_BITS + bit) * PIECE_COLS
        count = jnp.where(live, tab_ref[base], 0)

        def one(p, carry, base=base, rows=rows, bit=bit):
            loc = pl.multiple_of(tab_ref[base + 1 + p], RUN_ALIGN)
            rem = pl.multiple_of(tab_ref[base + 1 + N_EXPERTS + p], RUN_ALIGN)
            lsl = local_ref.at[pl.ds(loc, rows)]
            rsl = remote_ref.at[pl.ds(rem, rows)]
            src, dst = (lsl, rsl) if to_remote else (rsl, lsl)
            copy = pltpu.make_async_copy(src, dst, sems.at[bit, p])
            if wait:
                copy.wait()
            else:
                copy.start()
            return carry

        lax.fori_loop(0, count, one, 0)


def _selection(meta, rows):
    r = lax.broadcasted_iota(jnp.int32, (TOKEN_TILE, rows), 1)
    pos0 = meta[:, 4:5].astype(jnp.int32)
    pos1 = meta[:, 5:6].astype(jnp.int32)
    return jnp.where(r == pos0, 1.0, 0.0).astype(_BF16), jnp.where(r == pos1, 1.0, 0.0).astype(_BF16)


def _dispatch_kernel(tab_ref, meta_ref, h3_ref, xs_in_ref, xs_ref, buf0_ref, buf1_ref, sems, *, ntiles):
    del xs_in_ref
    t = pl.program_id(0)
    last = ntiles - 1
    bufs = (buf0_ref, buf1_ref)

    def step(slot):
        mine, other = bufs[slot], bufs[1 - slot]
        _move_runs(tab_ref, jnp.clip(t - 2, 0, last), t >= 2, mine, xs_ref, sems.at[slot],
                   to_remote=True, wait=True)
        _move_runs(tab_ref, jnp.clip(t - 1, 0, last), jnp.logical_and(t >= 1, t <= ntiles), other, xs_ref,
                   sems.at[1 - slot], to_remote=True, wait=False)

        meta = meta_ref[...]
        sel0, sel1 = _selection(meta, SORT_ROWS)
        tn = (((0,), (0,)), ((), ()))
        mine[:, 0:D_MODEL] = lax.dot_general(
            sel0 + sel1, h3_ref[...], tn, preferred_element_type=_F32).astype(_BF16)
        lane = lax.broadcasted_iota(jnp.int32, (TOKEN_TILE, 128), 1)
        cols = []
        for k in range(2):
            w = meta[:, 2 + k:3 + k]
            hi = w.astype(_BF16).astype(_F32)
            mid = (w - hi).astype(_BF16).astype(_F32)
            lo = (w - hi) - mid
            cols.append(jnp.where(lane == 0, hi, jnp.where(lane == 1, mid, jnp.where(lane == 2, lo, 0.0)))
                        .astype(_BF16))
        mine[:, D_MODEL:XS_COLS] = (
            lax.dot_general(sel0, cols[0], tn, preferred_element_type=_F32)
            + lax.dot_general(sel1, cols[1], tn, preferred_element_type=_F32)).astype(_BF16)

    for slot in range(2):
        pl.when(t % 2 == slot)(functools.partial(step, slot))


def _dispatch_call(tab, meta, h3, xs_zero):
    n = h3.shape[0]
    tm = TOKEN_TILE
    ntiles = n // tm
    tile = lambda i, tab: (jnp.minimum(i, ntiles - 1), 0)
    grid_spec = pltpu.PrefetchScalarGridSpec(
        num_scalar_prefetch=1,
        grid=(ntiles + 2,),
        in_specs=[pl.BlockSpec((tm, 128), tile),
                  pl.BlockSpec((tm, D_MODEL), tile),
                  pl.BlockSpec(memory_space=pl.ANY)],
        out_specs=pl.BlockSpec(memory_space=pl.ANY),
        scratch_shapes=[pltpu.VMEM((SORT_ROWS, XS_COLS), _BF16), pltpu.VMEM((SORT_ROWS, XS_COLS), _BF16),
                        pltpu.SemaphoreType.DMA((2, RUN_BITS, N_EXPERTS))],
    )
    return pl.pallas_call(
        functools.partial(_dispatch_kernel, ntiles=ntiles),
        grid_spec=grid_spec,
        out_shape=jax.ShapeDtypeStruct(xs_zero.shape, xs_zero.dtype),
        input_output_aliases={3: 0},
        compiler_params=_params(1),
        name="moe_dispatch",
    )(tab, meta, h3, xs_zero)


def _ffn_kernel(be_ref, nv_ref, xs_ref, wg_ref, wu_ref, wd_ref, ys_ref, wgu_ref, wdn_ref):
    i = pl.program_id(0)

    @pl.when(jnp.logical_or(i == 0, be_ref[i] != be_ref[jnp.maximum(i - 1, 0)]))
    def _():
        wgu_ref[:, 0:D_EXPERT] = wg_ref[...].astype(_BF16)
        wgu_ref[:, D_EXPERT:2 * D_EXPERT] = wu_ref[...].astype(_BF16)
        wdn_ref[...] = wd_ref[...].astype(_BF16)

    @pl.when(i < nv_ref[0])
    def _():
        xb = xs_ref[:, 0:D_MODEL]
        wcols = xs_ref[:, D_MODEL:XS_COLS].astype(_F32)
        wt = wcols[:, 0:1] + wcols[:, 1:2] + wcols[:, 2:3]
        gu = _dot(xb, wgu_ref[...])
        g = gu[:, 0:D_EXPERT]
        a = (g * jax.nn.sigmoid(g)) * gu[:, D_EXPERT:2 * D_EXPERT]
        y = _dot(a.astype(_BF16), wdn_ref[...])
        ys_ref[...] = (y * wt).astype(ys_ref.dtype)

    @pl.when(i >= nv_ref[0])
    def _():
        ys_ref[...] = jnp.zeros_like(ys_ref)


def _ffn_call(blk_expert, n_valid, xs, w_gate, w_up, w_down, layer):
    n_blocks = xs.shape[0] // MOE_BLOCK
    wspec = lambda shape: pl.BlockSpec((None, None) + shape, lambda i, be, nv: (layer, be[i], 0, 0))
    last_live = lambda i, be, nv: (jnp.minimum(i, nv[0] - 1), 0)
    grid_spec = pltpu.PrefetchScalarGridSpec(
        num_scalar_prefetch=2,
        grid=(n_blocks,),
        in_specs=[pl.BlockSpec((MOE_BLOCK, XS_COLS), last_live),
                  wspec((D_MODEL, D_EXPERT)), wspec((D_MODEL, D_EXPERT)), wspec((D_EXPERT, D_MODEL))],
        out_specs=pl.BlockSpec((MOE_BLOCK, D_MODEL), lambda i, be, nv: (i, 0)),
        scratch_shapes=[pltpu.VMEM((D_MODEL, 2 * D_EXPERT), _BF16), pltpu.VMEM((D_EXPERT, D_MODEL), _BF16)],
    )
    return pl.pallas_call(
        _ffn_kernel,
        grid_spec=grid_spec,
        out_shape=jax.ShapeDtypeStruct((xs.shape[0], D_MODEL), _BF16),
        compiler_params=_params(1),
        name="moe_experts",
    )(blk_expert, n_valid, xs, w_gate, w_up, w_down)


def _combine_kernel(tab_ref, meta_ref, x2_ref, ys_ref, out_ref, buf0_ref, buf1_ref, sems, *, ntiles):
    t = pl.program_id(0)
    last = ntiles - 1
    bufs = (buf0_ref, buf1_ref)

    @pl.when(t == 0)
    def _():
        buf0_ref[...] = jnp.zeros_like(buf0_ref)
        buf1_ref[...] = jnp.zeros_like(buf1_ref)

    def step(slot):
        mine, other = bufs[slot], bufs[1 - slot]
        _move_runs(tab_ref, jnp.clip(t - 1, 0, last), t >= 1, other, ys_ref, sems.at[1 - slot],
                   to_remote=False, wait=True)
        _move_runs(tab_ref, jnp.minimum(t, last), t <= last, mine, ys_ref, sems.at[slot],
                   to_remote=False, wait=False)
        sel0, sel1 = _selection(meta_ref[...], SORT_ROWS)
        out_ref[...] = x2_ref[...] + _dot(sel0 + sel1, other[...])

    for slot in range(2):
        pl.when(t % 2 == slot)(functools.partial(step, slot))


def _combine_call(tab, meta, x2, ys):
    n = x2.shape[0]
    tm = TOKEN_TILE
    ntiles = n // tm
    tile = lambda i, tab: (jnp.maximum(i - 1, 0), 0)
    grid_spec = pltpu.PrefetchScalarGridSpec(
        num_scalar_prefetch=1,
        grid=(ntiles + 1,),
        in_specs=[pl.BlockSpec((tm, 128), tile),
                  pl.BlockSpec((tm, D_MODEL), tile),
                  pl.BlockSpec(memory_space=pl.ANY)],
        out_specs=pl.BlockSpec((tm, D_MODEL), tile),
        scratch_shapes=[pltpu.VMEM((SORT_ROWS, D_MODEL), _BF16), pltpu.VMEM((SORT_ROWS, D_MODEL), _BF16),
                        pltpu.SemaphoreType.DMA((2, RUN_BITS, N_EXPERTS))],
    )
    return pl.pallas_call(
        functools.partial(_combine_kernel, ntiles=ntiles),
        grid_spec=grid_spec,
        out_shape=jax.ShapeDtypeStruct((n, D_MODEL), _F32),
        compiler_params=_params(1),
        name="moe_combine",
    )(tab, meta, x2, ys)


def _moe_tables(cnt, n_tokens):
    ntiles = n_tokens // TOKEN_TILE
    counts = cnt.reshape(ntiles, 8, 128)[:, 0, :N_EXPERTS].astype(jnp.int32)
    units = (counts + RUN_ALIGN - 1) // RUN_ALIGN
    padded = units * RUN_ALIGN
    local = jnp.cumsum(padded, axis=1) - padded
    total = jnp.sum(padded, axis=0)
    total_blk = ((total + MOE_BLOCK - 1) // MOE_BLOCK) * MOE_BLOCK
    ends = jnp.cumsum(total_blk)
    starts = ends - total_blk
    remote = starts[None, :] + jnp.cumsum(padded, axis=0) - padded
    bits = jnp.arange(RUN_BITS, dtype=jnp.int32)[None, :, None]
    has = (units[:, None, :] >> bits) & 1
    done = (units[:, None, :] & ((1 << bits) - 1)) * RUN_ALIGN
    rank = jnp.cumsum(has, axis=2) - 1
    place = jnp.logical_and(rank[..., None] == jnp.arange(N_EXPERTS, dtype=jnp.int32), has[..., None] == 1)
    compact = lambda rows: jnp.sum(jnp.where(place, rows[..., None], 0), axis=2)
    tab = jnp.concatenate([jnp.sum(has, axis=2, keepdims=True), compact(local[:, None, :] + done),
                           compact(remote[:, None, :] + done)], axis=-1).reshape(-1).astype(jnp.int32)
    n_blocks = _xs_rows(n_tokens) // MOE_BLOCK
    blk_start = jnp.arange(n_blocks, dtype=jnp.int32) * MOE_BLOCK
    blk_expert = jnp.minimum(jnp.sum((ends[None, :] <= blk_start[:, None]).astype(jnp.int32), axis=1),
                             N_EXPERTS - 1)
    n_valid = (ends[-1] // MOE_BLOCK).astype(jnp.int32).reshape(1)
    return tab, blk_expert, n_valid


def _xs_rows(n_tokens):
    ntiles = n_tokens // TOKEN_TILE
    worst = 2 * n_tokens + ntiles * N_EXPERTS * (RUN_ALIGN - 1) + N_EXPERTS * (MOE_BLOCK - 1)
    return ((worst + MOE_BLOCK - 1) // MOE_BLOCK) * MOE_BLOCK


def _pad_last(a, width):
    return jnp.pad(a, [(0, 0)] * (a.ndim - 1) + [(0, width - a.shape[-1])])


def _swap_mid_heads(a, axis):
    shape = a.shape
    a = a.reshape(shape[:axis] + (4, 64) + shape[axis + 1:])
    a = jnp.take(a, jnp.asarray([0, 2, 1, 3]), axis=axis)
    return a.reshape(shape)


def _layer_weights(l, p):
    w_in = p["w_in"][l]
    a, b = w_in[:, :A_COLS], w_in[:, A_COLS:A_COLS + B_COLS]
    c = w_in[:, A_COLS + B_COLS:A_COLS + B_COLS + C_COLS]
    d = w_in[:, A_COLS + B_COLS + C_COLS:]
    zeros = lambda w: jnp.zeros((D_MODEL, w), _F32)
    half = MLA_ROPE // 2
    k_rope = a[:, 384:416]
    k_rope_swap = jnp.concatenate([k_rope[:, half:], k_rope[:, :half]], axis=1)
    w_in_p = jnp.concatenate([
        a[:, :384], zeros(64), k_rope, zeros(32), zeros(64), k_rope_swap, zeros(32), b,
        _swap_mid_heads(c[:, :256], 1), c[:, 256:384], c[:, 384:512], d], axis=1).astype(_BF16)
    assert w_in_p.shape[1] == IN_COLS_PADDED

    w_uq = p["mla_w_uq"][l].reshape(MLA_Q_RANK, MLA_HEADS, MLA_QK_DIM)
    w_uq_swap = jnp.concatenate([jnp.zeros_like(w_uq[..., :MLA_NOPE]), w_uq[..., MLA_NOPE + half:],
                                 w_uq[..., MLA_NOPE:MLA_NOPE + half]], axis=-1)
    w_uq = jnp.concatenate([_pad_last(w_uq, HEAD_LANES).reshape(MLA_Q_RANK, -1),
                            _pad_last(w_uq_swap, HEAD_LANES).reshape(MLA_Q_RANK, -1)], axis=1).astype(_BF16)

    def swap_gain(g):
        return _pad_last(jnp.concatenate([jnp.zeros((MLA_NOPE,), _F32), g[MLA_NOPE + half:],
                                          g[MLA_NOPE:MLA_NOPE + half]]), HEAD_LANES)[None]
    w_ukv = p["mla_w_ukv"][l].reshape(MLA_KV_RANK, MLA_HEADS, MLA_NOPE + MLA_V)
    k_nope = _pad_last(w_ukv[..., :MLA_NOPE], HEAD_LANES).reshape(MLA_KV_RANK, MLA_HEADS * HEAD_LANES)
    v_part = _pad_last(w_ukv[..., MLA_NOPE:], HEAD_LANES).reshape(MLA_KV_RANK, MLA_HEADS * HEAD_LANES)
    w_ukv_p = jnp.concatenate([k_nope, v_part], axis=1).astype(_BF16)

    eye = jnp.eye(len(POOL_WINDOWS), dtype=_F32)
    w_pool = jnp.einsum("gcd,gh->gchd", p["pool_w"][l], eye).reshape(256, 256).astype(_BF16)

    g_mo = p["mix_out_norm"][l]
    g_mo = jnp.concatenate([g_mo[:512], _swap_mid_heads(g_mo[512:768], 0), g_mo[768:]])
    w_mo = p["w_mix_out"][l]
    w_mo = jnp.concatenate([w_mo[:512], _swap_mid_heads(w_mo[512:768], 0), w_mo[768:]], axis=0)

    w_router = _pad_last(jnp.concatenate([p["w_expert"][l], p["w_group"][l]], axis=1), 128)
    w_router_hi = w_router.astype(_BF16)
    w_router_lo = (w_router - w_router_hi.astype(_F32)).astype(_BF16)
    b_router = jnp.concatenate([p["b_expert"][l], p["b_group"][l]])
    tm = TOKEN_TILE
    ltri = (jnp.arange(tm)[None, :] < jnp.arange(tm)[:, None]).astype(_BF16)
    utri = (jnp.arange(128)[:, None] < jnp.arange(128)[None, :]).astype(_BF16)
    tile2 = lambda g: jnp.concatenate([g, g])[None]
    return dict(
        g_mix=p["norm_mix"][l][None], w_in=w_in_p,
        g_cq=p["mla_g_cq"][l][None], w_uq=w_uq, g_ckv=p["mla_g_ckv"][l][None], w_ukv=w_ukv_p,
        g_q=_pad_last(p["mla_g_q"][l], HEAD_LANES)[None], g_k=_pad_last(p["mla_g_k"][l], HEAD_LANES)[None],
        g_q_swap=swap_gain(p["mla_g_q"][l]), g_k_swap=swap_gain(p["mla_g_k"][l]),
        w_pool=w_pool, pool_scale=p["pool_scale"][l][None], conv_w=p["conv_w"][l],
        conv_b=p["conv_b"][l][None], g_sq=tile2(p["swa_g_q"][l]), g_sk=tile2(p["swa_g_k"][l]),
        sinks=jnp.take(p["swa_sinks"][l], jnp.asarray([0, 2, 1, 3])),
        g_mo=g_mo[None], w_mo=w_mo.astype(_BF16), g_xa=p["norm_xa"][l][None],
        xa_w_q=p["xa_w_q"][l].astype(_BF16), g_xq=jnp.tile(p["xa_g_q"][l], 4)[None],
        xa_w_o=p["xa_w_o"][l].astype(_BF16), g_ffn=p["norm_ffn"][l][None],
        w_router_hi=w_router_hi, w_router_lo=w_router_lo, b_router=_pad_last(b_router, 128)[None],
        ltri=ltri, utri=utri,
    )


def kernel(x, mem, positions, norm_mix, w_in, mla_g_cq, mla_w_uq, mla_g_ckv, mla_w_ukv, mla_g_q, mla_g_k, pool_w, pool_scale, swa_g_q, swa_g_k, swa_sinks, conv_w, conv_b, mix_out_norm, w_mix_out, norm_xa, norm_mem, xa_w_q, xa_w_kv, xa_g_q, xa_g_k, xa_w_o, norm_ffn, w_group, b_group, w_expert, b_expert, w_gate, w_up, w_down):
    p = dict(norm_mix=norm_mix, w_in=w_in, mla_g_cq=mla_g_cq, mla_w_uq=mla_w_uq, mla_g_ckv=mla_g_ckv,
             mla_w_ukv=mla_w_ukv, mla_g_q=mla_g_q, mla_g_k=mla_g_k, pool_w=pool_w, pool_scale=pool_scale,
             swa_g_q=swa_g_q, swa_g_k=swa_g_k, swa_sinks=swa_sinks, conv_w=conv_w, conv_b=conv_b,
             mix_out_norm=mix_out_norm, w_mix_out=w_mix_out, norm_xa=norm_xa, xa_w_q=xa_w_q,
             xa_g_q=xa_g_q, xa_w_o=xa_w_o, norm_ffn=norm_ffn, w_group=w_group, b_group=b_group,
             w_expert=w_expert, b_expert=b_expert)
    batch, seq_len, _ = x.shape
    depth = w_in.shape[0]
    n = batch * seq_len
    assert seq_len % MIX_TILE == 0 and MIX_TILE % TOKEN_TILE == 0
    assert seq_len % PROJ_TILE == 0 and seq_len % ATTN_TILE == 0
    assert mem.shape[1] == MEM_LEN

    xf = x.reshape(n, D_MODEL)
    tabs = _rope_tables(positions.astype(_F32).reshape(n, 1), n)
    kx, vx = _memkv_call(mem.reshape(batch * MEM_LEN, D_MODEL), norm_mem[:, None, :],
                         xa_w_kv.astype(_BF16), jnp.tile(xa_g_k, (1, 4))[:, None, :], depth, batch)
    for l in range(depth):
        lw = _layer_weights(l, p)
        qm, km, vm, qs, ks, vs, yb, yd = _proj_call(xf, lw, tabs, seq_len)
        ya = _mla_attn_call(qm, km, vm, batch, seq_len)
        yc = _swa_call(qs, ks, vs, lw["sinks"], seq_len)
        x2, h3, meta, cnt = _mix_call(ya, yb, yc, yd, xf, lw, kx[l], vx[l], seq_len)
        tab, blk_expert, n_valid = _moe_tables(cnt, n)
        xs = _dispatch_call(tab, meta, h3, jnp.zeros((_xs_rows(n), XS_COLS), _BF16))
        ys = _ffn_call(blk_expert, n_valid, xs, w_gate, w_up, w_down, l)
        xf = _combine_call(tab, meta, x2, ys)
    return xf.reshape(batch, seq_len, D_MODEL)
```

```python
import functools

import jax
import jax.numpy as jnp
import numpy as np
from jax import lax
from jax.experimental import pallas as pl
from jax.experimental.pallas import tpu as pltpu

EPS = 1e-6
NEG_BIG = -1e30
LOG2_E = 1.4426950408889634
ROPE_THETA = 10000.0

D_MODEL = 1024
MEM_LEN = 256
GROUP_W = 256

MLA_HEADS = 4
MLA_Q_RANK = 256
MLA_KV_RANK = 128
MLA_NOPE = 64
MLA_ROPE = 32
MLA_QK_DIM = MLA_NOPE + MLA_ROPE
MLA_V = 64
HEAD_LANES = 128

POOL_WINDOWS = (2, 4, 8, 16)
POOL_GROUP = 64
POOL_HALO = 16

SWA_HEADS = 4
SWA_KV_HEADS = 2
SWA_HEAD_DIM = 64
SWA_WINDOW = 128

CONV_CH = 256

XA_HEADS = 4
XA_HEAD_DIM = 64

N_EXPERT_GROUPS = 4
EXPERTS_PER_GROUP = 8
N_EXPERTS = 32
D_EXPERT = 256
MOE_BLOCK = 512

A_COLS = MLA_Q_RANK + MLA_KV_RANK + MLA_ROPE
B_COLS = GROUP_W
C_COLS = (SWA_HEADS + 2 * SWA_KV_HEADS) * SWA_HEAD_DIM
COL_CQ, COL_CKV, COL_KROPE, COL_KROPE_SWAP = 0, 256, 384, 512
COL_POOL, COL_SWA_Q, COL_SWA_K, COL_SWA_V, COL_CONV = 640, 896, 1152, 1280, 1408
IN_COLS_PADDED = COL_CONV + 3 * CONV_CH

TOKEN_TILE = 512
PROJ_TILE = 512
MIX_TILE = 1024
ATTN_TILE = 512
RUN_ALIGN = 16
RUN_BITS = 6
SORT_ROWS = 2 * TOKEN_TILE + N_EXPERTS * RUN_ALIGN
SORT_CHUNK = 256
PIECE_COLS = 1 + 2 * N_EXPERTS
XS_COLS = D_MODEL + 128
VMEM_LIMIT = 56 * 1024 * 1024

_F32 = jnp.float32
_BF16 = jnp.bfloat16


def _params(n_axes):
    return pltpu.CompilerParams(dimension_semantics=("arbitrary",) * n_axes,
                                vmem_limit_bytes=VMEM_LIMIT)


def _dot(a, b):
    return jnp.dot(a, b, preferred_element_type=_F32)


def _dot_nt(a, b):
    return lax.dot_general(a, b, (((1,), (1,)), ((), ())), preferred_element_type=_F32)


def _rms_factor(x, width):
    ss = jnp.sum(x * x, axis=-1, keepdims=True)
    return lax.rsqrt(ss * (1.0 / width) + EPS)


def _rms_scale(x, width):
    return x * _rms_factor(x, width)


def _pair_norm64(x, lane_lo):
    x2 = x * x
    s_all = jnp.sum(x2, axis=-1, keepdims=True)
    s_lo = jnp.sum(jnp.where(lane_lo, x2, 0.0), axis=-1, keepdims=True)
    ss = jnp.where(lane_lo, s_lo, s_all - s_lo)
    return x * lax.rsqrt(ss * (1.0 / 64.0) + EPS)


def _rope_kernel(pos_ref, freq_ref, c_ref, s_ref):
    ang = pos_ref[...] * freq_ref[...]
    c_ref[...] = jnp.cos(ang)
    s_ref[...] = jnp.sin(ang)


def _rope_tables(positions):
    half = MLA_ROPE // 2
    per_row = HEAD_LANES // half
    n = positions.size
    inv_freq = ROPE_THETA ** (-jnp.arange(half, dtype=_F32) / half)
    pos = jnp.repeat(positions.astype(_F32).reshape(n // per_row, per_row), half, axis=1)
    rows = n // per_row
    tile = min(rows, 1024)
    tab = pl.BlockSpec((tile, HEAD_LANES), lambda i: (i, 0))
    shp = jax.ShapeDtypeStruct((rows, HEAD_LANES), _F32)
    cos, sin = pl.pallas_call(
        _rope_kernel,
        grid=(rows // tile,),
        in_specs=[tab, pl.BlockSpec((1, HEAD_LANES), lambda i: (0, 0))],
        out_specs=[tab, tab],
        out_shape=[shp, shp],
        compiler_params=_params(1),
        name="rope_tables",
    )(pos, jnp.tile(inv_freq, per_row)[None])
    cos, sin = cos.reshape(n, half), sin.reshape(n, half)
    pad = jnp.zeros((n, HEAD_LANES - MLA_QK_DIM), _F32)
    c_tab = jnp.concatenate([jnp.ones((n, MLA_NOPE), _F32), cos, cos, pad], axis=1)
    s_tab = jnp.concatenate([jnp.zeros((n, MLA_NOPE), _F32), -sin, sin, pad], axis=1)
    return c_tab, s_tab


def _proj_kernel(x_ref, gmix_ref, win_ref, gcq_ref, wuq_ref, gckv_ref, wukv_ref, gq_ref, gqs_ref,
                 gk_ref, gks_ref, c_ref, s_ref, wpool_ref, pscale_ref, convw_ref, convb_ref, gsq_ref,
                 gsk_ref, qm_ref, km_ref, vm_ref, qs_ref, ks_ref, vs_ref, yb_ref, yd_ref, halo_ref,
                 *, tiles_per_seq):
    tm = PROJ_TILE
    i = pl.program_id(0)
    seq_tile = i % tiles_per_seq

    x = x_ref[...]
    h = _rms_scale(x, D_MODEL) * gmix_ref[...]
    u = _dot(h.astype(_BF16), win_ref[...])

    c = c_ref[...]
    s = s_ref[...]
    hw = MLA_HEADS * HEAD_LANES

    cq = _rms_scale(u[:, COL_CQ:COL_CQ + MLA_Q_RANK], MLA_Q_RANK) * gcq_ref[...]
    q = _dot(cq.astype(_BF16), wuq_ref[...])
    gq = gq_ref[...] * (MLA_QK_DIM ** -0.5 * LOG2_E)
    gqs = gqs_ref[...] * (MLA_QK_DIM ** -0.5 * LOG2_E)
    for hd in range(MLA_HEADS):
        sl = slice(hd * HEAD_LANES, (hd + 1) * HEAD_LANES)
        xq = q[:, sl]
        r = _rms_factor(xq, MLA_QK_DIM)
        qm_ref[:, sl] = ((xq * r) * (gq * c) + (q[:, hw + sl.start:hw + sl.stop] * r) * (gqs * s)).astype(_BF16)

    ckv = _rms_scale(u[:, COL_CKV:COL_CKV + MLA_KV_RANK], MLA_KV_RANK) * gckv_ref[...]
    kv = _dot(ckv.astype(_BF16), wukv_ref[...])
    krope = u[:, COL_KROPE:COL_KROPE + HEAD_LANES]
    krope_swap = u[:, COL_KROPE_SWAP:COL_KROPE_SWAP + HEAD_LANES]
    gk = gk_ref[...]
    gks = gks_ref[...]
    for hd in range(MLA_HEADS):
        sl = slice(hd * HEAD_LANES, (hd + 1) * HEAD_LANES)
        xk = kv[:, sl] + krope
        r = _rms_factor(xk, MLA_QK_DIM)
        km_ref[:, sl] = ((xk * r) * (gk * c) + (krope_swap * r) * (gks * s)).astype(_BF16)
    vm_ref[...] = kv[:, hw:hw + MLA_HEADS * MLA_V].astype(_BF16)

    lane_lo = lax.broadcasted_iota(jnp.int32, (1, HEAD_LANES), 1) < 64
    gsq = gsq_ref[...]
    for blk in range(2):
        xq = u[:, COL_SWA_Q + blk * 128:COL_SWA_Q + (blk + 1) * 128]
        qs_ref[:, blk * 128:(blk + 1) * 128] = (
            _pair_norm64(xq, lane_lo) * gsq * (SWA_HEAD_DIM ** -0.5)).astype(_BF16)
    ks_ref[...] = (_pair_norm64(u[:, COL_SWA_K:COL_SWA_K + 128], lane_lo) * gsk_ref[...]).astype(_BF16)
    vs_ref[...] = u[:, COL_SWA_V:COL_SWA_V + 128].astype(_BF16)

    @pl.when(seq_tile == 0)
    def _():
        halo_ref[...] = jnp.zeros_like(halo_ref)

    halo = halo_ref[...]
    up = u[:, COL_POOL:COL_POOL + GROUP_W]
    ud = u[:, COL_CONV:COL_CONV + 3 * CONV_CH]

    b = jnp.concatenate([halo[:, 0:256], up], axis=0)
    w2 = b + pltpu.roll(b, 1, 0)
    w4 = w2 + pltpu.roll(w2, 2, 0)
    w8 = w4 + pltpu.roll(w4, 4, 0)
    w16 = w8 + pltpu.roll(w8, 8, 0)
    lane = lax.broadcasted_iota(jnp.int32, (1, 256), 1)
    win = jnp.where(lane < 64, w2, jnp.where(lane < 128, w4, jnp.where(lane < 192, w8, w16)))
    win = win[POOL_HALO:, :]
    width = jnp.where(lane < 64, 2, jnp.where(lane < 128, 4, jnp.where(lane < 192, 8, 16)))
    t = seq_tile * tm + lax.broadcasted_iota(jnp.int32, (tm, 1), 0)
    count = jnp.minimum(t + 1, width).astype(_F32)
    pooled = win / count - up
    yb_ref[...] = (_dot(pooled.astype(_BF16), wpool_ref[...]) * pscale_ref[...]).astype(_BF16)

    z = ud[:, 256:512] * ud[:, 512:768]
    zh = halo[:, 512:768] * halo[:, 768:1024]
    zb = jnp.concatenate([zh, z], axis=0)
    cw = convw_ref[...]
    conv = (pltpu.roll(zb, 2, 0)[POOL_HALO:, :] * cw[0:1, :]
            + pltpu.roll(zb, 1, 0)[POOL_HALO:, :] * cw[1:2, :]
            + z * cw[2:3, :])
    yd_ref[...] = (ud[:, 0:256] * (conv + convb_ref[...])).astype(_BF16)

    halo_ref[:, 0:256] = up[tm - POOL_HALO:, :]
    halo_ref[:, 256:1024] = ud[tm - POOL_HALO:, :]


def _proj_call(x, lw, tabs, seq_len):
    n = x.shape[0]
    tm = PROJ_TILE
    tiles_per_seq = seq_len // tm

    def full(a):
        nd = a.ndim
        return pl.BlockSpec(a.shape, lambda i, _nd=nd: (0,) * _nd)

    def rows(width):
        return pl.BlockSpec((tm, width), lambda i: (i, 0))

    consts = [lw["g_mix"], lw["w_in"], lw["g_cq"], lw["w_uq"], lw["g_ckv"], lw["w_ukv"],
              lw["g_q"], lw["g_q_swap"], lw["g_k"], lw["g_k_swap"]]
    consts2 = [lw["w_pool"], lw["pool_scale"], lw["conv_w"], lw["conv_b"], lw["g_sq"], lw["g_sk"]]
    out_widths = [512, 512, 256, 256, 128, 128, 256, 256]
    return pl.pallas_call(
        functools.partial(_proj_kernel, tiles_per_seq=tiles_per_seq),
        grid=(n // tm,),
        in_specs=[rows(D_MODEL)] + [full(a) for a in consts] + [rows(HEAD_LANES)] * 2
                 + [full(a) for a in consts2],
        out_specs=[rows(w) for w in out_widths],
        out_shape=[jax.ShapeDtypeStruct((n, w), _BF16) for w in out_widths],
        scratch_shapes=[pltpu.VMEM((POOL_HALO, 1024), _F32)],
        compiler_params=_params(1),
        name="proj_in",
    )(x, *consts, *tabs, *consts2)


def _mla_attn_kernel(qi_ref, ki_ref, q_ref, k_ref, v_ref, o_ref, m_ref, l_ref, acc_ref):
    tq = tk = ATTN_TILE
    p_id = pl.program_id(1)
    qi = qi_ref[p_id]
    ki = ki_ref[p_id]

    @pl.when(ki == 0)
    def _():
        m_ref[...] = jnp.full_like(m_ref, NEG_BIG)
        l_ref[...] = jnp.zeros_like(l_ref)
        acc_ref[...] = jnp.zeros_like(acc_ref)

    lane_lo = lax.broadcasted_iota(jnp.int32, (1, HEAD_LANES), 1) < 64

    def accumulate(on_diagonal):
        if on_diagonal:
            visible = (lax.broadcasted_iota(jnp.int32, (tq, tk), 1)
                       <= lax.broadcasted_iota(jnp.int32, (tq, tk), 0))
        for pair in range(MLA_HEADS // 2):
            vblk = v_ref[:, pair * 128:(pair + 1) * 128]
            alphas = []
            pvs = []
            for sub in range(2):
                hd = 2 * pair + sub
                sl = slice(hd * HEAD_LANES, (hd + 1) * HEAD_LANES)
                s = _dot_nt(q_ref[:, sl], k_ref[:, sl])
                if on_diagonal:
                    s = jnp.where(visible, s, NEG_BIG)
                m_prev = m_ref[hd]
                m_new = jnp.maximum(m_prev, jnp.max(s, axis=-1, keepdims=True))
                alpha = jnp.exp2(m_prev - m_new)
                p = jnp.exp2(s - jnp.tile(m_new, (1, tk // HEAD_LANES)))
                l_ref[hd] = alpha * l_ref[hd] + jnp.sum(p, axis=-1, keepdims=True)
                m_ref[hd] = m_new
                alphas.append(alpha)
                pvs.append(_dot(p.astype(_BF16), vblk))
            psl = slice(pair * 128, (pair + 1) * 128)
            acc_ref[:, psl] = (acc_ref[:, psl] * jnp.where(lane_lo, alphas[0], alphas[1])
                               + jnp.where(lane_lo, pvs[0], pvs[1]))

    @pl.when(ki < qi)
    def _():
        accumulate(False)

    @pl.when(ki == qi)
    def _():
        accumulate(True)
        for pair in range(MLA_HEADS // 2):
            psl = slice(pair * 128, (pair + 1) * 128)
            denom = jnp.where(lane_lo, l_ref[2 * pair], l_ref[2 * pair + 1])
            o_ref[:, psl] = (acc_ref[:, psl] / denom).astype(o_ref.dtype)


def _mla_attn_call(qm, km, vm, batch, seq_len):
    tq = ATTN_TILE
    nq = seq_len // tq
    pairs = [(q, k) for q in range(nq) for k in range(q + 1)]
    qi = jnp.asarray([p[0] for p in pairs], jnp.int32)
    ki = jnp.asarray([p[1] for p in pairs], jnp.int32)
    grid_spec = pltpu.PrefetchScalarGridSpec(
        num_scalar_prefetch=2,
        grid=(batch, len(pairs)),
        in_specs=[
            pl.BlockSpec((tq, 512), lambda b, p, qi, ki: (b * nq + qi[p], 0)),
            pl.BlockSpec((tq, 512), lambda b, p, qi, ki: (b * nq + ki[p], 0)),
            pl.BlockSpec((tq, 256), lambda b, p, qi, ki: (b * nq + ki[p], 0)),
        ],
        out_specs=pl.BlockSpec((tq, 256), lambda b, p, qi, ki: (b * nq + qi[p], 0)),
        scratch_shapes=[pltpu.VMEM((MLA_HEADS, tq, HEAD_LANES), _F32),
                        pltpu.VMEM((MLA_HEADS, tq, HEAD_LANES), _F32),
                        pltpu.VMEM((tq, 256), _F32)],
    )
    return pl.pallas_call(
        _mla_attn_kernel,
        grid_spec=grid_spec,
        out_shape=jax.ShapeDtypeStruct((batch * seq_len, 256), _BF16),
        compiler_params=_params(2),
        name="mla_attention",
    )(qi, ki, qm, km, vm)


def _swa_kernel(sink_ref, q_ref, kc_ref, kp_ref, vc_ref, vp_ref, o_ref, *, tiles_per_seq):
    w = SWA_WINDOW
    i = pl.program_id(0)
    oldest = jnp.where((i % tiles_per_seq) == 0, 0, -w)
    lane_lo = lax.broadcasted_iota(jnp.int32, (1, 128), 1) < 64
    qpos = lax.broadcasted_iota(jnp.int32, (w, 2 * w), 0)
    kpos = lax.broadcasted_iota(jnp.int32, (w, 2 * w), 1) - w
    band = jnp.logical_and(kpos <= qpos, kpos > qpos - w)

    for jb in range(TOKEN_TILE // w):
        rs = slice(jb * w, (jb + 1) * w)
        if jb == 0:
            kprev, vprev = kp_ref[...], vp_ref[...]
            visible = jnp.logical_and(band, kpos >= oldest)
        else:
            ps = slice((jb - 1) * w, jb * w)
            kprev, vprev = kc_ref[ps, :], vc_ref[ps, :]
            visible = band
        kk = jnp.concatenate([kprev, kc_ref[rs, :]], axis=0)
        vv = jnp.concatenate([vprev, vc_ref[rs, :]], axis=0)
        zero = jnp.zeros_like(kk)
        k_half = (jnp.where(lane_lo, kk, zero), jnp.where(lane_lo, zero, kk))
        for blk in range(2):
            qblk = q_ref[rs, blk * 128:(blk + 1) * 128]
            outs = []
            for half in range(2):
                sink = sink_ref[2 * blk + half]
                s = _dot_nt(qblk, k_half[half])
                s = jnp.where(visible, s, NEG_BIG)
                m = jnp.maximum(jnp.max(s, axis=-1, keepdims=True), sink)
                p = jnp.exp(s - m)
                denom = jnp.sum(p, axis=-1, keepdims=True) + jnp.exp(sink - m)
                outs.append(_dot(p.astype(_BF16), vv) / denom)
            o_ref[rs, blk * 128:(blk + 1) * 128] = jnp.where(lane_lo, outs[0], outs[1]).astype(o_ref.dtype)


def _swa_call(qs, ks, vs, sinks, seq_len):
    n = qs.shape[0]
    tm = TOKEN_TILE
    per = tm // SWA_WINDOW
    prev = lambda i, s: (jnp.maximum(i * per - 1, 0), 0)
    grid_spec = pltpu.PrefetchScalarGridSpec(
        num_scalar_prefetch=1,
        grid=(n // tm,),
        in_specs=[
            pl.BlockSpec((tm, 256), lambda i, s: (i, 0)),
            pl.BlockSpec((tm, 128), lambda i, s: (i, 0)),
            pl.BlockSpec((SWA_WINDOW, 128), prev),
            pl.BlockSpec((tm, 128), lambda i, s: (i, 0)),
            pl.BlockSpec((SWA_WINDOW, 128), prev),
        ],
        out_specs=pl.BlockSpec((tm, 256), lambda i, s: (i, 0)),
    )
    return pl.pallas_call(
        functools.partial(_swa_kernel, tiles_per_seq=seq_len // tm),
        grid_spec=grid_spec,
        out_shape=jax.ShapeDtypeStruct((n, 256), _BF16),
        compiler_params=_params(1),
        name="swa_attention",
    )(sinks, qs, ks, ks, vs, vs)


def _memkv_kernel(mem_ref, gmem_ref, wkv_ref, gk_ref, k_ref, v_ref):
    m = _rms_scale(mem_ref[...], D_MODEL) * gmem_ref[...]
    kv = _dot(m.astype(_BF16), wkv_ref[...])
    lane_lo = lax.broadcasted_iota(jnp.int32, (1, 128), 1) < 64
    lane = lax.broadcasted_iota(jnp.int32, (1, 256), 1)
    k = jnp.concatenate([_pair_norm64(kv[:, 0:128], lane_lo), _pair_norm64(kv[:, 128:256], lane_lo)],
                        axis=1) * gk_ref[...]
    v = kv[:, 256:512]
    for hd in range(XA_HEADS):
        own = jnp.logical_and(lane >= hd * XA_HEAD_DIM, lane < (hd + 1) * XA_HEAD_DIM)
        k_ref[hd] = jnp.where(own, k, 0.0).astype(_BF16)
        v_ref[hd] = jnp.where(own, v, 0.0).astype(_BF16)


def _memkv_call(mem2d, g_mem, w_kv, g_k4, depth, batch):
    out = jax.ShapeDtypeStruct((depth, batch, XA_HEADS, MEM_LEN, 256), _BF16)
    ospec = pl.BlockSpec((None, None, XA_HEADS, MEM_LEN, 256), lambda l, b: (l, b, 0, 0, 0))
    return pl.pallas_call(
        _memkv_kernel,
        grid=(depth, batch),
        in_specs=[pl.BlockSpec((MEM_LEN, D_MODEL), lambda l, b: (b, 0)),
                  pl.BlockSpec((None, 1, D_MODEL), lambda l, b: (l, 0, 0)),
                  pl.BlockSpec((None, D_MODEL, 512), lambda l, b: (l, 0, 0)),
                  pl.BlockSpec((None, 1, 256), lambda l, b: (l, 0, 0))],
        out_specs=[ospec, ospec],
        out_shape=[out, out],
        compiler_params=_params(2),
        name="memory_kv",
    )(mem2d, g_mem, w_kv, g_k4)


def _mix_kernel(ya_ref, yb_ref, yc_ref, yd_ref, x_ref, gmo_ref, wmo_ref, gxa_ref, wq_ref, gxq_ref,
                kx_ref, vx_ref, wo_ref, gffn_ref, wrh_ref, wrl_ref, br_ref, ltri_ref, utri_ref,
                x2_ref, h3_ref, meta_ref, cnt_ref):
    tm = MIX_TILE
    gmo = gmo_ref[...]
    lane_lo = lax.broadcasted_iota(jnp.int32, (1, 128), 1) < 64

    def rows_to_logits(rs):
        parts = []
        for g, ref in enumerate((ya_ref, yb_ref, yc_ref, yd_ref)):
            yg = ref[rs, :].astype(_F32)
            parts.append((_rms_scale(yg, GROUP_W) * gmo[:, g * 256:(g + 1) * 256]).astype(_BF16))
        y = jnp.concatenate(parts, axis=1)
        x1 = x_ref[rs, :] + _dot(y, wmo_ref[...])

        h = (_rms_scale(x1, D_MODEL) * gxa_ref[...]).astype(_BF16)
        q = _dot(h, wq_ref[...])
        qn = jnp.concatenate([_pair_norm64(q[:, 0:128], lane_lo), _pair_norm64(q[:, 128:256], lane_lo)],
                             axis=1)
        qn = (qn * gxq_ref[...] * (XA_HEAD_DIM ** -0.5)).astype(_BF16)
        o = jnp.zeros((x1.shape[0], 256), _F32)
        for hd in range(XA_HEADS):
            s = _dot_nt(qn, kx_ref[hd])
            m = jnp.max(s, axis=-1, keepdims=True)
            p = jnp.exp(s - m)
            denom = jnp.sum(p, axis=-1, keepdims=True)
            o = o + _dot(p.astype(_BF16), vx_ref[hd]) / denom
        x2 = x1 + _dot(o.astype(_BF16), wo_ref[...])
        x2_ref[rs, :] = x2

        h3 = _rms_scale(x2, D_MODEL) * gffn_ref[...]
        h3_hi = h3.astype(_BF16)
        h3_lo = (h3 - h3_hi.astype(_F32)).astype(_BF16)
        h3_ref[rs, :] = h3_hi
        return (_dot(h3_hi, wrh_ref[...]) + (_dot(h3_hi, wrl_ref[...]) + _dot(h3_lo, wrh_ref[...]))
                + br_ref[...])

    all_logits = rows_to_logits(slice(0, tm))
    for st in range(tm // TOKEN_TILE):
        meta, counts = _route(all_logits[st * TOKEN_TILE:(st + 1) * TOKEN_TILE, :], ltri_ref, utri_ref)
        meta_ref[st * TOKEN_TILE:(st + 1) * TOKEN_TILE, :] = meta
        cnt_ref[st * 8:(st + 1) * 8, :] = jnp.broadcast_to(counts, (8, 128))


def _route(logits, ltri_ref, utri_ref):
    lane = lax.broadcasted_iota(jnp.int32, (TOKEN_TILE, 128), 1).astype(_F32)
    far = 1e9
    is_group = jnp.logical_and(lane >= N_EXPERTS, lane < N_EXPERTS + N_EXPERT_GROUPS)
    gl = jnp.where(is_group, logits, -jnp.inf)
    gmax = jnp.max(gl, axis=-1, keepdims=True)
    gidx = jnp.min(jnp.where(gl == gmax, lane, far), axis=-1, keepdims=True) - N_EXPERTS
    g_w = 1.0 / jnp.sum(jnp.where(is_group, jnp.exp(logits - gmax), 0.0), axis=-1, keepdims=True)
    in_group = jnp.logical_and(lane < N_EXPERTS, jnp.floor(lane * (1.0 / EXPERTS_PER_GROUP)) == gidx)
    el = jnp.where(in_group, logits, -jnp.inf)
    emax = jnp.max(el, axis=-1, keepdims=True)
    ep = jnp.where(in_group, jnp.exp(logits - emax), 0.0)
    prob = jnp.where(in_group, ep / jnp.sum(ep, axis=-1, keepdims=True), -1.0)
    p1 = jnp.max(prob, axis=-1, keepdims=True)
    e0 = jnp.min(jnp.where(prob == p1, lane, far), axis=-1, keepdims=True)
    prob2 = jnp.where(lane == e0, -1.0, prob)
    p2 = jnp.max(prob2, axis=-1, keepdims=True)
    e1 = jnp.min(jnp.where(prob2 == p2, lane, far), axis=-1, keepdims=True)
    w0 = g_w * (p1 / (p1 + p2))
    w1 = g_w * (p2 / (p1 + p2))

    onehot = jnp.where(jnp.logical_or(lane == e0, lane == e1), 1.0, 0.0)
    prefix = _dot(ltri_ref[...], onehot.astype(_BF16))
    counts = jnp.sum(onehot, axis=0, keepdims=True)
    units = jnp.floor((counts + (RUN_ALIGN - 1)) * (1.0 / RUN_ALIGN))
    offs = _dot(jnp.broadcast_to(units, (8, 128)).astype(_BF16), utri_ref[...])[0:1, :] * RUN_ALIGN
    where_to = prefix + offs
    pos0 = jnp.sum(jnp.where(lane == e0, where_to, 0.0), axis=-1, keepdims=True)
    pos1 = jnp.sum(jnp.where(lane == e1, where_to, 0.0), axis=-1, keepdims=True)

    meta = jnp.where(lane == 0, e0,
           jnp.where(lane == 1, e1,
           jnp.where(lane == 2, w0,
           jnp.where(lane == 3, w1,
           jnp.where(lane == 4, pos0,
           jnp.where(lane == 5, pos1, 0.0))))))
    return meta, counts


def _mix_call(ya, yb, yc, yd, x, lw, kx, vx, seq_len):
    n = x.shape[0]
    tm = MIX_TILE
    tiles_per_seq = seq_len // tm

    def full(a):
        nd = a.ndim
        return pl.BlockSpec(a.shape, lambda i, _nd=nd: (0,) * _nd)

    def rows(width):
        return pl.BlockSpec((tm, width), lambda i: (i, 0))

    kvspec = pl.BlockSpec((None, XA_HEADS, MEM_LEN, 256), lambda i: (i // tiles_per_seq, 0, 0, 0))
    consts_a = [lw["g_mo"], lw["w_mo"], lw["g_xa"], lw["xa_w_q"], lw["g_xq"]]
    consts_b = [lw["xa_w_o"], lw["g_ffn"], lw["w_router_hi"], lw["w_router_lo"], lw["b_router"],
                lw["ltri"], lw["utri"]]
    ntiles = n // tm
    return pl.pallas_call(
        _mix_kernel,
        grid=(ntiles,),
        in_specs=[rows(256)] * 4 + [rows(D_MODEL)] + [full(a) for a in consts_a]
                 + [kvspec, kvspec] + [full(a) for a in consts_b],
        out_specs=[rows(D_MODEL), rows(D_MODEL), rows(128),
                   pl.BlockSpec((8 * (tm // TOKEN_TILE), 128), lambda i: (i, 0))],
        out_shape=[jax.ShapeDtypeStruct((n, D_MODEL), _F32), jax.ShapeDtypeStruct((n, D_MODEL), _BF16),
                   jax.ShapeDtypeStruct((n, 128), _F32),
                   jax.ShapeDtypeStruct((n // TOKEN_TILE * 8, 128), _F32)],
        compiler_params=_params(1),
        name="mix_xattn_router",
    )(ya, yb, yc, yd, x, *consts_a, kx, vx, *consts_b)


def _move_runs(tab_ref, t, live, local_ref, remote_ref, sems, *, to_remote, wait):
    for bit in range(RUN_BITS):
        rows = RUN_ALIGN << bit
        base = (t * RUN_BITS + bit) * PIECE_COLS
        count = jnp.where(live, tab_ref[base], 0)

        def one(p, carry, base=base, rows=rows, bit=bit):
            loc = pl.multiple_of(tab_ref[base + 1 + p], RUN_ALIGN)
            rem = pl.multiple_of(tab_ref[base + 1 + N_EXPERTS + p], RUN_ALIGN)
            lsl = local_ref.at[pl.ds(loc, rows)]
            rsl = remote_ref.at[pl.ds(rem, rows)]
            src, dst = (lsl, rsl) if to_remote else (rsl, lsl)
            copy = pltpu.make_async_copy(src, dst, sems.at[bit, p])
            if wait:
                copy.wait()
            else:
                copy.start()
            return carry

        lax.fori_loop(0, count, one, 0)


def _selection(meta, first, rows):
    r = first + lax.broadcasted_iota(jnp.int32, (TOKEN_TILE, rows), 1)
    pos0 = meta[:, 4:5].astype(jnp.int32)
    pos1 = meta[:, 5:6].astype(jnp.int32)
    return jnp.where(r == pos0, 1.0, 0.0).astype(_BF16), jnp.where(r == pos1, 1.0, 0.0).astype(_BF16)


def _dispatch_kernel(tab_ref, used_ref, meta_ref, h3_ref, xs_in_ref, xs_ref, buf0_ref, buf1_ref, sems,
                     *, ntiles):
    del xs_in_ref
    t = pl.program_id(0)
    last = ntiles - 1
    bufs = (buf0_ref, buf1_ref)
    used = used_ref[jnp.minimum(t, last)]

    def step(slot):
        mine, other = bufs[slot], bufs[1 - slot]
        _move_runs(tab_ref, jnp.clip(t - 2, 0, last), t >= 2, mine, xs_ref, sems.at[slot],
                   to_remote=True, wait=True)
        _move_runs(tab_ref, jnp.clip(t - 1, 0, last), jnp.logical_and(t >= 1, t <= ntiles), other, xs_ref,
                   sems.at[1 - slot], to_remote=True, wait=False)

        meta = meta_ref[...]
        lane = lax.broadcasted_iota(jnp.int32, (TOKEN_TILE, 128), 1)
        cols = []
        for k in range(2):
            w = meta[:, 2 + k:3 + k]
            hi = w.astype(_BF16).astype(_F32)
            mid = (w - hi).astype(_BF16).astype(_F32)
            lo = (w - hi) - mid
            cols.append(jnp.where(lane == 0, hi, jnp.where(lane == 1, mid, jnp.where(lane == 2, lo, 0.0)))
                        .astype(_BF16))

        def sort_rows(first, rows):
            sel0, sel1 = _selection(meta, first, rows)
            tn = (((0,), (0,)), ((), ()))
            mine[first:first + rows, 0:D_MODEL] = lax.dot_general(
                sel0 + sel1, h3_ref[...], tn, preferred_element_type=_F32).astype(_BF16)
            mine[first:first + rows, D_MODEL:XS_COLS] = (
                lax.dot_general(sel0, cols[0], tn, preferred_element_type=_F32)
                + lax.dot_general(sel1, cols[1], tn, preferred_element_type=_F32)).astype(_BF16)

        sort_rows(0, 2 * TOKEN_TILE)
        for first in range(2 * TOKEN_TILE, SORT_ROWS, SORT_CHUNK):
            pl.when(used > first)(functools.partial(sort_rows, first, SORT_CHUNK))

    for slot in range(2):
        pl.when(t % 2 == slot)(functools.partial(step, slot))


def _dispatch_call(tab, used, meta, h3, xs_zero):
    n = h3.shape[0]
    tm = TOKEN_TILE
    ntiles = n // tm
    tile = lambda i, tab, used: (jnp.minimum(i, ntiles - 1), 0)
    grid_spec = pltpu.PrefetchScalarGridSpec(
        num_scalar_prefetch=2,
        grid=(ntiles + 2,),
        in_specs=[pl.BlockSpec((tm, 128), tile),
                  pl.BlockSpec((tm, D_MODEL), tile),
                  pl.BlockSpec(memory_space=pl.ANY)],
        out_specs=pl.BlockSpec(memory_space=pl.ANY),
        scratch_shapes=[pltpu.VMEM((SORT_ROWS, XS_COLS), _BF16), pltpu.VMEM((SORT_ROWS, XS_COLS), _BF16),
                        pltpu.SemaphoreType.DMA((2, RUN_BITS, N_EXPERTS))],
    )
    return pl.pallas_call(
        functools.partial(_dispatch_kernel, ntiles=ntiles),
        grid_spec=grid_spec,
        out_shape=jax.ShapeDtypeStruct(xs_zero.shape, xs_zero.dtype),
        input_output_aliases={4: 0},
        compiler_params=_params(1),
        name="moe_dispatch",
    )(tab, used, meta, h3, xs_zero)


def _ffn_kernel(be_ref, nv_ref, xs_ref, wg_ref, wu_ref, wd_ref, ys_ref, wgu_ref, wdn_ref):
    i = pl.program_id(0)

    @pl.when(jnp.logical_or(i == 0, be_ref[i] != be_ref[jnp.maximum(i - 1, 0)]))
    def _():
        wgu_ref[:, 0:D_EXPERT] = wg_ref[...].astype(_BF16)
        wgu_ref[:, D_EXPERT:2 * D_EXPERT] = wu_ref[...].astype(_BF16)
        wdn_ref[...] = wd_ref[...].astype(_BF16)

    @pl.when(i < nv_ref[0])
    def _():
        xb = xs_ref[:, 0:D_MODEL]
        wcols = xs_ref[:, D_MODEL:XS_COLS].astype(_F32)
        wt = wcols[:, 0:1] + wcols[:, 1:2] + wcols[:, 2:3]
        gu = _dot(xb, wgu_ref[...])
        g = gu[:, 0:D_EXPERT]
        a = (g * jax.nn.sigmoid(g)) * gu[:, D_EXPERT:2 * D_EXPERT]
        y = _dot(a.astype(_BF16), wdn_ref[...])
        ys_ref[...] = (y * wt).astype(ys_ref.dtype)

    @pl.when(i >= nv_ref[0])
    def _():
        ys_ref[...] = jnp.zeros_like(ys_ref)


def _ffn_call(blk_expert, n_valid, xs, w_gate, w_up, w_down, layer):
    n_blocks = xs.shape[0] // MOE_BLOCK
    wspec = lambda shape: pl.BlockSpec((None, None) + shape, lambda i, be, nv: (layer, be[i], 0, 0))
    last_live = lambda i, be, nv: (jnp.minimum(i, nv[0] - 1), 0)
    grid_spec = pltpu.PrefetchScalarGridSpec(
        num_scalar_prefetch=2,
        grid=(n_blocks,),
        in_specs=[pl.BlockSpec((MOE_BLOCK, XS_COLS), last_live),
                  wspec((D_MODEL, D_EXPERT)), wspec((D_MODEL, D_EXPERT)), wspec((D_EXPERT, D_MODEL))],
        out_specs=pl.BlockSpec((MOE_BLOCK, D_MODEL), lambda i, be, nv: (i, 0)),
        scratch_shapes=[pltpu.VMEM((D_MODEL, 2 * D_EXPERT), _BF16), pltpu.VMEM((D_EXPERT, D_MODEL), _BF16)],
    )
    return pl.pallas_call(
        _ffn_kernel,
        grid_spec=grid_spec,
        out_shape=jax.ShapeDtypeStruct((xs.shape[0], D_MODEL), _BF16),
        compiler_params=_params(1),
        name="moe_experts",
    )(blk_expert, n_valid, xs, w_gate, w_up, w_down)


def _combine_kernel(tab_ref, used_ref, meta_ref, x2_ref, ys_ref, out_ref, buf0_ref, buf1_ref, sems,
                    *, ntiles):
    t = pl.program_id(0)
    last = ntiles - 1
    bufs = (buf0_ref, buf1_ref)
    used = used_ref[jnp.clip(t - 1, 0, last)]

    @pl.when(t == 0)
    def _():
        buf0_ref[...] = jnp.zeros_like(buf0_ref)
        buf1_ref[...] = jnp.zeros_like(buf1_ref)

    def step(slot):
        mine, other = bufs[slot], bufs[1 - slot]
        _move_runs(tab_ref, jnp.clip(t - 1, 0, last), t >= 1, other, ys_ref, sems.at[1 - slot],
                   to_remote=False, wait=True)
        _move_runs(tab_ref, jnp.minimum(t, last), t <= last, mine, ys_ref, sems.at[slot],
                   to_remote=False, wait=False)
        meta = meta_ref[...]

        def gathered(first, rows):
            sel0, sel1 = _selection(meta, first, rows)
            return _dot(sel0 + sel1, other[first:first + rows, :])

        out_ref[...] = x2_ref[...] + gathered(0, 2 * TOKEN_TILE)
        for first in range(2 * TOKEN_TILE, SORT_ROWS, SORT_CHUNK):
            @pl.when(used > first)
            def _(first=first):
                out_ref[...] += gathered(first, SORT_CHUNK)

    for slot in range(2):
        pl.when(t % 2 == slot)(functools.partial(step, slot))


def _combine_call(tab, used, meta, x2, ys):
    n = x2.shape[0]
    tm = TOKEN_TILE
    ntiles = n // tm
    tile = lambda i, tab, used: (jnp.maximum(i - 1, 0), 0)
    grid_spec = pltpu.PrefetchScalarGridSpec(
        num_scalar_prefetch=2,
        grid=(ntiles + 1,),
        in_specs=[pl.BlockSpec((tm, 128), tile),
                  pl.BlockSpec((tm, D_MODEL), tile),
                  pl.BlockSpec(memory_space=pl.ANY)],
        out_specs=pl.BlockSpec((tm, D_MODEL), tile),
        scratch_shapes=[pltpu.VMEM((SORT_ROWS, D_MODEL), _BF16), pltpu.VMEM((SORT_ROWS, D_MODEL), _BF16),
                        pltpu.SemaphoreType.DMA((2, RUN_BITS, N_EXPERTS))],
    )
    return pl.pallas_call(
        functools.partial(_combine_kernel, ntiles=ntiles),
        grid_spec=grid_spec,
        out_shape=jax.ShapeDtypeStruct((n, D_MODEL), _F32),
        compiler_params=_params(1),
        name="moe_combine",
    )(tab, used, meta, x2, ys)


def _moe_tables(cnt, n_tokens):
    ntiles = n_tokens // TOKEN_TILE
    counts = cnt.reshape(ntiles, 8, 128)[:, 0, :N_EXPERTS].astype(jnp.int32)
    units = (counts + RUN_ALIGN - 1) // RUN_ALIGN
    padded = units * RUN_ALIGN
    local = jnp.cumsum(padded, axis=1) - padded
    total = jnp.sum(padded, axis=0)
    total_blk = ((total + MOE_BLOCK - 1) // MOE_BLOCK) * MOE_BLOCK
    ends = jnp.cumsum(total_blk)
    starts = ends - total_blk
    remote = starts[None, :] + jnp.cumsum(padded, axis=0) - padded
    bits = jnp.arange(RUN_BITS, dtype=jnp.int32)[None, :, None]
    has = (units[:, None, :] >> bits) & 1
    done = (units[:, None, :] & ((1 << bits) - 1)) * RUN_ALIGN
    rank = jnp.cumsum(has, axis=2) - 1
    place = jnp.logical_and(rank[..., None] == jnp.arange(N_EXPERTS, dtype=jnp.int32), has[..., None] == 1)
    compact = lambda rows: jnp.sum(jnp.where(place, rows[..., None], 0), axis=2)
    tab = jnp.concatenate([jnp.sum(has, axis=2, keepdims=True), compact(local[:, None, :] + done),
                           compact(remote[:, None, :] + done)], axis=-1).reshape(-1).astype(jnp.int32)
    n_blocks = _xs_rows(n_tokens) // MOE_BLOCK
    blk_start = jnp.arange(n_blocks, dtype=jnp.int32) * MOE_BLOCK
    blk_expert = jnp.minimum(jnp.sum((ends[None, :] <= blk_start[:, None]).astype(jnp.int32), axis=1),
                             N_EXPERTS - 1)
    n_valid = (ends[-1] // MOE_BLOCK).astype(jnp.int32).reshape(1)
    used = jnp.sum(padded, axis=1).astype(jnp.int32)
    return tab, used, blk_expert, n_valid


def _xs_rows(n_tokens):
    ntiles = n_tokens // TOKEN_TILE
    worst = 2 * n_tokens + ntiles * N_EXPERTS * (RUN_ALIGN - 1) + N_EXPERTS * (MOE_BLOCK - 1)
    return ((worst + MOE_BLOCK - 1) // MOE_BLOCK) * MOE_BLOCK


def _pad_last(a, width):
    return jnp.pad(a, [(0, 0)] * (a.ndim - 1) + [(0, width - a.shape[-1])])


def _swap_mid_heads(a, axis):
    shape = a.shape
    a = a.reshape(shape[:axis] + (4, 64) + shape[axis + 1:])
    a = jnp.take(a, jnp.asarray([0, 2, 1, 3]), axis=axis)
    return a.reshape(shape)


def _layer_weights(l, p):
    w_in = p["w_in"][l]
    a, b = w_in[:, :A_COLS], w_in[:, A_COLS:A_COLS + B_COLS]
    c = w_in[:, A_COLS + B_COLS:A_COLS + B_COLS + C_COLS]
    d = w_in[:, A_COLS + B_COLS + C_COLS:]
    zeros = lambda w: jnp.zeros((D_MODEL, w), _F32)
    half = MLA_ROPE // 2
    k_rope = a[:, 384:416]
    k_rope_swap = jnp.concatenate([k_rope[:, half:], k_rope[:, :half]], axis=1)
    w_in_p = jnp.concatenate([
        a[:, :384], zeros(64), k_rope, zeros(32), zeros(64), k_rope_swap, zeros(32), b,
        _swap_mid_heads(c[:, :256], 1), c[:, 256:384], c[:, 384:512], d], axis=1).astype(_BF16)
    assert w_in_p.shape[1] == IN_COLS_PADDED

    w_uq = p["mla_w_uq"][l].reshape(MLA_Q_RANK, MLA_HEADS, MLA_QK_DIM)
    w_uq_swap = jnp.concatenate([jnp.zeros_like(w_uq[..., :MLA_NOPE]), w_uq[..., MLA_NOPE + half:],
                                 w_uq[..., MLA_NOPE:MLA_NOPE + half]], axis=-1)
    w_uq = jnp.concatenate([_pad_last(w_uq, HEAD_LANES).reshape(MLA_Q_RANK, -1),
                            _pad_last(w_uq_swap, HEAD_LANES).reshape(MLA_Q_RANK, -1)], axis=1).astype(_BF16)

    def swap_gain(g):
        return _pad_last(jnp.concatenate([jnp.zeros((MLA_NOPE,), _F32), g[MLA_NOPE + half:],
                                          g[MLA_NOPE:MLA_NOPE + half]]), HEAD_LANES)[None]
    w_ukv = p["mla_w_ukv"][l].reshape(MLA_KV_RANK, MLA_HEADS, MLA_NOPE + MLA_V)
    k_nope = _pad_last(w_ukv[..., :MLA_NOPE], HEAD_LANES).reshape(MLA_KV_RANK, MLA_HEADS * HEAD_LANES)
    v_part = w_ukv[..., MLA_NOPE:].reshape(MLA_KV_RANK, MLA_HEADS * MLA_V)
    w_ukv_p = jnp.concatenate([k_nope, v_part], axis=1).astype(_BF16)

    eye = jnp.eye(len(POOL_WINDOWS), dtype=_F32)
    w_pool = jnp.einsum("gcd,gh->gchd", p["pool_w"][l], eye).reshape(256, 256).astype(_BF16)

    g_mo = p["mix_out_norm"][l]
    g_mo = jnp.concatenate([g_mo[:512], _swap_mid_heads(g_mo[512:768], 0), g_mo[768:]])
    w_mo = p["w_mix_out"][l]
    w_mo = jnp.concatenate([w_mo[:512], _swap_mid_heads(w_mo[512:768], 0), w_mo[768:]], axis=0)

    w_router = _pad_last(jnp.concatenate([p["w_expert"][l], p["w_group"][l]], axis=1), 128)
    w_router_hi = w_router.astype(_BF16)
    w_router_lo = (w_router - w_router_hi.astype(_F32)).astype(_BF16)
    b_router = jnp.concatenate([p["b_expert"][l], p["b_group"][l]])
    tm = TOKEN_TILE
    ltri = (jnp.arange(tm)[None, :] < jnp.arange(tm)[:, None]).astype(_BF16)
    utri = (jnp.arange(128)[:, None] < jnp.arange(128)[None, :]).astype(_BF16)
    tile2 = lambda g: jnp.concatenate([g, g])[None]
    return dict(
        g_mix=p["norm_mix"][l][None], w_in=w_in_p,
        g_cq=p["mla_g_cq"][l][None], w_uq=w_uq, g_ckv=p["mla_g_ckv"][l][None], w_ukv=w_ukv_p,
        g_q=_pad_last(p["mla_g_q"][l], HEAD_LANES)[None], g_k=_pad_last(p["mla_g_k"][l], HEAD_LANES)[None],
        g_q_swap=swap_gain(p["mla_g_q"][l]), g_k_swap=swap_gain(p["mla_g_k"][l]),
        w_pool=w_pool, pool_scale=p["pool_scale"][l][None], conv_w=p["conv_w"][l],
        conv_b=p["conv_b"][l][None], g_sq=tile2(p["swa_g_q"][l]), g_sk=tile2(p["swa_g_k"][l]),
        sinks=jnp.take(p["swa_sinks"][l], jnp.asarray([0, 2, 1, 3])),
        g_mo=g_mo[None], w_mo=w_mo.astype(_BF16), g_xa=p["norm_xa"][l][None],
        xa_w_q=p["xa_w_q"][l].astype(_BF16), g_xq=jnp.tile(p["xa_g_q"][l], 4)[None],
        xa_w_o=p["xa_w_o"][l].astype(_BF16), g_ffn=p["norm_ffn"][l][None],
        w_router_hi=w_router_hi, w_router_lo=w_router_lo, b_router=_pad_last(b_router, 128)[None],
        ltri=ltri, utri=utri,
    )


def kernel(x, mem, positions, norm_mix, w_in, mla_g_cq, mla_w_uq, mla_g_ckv, mla_w_ukv, mla_g_q, mla_g_k, pool_w, pool_scale, swa_g_q, swa_g_k, swa_sinks, conv_w, conv_b, mix_out_norm, w_mix_out, norm_xa, norm_mem, xa_w_q, xa_w_kv, xa_g_q, xa_g_k, xa_w_o, norm_ffn, w_group, b_group, w_expert, b_expert, w_gate, w_up, w_down):
    p = dict(norm_mix=norm_mix, w_in=w_in, mla_g_cq=mla_g_cq, mla_w_uq=mla_w_uq, mla_g_ckv=mla_g_ckv,
             mla_w_ukv=mla_w_ukv, mla_g_q=mla_g_q, mla_g_k=mla_g_k, pool_w=pool_w, pool_scale=pool_scale,
             swa_g_q=swa_g_q, swa_g_k=swa_g_k, swa_sinks=swa_sinks, conv_w=conv_w, conv_b=conv_b,
             mix_out_norm=mix_out_norm, w_mix_out=w_mix_out, norm_xa=norm_xa, xa_w_q=xa_w_q,
             xa_g_q=xa_g_q, xa_w_o=xa_w_o, norm_ffn=norm_ffn, w_group=w_group, b_group=b_group,
             w_expert=w_expert, b_expert=b_expert)
    batch, seq_len, _ = x.shape
    depth = w_in.shape[0]
    n = batch * seq_len
    assert seq_len % MIX_TILE == 0 and MIX_TILE % TOKEN_TILE == 0
    assert seq_len % PROJ_TILE == 0 and seq_len % ATTN_TILE == 0
    assert mem.shape[1] == MEM_LEN

    xf = x.reshape(n, D_MODEL)
    tabs = _rope_tables(positions)
    kx, vx = _memkv_call(mem.reshape(batch * MEM_LEN, D_MODEL), norm_mem[:, None, :],
                         xa_w_kv.astype(_BF16), jnp.tile(xa_g_k, (1, 4))[:, None, :], depth, batch)
    for l in range(depth):
        lw = _layer_weights(l, p)
        qm, km, vm, qs, ks, vs, yb, yd = _proj_call(xf, lw, tabs, seq_len)
        ya = _mla_attn_call(qm, km, vm, batch, seq_len)
        yc = _swa_call(qs, ks, vs, lw["sinks"], seq_len)
        x2, h3, meta, cnt = _mix_call(ya, yb, yc, yd, xf, lw, kx[l], vx[l], seq_len)
        tab, used, blk_expert, n_valid = _moe_tables(cnt, n)
        xs = _dispatch_call(tab, used, meta, h3, jnp.zeros((_xs_rows(n), XS_COLS), _BF16))
        ys = _ffn_call(blk_expert, n_valid, xs, w_gate, w_up, w_down, l)
        xf = _combine_call(tab, used, meta, x2, ys)
    return xf.reshape(batch, seq_len, D_MODEL)
```

```python
import functools

import jax
import jax.numpy as jnp
import numpy as np
from jax import lax
from jax.experimental import pallas as pl
from jax.experimental.pallas import tpu as pltpu

EPS = 1e-6
NEG_BIG = -1e30
LOG2_E = 1.4426950408889634
ROPE_THETA = 10000.0

D_MODEL = 1024
MEM_LEN = 256
GROUP_W = 256

MLA_HEADS = 4
MLA_Q_RANK = 256
MLA_KV_RANK = 128
MLA_NOPE = 64
MLA_ROPE = 32
MLA_QK_DIM = MLA_NOPE + MLA_ROPE
MLA_V = 64
HEAD_LANES = 128

POOL_WINDOWS = (2, 4, 8, 16)
POOL_GROUP = 64
POOL_HALO = 16

SWA_HEADS = 4
SWA_KV_HEADS = 2
SWA_HEAD_DIM = 64
SWA_WINDOW = 128

CONV_CH = 256

XA_HEADS = 4
XA_HEAD_DIM = 64

N_EXPERT_GROUPS = 4
EXPERTS_PER_GROUP = 8
N_EXPERTS = 32
D_EXPERT = 256
MOE_BLOCK = 512

A_COLS = MLA_Q_RANK + MLA_KV_RANK + MLA_ROPE
B_COLS = GROUP_W
C_COLS = (SWA_HEADS + 2 * SWA_KV_HEADS) * SWA_HEAD_DIM
COL_CQ, COL_CKV, COL_KROPE, COL_KROPE_SWAP = 0, 256, 384, 512
COL_POOL, COL_SWA_Q, COL_SWA_K, COL_SWA_V, COL_CONV = 640, 896, 1152, 1280, 1408
IN_COLS_PADDED = COL_CONV + 3 * CONV_CH

TOKEN_TILE = 512
PROJ_TILE = 512
MIX_TILE = 1024
ATTN_TILE = 512
ATTN_Q_TILE = 1024
RUN_ALIGN = 16
RUN_BITS = 6
SORT_ROWS = 2 * TOKEN_TILE + N_EXPERTS * RUN_ALIGN
SORT_CHUNK = 256
PIECE_COLS = 1 + 2 * N_EXPERTS
XS_COLS = D_MODEL + 128
VMEM_LIMIT = 56 * 1024 * 1024

_F32 = jnp.float32
_BF16 = jnp.bfloat16


def _params(n_axes):
    return pltpu.CompilerParams(dimension_semantics=("arbitrary",) * n_axes,
                                vmem_limit_bytes=VMEM_LIMIT)


def _dot(a, b):
    return jnp.dot(a, b, preferred_element_type=_F32)


def _dot_nt(a, b):
    return lax.dot_general(a, b, (((1,), (1,)), ((), ())), preferred_element_type=_F32)


def _rms_factor(x, width):
    ss = jnp.sum(x * x, axis=-1, keepdims=True)
    return lax.rsqrt(ss * (1.0 / width) + EPS)


def _rms_scale(x, width):
    return x * _rms_factor(x, width)


def _pair_norm64(x, lane_lo):
    x2 = x * x
    s_all = jnp.sum(x2, axis=-1, keepdims=True)
    s_lo = jnp.sum(jnp.where(lane_lo, x2, 0.0), axis=-1, keepdims=True)
    ss = jnp.where(lane_lo, s_lo, s_all - s_lo)
    return x * lax.rsqrt(ss * (1.0 / 64.0) + EPS)


def _rope_kernel(pos_ref, freq_ref, c_ref, s_ref):
    ang = pos_ref[...] * freq_ref[...]
    c_ref[...] = jnp.cos(ang)
    s_ref[...] = jnp.sin(ang)


def _rope_tables(positions):
    half = MLA_ROPE // 2
    per_row = HEAD_LANES // half
    n = positions.size
    inv_freq = ROPE_THETA ** (-jnp.arange(half, dtype=_F32) / half)
    pos = jnp.repeat(positions.astype(_F32).reshape(n // per_row, per_row), half, axis=1)
    rows = n // per_row
    tile = min(rows, 1024)
    tab = pl.BlockSpec((tile, HEAD_LANES), lambda i: (i, 0))
    shp = jax.ShapeDtypeStruct((rows, HEAD_LANES), _F32)
    cos, sin = pl.pallas_call(
        _rope_kernel,
        grid=(rows // tile,),
        in_specs=[tab, pl.BlockSpec((1, HEAD_LANES), lambda i: (0, 0))],
        out_specs=[tab, tab],
        out_shape=[shp, shp],
        compiler_params=_params(1),
        name="rope_tables",
    )(pos, jnp.tile(inv_freq, per_row)[None])
    cos, sin = cos.reshape(n, half), sin.reshape(n, half)
    pad = jnp.zeros((n, HEAD_LANES - MLA_QK_DIM), _F32)
    c_tab = jnp.concatenate([jnp.ones((n, MLA_NOPE), _F32), cos, cos, pad], axis=1)
    s_tab = jnp.concatenate([jnp.zeros((n, MLA_NOPE), _F32), -sin, sin, pad], axis=1)
    return c_tab, s_tab


def _swa_tile(q, k, v, k_before, v_before, sink_ref, first_of_sequence):
    w = SWA_WINDOW
    oldest = jnp.where(first_of_sequence, 0, -w)
    lane_lo = lax.broadcasted_iota(jnp.int32, (1, 128), 1) < 64
    qpos = lax.broadcasted_iota(jnp.int32, (w, 2 * w), 0)
    kpos = lax.broadcasted_iota(jnp.int32, (w, 2 * w), 1) - w
    band = jnp.logical_and(kpos <= qpos, kpos > qpos - w)

    out_rows = []
    for jb in range(q.shape[0] // w):
        rs = slice(jb * w, (jb + 1) * w)
        if jb == 0:
            kprev, vprev = k_before, v_before
            visible = jnp.logical_and(band, kpos >= oldest)
        else:
            ps = slice((jb - 1) * w, jb * w)
            kprev, vprev = k[ps, :], v[ps, :]
            visible = band
        kk = jnp.concatenate([kprev, k[rs, :]], axis=0)
        vv = jnp.concatenate([vprev, v[rs, :]], axis=0)
        zero = jnp.zeros_like(kk)
        k_half = (jnp.where(lane_lo, kk, zero), jnp.where(lane_lo, zero, kk))
        pair_out = []
        for blk in range(2):
            qblk = q[rs, blk * 128:(blk + 1) * 128]
            outs = []
            for half in range(2):
                sink = sink_ref[2 * blk + half]
                s = _dot_nt(qblk, k_half[half])
                s = jnp.where(visible, s, NEG_BIG)
                m = jnp.maximum(jnp.max(s, axis=-1, keepdims=True), sink)
                p = jnp.exp(s - m)
                denom = jnp.sum(p, axis=-1, keepdims=True) + jnp.exp(sink - m)
                outs.append(_dot(p.astype(_BF16), vv) / denom)
            pair_out.append(jnp.where(lane_lo, outs[0], outs[1]))
        out_rows.append(jnp.concatenate(pair_out, axis=1))
    return jnp.concatenate(out_rows, axis=0)


def _proj_kernel(x_ref, gmix_ref, win_ref, gcq_ref, wuq_ref, gckv_ref, wukv_ref, gq_ref, gqs_ref,
                 gk_ref, gks_ref, c_ref, s_ref, wpool_ref, pscale_ref, convw_ref, convb_ref, gsq_ref,
                 gsk_ref, sink_ref, qm_ref, km_ref, vm_ref, yb_ref, yc_ref, yd_ref, halo_ref, swa_halo_ref,
                 *, tiles_per_seq):
    tm = PROJ_TILE
    i = pl.program_id(0)
    seq_tile = i % tiles_per_seq

    x = x_ref[...]
    h = _rms_scale(x, D_MODEL) * gmix_ref[...]
    u = _dot(h.astype(_BF16), win_ref[...])

    c = c_ref[...]
    s = s_ref[...]
    hw = MLA_HEADS * HEAD_LANES

    cq = _rms_scale(u[:, COL_CQ:COL_CQ + MLA_Q_RANK], MLA_Q_RANK) * gcq_ref[...]
    q = _dot(cq.astype(_BF16), wuq_ref[...])
    gq = gq_ref[...] * (MLA_QK_DIM ** -0.5 * LOG2_E)
    gqs = gqs_ref[...] * (MLA_QK_DIM ** -0.5 * LOG2_E)
    for hd in range(MLA_HEADS):
        sl = slice(hd * HEAD_LANES, (hd + 1) * HEAD_LANES)
        xq = q[:, sl]
        r = _rms_factor(xq, MLA_QK_DIM)
        qm_ref[:, sl] = ((xq * r) * (gq * c) + (q[:, hw + sl.start:hw + sl.stop] * r) * (gqs * s)).astype(_BF16)

    ckv = _rms_scale(u[:, COL_CKV:COL_CKV + MLA_KV_RANK], MLA_KV_RANK) * gckv_ref[...]
    kv = _dot(ckv.astype(_BF16), wukv_ref[...])
    krope = u[:, COL_KROPE:COL_KROPE + HEAD_LANES]
    krope_swap = u[:, COL_KROPE_SWAP:COL_KROPE_SWAP + HEAD_LANES]
    gk = gk_ref[...]
    gks = gks_ref[...]
    for hd in range(MLA_HEADS):
        sl = slice(hd * HEAD_LANES, (hd + 1) * HEAD_LANES)
        xk = kv[:, sl] + krope
        r = _rms_factor(xk, MLA_QK_DIM)
        km_ref[:, sl] = ((xk * r) * (gk * c) + (krope_swap * r) * (gks * s)).astype(_BF16)
    vm_ref[...] = kv[:, hw:hw + MLA_HEADS * MLA_V].astype(_BF16)

    lane_lo = lax.broadcasted_iota(jnp.int32, (1, HEAD_LANES), 1) < 64
    gsq = gsq_ref[...] * (SWA_HEAD_DIM ** -0.5)
    qs = jnp.concatenate(
        [(_pair_norm64(u[:, COL_SWA_Q + blk * 128:COL_SWA_Q + (blk + 1) * 128], lane_lo) * gsq).astype(_BF16)
         for blk in range(2)], axis=1)
    ks = (_pair_norm64(u[:, COL_SWA_K:COL_SWA_K + 128], lane_lo) * gsk_ref[...]).astype(_BF16)
    vs = u[:, COL_SWA_V:COL_SWA_V + 128].astype(_BF16)

    @pl.when(seq_tile == 0)
    def _():
        halo_ref[...] = jnp.zeros_like(halo_ref)
        swa_halo_ref[...] = jnp.zeros_like(swa_halo_ref)

    swa_halo = swa_halo_ref[...].astype(_BF16)
    yc_ref[...] = _swa_tile(qs, ks, vs, swa_halo[:, 0:128], swa_halo[:, 128:256], sink_ref,
                            seq_tile == 0).astype(_BF16)
    swa_halo_ref[:, 0:128] = ks[tm - SWA_WINDOW:, :].astype(_F32)
    swa_halo_ref[:, 128:256] = vs[tm - SWA_WINDOW:, :].astype(_F32)

    halo = halo_ref[...]
    up = u[:, COL_POOL:COL_POOL + GROUP_W]
    ud = u[:, COL_CONV:COL_CONV + 3 * CONV_CH]

    b = jnp.concatenate([halo[:, 0:256], up], axis=0)
    w2 = b + pltpu.roll(b, 1, 0)
    w4 = w2 + pltpu.roll(w2, 2, 0)
    w8 = w4 + pltpu.roll(w4, 4, 0)
    w16 = w8 + pltpu.roll(w8, 8, 0)
    lane = lax.broadcasted_iota(jnp.int32, (1, 256), 1)
    win = jnp.where(lane < 64, w2, jnp.where(lane < 128, w4, jnp.where(lane < 192, w8, w16)))
    win = win[POOL_HALO:, :]
    width = jnp.where(lane < 64, 2, jnp.where(lane < 128, 4, jnp.where(lane < 192, 8, 16)))
    t = seq_tile * tm + lax.broadcasted_iota(jnp.int32, (tm, 1), 0)
    count = jnp.minimum(t + 1, width).astype(_F32)
    pooled = win / count - up
    yb_ref[...] = (_dot(pooled.astype(_BF16), wpool_ref[...]) * pscale_ref[...]).astype(_BF16)

    z = ud[:, 256:512] * ud[:, 512:768]
    zh = halo[:, 512:768] * halo[:, 768:1024]
    zb = jnp.concatenate([zh, z], axis=0)
    cw = convw_ref[...]
    conv = (pltpu.roll(zb, 2, 0)[POOL_HALO:, :] * cw[0:1, :]
            + pltpu.roll(zb, 1, 0)[POOL_HALO:, :] * cw[1:2, :]
            + z * cw[2:3, :])
    yd_ref[...] = (ud[:, 0:256] * (conv + convb_ref[...])).astype(_BF16)

    halo_ref[:, 0:256] = up[tm - POOL_HALO:, :]
    halo_ref[:, 256:1024] = ud[tm - POOL_HALO:, :]


def _proj_call(x, lw, tabs, seq_len):
    n = x.shape[0]
    tm = PROJ_TILE
    tiles_per_seq = seq_len // tm

    def full(a):
        nd = a.ndim
        return pl.BlockSpec(a.shape, lambda i, _nd=nd: (0,) * _nd)

    def rows(width):
        return pl.BlockSpec((tm, width), lambda i: (i, 0))

    consts = [lw["g_mix"], lw["w_in"], lw["g_cq"], lw["w_uq"], lw["g_ckv"], lw["w_ukv"],
              lw["g_q"], lw["g_q_swap"], lw["g_k"], lw["g_k_swap"]]
    consts2 = [lw["w_pool"], lw["pool_scale"], lw["conv_w"], lw["conv_b"], lw["g_sq"], lw["g_sk"]]
    out_widths = [512, 512, 256, 256, 256, 256]
    return pl.pallas_call(
        functools.partial(_proj_kernel, tiles_per_seq=tiles_per_seq),
        grid=(n // tm,),
        in_specs=[rows(D_MODEL)] + [full(a) for a in consts] + [rows(HEAD_LANES)] * 2
                 + [full(a) for a in consts2] + [pl.BlockSpec(memory_space=pltpu.SMEM)],
        out_specs=[rows(w) for w in out_widths],
        out_shape=[jax.ShapeDtypeStruct((n, w), _BF16) for w in out_widths],
        scratch_shapes=[pltpu.VMEM((POOL_HALO, 1024), _F32), pltpu.VMEM((SWA_WINDOW, 256), _F32)],
        compiler_params=_params(1),
        name="proj_in",
    )(x, *consts, *tabs, *consts2, lw["sinks"])


def _mla_attn_kernel(q_ref, k_hbm, v_hbm, o_ref, kbuf, vbuf, sems, m_ref, l_ref, acc_ref, *, seq_len):
    tq, tk = ATTN_Q_TILE, ATTN_TILE
    ratio = tq // tk
    assert ratio == 2
    b = pl.program_id(0)
    qb = pl.program_id(1)
    row0 = b * seq_len

    def fetch(j, slot):
        rows = pl.ds(pl.multiple_of(row0 + j * tk, tk), tk)
        return (pltpu.make_async_copy(k_hbm.at[rows], kbuf.at[slot], sems.at[0, slot]),
                pltpu.make_async_copy(v_hbm.at[rows], vbuf.at[slot], sems.at[1, slot]))

    def start(j, slot):
        for cp in fetch(j, slot):
            cp.start()

    def wait(j, slot):
        for cp in fetch(j, slot):
            cp.wait()

    m_ref[...] = jnp.full_like(m_ref, NEG_BIG)
    l_ref[...] = jnp.zeros_like(l_ref)
    acc_ref[...] = jnp.zeros_like(acc_ref)
    lane_lo = lax.broadcasted_iota(jnp.int32, (1, HEAD_LANES), 1) < 64

    def accumulate(slot, rows, diagonal_from):
        nrows = rows.stop - rows.start
        if diagonal_from is not None:
            visible = (lax.broadcasted_iota(jnp.int32, (nrows, tk), 1)
                       <= lax.broadcasted_iota(jnp.int32, (nrows, tk), 0) + (rows.start - diagonal_from))
        for pair in range(MLA_HEADS // 2):
            vblk = vbuf[slot, :, pair * 128:(pair + 1) * 128]
            alphas = []
            pvs = []
            for sub in range(2):
                hd = 2 * pair + sub
                sl = slice(hd * HEAD_LANES, (hd + 1) * HEAD_LANES)
                s = _dot_nt(q_ref[rows, sl], kbuf[slot, :, sl])
                if diagonal_from is not None:
                    s = jnp.where(visible, s, NEG_BIG)
                m_prev = m_ref[hd, rows, :]
                m_new = jnp.maximum(m_prev, jnp.max(s, axis=-1, keepdims=True))
                alpha = jnp.exp2(m_prev - m_new)
                p = jnp.exp2(s - jnp.tile(m_new, (1, tk // HEAD_LANES)))
                l_ref[hd, rows, :] = alpha * l_ref[hd, rows, :] + jnp.sum(p, axis=-1, keepdims=True)
                m_ref[hd, rows, :] = m_new
                alphas.append(alpha)
                pvs.append(_dot(p.astype(_BF16), vblk))
            psl = slice(pair * 128, (pair + 1) * 128)
            acc_ref[rows, psl] = (acc_ref[rows, psl] * jnp.where(lane_lo, alphas[0], alphas[1])
                                  + jnp.where(lane_lo, pvs[0], pvs[1]))

    everything = slice(0, tq)
    start(0, 0)

    def two_tiles(pair_idx, carry):
        j = pair_idx * ratio
        wait(j, 0)
        start(j + 1, 1)
        accumulate(0, everything, None)
        wait(j + 1, 1)
        start(j + 2, 0)
        accumulate(1, everything, None)
        return carry

    lax.fori_loop(0, qb, two_tiles, 0)

    j = qb * ratio
    wait(j, 0)
    start(j + 1, 1)
    accumulate(0, everything, 0)
    wait(j + 1, 1)
    accumulate(1, slice(tk, tq), tk)

    for pair in range(MLA_HEADS // 2):
        psl = slice(pair * 128, (pair + 1) * 128)
        denom = jnp.where(lane_lo, l_ref[2 * pair], l_ref[2 * pair + 1])
        o_ref[:, psl] = (acc_ref[:, psl] / denom).astype(o_ref.dtype)


def _mla_attn_call(qm, km, vm, batch, seq_len):
    tq, tk = ATTN_Q_TILE, ATTN_TILE
    nq = seq_len // tq
    return pl.pallas_call(
        functools.partial(_mla_attn_kernel, seq_len=seq_len),
        grid=(batch, nq),
        in_specs=[pl.BlockSpec((tq, 512), lambda b, q: (b * nq + q, 0)),
                  pl.BlockSpec(memory_space=pl.ANY), pl.BlockSpec(memory_space=pl.ANY)],
        out_specs=pl.BlockSpec((tq, 256), lambda b, q: (b * nq + q, 0)),
        out_shape=jax.ShapeDtypeStruct((batch * seq_len, 256), _BF16),
        scratch_shapes=[pltpu.VMEM((2, tk, 512), _BF16), pltpu.VMEM((2, tk, 256), _BF16),
                        pltpu.SemaphoreType.DMA((2, 2)),
                        pltpu.VMEM((MLA_HEADS, tq, HEAD_LANES), _F32),
                        pltpu.VMEM((MLA_HEADS, tq, HEAD_LANES), _F32),
                        pltpu.VMEM((tq, 256), _F32)],
        compiler_params=_params(2),
        name="mla_attention",
    )(qm, km, vm)


def _memkv_kernel(mem_ref, gmem_ref, wkv_ref, gk_ref, k_ref, v_ref):
    m = _rms_scale(mem_ref[...], D_MODEL) * gmem_ref[...]
    kv = _dot(m.astype(_BF16), wkv_ref[...])
    lane_lo = lax.broadcasted_iota(jnp.int32, (1, 128), 1) < 64
    lane = lax.broadcasted_iota(jnp.int32, (1, 256), 1)
    k = jnp.concatenate([_pair_norm64(kv[:, 0:128], lane_lo), _pair_norm64(kv[:, 128:256], lane_lo)],
                        axis=1) * gk_ref[...]
    v = kv[:, 256:512]
    for hd in range(XA_HEADS):
        own = jnp.logical_and(lane >= hd * XA_HEAD_DIM, lane < (hd + 1) * XA_HEAD_DIM)
        k_ref[hd] = jnp.where(own, k, 0.0).astype(_BF16)
        v_ref[hd] = jnp.where(own, v, 0.0).astype(_BF16)


def _memkv_call(mem2d, g_mem, w_kv, g_k4, depth, batch):
    out = jax.ShapeDtypeStruct((depth, batch, XA_HEADS, MEM_LEN, 256), _BF16)
    ospec = pl.BlockSpec((None, None, XA_HEADS, MEM_LEN, 256), lambda l, b: (l, b, 0, 0, 0))
    return pl.pallas_call(
        _memkv_kernel,
        grid=(depth, batch),
        in_specs=[pl.BlockSpec((MEM_LEN, D_MODEL), lambda l, b: (b, 0)),
                  pl.BlockSpec((None, 1, D_MODEL), lambda l, b: (l, 0, 0)),
                  pl.BlockSpec((None, D_MODEL, 512), lambda l, b: (l, 0, 0)),
                  pl.BlockSpec((None, 1, 256), lambda l, b: (l, 0, 0))],
        out_specs=[ospec, ospec],
        out_shape=[out, out],
        compiler_params=_params(2),
        name="memory_kv",
    )(mem2d, g_mem, w_kv, g_k4)


def _mix_kernel(ya_ref, yb_ref, yc_ref, yd_ref, x_ref, gmo_ref, wmo_ref, gxa_ref, wq_ref, gxq_ref,
                kx_ref, vx_ref, wo_ref, gffn_ref, wrh_ref, wrl_ref, br_ref, ltri_ref, utri_ref,
                x2_ref, h3_ref, meta_ref, cnt_ref):
    tm = MIX_TILE
    gmo = gmo_ref[...]
    lane_lo = lax.broadcasted_iota(jnp.int32, (1, 128), 1) < 64

    def rows_to_logits(rs):
        parts = []
        for g, ref in enumerate((ya_ref, yb_ref, yc_ref, yd_ref)):
            yg = ref[rs, :].astype(_F32)
            parts.append((_rms_scale(yg, GROUP_W) * gmo[:, g * 256:(g + 1) * 256]).astype(_BF16))
        y = jnp.concatenate(parts, axis=1)
        x1 = x_ref[rs, :] + _dot(y, wmo_ref[...])

        h = (_rms_scale(x1, D_MODEL) * gxa_ref[...]).astype(_BF16)
        q = _dot(h, wq_ref[...])
        qn = jnp.concatenate([_pair_norm64(q[:, 0:128], lane_lo), _pair_norm64(q[:, 128:256], lane_lo)],
                             axis=1)
        qn = (qn * gxq_ref[...] * (XA_HEAD_DIM ** -0.5)).astype(_BF16)
        o = jnp.zeros((x1.shape[0], 256), _F32)
        for hd in range(XA_HEADS):
            s = _dot_nt(qn, kx_ref[hd])
            m = jnp.max(s, axis=-1, keepdims=True)
            p = jnp.exp(s - m)
            denom = jnp.sum(p, axis=-1, keepdims=True)
            o = o + _dot(p.astype(_BF16), vx_ref[hd]) / denom
        x2 = x1 + _dot(o.astype(_BF16), wo_ref[...])
        x2_ref[rs, :] = x2

        h3 = _rms_scale(x2, D_MODEL) * gffn_ref[...]
        h3_hi = h3.astype(_BF16)
        h3_lo = (h3 - h3_hi.astype(_F32)).astype(_BF16)
        h3_ref[rs, :] = h3_hi
        return (_dot(h3_hi, wrh_ref[...]) + (_dot(h3_hi, wrl_ref[...]) + _dot(h3_lo, wrh_ref[...]))
                + br_ref[...])

    all_logits = rows_to_logits(slice(0, tm))
    for st in range(tm // TOKEN_TILE):
        meta, counts = _route(all_logits[st * TOKEN_TILE:(st + 1) * TOKEN_TILE, :], ltri_ref, utri_ref)
        meta_ref[st * TOKEN_TILE:(st + 1) * TOKEN_TILE, :] = meta
        cnt_ref[st * 8:(st + 1) * 8, :] = jnp.broadcast_to(counts, (8, 128))


def _route(logits, ltri_ref, utri_ref):
    lane = lax.broadcasted_iota(jnp.int32, (TOKEN_TILE, 128), 1).astype(_F32)
    far = 1e9
    is_group = jnp.logical_and(lane >= N_EXPERTS, lane < N_EXPERTS + N_EXPERT_GROUPS)
    gl = jnp.where(is_group, logits, -jnp.inf)
    gmax = jnp.max(gl, axis=-1, keepdims=True)
    gidx = jnp.min(jnp.where(gl == gmax, lane, far), axis=-1, keepdims=True) - N_EXPERTS
    g_w = 1.0 / jnp.sum(jnp.where(is_group, jnp.exp(logits - gmax), 0.0), axis=-1, keepdims=True)
    in_group = jnp.logical_and(lane < N_EXPERTS, jnp.floor(lane * (1.0 / EXPERTS_PER_GROUP)) == gidx)
    el = jnp.where(in_group, logits, -jnp.inf)
    emax = jnp.max(el, axis=-1, keepdims=True)
    ep = jnp.where(in_group, jnp.exp(logits - emax), 0.0)
    prob = jnp.where(in_group, ep / jnp.sum(ep, axis=-1, keepdims=True), -1.0)
    p1 = jnp.max(prob, axis=-1, keepdims=True)
    e0 = jnp.min(jnp.where(prob == p1, lane, far), axis=-1, keepdims=True)
    prob2 = jnp.where(lane == e0, -1.0, prob)
    p2 = jnp.max(prob2, axis=-1, keepdims=True)
    e1 = jnp.min(jnp.where(prob2 == p2, lane, far), axis=-1, keepdims=True)
    w0 = g_w * (p1 / (p1 + p2))
    w1 = g_w * (p2 / (p1 + p2))

    onehot = jnp.where(jnp.logical_or(lane == e0, lane == e1), 1.0, 0.0)
    prefix = _dot(ltri_ref[...], onehot.astype(_BF16))
    counts = jnp.sum(onehot, axis=0, keepdims=True)
    units = jnp.floor((counts + (RUN_ALIGN - 1)) * (1.0 / RUN_ALIGN))
    offs = _dot(jnp.broadcast_to(units, (8, 128)).astype(_BF16), utri_ref[...])[0:1, :] * RUN_ALIGN
    where_to = prefix + offs
    pos0 = jnp.sum(jnp.where(lane == e0, where_to, 0.0), axis=-1, keepdims=True)
    pos1 = jnp.sum(jnp.where(lane == e1, where_to, 0.0), axis=-1, keepdims=True)

    meta = jnp.where(lane == 0, e0,
           jnp.where(lane == 1, e1,
           jnp.where(lane == 2, w0,
           jnp.where(lane == 3, w1,
           jnp.where(lane == 4, pos0,
           jnp.where(lane == 5, pos1, 0.0))))))
    return meta, counts


def _mix_call(ya, yb, yc, yd, x, lw, kx, vx, seq_len):
    n = x.shape[0]
    tm = MIX_TILE
    tiles_per_seq = seq_len // tm

    def full(a):
        nd = a.ndim
        return pl.BlockSpec(a.shape, lambda i, _nd=nd: (0,) * _nd)

    def rows(width):
        return pl.BlockSpec((tm, width), lambda i: (i, 0))

    kvspec = pl.BlockSpec((None, XA_HEADS, MEM_LEN, 256), lambda i: (i // tiles_per_seq, 0, 0, 0))
    consts_a = [lw["g_mo"], lw["w_mo"], lw["g_xa"], lw["xa_w_q"], lw["g_xq"]]
    consts_b = [lw["xa_w_o"], lw["g_ffn"], lw["w_router_hi"], lw["w_router_lo"], lw["b_router"],
                lw["ltri"], lw["utri"]]
    ntiles = n // tm
    return pl.pallas_call(
        _mix_kernel,
        grid=(ntiles,),
        in_specs=[rows(256)] * 4 + [rows(D_MODEL)] + [full(a) for a in consts_a]
                 + [kvspec, kvspec] + [full(a) for a in consts_b],
        out_specs=[rows(D_MODEL), rows(D_MODEL), rows(128),
                   pl.BlockSpec((8 * (tm // TOKEN_TILE), 128), lambda i: (i, 0))],
        out_shape=[jax.ShapeDtypeStruct((n, D_MODEL), _F32), jax.ShapeDtypeStruct((n, D_MODEL), _BF16),
                   jax.ShapeDtypeStruct((n, 128), _F32),
                   jax.ShapeDtypeStruct((n // TOKEN_TILE * 8, 128), _F32)],
        compiler_params=_params(1),
        name="mix_xattn_router",
    )(ya, yb, yc, yd, x, *consts_a, kx, vx, *consts_b)


def _move_runs(tab_ref, t, live, local_ref, remote_ref, sems, *, to_remote, wait):
    for bit in range(RUN_BITS):
        rows = RUN_ALIGN << bit
        base = (t * RUN_BITS + bit) * PIECE_COLS
        count = jnp.where(live, tab_ref[base], 0)

        def one(p, carry, base=base, rows=rows, bit=bit):
            loc = pl.multiple_of(tab_ref[base + 1 + p], RUN_ALIGN)
            rem = pl.multiple_of(tab_ref[base + 1 + N_EXPERTS + p], RUN_ALIGN)
            lsl = local_ref.at[pl.ds(loc, rows)]
            rsl = remote_ref.at[pl.ds(rem, rows)]
            src, dst = (lsl, rsl) if to_remote else (rsl, lsl)
            copy = pltpu.make_async_copy(src, dst, sems.at[bit, p])
            if wait:
                copy.wait()
            else:
                copy.start()
            return carry

        lax.fori_loop(0, count, one, 0)


def _selection(meta, first, rows):
    r = first + lax.broadcasted_iota(jnp.int32, (TOKEN_TILE, rows), 1)
    pos0 = meta[:, 4:5].astype(jnp.int32)
    pos1 = meta[:, 5:6].astype(jnp.int32)
    return jnp.where(r == pos0, 1.0, 0.0).astype(_BF16), jnp.where(r == pos1, 1.0, 0.0).astype(_BF16)


def _dispatch_kernel(tab_ref, used_ref, meta_ref, h3_ref, xs_in_ref, xs_ref, buf0_ref, buf1_ref, sems,
                     *, ntiles):
    del xs_in_ref
    t = pl.program_id(0)
    last = ntiles - 1
    bufs = (buf0_ref, buf1_ref)
    used = used_ref[jnp.minimum(t, last)]

    def step(slot):
        mine, other = bufs[slot], bufs[1 - slot]
        _move_runs(tab_ref, jnp.clip(t - 2, 0, last), t >= 2, mine, xs_ref, sems.at[slot],
                   to_remote=True, wait=True)
        _move_runs(tab_ref, jnp.clip(t - 1, 0, last), jnp.logical_and(t >= 1, t <= ntiles), other, xs_ref,
                   sems.at[1 - slot], to_remote=True, wait=False)

        meta = meta_ref[...]
        lane = lax.broadcasted_iota(jnp.int32, (TOKEN_TILE, 128), 1)
        cols = []
        for k in range(2):
            w = meta[:, 2 + k:3 + k]
            hi = w.astype(_BF16).astype(_F32)
            mid = (w - hi).astype(_BF16).astype(_F32)
            lo = (w - hi) - mid
            cols.append(jnp.where(lane == 0, hi, jnp.where(lane == 1, mid, jnp.where(lane == 2, lo, 0.0)))
                        .astype(_BF16))

        def sort_rows(first, rows):
            sel0, sel1 = _selection(meta, first, rows)
            tn = (((0,), (0,)), ((), ()))
            mine[first:first + rows, 0:D_MODEL] = lax.dot_general(
                sel0 + sel1, h3_ref[...], tn, preferred_element_type=_F32).astype(_BF16)
            mine[first:first + rows, D_MODEL:XS_COLS] = (
                lax.dot_general(sel0, cols[0], tn, preferred_element_type=_F32)
                + lax.dot_general(sel1, cols[1], tn, preferred_element_type=_F32)).astype(_BF16)

        sort_rows(0, 2 * TOKEN_TILE)
        for first in range(2 * TOKEN_TILE, SORT_ROWS, SORT_CHUNK):
            pl.when(used > first)(functools.partial(sort_rows, first, SORT_CHUNK))

    for slot in range(2):
        pl.when(t % 2 == slot)(functools.partial(step, slot))


def _dispatch_call(tab, used, meta, h3, xs_zero):
    n = h3.shape[0]
    tm = TOKEN_TILE
    ntiles = n // tm
    tile = lambda i, tab, used: (jnp.minimum(i, ntiles - 1), 0)
    grid_spec = pltpu.PrefetchScalarGridSpec(
        num_scalar_prefetch=2,
        grid=(ntiles + 2,),
        in_specs=[pl.BlockSpec((tm, 128), tile),
                  pl.BlockSpec((tm, D_MODEL), tile),
                  pl.BlockSpec(memory_space=pl.ANY)],
        out_specs=pl.BlockSpec(memory_space=pl.ANY),
        scratch_shapes=[pltpu.VMEM((SORT_ROWS, XS_COLS), _BF16), pltpu.VMEM((SORT_ROWS, XS_COLS), _BF16),
                        pltpu.SemaphoreType.DMA((2, RUN_BITS, N_EXPERTS))],
    )
    return pl.pallas_call(
        functools.partial(_dispatch_kernel, ntiles=ntiles),
        grid_spec=grid_spec,
        out_shape=jax.ShapeDtypeStruct(xs_zero.shape, xs_zero.dtype),
        input_output_aliases={4: 0},
        compiler_params=_params(1),
        name="moe_dispatch",
    )(tab, used, meta, h3, xs_zero)


def _ffn_kernel(be_ref, nv_ref, xs_ref, wg_ref, wu_ref, wd_ref, ys_ref, wgu_ref, wdn_ref):
    i = pl.program_id(0)

    @pl.when(jnp.logical_or(i == 0, be_ref[i] != be_ref[jnp.maximum(i - 1, 0)]))
    def _():
        wgu_ref[:, 0:D_EXPERT] = wg_ref[...].astype(_BF16)
        wgu_ref[:, D_EXPERT:2 * D_EXPERT] = wu_ref[...].astype(_BF16)
        wdn_ref[...] = wd_ref[...].astype(_BF16)

    @pl.when(i < nv_ref[0])
    def _():
        xb = xs_ref[:, 0:D_MODEL]
        wcols = xs_ref[:, D_MODEL:XS_COLS].astype(_F32)
        wt = wcols[:, 0:1] + wcols[:, 1:2] + wcols[:, 2:3]
        gu = _dot(xb, wgu_ref[...])
        g = gu[:, 0:D_EXPERT]
        a = (g * jax.nn.sigmoid(g)) * gu[:, D_EXPERT:2 * D_EXPERT]
        y = _dot(a.astype(_BF16), wdn_ref[...])
        ys_ref[...] = (y * wt).astype(ys_ref.dtype)

    @pl.when(i >= nv_ref[0])
    def _():
        ys_ref[...] = jnp.zeros_like(ys_ref)


def _ffn_call(blk_expert, n_valid, xs, w_gate, w_up, w_down, layer):
    n_blocks = xs.shape[0] // MOE_BLOCK
    wspec = lambda shape: pl.BlockSpec((None, None) + shape, lambda i, be, nv: (layer, be[i], 0, 0))
    last_live = lambda i, be, nv: (jnp.minimum(i, nv[0] - 1), 0)
    grid_spec = pltpu.PrefetchScalarGridSpec(
        num_scalar_prefetch=2,
        grid=(n_blocks,),
        in_specs=[pl.BlockSpec((MOE_BLOCK, XS_COLS), last_live),
                  wspec((D_MODEL, D_EXPERT)), wspec((D_MODEL, D_EXPERT)), wspec((D_EXPERT, D_MODEL))],
        out_specs=pl.BlockSpec((MOE_BLOCK, D_MODEL), lambda i, be, nv: (i, 0)),
        scratch_shapes=[pltpu.VMEM((D_MODEL, 2 * D_EXPERT), _BF16), pltpu.VMEM((D_EXPERT, D_MODEL), _BF16)],
    )
    return pl.pallas_call(
        _ffn_kernel,
        grid_spec=grid_spec,
        out_shape=jax.ShapeDtypeStruct((xs.shape[0], D_MODEL), _BF16),
        compiler_params=_params(1),
        name="moe_experts",
    )(blk_expert, n_valid, xs, w_gate, w_up, w_down)


def _combine_kernel(tab_ref, used_ref, meta_ref, x2_ref, ys_ref, out_ref, buf0_ref, buf1_ref, sems,
                    *, ntiles):
    t = pl.program_id(0)
    last = ntiles - 1
    bufs = (buf0_ref, buf1_ref)
    used = used_ref[jnp.clip(t - 1, 0, last)]

    @pl.when(t == 0)
    def _():
        buf0_ref[...] = jnp.zeros_like(buf0_ref)
        buf1_ref[...] = jnp.zeros_like(buf1_ref)

    def step(slot):
        mine, other = bufs[slot], bufs[1 - slot]
        _move_runs(tab_ref, jnp.clip(t - 1, 0, last), t >= 1, other, ys_ref, sems.at[1 - slot],
                   to_remote=False, wait=True)
        _move_runs(tab_ref, jnp.minimum(t, last), t <= last, mine, ys_ref, sems.at[slot],
                   to_remote=False, wait=False)
        meta = meta_ref[...]

        def gathered(first, rows):
            sel0, sel1 = _selection(meta, first, rows)
            return _dot(sel0 + sel1, other[first:first + rows, :])

        out_ref[...] = x2_ref[...] + gathered(0, 2 * TOKEN_TILE)
        for first in range(2 * TOKEN_TILE, SORT_ROWS, SORT_CHUNK):
            @pl.when(used > first)
            def _(first=first):
                out_ref[...] += gathered(first, SORT_CHUNK)

    for slot in range(2):
        pl.when(t % 2 == slot)(functools.partial(step, slot))


def _combine_call(tab, used, meta, x2, ys):
    n = x2.shape[0]
    tm = TOKEN_TILE
    ntiles = n // tm
    tile = lambda i, tab, used: (jnp.maximum(i - 1, 0), 0)
    grid_spec = pltpu.PrefetchScalarGridSpec(
        num_scalar_prefetch=2,
        grid=(ntiles + 1,),
        in_specs=[pl.BlockSpec((tm, 128), tile),
                  pl.BlockSpec((tm, D_MODEL), tile),
                  pl.BlockSpec(memory_space=pl.ANY)],
        out_specs=pl.BlockSpec((tm, D_MODEL), tile),
        scratch_shapes=[pltpu.VMEM((SORT_ROWS, D_MODEL), _BF16), pltpu.VMEM((SORT_ROWS, D_MODEL), _BF16),
                        pltpu.SemaphoreType.DMA((2, RUN_BITS, N_EXPERTS))],
    )
    return pl.pallas_call(
        functools.partial(_combine_kernel, ntiles=ntiles),
        grid_spec=grid_spec,
        out_shape=jax.ShapeDtypeStruct((n, D_MODEL), _F32),
        compiler_params=_params(1),
        name="moe_combine",
    )(tab, used, meta, x2, ys)


def _moe_tables(cnt, n_tokens):
    ntiles = n_tokens // TOKEN_TILE
    counts = cnt.reshape(ntiles, 8, 128)[:, 0, :N_EXPERTS].astype(jnp.int32)
    units = (counts + RUN_ALIGN - 1) // RUN_ALIGN
    padded = units * RUN_ALIGN
    local = jnp.cumsum(padded, axis=1) - padded
    total = jnp.sum(padded, axis=0)
    total_blk = ((total + MOE_BLOCK - 1) // MOE_BLOCK) * MOE_BLOCK
    ends = jnp.cumsum(total_blk)
    starts = ends - total_blk
    remote = starts[None, :] + jnp.cumsum(padded, axis=0) - padded
    bits = jnp.arange(RUN_BITS, dtype=jnp.int32)[None, :, None]
    has = (units[:, None, :] >> bits) & 1
    done = (units[:, None, :] & ((1 << bits) - 1)) * RUN_ALIGN
    rank = jnp.cumsum(has, axis=2) - 1
    place = jnp.logical_and(rank[..., None] == jnp.arange(N_EXPERTS, dtype=jnp.int32), has[..., None] == 1)
    compact = lambda rows: jnp.sum(jnp.where(place, rows[..., None], 0), axis=2)
    tab = jnp.concatenate([jnp.sum(has, axis=2, keepdims=True), compact(local[:, None, :] + done),
                           compact(remote[:, None, :] + done)], axis=-1).reshape(-1).astype(jnp.int32)
    n_blocks = _xs_rows(n_tokens) // MOE_BLOCK
    blk_start = jnp.arange(n_blocks, dtype=jnp.int32) * MOE_BLOCK
    blk_expert = jnp.minimum(jnp.sum((ends[None, :] <= blk_start[:, None]).astype(jnp.int32), axis=1),
                             N_EXPERTS - 1)
    n_valid = (ends[-1] // MOE_BLOCK).astype(jnp.int32).reshape(1)
    used = jnp.sum(padded, axis=1).astype(jnp.int32)
    return tab, used, blk_expert, n_valid


def _xs_rows(n_tokens):
    ntiles = n_tokens // TOKEN_TILE
    worst = 2 * n_tokens + ntiles * N_EXPERTS * (RUN_ALIGN - 1) + N_EXPERTS * (MOE_BLOCK - 1)
    return ((worst + MOE_BLOCK - 1) // MOE_BLOCK) * MOE_BLOCK


def _pad_last(a, width):
    return jnp.pad(a, [(0, 0)] * (a.ndim - 1) + [(0, width - a.shape[-1])])


def _swap_mid_heads(a, axis):
    shape = a.shape
    a = a.reshape(shape[:axis] + (4, 64) + shape[axis + 1:])
    a = jnp.take(a, jnp.asarray([0, 2, 1, 3]), axis=axis)
    return a.reshape(shape)


def _layer_weights(l, p):
    w_in = p["w_in"][l]
    a, b = w_in[:, :A_COLS], w_in[:, A_COLS:A_COLS + B_COLS]
    c = w_in[:, A_COLS + B_COLS:A_COLS + B_COLS + C_COLS]
    d = w_in[:, A_COLS + B_COLS + C_COLS:]
    zeros = lambda w: jnp.zeros((D_MODEL, w), _F32)
    half = MLA_ROPE // 2
    k_rope = a[:, 384:416]
    k_rope_swap = jnp.concatenate([k_rope[:, half:], k_rope[:, :half]], axis=1)
    w_in_p = jnp.concatenate([
        a[:, :384], zeros(64), k_rope, zeros(32), zeros(64), k_rope_swap, zeros(32), b,
        _swap_mid_heads(c[:, :256], 1), c[:, 256:384], c[:, 384:512], d], axis=1).astype(_BF16)
    assert w_in_p.shape[1] == IN_COLS_PADDED

    w_uq = p["mla_w_uq"][l].reshape(MLA_Q_RANK, MLA_HEADS, MLA_QK_DIM)
    w_uq_swap = jnp.concatenate([jnp.zeros_like(w_uq[..., :MLA_NOPE]), w_uq[..., MLA_NOPE + half:],
                                 w_uq[..., MLA_NOPE:MLA_NOPE + half]], axis=-1)
    w_uq = jnp.concatenate([_pad_last(w_uq, HEAD_LANES).reshape(MLA_Q_RANK, -1),
                            _pad_last(w_uq_swap, HEAD_LANES).reshape(MLA_Q_RANK, -1)], axis=1).astype(_BF16)

    def swap_gain(g):
        return _pad_last(jnp.concatenate([jnp.zeros((MLA_NOPE,), _F32), g[MLA_NOPE + half:],
                                          g[MLA_NOPE:MLA_NOPE + half]]), HEAD_LANES)[None]
    w_ukv = p["mla_w_ukv"][l].reshape(MLA_KV_RANK, MLA_HEADS, MLA_NOPE + MLA_V)
    k_nope = _pad_last(w_ukv[..., :MLA_NOPE], HEAD_LANES).reshape(MLA_KV_RANK, MLA_HEADS * HEAD_LANES)
    v_part = w_ukv[..., MLA_NOPE:].reshape(MLA_KV_RANK, MLA_HEADS * MLA_V)
    w_ukv_p = jnp.concatenate([k_nope, v_part], axis=1).astype(_BF16)

    eye = jnp.eye(len(POOL_WINDOWS), dtype=_F32)
    w_pool = jnp.einsum("gcd,gh->gchd", p["pool_w"][l], eye).reshape(256, 256).astype(_BF16)

    g_mo = p["mix_out_norm"][l]
    g_mo = jnp.concatenate([g_mo[:512], _swap_mid_heads(g_mo[512:768], 0), g_mo[768:]])
    w_mo = p["w_mix_out"][l]
    w_mo = jnp.concatenate([w_mo[:512], _swap_mid_heads(w_mo[512:768], 0), w_mo[768:]], axis=0)

    w_router = _pad_last(jnp.concatenate([p["w_expert"][l], p["w_group"][l]], axis=1), 128)
    w_router_hi = w_router.astype(_BF16)
    w_router_lo = (w_router - w_router_hi.astype(_F32)).astype(_BF16)
    b_router = jnp.concatenate([p["b_expert"][l], p["b_group"][l]])
    tm = TOKEN_TILE
    ltri = (jnp.arange(tm)[None, :] < jnp.arange(tm)[:, None]).astype(_BF16)
    utri = (jnp.arange(128)[:, None] < jnp.arange(128)[None, :]).astype(_BF16)
    tile2 = lambda g: jnp.concatenate([g, g])[None]
    return dict(
        g_mix=p["norm_mix"][l][None], w_in=w_in_p,
        g_cq=p["mla_g_cq"][l][None], w_uq=w_uq, g_ckv=p["mla_g_ckv"][l][None], w_ukv=w_ukv_p,
        g_q=_pad_last(p["mla_g_q"][l], HEAD_LANES)[None], g_k=_pad_last(p["mla_g_k"][l], HEAD_LANES)[None],
        g_q_swap=swap_gain(p["mla_g_q"][l]), g_k_swap=swap_gain(p["mla_g_k"][l]),
        w_pool=w_pool, pool_scale=p["pool_scale"][l][None], conv_w=p["conv_w"][l],
        conv_b=p["conv_b"][l][None], g_sq=tile2(p["swa_g_q"][l]), g_sk=tile2(p["swa_g_k"][l]),
        sinks=jnp.take(p["swa_sinks"][l], jnp.asarray([0, 2, 1, 3])),
        g_mo=g_mo[None], w_mo=w_mo.astype(_BF16), g_xa=p["norm_xa"][l][None],
        xa_w_q=p["xa_w_q"][l].astype(_BF16), g_xq=jnp.tile(p["xa_g_q"][l], 4)[None],
        xa_w_o=p["xa_w_o"][l].astype(_BF16), g_ffn=p["norm_ffn"][l][None],
        w_router_hi=w_router_hi, w_router_lo=w_router_lo, b_router=_pad_last(b_router, 128)[None],
        ltri=ltri, utri=utri,
    )


def kernel(x, mem, positions, norm_mix, w_in, mla_g_cq, mla_w_uq, mla_g_ckv, mla_w_ukv, mla_g_q, mla_g_k, pool_w, pool_scale, swa_g_q, swa_g_k, swa_sinks, conv_w, conv_b, mix_out_norm, w_mix_out, norm_xa, norm_mem, xa_w_q, xa_w_kv, xa_g_q, xa_g_k, xa_w_o, norm_ffn, w_group, b_group, w_expert, b_expert, w_gate, w_up, w_down):
    p = dict(norm_mix=norm_mix, w_in=w_in, mla_g_cq=mla_g_cq, mla_w_uq=mla_w_uq, mla_g_ckv=mla_g_ckv,
             mla_w_ukv=mla_w_ukv, mla_g_q=mla_g_q, mla_g_k=mla_g_k, pool_w=pool_w, pool_scale=pool_scale,
             swa_g_q=swa_g_q, swa_g_k=swa_g_k, swa_sinks=swa_sinks, conv_w=conv_w, conv_b=conv_b,
             mix_out_norm=mix_out_norm, w_mix_out=w_mix_out, norm_xa=norm_xa, xa_w_q=xa_w_q,
             xa_g_q=xa_g_q, xa_w_o=xa_w_o, norm_ffn=norm_ffn, w_group=w_group, b_group=b_group,
             w_expert=w_expert, b_expert=b_expert)
    batch, seq_len, _ = x.shape
    depth = w_in.shape[0]
    n = batch * seq_len
    assert seq_len % MIX_TILE == 0 and MIX_TILE % TOKEN_TILE == 0
    assert seq_len % PROJ_TILE == 0 and seq_len % ATTN_Q_TILE == 0
    assert mem.shape[1] == MEM_LEN

    xf = x.reshape(n, D_MODEL)
    tabs = _rope_tables(positions)
    kx, vx = _memkv_call(mem.reshape(batch * MEM_LEN, D_MODEL), norm_mem[:, None, :],
                         xa_w_kv.astype(_BF16), jnp.tile(xa_g_k, (1, 4))[:, None, :], depth, batch)
    for l in range(depth):
        lw = _layer_weights(l, p)
        qm, km, vm, yb, yc, yd = _proj_call(xf, lw, tabs, seq_len)
        ya = _mla_attn_call(qm, km, vm, batch, seq_len)
        x2, h3, meta, cnt = _mix_call(ya, yb, yc, yd, xf, lw, kx[l], vx[l], seq_len)
        tab, used, blk_expert, n_valid = _moe_tables(cnt, n)
        xs = _dispatch_call(tab, used, meta, h3, jnp.zeros((_xs_rows(n), XS_COLS), _BF16))
        ys = _ffn_call(blk_expert, n_valid, xs, w_gate, w_up, w_down, l)
        xf = _combine_call(tab, used, meta, x2, ys)
    return xf.reshape(batch, seq_len, D_MODEL)
```

```python
import functools

import jax
import jax.numpy as jnp
import numpy as np
from jax import lax
from jax.experimental import pallas as pl
from jax.experimental.pallas import tpu as pltpu

EPS = 1e-6
NEG_BIG = -1e30
LOG2_E = 1.4426950408889634
ROPE_THETA = 10000.0

D_MODEL = 1024
MEM_LEN = 256
GROUP_W = 256

MLA_HEADS = 4
MLA_Q_RANK = 256
MLA_KV_RANK = 128
MLA_NOPE = 64
MLA_ROPE = 32
MLA_QK_DIM = MLA_NOPE + MLA_ROPE
MLA_V = 64
HEAD_LANES = 128

POOL_WINDOWS = (2, 4, 8, 16)
POOL_GROUP = 64
POOL_HALO = 16

SWA_HEADS = 4
SWA_KV_HEADS = 2
SWA_HEAD_DIM = 64
SWA_WINDOW = 128

CONV_CH = 256

XA_HEADS = 4
XA_HEAD_DIM = 64

N_EXPERT_GROUPS = 4
EXPERTS_PER_GROUP = 8
N_EXPERTS = 32
D_EXPERT = 256
MOE_BLOCK = 512

A_COLS = MLA_Q_RANK + MLA_KV_RANK + MLA_ROPE
B_COLS = GROUP_W
C_COLS = (SWA_HEADS + 2 * SWA_KV_HEADS) * SWA_HEAD_DIM
COL_CQ, COL_CKV, COL_KROPE, COL_KROPE_SWAP = 0, 256, 384, 512
COL_POOL, COL_SWA_Q, COL_SWA_K, COL_SWA_V, COL_CONV = 640, 896, 1152, 1280, 1408
IN_COLS_PADDED = COL_CONV + 3 * CONV_CH

TOKEN_TILE = 512
PROJ_TILE = 512
MIX_TILE = 1024
ATTN_TILE = 512
ATTN_Q_TILE = 1024
RUN_ALIGN = 16
RUN_BITS = 6
SORT_ROWS = 2 * TOKEN_TILE + N_EXPERTS * RUN_ALIGN
SORT_CHUNK = 256
PIECE_COLS = 1 + 2 * N_EXPERTS
XS_COLS = D_MODEL + 128
VMEM_LIMIT = 56 * 1024 * 1024

_F32 = jnp.float32
_BF16 = jnp.bfloat16


def _params(n_axes):
    return pltpu.CompilerParams(dimension_semantics=("arbitrary",) * n_axes,
                                vmem_limit_bytes=VMEM_LIMIT)


def _dot(a, b):
    return jnp.dot(a, b, preferred_element_type=_F32)


def _dot_nt(a, b):
    return lax.dot_general(a, b, (((1,), (1,)), ((), ())), preferred_element_type=_F32)


def _rms_factor(x, width):
    ss = jnp.sum(x * x, axis=-1, keepdims=True)
    return lax.rsqrt(ss * (1.0 / width) + EPS)


def _rms_scale(x, width):
    return x * _rms_factor(x, width)


def _pair_norm64(x, lane_lo):
    x2 = x * x
    s_all = jnp.sum(x2, axis=-1, keepdims=True)
    s_lo = jnp.sum(jnp.where(lane_lo, x2, 0.0), axis=-1, keepdims=True)
    ss = jnp.where(lane_lo, s_lo, s_all - s_lo)
    return x * lax.rsqrt(ss * (1.0 / 64.0) + EPS)


def _rope_kernel(pos_ref, freq_ref, c_ref, s_ref):
    ang = pos_ref[...] * freq_ref[...]
    c_ref[...] = jnp.cos(ang)
    s_ref[...] = jnp.sin(ang)


def _rope_tables(positions):
    half = MLA_ROPE // 2
    per_row = HEAD_LANES // half
    n = positions.size
    inv_freq = ROPE_THETA ** (-jnp.arange(half, dtype=_F32) / half)
    pos = jnp.repeat(positions.astype(_F32).reshape(n // per_row, per_row), half, axis=1)
    rows = n // per_row
    tile = min(rows, 1024)
    tab = pl.BlockSpec((tile, HEAD_LANES), lambda i: (i, 0))
    shp = jax.ShapeDtypeStruct((rows, HEAD_LANES), _F32)
    cos, sin = pl.pallas_call(
        _rope_kernel,
        grid=(rows // tile,),
        in_specs=[tab, pl.BlockSpec((1, HEAD_LANES), lambda i: (0, 0))],
        out_specs=[tab, tab],
        out_shape=[shp, shp],
        compiler_params=_params(1),
        name="rope_tables",
    )(pos, jnp.tile(inv_freq, per_row)[None])
    cos, sin = cos.reshape(n, half), sin.reshape(n, half)
    pad = jnp.zeros((n, HEAD_LANES - MLA_QK_DIM), _F32)
    c_tab = jnp.concatenate([jnp.ones((n, MLA_NOPE), _F32), cos, cos, pad], axis=1)
    s_tab = jnp.concatenate([jnp.zeros((n, MLA_NOPE), _F32), -sin, sin, pad], axis=1)
    return c_tab, s_tab


def _swa_tile(q, k, v, k_before, v_before, sink_ref, first_of_sequence):
    w = SWA_WINDOW
    oldest = jnp.where(first_of_sequence, 0, -w)
    lane_lo = lax.broadcasted_iota(jnp.int32, (1, 128), 1) < 64
    qpos = lax.broadcasted_iota(jnp.int32, (w, 2 * w), 0)
    kpos = lax.broadcasted_iota(jnp.int32, (w, 2 * w), 1) - w
    band = jnp.logical_and(kpos <= qpos, kpos > qpos - w)

    out_rows = []
    for jb in range(q.shape[0] // w):
        rs = slice(jb * w, (jb + 1) * w)
        if jb == 0:
            kprev, vprev = k_before, v_before
            visible = jnp.logical_and(band, kpos >= oldest)
        else:
            ps = slice((jb - 1) * w, jb * w)
            kprev, vprev = k[ps, :], v[ps, :]
            visible = band
        kk = jnp.concatenate([kprev, k[rs, :]], axis=0)
        vv = jnp.concatenate([vprev, v[rs, :]], axis=0)
        zero = jnp.zeros_like(kk)
        k_half = (jnp.where(lane_lo, kk, zero), jnp.where(lane_lo, zero, kk))
        pair_out = []
        for blk in range(2):
            qblk = q[rs, blk * 128:(blk + 1) * 128]
            outs = []
            for half in range(2):
                sink = sink_ref[2 * blk + half]
                s = _dot_nt(qblk, k_half[half])
                s = jnp.where(visible, s, NEG_BIG)
                m = jnp.maximum(jnp.max(s, axis=-1, keepdims=True), sink)
                p = jnp.exp(s - m)
                denom = jnp.sum(p, axis=-1, keepdims=True) + jnp.exp(sink - m)
                outs.append(_dot(p.astype(_BF16), vv) / denom)
            pair_out.append(jnp.where(lane_lo, outs[0], outs[1]))
        out_rows.append(jnp.concatenate(pair_out, axis=1))
    return jnp.concatenate(out_rows, axis=0)


def _proj_kernel(x_ref, gmix_ref, win_ref, gcq_ref, wuq_ref, gckv_ref, wukv_ref, gq_ref, gqs_ref,
                 gk_ref, gks_ref, c_ref, s_ref, wpool_ref, pscale_ref, convw_ref, convb_ref, gsq_ref,
                 gsk_ref, sink_ref, qm_ref, km_ref, vm_ref, yb_ref, yc_ref, yd_ref, halo_ref, swa_halo_ref,
                 *, tiles_per_seq):
    tm = PROJ_TILE
    i = pl.program_id(0)
    seq_tile = i % tiles_per_seq

    x = x_ref[...]
    h = _rms_scale(x, D_MODEL) * gmix_ref[...]
    u = _dot(h.astype(_BF16), win_ref[...])

    c = c_ref[...]
    s = s_ref[...]
    hw = MLA_HEADS * HEAD_LANES

    cq = _rms_scale(u[:, COL_CQ:COL_CQ + MLA_Q_RANK], MLA_Q_RANK) * gcq_ref[...]
    q = _dot(cq.astype(_BF16), wuq_ref[...])
    gq = gq_ref[...] * (MLA_QK_DIM ** -0.5 * LOG2_E)
    gqs = gqs_ref[...] * (MLA_QK_DIM ** -0.5 * LOG2_E)
    for hd in range(MLA_HEADS):
        sl = slice(hd * HEAD_LANES, (hd + 1) * HEAD_LANES)
        xq = q[:, sl]
        r = _rms_factor(xq, MLA_QK_DIM)
        qm_ref[:, sl] = ((xq * r) * (gq * c) + (q[:, hw + sl.start:hw + sl.stop] * r) * (gqs * s)).astype(_BF16)

    ckv = _rms_scale(u[:, COL_CKV:COL_CKV + MLA_KV_RANK], MLA_KV_RANK) * gckv_ref[...]
    kv = _dot(ckv.astype(_BF16), wukv_ref[...])
    krope = u[:, COL_KROPE:COL_KROPE + HEAD_LANES]
    krope_swap = u[:, COL_KROPE_SWAP:COL_KROPE_SWAP + HEAD_LANES]
    gk = gk_ref[...]
    gks = gks_ref[...]
    for hd in range(MLA_HEADS):
        sl = slice(hd * HEAD_LANES, (hd + 1) * HEAD_LANES)
        xk = kv[:, sl] + krope
        r = _rms_factor(xk, MLA_QK_DIM)
        km_ref[:, sl] = ((xk * r) * (gk * c) + (krope_swap * r) * (gks * s)).astype(_BF16)
    vm_ref[...] = kv[:, hw:hw + MLA_HEADS * MLA_V].astype(_BF16)

    lane_lo = lax.broadcasted_iota(jnp.int32, (1, HEAD_LANES), 1) < 64
    gsq = gsq_ref[...] * (SWA_HEAD_DIM ** -0.5)
    qs = jnp.concatenate(
        [(_pair_norm64(u[:, COL_SWA_Q + blk * 128:COL_SWA_Q + (blk + 1) * 128], lane_lo) * gsq).astype(_BF16)
         for blk in range(2)], axis=1)
    ks = (_pair_norm64(u[:, COL_SWA_K:COL_SWA_K + 128], lane_lo) * gsk_ref[...]).astype(_BF16)
    vs = u[:, COL_SWA_V:COL_SWA_V + 128].astype(_BF16)

    @pl.when(seq_tile == 0)
    def _():
        halo_ref[...] = jnp.zeros_like(halo_ref)
        swa_halo_ref[...] = jnp.zeros_like(swa_halo_ref)

    swa_halo = swa_halo_ref[...].astype(_BF16)
    yc_ref[...] = _swa_tile(qs, ks, vs, swa_halo[:, 0:128], swa_halo[:, 128:256], sink_ref,
                            seq_tile == 0).astype(_BF16)
    swa_halo_ref[:, 0:128] = ks[tm - SWA_WINDOW:, :].astype(_F32)
    swa_halo_ref[:, 128:256] = vs[tm - SWA_WINDOW:, :].astype(_F32)

    halo = halo_ref[...]
    up = u[:, COL_POOL:COL_POOL + GROUP_W]
    ud = u[:, COL_CONV:COL_CONV + 3 * CONV_CH]

    b = jnp.concatenate([halo[:, 0:256], up], axis=0)
    w2 = b + pltpu.roll(b, 1, 0)
    w4 = w2 + pltpu.roll(w2, 2, 0)
    w8 = w4 + pltpu.roll(w4, 4, 0)
    w16 = w8 + pltpu.roll(w8, 8, 0)
    lane = lax.broadcasted_iota(jnp.int32, (1, 256), 1)
    win = jnp.where(lane < 64, w2, jnp.where(lane < 128, w4, jnp.where(lane < 192, w8, w16)))
    win = win[POOL_HALO:, :]
    width = jnp.where(lane < 64, 2, jnp.where(lane < 128, 4, jnp.where(lane < 192, 8, 16)))
    t = seq_tile * tm + lax.broadcasted_iota(jnp.int32, (tm, 1), 0)
    count = jnp.minimum(t + 1, width).astype(_F32)
    pooled = win / count - up
    yb_ref[...] = (_dot(pooled.astype(_BF16), wpool_ref[...]) * pscale_ref[...]).astype(_BF16)

    z = ud[:, 256:512] * ud[:, 512:768]
    zh = halo[:, 512:768] * halo[:, 768:1024]
    zb = jnp.concatenate([zh, z], axis=0)
    cw = convw_ref[...]
    conv = (pltpu.roll(zb, 2, 0)[POOL_HALO:, :] * cw[0:1, :]
            + pltpu.roll(zb, 1, 0)[POOL_HALO:, :] * cw[1:2, :]
            + z * cw[2:3, :])
    yd_ref[...] = (ud[:, 0:256] * (conv + convb_ref[...])).astype(_BF16)

    halo_ref[:, 0:256] = up[tm - POOL_HALO:, :]
    halo_ref[:, 256:1024] = ud[tm - POOL_HALO:, :]


def _proj_call(x, lw, tabs, seq_len):
    n = x.shape[0]
    tm = PROJ_TILE
    tiles_per_seq = seq_len // tm

    def full(a):
        nd = a.ndim
        return pl.BlockSpec(a.shape, lambda i, _nd=nd: (0,) * _nd)

    def rows(width):
        return pl.BlockSpec((tm, width), lambda i: (i, 0))

    consts = [lw["g_mix"], lw["w_in"], lw["g_cq"], lw["w_uq"], lw["g_ckv"], lw["w_ukv"],
              lw["g_q"], lw["g_q_swap"], lw["g_k"], lw["g_k_swap"]]
    consts2 = [lw["w_pool"], lw["pool_scale"], lw["conv_w"], lw["conv_b"], lw["g_sq"], lw["g_sk"]]
    out_widths = [512, 512, 256, 256, 256, 256]
    return pl.pallas_call(
        functools.partial(_proj_kernel, tiles_per_seq=tiles_per_seq),
        grid=(n // tm,),
        in_specs=[rows(D_MODEL)] + [full(a) for a in consts] + [rows(HEAD_LANES)] * 2
                 + [full(a) for a in consts2] + [pl.BlockSpec(memory_space=pltpu.SMEM)],
        out_specs=[rows(w) for w in out_widths],
        out_shape=[jax.ShapeDtypeStruct((n, w), _BF16) for w in out_widths],
        scratch_shapes=[pltpu.VMEM((POOL_HALO, 1024), _F32), pltpu.VMEM((SWA_WINDOW, 256), _F32)],
        compiler_params=_params(1),
        name="proj_in",
    )(x, *consts, *tabs, *consts2, lw["sinks"])


def _mla_attn_kernel(q_ref, k_hbm, v_hbm, o_ref, kbuf, vbuf, sems, m_ref, l_ref, acc_ref, *, seq_len):
    tq, tk = ATTN_Q_TILE, ATTN_TILE
    ratio = tq // tk
    assert ratio == 2
    b = pl.program_id(0)
    qb = pl.program_id(1)
    row0 = b * seq_len

    def fetch(j, slot):
        rows = pl.ds(pl.multiple_of(row0 + j * tk, tk), tk)
        return (pltpu.make_async_copy(k_hbm.at[rows], kbuf.at[slot], sems.at[0, slot]),
                pltpu.make_async_copy(v_hbm.at[rows], vbuf.at[slot], sems.at[1, slot]))

    def start(j, slot):
        for cp in fetch(j, slot):
            cp.start()

    def wait(j, slot):
        for cp in fetch(j, slot):
            cp.wait()

    m_ref[...] = jnp.full_like(m_ref, NEG_BIG)
    l_ref[...] = jnp.zeros_like(l_ref)
    acc_ref[...] = jnp.zeros_like(acc_ref)
    lane_lo = lax.broadcasted_iota(jnp.int32, (1, HEAD_LANES), 1) < 64

    def accumulate(slot, rows, diagonal_from):
        nrows = rows.stop - rows.start
        if diagonal_from is not None:
            visible = (lax.broadcasted_iota(jnp.int32, (nrows, tk), 1)
                       <= lax.broadcasted_iota(jnp.int32, (nrows, tk), 0) + (rows.start - diagonal_from))
        for pair in range(MLA_HEADS // 2):
            vblk = vbuf[slot, :, pair * 128:(pair + 1) * 128]
            alphas = []
            pvs = []
            for sub in range(2):
                hd = 2 * pair + sub
                sl = slice(hd * HEAD_LANES, (hd + 1) * HEAD_LANES)
                s = _dot_nt(q_ref[rows, sl], kbuf[slot, :, sl])
                if diagonal_from is not None:
                    s = jnp.where(visible, s, NEG_BIG)
                m_prev = m_ref[hd, rows, :]
                m_new = jnp.maximum(m_prev, jnp.max(s, axis=-1, keepdims=True))
                alpha = jnp.exp2(m_prev - m_new)
                p = jnp.exp2(s - jnp.tile(m_new, (1, tk // HEAD_LANES)))
                l_ref[hd, rows, :] = alpha * l_ref[hd, rows, :] + jnp.sum(p, axis=-1, keepdims=True)
                m_ref[hd, rows, :] = m_new
                alphas.append(alpha)
                pvs.append(_dot(p.astype(_BF16), vblk))
            psl = slice(pair * 128, (pair + 1) * 128)
            acc_ref[rows, psl] = (acc_ref[rows, psl] * jnp.where(lane_lo, alphas[0], alphas[1])
                                  + jnp.where(lane_lo, pvs[0], pvs[1]))

    everything = slice(0, tq)
    start(0, 0)

    def two_tiles(pair_idx, carry):
        j = pair_idx * ratio
        wait(j, 0)
        start(j + 1, 1)
        accumulate(0, everything, None)
        wait(j + 1, 1)
        start(j + 2, 0)
        accumulate(1, everything, None)
        return carry

    lax.fori_loop(0, qb, two_tiles, 0)

    j = qb * ratio
    wait(j, 0)
    start(j + 1, 1)
    accumulate(0, everything, 0)
    wait(j + 1, 1)
    accumulate(1, slice(tk, tq), tk)

    for pair in range(MLA_HEADS // 2):
        psl = slice(pair * 128, (pair + 1) * 128)
        denom = jnp.where(lane_lo, l_ref[2 * pair], l_ref[2 * pair + 1])
        o_ref[:, psl] = (acc_ref[:, psl] / denom).astype(o_ref.dtype)


def _mla_attn_call(qm, km, vm, batch, seq_len):
    tq, tk = ATTN_Q_TILE, ATTN_TILE
    nq = seq_len // tq
    return pl.pallas_call(
        functools.partial(_mla_attn_kernel, seq_len=seq_len),
        grid=(batch, nq),
        in_specs=[pl.BlockSpec((tq, 512), lambda b, q: (b * nq + q, 0)),
                  pl.BlockSpec(memory_space=pl.ANY), pl.BlockSpec(memory_space=pl.ANY)],
        out_specs=pl.BlockSpec((tq, 256), lambda b, q: (b * nq + q, 0)),
        out_shape=jax.ShapeDtypeStruct((batch * seq_len, 256), _BF16),
        scratch_shapes=[pltpu.VMEM((2, tk, 512), _BF16), pltpu.VMEM((2, tk, 256), _BF16),
                        pltpu.SemaphoreType.DMA((2, 2)),
                        pltpu.VMEM((MLA_HEADS, tq, HEAD_LANES), _F32),
                        pltpu.VMEM((MLA_HEADS, tq, HEAD_LANES), _F32),
                        pltpu.VMEM((tq, 256), _F32)],
        compiler_params=_params(2),
        name="mla_attention",
    )(qm, km, vm)


def _memkv_kernel(mem_ref, gmem_ref, wkv_ref, gk_ref, k_ref, v_ref):
    m = _rms_scale(mem_ref[...], D_MODEL) * gmem_ref[...]
    kv = _dot(m.astype(_BF16), wkv_ref[...])
    lane_lo = lax.broadcasted_iota(jnp.int32, (1, 128), 1) < 64
    lane = lax.broadcasted_iota(jnp.int32, (1, 256), 1)
    k = jnp.concatenate([_pair_norm64(kv[:, 0:128], lane_lo), _pair_norm64(kv[:, 128:256], lane_lo)],
                        axis=1) * gk_ref[...]
    v = kv[:, 256:512]
    for hd in range(XA_HEADS):
        own = jnp.logical_and(lane >= hd * XA_HEAD_DIM, lane < (hd + 1) * XA_HEAD_DIM)
        k_ref[hd] = jnp.where(own, k, 0.0).astype(_BF16)
        v_ref[hd] = jnp.where(own, v, 0.0).astype(_BF16)


def _memkv_call(mem2d, g_mem, w_kv, g_k4, depth, batch):
    out = jax.ShapeDtypeStruct((depth, batch, XA_HEADS, MEM_LEN, 256), _BF16)
    ospec = pl.BlockSpec((None, None, XA_HEADS, MEM_LEN, 256), lambda l, b: (l, b, 0, 0, 0))
    return pl.pallas_call(
        _memkv_kernel,
        grid=(depth, batch),
        in_specs=[pl.BlockSpec((MEM_LEN, D_MODEL), lambda l, b: (b, 0)),
                  pl.BlockSpec((None, 1, D_MODEL), lambda l, b: (l, 0, 0)),
                  pl.BlockSpec((None, D_MODEL, 512), lambda l, b: (l, 0, 0)),
                  pl.BlockSpec((None, 1, 256), lambda l, b: (l, 0, 0))],
        out_specs=[ospec, ospec],
        out_shape=[out, out],
        compiler_params=_params(2),
        name="memory_kv",
    )(mem2d, g_mem, w_kv, g_k4)


def _mix_kernel(ya_ref, yb_ref, yc_ref, yd_ref, x_ref, gmo_ref, wmo_ref, gxa_ref, wq_ref, gxq_ref,
                kx_ref, vx_ref, wo_ref, gffn_ref, wrh_ref, wrl_ref, br_ref, ltri_ref, utri_ref,
                x2_ref, h3_ref, meta_ref, cnt_ref):
    tm = MIX_TILE
    gmo = gmo_ref[...]
    lane_lo = lax.broadcasted_iota(jnp.int32, (1, 128), 1) < 64

    def rows_to_logits(rs):
        parts = []
        for g, ref in enumerate((ya_ref, yb_ref, yc_ref, yd_ref)):
            yg = ref[rs, :].astype(_F32)
            parts.append((_rms_scale(yg, GROUP_W) * gmo[:, g * 256:(g + 1) * 256]).astype(_BF16))
        y = jnp.concatenate(parts, axis=1)
        x1 = x_ref[rs, :] + _dot(y, wmo_ref[...])

        h = (_rms_scale(x1, D_MODEL) * gxa_ref[...]).astype(_BF16)
        q = _dot(h, wq_ref[...])
        qn = jnp.concatenate([_pair_norm64(q[:, 0:128], lane_lo), _pair_norm64(q[:, 128:256], lane_lo)],
                             axis=1)
        qn = (qn * gxq_ref[...] * (XA_HEAD_DIM ** -0.5)).astype(_BF16)
        o = jnp.zeros((x1.shape[0], 256), _F32)
        for hd in range(XA_HEADS):
            s = _dot_nt(qn, kx_ref[hd])
            m = jnp.max(s, axis=-1, keepdims=True)
            p = jnp.exp(s - m)
            denom = jnp.sum(p, axis=-1, keepdims=True)
            o = o + _dot(p.astype(_BF16), vx_ref[hd]) / denom
        x2 = x1 + _dot(o.astype(_BF16), wo_ref[...])
        x2_ref[rs, :] = x2

        h3 = _rms_scale(x2, D_MODEL) * gffn_ref[...]
        h3_hi = h3.astype(_BF16)
        h3_lo = (h3 - h3_hi.astype(_F32)).astype(_BF16)
        h3_ref[rs, :] = h3_hi
        return (_dot(h3_hi, wrh_ref[...]) + (_dot(h3_hi, wrl_ref[...]) + _dot(h3_lo, wrh_ref[...]))
                + br_ref[...])

    all_logits = rows_to_logits(slice(0, tm))
    for st in range(tm // TOKEN_TILE):
        meta, counts = _route(all_logits[st * TOKEN_TILE:(st + 1) * TOKEN_TILE, :], ltri_ref, utri_ref)
        meta_ref[st * TOKEN_TILE:(st + 1) * TOKEN_TILE, :] = meta
        cnt_ref[st * 8:(st + 1) * 8, :] = jnp.broadcast_to(counts, (8, 128))


def _route(logits, ltri_ref, utri_ref):
    lane = lax.broadcasted_iota(jnp.int32, (TOKEN_TILE, 128), 1).astype(_F32)
    far = 1e9
    is_group = jnp.logical_and(lane >= N_EXPERTS, lane < N_EXPERTS + N_EXPERT_GROUPS)
    gl = jnp.where(is_group, logits, -jnp.inf)
    gmax = jnp.max(gl, axis=-1, keepdims=True)
    gidx = jnp.min(jnp.where(gl == gmax, lane, far), axis=-1, keepdims=True) - N_EXPERTS
    g_w = 1.0 / jnp.sum(jnp.where(is_group, jnp.exp(logits - gmax), 0.0), axis=-1, keepdims=True)
    in_group = jnp.logical_and(lane < N_EXPERTS, jnp.floor(lane * (1.0 / EXPERTS_PER_GROUP)) == gidx)
    el = jnp.where(in_group, logits, -jnp.inf)
    emax = jnp.max(el, axis=-1, keepdims=True)
    ep = jnp.where(in_group, jnp.exp(logits - emax), 0.0)
    prob = jnp.where(in_group, ep / jnp.sum(ep, axis=-1, keepdims=True), -1.0)
    p1 = jnp.max(prob, axis=-1, keepdims=True)
    e0 = jnp.min(jnp.where(prob == p1, lane, far), axis=-1, keepdims=True)
    prob2 = jnp.where(lane == e0, -1.0, prob)
    p2 = jnp.max(prob2, axis=-1, keepdims=True)
    e1 = jnp.min(jnp.where(prob2 == p2, lane, far), axis=-1, keepdims=True)
    w0 = g_w * (p1 / (p1 + p2))
    w1 = g_w * (p2 / (p1 + p2))

    onehot = jnp.where(jnp.logical_or(lane == e0, lane == e1), 1.0, 0.0)
    prefix = _dot(ltri_ref[...], onehot.astype(_BF16))
    counts = jnp.sum(onehot, axis=0, keepdims=True)
    units = jnp.floor((counts + (RUN_ALIGN - 1)) * (1.0 / RUN_ALIGN))
    offs = _dot(jnp.broadcast_to(units, (8, 128)).astype(_BF16), utri_ref[...])[0:1, :] * RUN_ALIGN
    where_to = prefix + offs
    pos0 = jnp.sum(jnp.where(lane == e0, where_to, 0.0), axis=-1, keepdims=True)
    pos1 = jnp.sum(jnp.where(lane == e1, where_to, 0.0), axis=-1, keepdims=True)

    meta = jnp.where(lane == 0, e0,
           jnp.where(lane == 1, e1,
           jnp.where(lane == 2, w0,
           jnp.where(lane == 3, w1,
           jnp.where(lane == 4, pos0,
           jnp.where(lane == 5, pos1, 0.0))))))
    return meta, counts


def _mix_call(ya, yb, yc, yd, x, lw, kx, vx, seq_len):
    n = x.shape[0]
    tm = MIX_TILE
    tiles_per_seq = seq_len // tm

    def full(a):
        nd = a.ndim
        return pl.BlockSpec(a.shape, lambda i, _nd=nd: (0,) * _nd)

    def rows(width):
        return pl.BlockSpec((tm, width), lambda i: (i, 0))

    kvspec = pl.BlockSpec((None, XA_HEADS, MEM_LEN, 256), lambda i: (i // tiles_per_seq, 0, 0, 0))
    consts_a = [lw["g_mo"], lw["w_mo"], lw["g_xa"], lw["xa_w_q"], lw["g_xq"]]
    consts_b = [lw["xa_w_o"], lw["g_ffn"], lw["w_router_hi"], lw["w_router_lo"], lw["b_router"],
                lw["ltri"], lw["utri"]]
    ntiles = n // tm
    return pl.pallas_call(
        _mix_kernel,
        grid=(ntiles,),
        in_specs=[rows(256)] * 4 + [rows(D_MODEL)] + [full(a) for a in consts_a]
                 + [kvspec, kvspec] + [full(a) for a in consts_b],
        out_specs=[rows(D_MODEL), rows(D_MODEL), rows(128),
                   pl.BlockSpec((8 * (tm // TOKEN_TILE), 128), lambda i: (i, 0))],
        out_shape=[jax.ShapeDtypeStruct((n, D_MODEL), _F32), jax.ShapeDtypeStruct((n, D_MODEL), _BF16),
                   jax.ShapeDtypeStruct((n, 128), _F32),
                   jax.ShapeDtypeStruct((n // TOKEN_TILE * 8, 128), _F32)],
        compiler_params=_params(1),
        name="mix_xattn_router",
    )(ya, yb, yc, yd, x, *consts_a, kx, vx, *consts_b)


def _move_runs(tab_ref, t, live, local_ref, remote_ref, sems, *, to_remote, wait):
    for bit in range(RUN_BITS):
        rows = RUN_ALIGN << bit
        base = (t * RUN_BITS + bit) * PIECE_COLS
        count = jnp.where(live, tab_ref[base], 0)

        def one(p, carry, base=base, rows=rows, bit=bit):
            loc = pl.multiple_of(tab_ref[base + 1 + p], RUN_ALIGN)
            rem = pl.multiple_of(tab_ref[base + 1 + N_EXPERTS + p], RUN_ALIGN)
            lsl = local_ref.at[pl.ds(loc, rows)]
            rsl = remote_ref.at[pl.ds(rem, rows)]
            src, dst = (lsl, rsl) if to_remote else (rsl, lsl)
            copy = pltpu.make_async_copy(src, dst, sems.at[bit, p])
            if wait:
                copy.wait()
            else:
                copy.start()
            return carry

        lax.fori_loop(0, count, one, 0)


def _selection(meta, first, rows):
    r = first + lax.broadcasted_iota(jnp.int32, (TOKEN_TILE, rows), 1)
    pos0 = meta[:, 4:5].astype(jnp.int32)
    pos1 = meta[:, 5:6].astype(jnp.int32)
    return jnp.where(r == pos0, 1.0, 0.0).astype(_BF16), jnp.where(r == pos1, 1.0, 0.0).astype(_BF16)


def _dispatch_kernel(tab_ref, used_ref, meta_ref, h3_ref, xs_in_ref, xs_ref, buf0_ref, buf1_ref, sems,
                     *, ntiles):
    del xs_in_ref
    t = pl.program_id(0)
    last = ntiles - 1
    bufs = (buf0_ref, buf1_ref)
    used = used_ref[jnp.minimum(t, last)]

    def step(slot):
        mine, other = bufs[slot], bufs[1 - slot]
        _move_runs(tab_ref, jnp.clip(t - 2, 0, last), t >= 2, mine, xs_ref, sems.at[slot],
                   to_remote=True, wait=True)
        _move_runs(tab_ref, jnp.clip(t - 1, 0, last), jnp.logical_and(t >= 1, t <= ntiles), other, xs_ref,
                   sems.at[1 - slot], to_remote=True, wait=False)

        meta = meta_ref[...]
        lane = lax.broadcasted_iota(jnp.int32, (TOKEN_TILE, 128), 1)
        cols = []
        for k in range(2):
            w = meta[:, 2 + k:3 + k]
            hi = w.astype(_BF16).astype(_F32)
            mid = (w - hi).astype(_BF16).astype(_F32)
            lo = (w - hi) - mid
            cols.append(jnp.where(lane == 0, hi, jnp.where(lane == 1, mid, jnp.where(lane == 2, lo, 0.0)))
                        .astype(_BF16))

        def sort_rows(first, rows):
            sel0, sel1 = _selection(meta, first, rows)
            tn = (((0,), (0,)), ((), ()))
            mine[first:first + rows, 0:D_MODEL] = lax.dot_general(
                sel0 + sel1, h3_ref[...], tn, preferred_element_type=_F32).astype(_BF16)
            mine[first:first + rows, D_MODEL:XS_COLS] = (
                lax.dot_general(sel0, cols[0], tn, preferred_element_type=_F32)
                + lax.dot_general(sel1, cols[1], tn, preferred_element_type=_F32)).astype(_BF16)

        sort_rows(0, 2 * TOKEN_TILE)
        for first in range(2 * TOKEN_TILE, SORT_ROWS, SORT_CHUNK):
            pl.when(used > first)(functools.partial(sort_rows, first, SORT_CHUNK))

    for slot in range(2):
        pl.when(t % 2 == slot)(functools.partial(step, slot))


def _dispatch_call(tab, used, meta, h3, xs_zero):
    n = h3.shape[0]
    tm = TOKEN_TILE
    ntiles = n // tm
    tile = lambda i, tab, used: (jnp.minimum(i, ntiles - 1), 0)
    grid_spec = pltpu.PrefetchScalarGridSpec(
        num_scalar_prefetch=2,
        grid=(ntiles + 2,),
        in_specs=[pl.BlockSpec((tm, 128), tile),
                  pl.BlockSpec((tm, D_MODEL), tile),
                  pl.BlockSpec(memory_space=pl.ANY)],
        out_specs=pl.BlockSpec(memory_space=pl.ANY),
        scratch_shapes=[pltpu.VMEM((SORT_ROWS, XS_COLS), _BF16), pltpu.VMEM((SORT_ROWS, XS_COLS), _BF16),
                        pltpu.SemaphoreType.DMA((2, RUN_BITS, N_EXPERTS))],
    )
    return pl.pallas_call(
        functools.partial(_dispatch_kernel, ntiles=ntiles),
        grid_spec=grid_spec,
        out_shape=jax.ShapeDtypeStruct(xs_zero.shape, xs_zero.dtype),
        input_output_aliases={4: 0},
        compiler_params=_params(1),
        name="moe_dispatch",
    )(tab, used, meta, h3, xs_zero)


def _ffn_kernel(first_ref, count_ref, xs_hbm, wg_ref, wu_ref, wd_ref, ys_hbm, xin, yout, in_sems, out_sems,
                wgu_ref, wdn_ref, *, n_blocks):
    e = pl.program_id(0)
    first = first_ref[e]
    count = count_ref[e]

    wgu_ref[:, 0:D_EXPERT] = wg_ref[...].astype(_BF16)
    wgu_ref[:, D_EXPERT:2 * D_EXPERT] = wu_ref[...].astype(_BF16)
    wdn_ref[...] = wd_ref[...].astype(_BF16)

    def block_rows(blk):
        return pl.ds(pl.multiple_of(blk * MOE_BLOCK, MOE_BLOCK), MOE_BLOCK)

    def fetch(i, slot):
        return pltpu.make_async_copy(xs_hbm.at[block_rows(first + i)], xin.at[slot], in_sems.at[slot])

    def write_back(blk, slot):
        return pltpu.make_async_copy(yout.at[slot], ys_hbm.at[block_rows(blk)], out_sems.at[slot])

    @pl.when(count > 0)
    def _():
        fetch(0, 0).start()

    def one_block(i, carry):
        slot = i % 2
        fetch(i, slot).wait()

        @pl.when(i + 1 < count)
        def _():
            fetch(i + 1, 1 - slot).start()

        @pl.when(i >= 2)
        def _():
            write_back(first + i - 2, slot).wait()

        xb = xin[slot, :, 0:D_MODEL]
        wcols = xin[slot, :, D_MODEL:XS_COLS].astype(_F32)
        wt = wcols[:, 0:1] + wcols[:, 1:2] + wcols[:, 2:3]
        gu = _dot(xb, wgu_ref[...])
        g = gu[:, 0:D_EXPERT]
        a = (g * jax.nn.sigmoid(g)) * gu[:, D_EXPERT:2 * D_EXPERT]
        y = _dot(a.astype(_BF16), wdn_ref[...])
        yout[slot] = (y * wt).astype(yout.dtype)
        write_back(first + i, slot).start()
        return carry

    lax.fori_loop(0, count, one_block, 0)

    for back in (1, 2):
        @pl.when(count >= back)
        def _(back=back):
            write_back(first + count - back, (count - back) % 2).wait()

    @pl.when(e == N_EXPERTS - 1)
    def _():
        yout[0] = jnp.zeros(yout.shape[1:], yout.dtype)

        def zero_block(blk, carry):
            copy = write_back(blk, 0)
            copy.start()
            copy.wait()
            return carry

        lax.fori_loop(first + count, n_blocks, zero_block, 0)


def _ffn_call(blk_first, blk_count, xs, w_gate, w_up, w_down, layer):
    n_blocks = xs.shape[0] // MOE_BLOCK
    wspec = lambda shape: pl.BlockSpec((None, None) + shape, lambda e, first, count: (layer, e, 0, 0))
    grid_spec = pltpu.PrefetchScalarGridSpec(
        num_scalar_prefetch=2,
        grid=(N_EXPERTS,),
        in_specs=[pl.BlockSpec(memory_space=pl.ANY),
                  wspec((D_MODEL, D_EXPERT)), wspec((D_MODEL, D_EXPERT)), wspec((D_EXPERT, D_MODEL))],
        out_specs=pl.BlockSpec(memory_space=pl.ANY),
        scratch_shapes=[pltpu.VMEM((2, MOE_BLOCK, XS_COLS), _BF16), pltpu.VMEM((2, MOE_BLOCK, D_MODEL), _BF16),
                        pltpu.SemaphoreType.DMA((2,)), pltpu.SemaphoreType.DMA((2,)),
                        pltpu.VMEM((D_MODEL, 2 * D_EXPERT), _BF16), pltpu.VMEM((D_EXPERT, D_MODEL), _BF16)],
    )
    return pl.pallas_call(
        functools.partial(_ffn_kernel, n_blocks=n_blocks),
        grid_spec=grid_spec,
        out_shape=jax.ShapeDtypeStruct((xs.shape[0], D_MODEL), _BF16),
        compiler_params=_params(1),
        name="moe_experts",
    )(blk_first, blk_count, xs, w_gate, w_up, w_down)


def _combine_kernel(tab_ref, used_ref, meta_ref, x2_ref, ys_ref, out_ref, buf0_ref, buf1_ref, sems,
                    *, ntiles):
    t = pl.program_id(0)
    last = ntiles - 1
    bufs = (buf0_ref, buf1_ref)
    used = used_ref[jnp.clip(t - 1, 0, last)]

    @pl.when(t == 0)
    def _():
        buf0_ref[...] = jnp.zeros_like(buf0_ref)
        buf1_ref[...] = jnp.zeros_like(buf1_ref)

    def step(slot):
        mine, other = bufs[slot], bufs[1 - slot]
        _move_runs(tab_ref, jnp.clip(t - 1, 0, last), t >= 1, other, ys_ref, sems.at[1 - slot],
                   to_remote=False, wait=True)
        _move_runs(tab_ref, jnp.minimum(t, last), t <= last, mine, ys_ref, sems.at[slot],
                   to_remote=False, wait=False)
        meta = meta_ref[...]

        def gathered(first, rows):
            sel0, sel1 = _selection(meta, first, rows)
            return _dot(sel0 + sel1, other[first:first + rows, :])

        out_ref[...] = x2_ref[...] + gathered(0, 2 * TOKEN_TILE)
        for first in range(2 * TOKEN_TILE, SORT_ROWS, SORT_CHUNK):
            @pl.when(used > first)
            def _(first=first):
                out_ref[...] += gathered(first, SORT_CHUNK)

    for slot in range(2):
        pl.when(t % 2 == slot)(functools.partial(step, slot))


def _combine_call(tab, used, meta, x2, ys):
    n = x2.shape[0]
    tm = TOKEN_TILE
    ntiles = n // tm
    tile = lambda i, tab, used: (jnp.maximum(i - 1, 0), 0)
    grid_spec = pltpu.PrefetchScalarGridSpec(
        num_scalar_prefetch=2,
        grid=(ntiles + 1,),
        in_specs=[pl.BlockSpec((tm, 128), tile),
                  pl.BlockSpec((tm, D_MODEL), tile),
                  pl.BlockSpec(memory_space=pl.ANY)],
        out_specs=pl.BlockSpec((tm, D_MODEL), tile),
        scratch_shapes=[pltpu.VMEM((SORT_ROWS, D_MODEL), _BF16), pltpu.VMEM((SORT_ROWS, D_MODEL), _BF16),
                        pltpu.SemaphoreType.DMA((2, RUN_BITS, N_EXPERTS))],
    )
    return pl.pallas_call(
        functools.partial(_combine_kernel, ntiles=ntiles),
        grid_spec=grid_spec,
        out_shape=jax.ShapeDtypeStruct((n, D_MODEL), _F32),
        compiler_params=_params(1),
        name="moe_combine",
    )(tab, used, meta, x2, ys)


def _moe_tables(cnt, n_tokens):
    ntiles = n_tokens // TOKEN_TILE
    counts = cnt.reshape(ntiles, 8, 128)[:, 0, :N_EXPERTS].astype(jnp.int32)
    units = (counts + RUN_ALIGN - 1) // RUN_ALIGN
    padded = units * RUN_ALIGN
    local = jnp.cumsum(padded, axis=1) - padded
    total = jnp.sum(padded, axis=0)
    total_blk = ((total + MOE_BLOCK - 1) // MOE_BLOCK) * MOE_BLOCK
    ends = jnp.cumsum(total_blk)
    starts = ends - total_blk
    remote = starts[None, :] + jnp.cumsum(padded, axis=0) - padded
    bits = jnp.arange(RUN_BITS, dtype=jnp.int32)[None, :, None]
    has = (units[:, None, :] >> bits) & 1
    done = (units[:, None, :] & ((1 << bits) - 1)) * RUN_ALIGN
    rank = jnp.cumsum(has, axis=2) - 1
    place = jnp.logical_and(rank[..., None] == jnp.arange(N_EXPERTS, dtype=jnp.int32), has[..., None] == 1)
    compact = lambda rows: jnp.sum(jnp.where(place, rows[..., None], 0), axis=2)
    tab = jnp.concatenate([jnp.sum(has, axis=2, keepdims=True), compact(local[:, None, :] + done),
                           compact(remote[:, None, :] + done)], axis=-1).reshape(-1).astype(jnp.int32)
    used = jnp.sum(padded, axis=1).astype(jnp.int32)
    blk_first = (starts // MOE_BLOCK).astype(jnp.int32)
    blk_count = (total_blk // MOE_BLOCK).astype(jnp.int32)
    return tab, used, blk_first, blk_count


def _xs_rows(n_tokens):
    ntiles = n_tokens // TOKEN_TILE
    worst = 2 * n_tokens + ntiles * N_EXPERTS * (RUN_ALIGN - 1) + N_EXPERTS * (MOE_BLOCK - 1)
    return ((worst + MOE_BLOCK - 1) // MOE_BLOCK) * MOE_BLOCK


def _pad_last(a, width):
    return jnp.pad(a, [(0, 0)] * (a.ndim - 1) + [(0, width - a.shape[-1])])


def _swap_mid_heads(a, axis):
    shape = a.shape
    a = a.reshape(shape[:axis] + (4, 64) + shape[axis + 1:])
    a = jnp.take(a, jnp.asarray([0, 2, 1, 3]), axis=axis)
    return a.reshape(shape)


def _layer_weights(l, p):
    w_in = p["w_in"][l]
    a, b = w_in[:, :A_COLS], w_in[:, A_COLS:A_COLS + B_COLS]
    c = w_in[:, A_COLS + B_COLS:A_COLS + B_COLS + C_COLS]
    d = w_in[:, A_COLS + B_COLS + C_COLS:]
    zeros = lambda w: jnp.zeros((D_MODEL, w), _F32)
    half = MLA_ROPE // 2
    k_rope = a[:, 384:416]
    k_rope_swap = jnp.concatenate([k_rope[:, half:], k_rope[:, :half]], axis=1)
    w_in_p = jnp.concatenate([
        a[:, :384], zeros(64), k_rope, zeros(32), zeros(64), k_rope_swap, zeros(32), b,
        _swap_mid_heads(c[:, :256], 1), c[:, 256:384], c[:, 384:512], d], axis=1).astype(_BF16)
    assert w_in_p.shape[1] == IN_COLS_PADDED

    w_uq = p["mla_w_uq"][l].reshape(MLA_Q_RANK, MLA_HEADS, MLA_QK_DIM)
    w_uq_swap = jnp.concatenate([jnp.zeros_like(w_uq[..., :MLA_NOPE]), w_uq[..., MLA_NOPE + half:],
                                 w_uq[..., MLA_NOPE:MLA_NOPE + half]], axis=-1)
    w_uq = jnp.concatenate([_pad_last(w_uq, HEAD_LANES).reshape(MLA_Q_RANK, -1),
                            _pad_last(w_uq_swap, HEAD_LANES).reshape(MLA_Q_RANK, -1)], axis=1).astype(_BF16)

    def swap_gain(g):
        return _pad_last(jnp.concatenate([jnp.zeros((MLA_NOPE,), _F32), g[MLA_NOPE + half:],
                                          g[MLA_NOPE:MLA_NOPE + half]]), HEAD_LANES)[None]
    w_ukv = p["mla_w_ukv"][l].reshape(MLA_KV_RANK, MLA_HEADS, MLA_NOPE + MLA_V)
    k_nope = _pad_last(w_ukv[..., :MLA_NOPE], HEAD_LANES).reshape(MLA_KV_RANK, MLA_HEADS * HEAD_LANES)
    v_part = w_ukv[..., MLA_NOPE:].reshape(MLA_KV_RANK, MLA_HEADS * MLA_V)
    w_ukv_p = jnp.concatenate([k_nope, v_part], axis=1).astype(_BF16)

    eye = jnp.eye(len(POOL_WINDOWS), dtype=_F32)
    w_pool = jnp.einsum("gcd,gh->gchd", p["pool_w"][l], eye).reshape(256, 256).astype(_BF16)

    g_mo = p["mix_out_norm"][l]
    g_mo = jnp.concatenate([g_mo[:512], _swap_mid_heads(g_mo[512:768], 0), g_mo[768:]])
    w_mo = p["w_mix_out"][l]
    w_mo = jnp.concatenate([w_mo[:512], _swap_mid_heads(w_mo[512:768], 0), w_mo[768:]], axis=0)

    w_router = _pad_last(jnp.concatenate([p["w_expert"][l], p["w_group"][l]], axis=1), 128)
    w_router_hi = w_router.astype(_BF16)
    w_router_lo = (w_router - w_router_hi.astype(_F32)).astype(_BF16)
    b_router = jnp.concatenate([p["b_expert"][l], p["b_group"][l]])
    tm = TOKEN_TILE
    ltri = (jnp.arange(tm)[None, :] < jnp.arange(tm)[:, None]).astype(_BF16)
    utri = (jnp.arange(128)[:, None] < jnp.arange(128)[None, :]).astype(_BF16)
    tile2 = lambda g: jnp.concatenate([g, g])[None]
    return dict(
        g_mix=p["norm_mix"][l][None], w_in=w_in_p,
        g_cq=p["mla_g_cq"][l][None], w_uq=w_uq, g_ckv=p["mla_g_ckv"][l][None], w_ukv=w_ukv_p,
        g_q=_pad_last(p["mla_g_q"][l], HEAD_LANES)[None], g_k=_pad_last(p["mla_g_k"][l], HEAD_LANES)[None],
        g_q_swap=swap_gain(p["mla_g_q"][l]), g_k_swap=swap_gain(p["mla_g_k"][l]),
        w_pool=w_pool, pool_scale=p["pool_scale"][l][None], conv_w=p["conv_w"][l],
        conv_b=p["conv_b"][l][None], g_sq=tile2(p["swa_g_q"][l]), g_sk=tile2(p["swa_g_k"][l]),
        sinks=jnp.take(p["swa_sinks"][l], jnp.asarray([0, 2, 1, 3])),
        g_mo=g_mo[None], w_mo=w_mo.astype(_BF16), g_xa=p["norm_xa"][l][None],
        xa_w_q=p["xa_w_q"][l].astype(_BF16), g_xq=jnp.tile(p["xa_g_q"][l], 4)[None],
        xa_w_o=p["xa_w_o"][l].astype(_BF16), g_ffn=p["norm_ffn"][l][None],
        w_router_hi=w_router_hi, w_router_lo=w_router_lo, b_router=_pad_last(b_router, 128)[None],
        ltri=ltri, utri=utri,
    )


def kernel(x, mem, positions, norm_mix, w_in, mla_g_cq, mla_w_uq, mla_g_ckv, mla_w_ukv, mla_g_q, mla_g_k, pool_w, pool_scale, swa_g_q, swa_g_k, swa_sinks, conv_w, conv_b, mix_out_norm, w_mix_out, norm_xa, norm_mem, xa_w_q, xa_w_kv, xa_g_q, xa_g_k, xa_w_o, norm_ffn, w_group, b_group, w_expert, b_expert, w_gate, w_up, w_down):
    p = dict(norm_mix=norm_mix, w_in=w_in, mla_g_cq=mla_g_cq, mla_w_uq=mla_w_uq, mla_g_ckv=mla_g_ckv,
             mla_w_ukv=mla_w_ukv, mla_g_q=mla_g_q, mla_g_k=mla_g_k, pool_w=pool_w, pool_scale=pool_scale,
             swa_g_q=swa_g_q, swa_g_k=swa_g_k, swa_sinks=swa_sinks, conv_w=conv_w, conv_b=conv_b,
             mix_out_norm=mix_out_norm, w_mix_out=w_mix_out, norm_xa=norm_xa, xa_w_q=xa_w_q,
             xa_g_q=xa_g_q, xa_w_o=xa_w_o, norm_ffn=norm_ffn, w_group=w_group, b_group=b_group,
             w_expert=w_expert, b_expert=b_expert)
    batch, seq_len, _ = x.shape
    depth = w_in.shape[0]
    n = batch * seq_len
    assert seq_len % MIX_TILE == 0 and MIX_TILE % TOKEN_TILE == 0
    assert seq_len % PROJ_TILE == 0 and seq_len % ATTN_Q_TILE == 0
    assert mem.shape[1] == MEM_LEN

    xf = x.reshape(n, D_MODEL)
    tabs = _rope_tables(positions)
    kx, vx = _memkv_call(mem.reshape(batch * MEM_LEN, D_MODEL), norm_mem[:, None, :],
                         xa_w_kv.astype(_BF16), jnp.tile(xa_g_k, (1, 4))[:, None, :], depth, batch)
    for l in range(depth):
        lw = _layer_weights(l, p)
        qm, km, vm, yb, yc, yd = _proj_call(xf, lw, tabs, seq_len)
        ya = _mla_attn_call(qm, km, vm, batch, seq_len)
        x2, h3, meta, cnt = _mix_call(ya, yb, yc, yd, xf, lw, kx[l], vx[l], seq_len)
        tab, used, blk_first, blk_count = _moe_tables(cnt, n)
        xs = _dispatch_call(tab, used, meta, h3, jnp.zeros((_xs_rows(n), XS_COLS), _BF16))
        ys = _ffn_call(blk_first, blk_count, xs, w_gate, w_up, w_down, l)
        xf = _combine_call(tab, used, meta, x2, ys)
    return xf.reshape(batch, seq_len, D_MODEL)
```

```python
import functools

import jax
import jax.numpy as jnp
import numpy as np
from jax import lax
from jax.experimental import pallas as pl
from jax.experimental.pallas import tpu as pltpu

EPS = 1e-6
NEG_BIG = -1e30
LOG2_E = 1.4426950408889634
ROPE_THETA = 10000.0

D_MODEL = 1024
MEM_LEN = 256
GROUP_W = 256

MLA_HEADS = 4
MLA_Q_RANK = 256
MLA_KV_RANK = 128
MLA_NOPE = 64
MLA_ROPE = 32
MLA_QK_DIM = MLA_NOPE + MLA_ROPE
MLA_V = 64
HEAD_LANES = 128

POOL_WINDOWS = (2, 4, 8, 16)
POOL_GROUP = 64
POOL_HALO = 16

SWA_HEADS = 4
SWA_KV_HEADS = 2
SWA_HEAD_DIM = 64
SWA_WINDOW = 128

CONV_CH = 256

XA_HEADS = 4
XA_HEAD_DIM = 64

N_EXPERT_GROUPS = 4
EXPERTS_PER_GROUP = 8
N_EXPERTS = 32
D_EXPERT = 256
MOE_BLOCK = 512

A_COLS = MLA_Q_RANK + MLA_KV_RANK + MLA_ROPE
B_COLS = GROUP_W
C_COLS = (SWA_HEADS + 2 * SWA_KV_HEADS) * SWA_HEAD_DIM
COL_CQ, COL_CKV, COL_KROPE, COL_KROPE_SWAP = 0, 256, 384, 512
COL_POOL, COL_SWA_Q, COL_SWA_K, COL_SWA_V, COL_CONV = 640, 896, 1152, 1280, 1408
IN_COLS_PADDED = COL_CONV + 3 * CONV_CH

TOKEN_TILE = 512
PROJ_TILE = 512
MIX_TILE = 1024
ATTN_TILE = 512
ATTN_Q_TILE = 1024
RUN_ALIGN = 16
RUN_BITS = 6
SORT_ROWS = 2 * TOKEN_TILE + N_EXPERTS * RUN_ALIGN
SORT_CHUNK = 256
TAIL_BITS = 5
ZERO_RING = 8
PIECE_COLS = 1 + 2 * N_EXPERTS
XS_COLS = D_MODEL + 128
VMEM_LIMIT = 56 * 1024 * 1024

_F32 = jnp.float32
_BF16 = jnp.bfloat16


def _params(n_axes):
    return pltpu.CompilerParams(dimension_semantics=("arbitrary",) * n_axes,
                                vmem_limit_bytes=VMEM_LIMIT)


def _dot(a, b):
    return jnp.dot(a, b, preferred_element_type=_F32)


def _dot_nt(a, b):
    return lax.dot_general(a, b, (((1,), (1,)), ((), ())), preferred_element_type=_F32)


def _rms_factor(x, width):
    ss = jnp.sum(x * x, axis=-1, keepdims=True)
    return lax.rsqrt(ss * (1.0 / width) + EPS)


def _rms_scale(x, width):
    return x * _rms_factor(x, width)


def _pair_norm64(x, lane_lo):
    x2 = x * x
    s_all = jnp.sum(x2, axis=-1, keepdims=True)
    s_lo = jnp.sum(jnp.where(lane_lo, x2, 0.0), axis=-1, keepdims=True)
    ss = jnp.where(lane_lo, s_lo, s_all - s_lo)
    return x * lax.rsqrt(ss * (1.0 / 64.0) + EPS)


def _rope_kernel(pos_ref, freq_ref, c_ref, s_ref):
    ang = pos_ref[...] * freq_ref[...]
    c_ref[...] = jnp.cos(ang)
    s_ref[...] = jnp.sin(ang)


def _rope_tables(positions):
    half = MLA_ROPE // 2
    per_row = HEAD_LANES // half
    n = positions.size
    inv_freq = ROPE_THETA ** (-jnp.arange(half, dtype=_F32) / half)
    pos = jnp.repeat(positions.astype(_F32).reshape(n // per_row, per_row), half, axis=1)
    rows = n // per_row
    tile = min(rows, 1024)
    tab = pl.BlockSpec((tile, HEAD_LANES), lambda i: (i, 0))
    shp = jax.ShapeDtypeStruct((rows, HEAD_LANES), _F32)
    cos, sin = pl.pallas_call(
        _rope_kernel,
        grid=(rows // tile,),
        in_specs=[tab, pl.BlockSpec((1, HEAD_LANES), lambda i: (0, 0))],
        out_specs=[tab, tab],
        out_shape=[shp, shp],
        compiler_params=_params(1),
        name="rope_tables",
    )(pos, jnp.tile(inv_freq, per_row)[None])
    return cos.reshape(n, half), sin.reshape(n, half)


def _rotary_lanes(cos, sin):
    rows = cos.shape[0]
    pad = jnp.zeros((rows, HEAD_LANES - MLA_QK_DIM), _F32)
    c = jnp.concatenate([jnp.ones((rows, MLA_NOPE), _F32), cos, cos, pad], axis=1)
    s = jnp.concatenate([jnp.zeros((rows, MLA_NOPE), _F32), -sin, sin, pad], axis=1)
    return c, s


def _swa_tile(q, k, v, k_before, v_before, sink_ref, first_of_sequence):
    w = SWA_WINDOW
    oldest = jnp.where(first_of_sequence, 0, -w)
    lane_lo = lax.broadcasted_iota(jnp.int32, (1, 128), 1) < 64
    qpos = lax.broadcasted_iota(jnp.int32, (w, 2 * w), 0)
    kpos = lax.broadcasted_iota(jnp.int32, (w, 2 * w), 1) - w
    band = jnp.logical_and(kpos <= qpos, kpos > qpos - w)

    out_rows = []
    for jb in range(q.shape[0] // w):
        rs = slice(jb * w, (jb + 1) * w)
        if jb == 0:
            kprev, vprev = k_before, v_before
            visible = jnp.logical_and(band, kpos >= oldest)
        else:
            ps = slice((jb - 1) * w, jb * w)
            kprev, vprev = k[ps, :], v[ps, :]
            visible = band
        kk = jnp.concatenate([kprev, k[rs, :]], axis=0)
        vv = jnp.concatenate([vprev, v[rs, :]], axis=0)
        zero = jnp.zeros_like(kk)
        k_half = (jnp.where(lane_lo, kk, zero), jnp.where(lane_lo, zero, kk))
        pair_out = []
        for blk in range(2):
            qblk = q[rs, blk * 128:(blk + 1) * 128]
            outs = []
            for half in range(2):
                sink = sink_ref[2 * blk + half]
                s = _dot_nt(qblk, k_half[half])
                s = jnp.where(visible, s, NEG_BIG)
                m = jnp.maximum(jnp.max(s, axis=-1, keepdims=True), sink)
                p = jnp.exp(s - m)
                denom = jnp.sum(p, axis=-1, keepdims=True) + jnp.exp(sink - m)
                outs.append(_dot(p.astype(_BF16), vv) / denom)
            pair_out.append(jnp.where(lane_lo, outs[0], outs[1]))
        out_rows.append(jnp.concatenate(pair_out, axis=1))
    return jnp.concatenate(out_rows, axis=0)


def _proj_kernel(x_ref, gmix_ref, win_ref, gcq_ref, wuq_ref, gckv_ref, wukv_ref, gq_ref, gqs_ref,
                 gk_ref, gks_ref, c_ref, s_ref, wpool_ref, pscale_ref, convw_ref, convb_ref, gsq_ref,
                 gsk_ref, sink_ref, qm_ref, km_ref, vm_ref, yb_ref, yc_ref, yd_ref, halo_ref, swa_halo_ref,
                 *, tiles_per_seq):
    tm = PROJ_TILE
    i = pl.program_id(0)
    seq_tile = i % tiles_per_seq

    x = x_ref[...]
    h = _rms_scale(x, D_MODEL) * gmix_ref[...]
    u = _dot(h.astype(_BF16), win_ref[...])

    c, s = _rotary_lanes(c_ref[...], s_ref[...])
    hw = MLA_HEADS * HEAD_LANES

    cq = _rms_scale(u[:, COL_CQ:COL_CQ + MLA_Q_RANK], MLA_Q_RANK) * gcq_ref[...]
    q = _dot(cq.astype(_BF16), wuq_ref[...])
    gq = gq_ref[...] * (MLA_QK_DIM ** -0.5 * LOG2_E)
    gqs = gqs_ref[...] * (MLA_QK_DIM ** -0.5 * LOG2_E)
    for hd in range(MLA_HEADS):
        sl = slice(hd * HEAD_LANES, (hd + 1) * HEAD_LANES)
        xq = q[:, sl]
        r = _rms_factor(xq, MLA_QK_DIM)
        qm_ref[:, sl] = ((xq * r) * (gq * c) + (q[:, hw + sl.start:hw + sl.stop] * r) * (gqs * s)).astype(_BF16)

    ckv = _rms_scale(u[:, COL_CKV:COL_CKV + MLA_KV_RANK], MLA_KV_RANK) * gckv_ref[...]
    kv = _dot(ckv.astype(_BF16), wukv_ref[...])
    krope = u[:, COL_KROPE:COL_KROPE + HEAD_LANES]
    krope_swap = u[:, COL_KROPE_SWAP:COL_KROPE_SWAP + HEAD_LANES]
    gk = gk_ref[...]
    gks = gks_ref[...]
    for hd in range(MLA_HEADS):
        sl = slice(hd * HEAD_LANES, (hd + 1) * HEAD_LANES)
        xk = kv[:, sl] + krope
        r = _rms_factor(xk, MLA_QK_DIM)
        km_ref[:, sl] = ((xk * r) * (gk * c) + (krope_swap * r) * (gks * s)).astype(_BF16)
    vm_ref[...] = kv[:, hw:hw + MLA_HEADS * MLA_V].astype(_BF16)

    lane_lo = lax.broadcasted_iota(jnp.int32, (1, HEAD_LANES), 1) < 64
    gsq = gsq_ref[...] * (SWA_HEAD_DIM ** -0.5)
    qs = jnp.concatenate(
        [(_pair_norm64(u[:, COL_SWA_Q + blk * 128:COL_SWA_Q + (blk + 1) * 128], lane_lo) * gsq).astype(_BF16)
         for blk in range(2)], axis=1)
    ks = (_pair_norm64(u[:, COL_SWA_K:COL_SWA_K + 128], lane_lo) * gsk_ref[...]).astype(_BF16)
    vs = u[:, COL_SWA_V:COL_SWA_V + 128].astype(_BF16)

    @pl.when(seq_tile == 0)
    def _():
        halo_ref[...] = jnp.zeros_like(halo_ref)
        swa_halo_ref[...] = jnp.zeros_like(swa_halo_ref)

    swa_halo = swa_halo_ref[...].astype(_BF16)
    yc_ref[...] = _swa_tile(qs, ks, vs, swa_halo[:, 0:128], swa_halo[:, 128:256], sink_ref,
                            seq_tile == 0).astype(_BF16)
    swa_halo_ref[:, 0:128] = ks[tm - SWA_WINDOW:, :].astype(_F32)
    swa_halo_ref[:, 128:256] = vs[tm - SWA_WINDOW:, :].astype(_F32)

    halo = halo_ref[...]
    up = u[:, COL_POOL:COL_POOL + GROUP_W]
    ud = u[:, COL_CONV:COL_CONV + 3 * CONV_CH]

    b = jnp.concatenate([halo[:, 0:256], up], axis=0)
    w2 = b + pltpu.roll(b, 1, 0)
    w4 = w2 + pltpu.roll(w2, 2, 0)
    w8 = w4 + pltpu.roll(w4, 4, 0)
    w16 = w8 + pltpu.roll(w8, 8, 0)
    lane = lax.broadcasted_iota(jnp.int32, (1, 256), 1)
    win = jnp.where(lane < 64, w2, jnp.where(lane < 128, w4, jnp.where(lane < 192, w8, w16)))
    win = win[POOL_HALO:, :]
    width = jnp.where(lane < 64, 2, jnp.where(lane < 128, 4, jnp.where(lane < 192, 8, 16)))
    t = seq_tile * tm + lax.broadcasted_iota(jnp.int32, (tm, 1), 0)
    count = jnp.minimum(t + 1, width).astype(_F32)
    pooled = win / count - up
    yb_ref[...] = (_dot(pooled.astype(_BF16), wpool_ref[...]) * pscale_ref[...]).astype(_BF16)

    z = ud[:, 256:512] * ud[:, 512:768]
    zh = halo[:, 512:768] * halo[:, 768:1024]
    zb = jnp.concatenate([zh, z], axis=0)
    cw = convw_ref[...]
    conv = (pltpu.roll(zb, 2, 0)[POOL_HALO:, :] * cw[0:1, :]
            + pltpu.roll(zb, 1, 0)[POOL_HALO:, :] * cw[1:2, :]
            + z * cw[2:3, :])
    yd_ref[...] = (ud[:, 0:256] * (conv + convb_ref[...])).astype(_BF16)

    halo_ref[:, 0:256] = up[tm - POOL_HALO:, :]
    halo_ref[:, 256:1024] = ud[tm - POOL_HALO:, :]


def _proj_call(x, lw, tabs, seq_len):
    n = x.shape[0]
    tm = PROJ_TILE
    tiles_per_seq = seq_len // tm

    def full(a):
        nd = a.ndim
        return pl.BlockSpec(a.shape, lambda i, _nd=nd: (0,) * _nd)

    def rows(width):
        return pl.BlockSpec((tm, width), lambda i: (i, 0))

    consts = [lw["g_mix"], lw["w_in"], lw["g_cq"], lw["w_uq"], lw["g_ckv"], lw["w_ukv"],
              lw["g_q"], lw["g_q_swap"], lw["g_k"], lw["g_k_swap"]]
    consts2 = [lw["w_pool"], lw["pool_scale"], lw["conv_w"], lw["conv_b"], lw["g_sq"], lw["g_sk"]]
    out_widths = [512, 512, 256, 256, 256, 256]
    return pl.pallas_call(
        functools.partial(_proj_kernel, tiles_per_seq=tiles_per_seq),
        grid=(n // tm,),
        in_specs=[rows(D_MODEL)] + [full(a) for a in consts] + [rows(MLA_ROPE // 2)] * 2
                 + [full(a) for a in consts2] + [pl.BlockSpec(memory_space=pltpu.SMEM)],
        out_specs=[rows(w) for w in out_widths],
        out_shape=[jax.ShapeDtypeStruct((n, w), _BF16) for w in out_widths],
        scratch_shapes=[pltpu.VMEM((POOL_HALO, 1024), _F32), pltpu.VMEM((SWA_WINDOW, 256), _F32)],
        compiler_params=_params(1),
        name="proj_in",
    )(x, *consts, *tabs, *consts2, lw["sinks"])


def _mla_attn_kernel(q_ref, k_hbm, v_hbm, o_ref, kbuf, vbuf, sems, m_ref, l_ref, acc_ref, *, seq_len):
    tq, tk = ATTN_Q_TILE, ATTN_TILE
    ratio = tq // tk
    assert ratio == 2
    b = pl.program_id(0)
    qb = pl.program_id(1)
    row0 = b * seq_len

    def fetch(j, slot):
        rows = pl.ds(pl.multiple_of(row0 + j * tk, tk), tk)
        return (pltpu.make_async_copy(k_hbm.at[rows], kbuf.at[slot], sems.at[0, slot]),
                pltpu.make_async_copy(v_hbm.at[rows], vbuf.at[slot], sems.at[1, slot]))

    def start(j, slot):
        for cp in fetch(j, slot):
            cp.start()

    def wait(j, slot):
        for cp in fetch(j, slot):
            cp.wait()

    m_ref[...] = jnp.full_like(m_ref, NEG_BIG)
    l_ref[...] = jnp.zeros_like(l_ref)
    acc_ref[...] = jnp.zeros_like(acc_ref)
    lane_lo = lax.broadcasted_iota(jnp.int32, (1, HEAD_LANES), 1) < 64

    def accumulate(slot, rows, diagonal_from):
        nrows = rows.stop - rows.start
        if diagonal_from is not None:
            visible = (lax.broadcasted_iota(jnp.int32, (nrows, tk), 1)
                       <= lax.broadcasted_iota(jnp.int32, (nrows, tk), 0) + (rows.start - diagonal_from))
        for pair in range(MLA_HEADS // 2):
            vblk = vbuf[slot, :, pair * 128:(pair + 1) * 128]
            alphas = []
            pvs = []
            for sub in range(2):
                hd = 2 * pair + sub
                sl = slice(hd * HEAD_LANES, (hd + 1) * HEAD_LANES)
                s = _dot_nt(q_ref[rows, sl], kbuf[slot, :, sl])
                if diagonal_from is not None:
                    s = jnp.where(visible, s, NEG_BIG)
                m_prev = m_ref[hd, rows, :]
                m_new = jnp.maximum(m_prev, jnp.max(s, axis=-1, keepdims=True))
                alpha = jnp.exp2(m_prev - m_new)
                p = jnp.exp2(s - jnp.tile(m_new, (1, tk // HEAD_LANES)))
                l_ref[hd, rows, :] = alpha * l_ref[hd, rows, :] + jnp.sum(p, axis=-1, keepdims=True)
                m_ref[hd, rows, :] = m_new
                alphas.append(alpha)
                pvs.append(_dot(p.astype(_BF16), vblk))
            psl = slice(pair * 128, (pair + 1) * 128)
            acc_ref[rows, psl] = (acc_ref[rows, psl] * jnp.where(lane_lo, alphas[0], alphas[1])
                                  + jnp.where(lane_lo, pvs[0], pvs[1]))

    everything = slice(0, tq)
    start(0, 0)

    def two_tiles(pair_idx, carry):
        j = pair_idx * ratio
        wait(j, 0)
        start(j + 1, 1)
        accumulate(0, everything, None)
        wait(j + 1, 1)
        start(j + 2, 0)
        accumulate(1, everything, None)
        return carry

    lax.fori_loop(0, qb, two_tiles, 0)

    j = qb * ratio
    wait(j, 0)
    start(j + 1, 1)
    accumulate(0, everything, 0)
    wait(j + 1, 1)
    accumulate(1, slice(tk, tq), tk)

    for pair in range(MLA_HEADS // 2):
        psl = slice(pair * 128, (pair + 1) * 128)
        denom = jnp.where(lane_lo, l_ref[2 * pair], l_ref[2 * pair + 1])
        o_ref[:, psl] = (acc_ref[:, psl] / denom).astype(o_ref.dtype)


def _mla_attn_call(qm, km, vm, batch, seq_len):
    tq, tk = ATTN_Q_TILE, ATTN_TILE
    nq = seq_len // tq
    return pl.pallas_call(
        functools.partial(_mla_attn_kernel, seq_len=seq_len),
        grid=(batch, nq),
        in_specs=[pl.BlockSpec((tq, 512), lambda b, q: (b * nq + q, 0)),
                  pl.BlockSpec(memory_space=pl.ANY), pl.BlockSpec(memory_space=pl.ANY)],
        out_specs=pl.BlockSpec((tq, 256), lambda b, q: (b * nq + q, 0)),
        out_shape=jax.ShapeDtypeStruct((batch * seq_len, 256), _BF16),
        scratch_shapes=[pltpu.VMEM((2, tk, 512), _BF16), pltpu.VMEM((2, tk, 256), _BF16),
                        pltpu.SemaphoreType.DMA((2, 2)),
                        pltpu.VMEM((MLA_HEADS, tq, HEAD_LANES), _F32),
                        pltpu.VMEM((MLA_HEADS, tq, HEAD_LANES), _F32),
                        pltpu.VMEM((tq, 256), _F32)],
        compiler_params=_params(2),
        name="mla_attention",
    )(qm, km, vm)


def _memkv_kernel(mem_ref, gmem_ref, wkv_ref, gk_ref, k_ref, v_ref):
    m = _rms_scale(mem_ref[...], D_MODEL) * gmem_ref[...]
    kv = _dot(m.astype(_BF16), wkv_ref[...])
    lane_lo = lax.broadcasted_iota(jnp.int32, (1, 128), 1) < 64
    lane = lax.broadcasted_iota(jnp.int32, (1, 256), 1)
    k = jnp.concatenate([_pair_norm64(kv[:, 0:128], lane_lo), _pair_norm64(kv[:, 128:256], lane_lo)],
                        axis=1) * gk_ref[...]
    v = kv[:, 256:512]
    for hd in range(XA_HEADS):
        own = jnp.logical_and(lane >= hd * XA_HEAD_DIM, lane < (hd + 1) * XA_HEAD_DIM)
        k_ref[hd] = jnp.where(own, k, 0.0).astype(_BF16)
        v_ref[hd] = jnp.where(own, v, 0.0).astype(_BF16)


def _memkv_call(mem2d, g_mem, w_kv, g_k4, depth, batch):
    out = jax.ShapeDtypeStruct((depth, batch, XA_HEADS, MEM_LEN, 256), _BF16)
    ospec = pl.BlockSpec((None, None, XA_HEADS, MEM_LEN, 256), lambda l, b: (l, b, 0, 0, 0))
    return pl.pallas_call(
        _memkv_kernel,
        grid=(depth, batch),
        in_specs=[pl.BlockSpec((MEM_LEN, D_MODEL), lambda l, b: (b, 0)),
                  pl.BlockSpec((None, 1, D_MODEL), lambda l, b: (l, 0, 0)),
                  pl.BlockSpec((None, D_MODEL, 512), lambda l, b: (l, 0, 0)),
                  pl.BlockSpec((None, 1, 256), lambda l, b: (l, 0, 0))],
        out_specs=[ospec, ospec],
        out_shape=[out, out],
        compiler_params=_params(2),
        name="memory_kv",
    )(mem2d, g_mem, w_kv, g_k4)


def _mix_kernel(ya_ref, yb_ref, yc_ref, yd_ref, x_ref, gmo_ref, wmo_ref, gxa_ref, wq_ref, gxq_ref,
                kx_ref, vx_ref, wo_ref, gffn_ref, wrh_ref, wrl_ref, br_ref, ltri_ref, utri_ref,
                x2_ref, h3_ref, meta_ref, cnt_ref):
    tm = MIX_TILE
    gmo = gmo_ref[...]
    lane_lo = lax.broadcasted_iota(jnp.int32, (1, 128), 1) < 64

    def rows_to_logits(rs):
        parts = []
        for g, ref in enumerate((ya_ref, yb_ref, yc_ref, yd_ref)):
            yg = ref[rs, :].astype(_F32)
            parts.append((_rms_scale(yg, GROUP_W) * gmo[:, g * 256:(g + 1) * 256]).astype(_BF16))
        y = jnp.concatenate(parts, axis=1)
        x1 = x_ref[rs, :] + _dot(y, wmo_ref[...])

        h = (_rms_scale(x1, D_MODEL) * gxa_ref[...]).astype(_BF16)
        q = _dot(h, wq_ref[...])
        qn = jnp.concatenate([_pair_norm64(q[:, 0:128], lane_lo), _pair_norm64(q[:, 128:256], lane_lo)],
                             axis=1)
        qn = (qn * gxq_ref[...] * (XA_HEAD_DIM ** -0.5)).astype(_BF16)
        o = jnp.zeros((x1.shape[0], 256), _F32)
        for hd in range(XA_HEADS):
            s = _dot_nt(qn, kx_ref[hd])
            m = jnp.max(s, axis=-1, keepdims=True)
            p = jnp.exp(s - m)
            denom = jnp.sum(p, axis=-1, keepdims=True)
            o = o + _dot(p.astype(_BF16), vx_ref[hd]) / denom
        x2 = x1 + _dot(o.astype(_BF16), wo_ref[...])
        x2_ref[rs, :] = x2

        h3 = _rms_scale(x2, D_MODEL) * gffn_ref[...]
        h3_hi = h3.astype(_BF16)
        h3_lo = (h3 - h3_hi.astype(_F32)).astype(_BF16)
        h3_ref[rs, :] = h3_hi
        return (_dot(h3_hi, wrh_ref[...]) + (_dot(h3_hi, wrl_ref[...]) + _dot(h3_lo, wrh_ref[...]))
                + br_ref[...])

    all_logits = rows_to_logits(slice(0, tm))
    for st in range(tm // TOKEN_TILE):
        meta, counts = _route(all_logits[st * TOKEN_TILE:(st + 1) * TOKEN_TILE, :], ltri_ref, utri_ref)
        meta_ref[st * TOKEN_TILE:(st + 1) * TOKEN_TILE, :] = meta
        cnt_ref[st * 8:(st + 1) * 8, :] = jnp.broadcast_to(counts, (8, 128))


def _route(logits, ltri_ref, utri_ref):
    lane = lax.broadcasted_iota(jnp.int32, (TOKEN_TILE, 128), 1).astype(_F32)
    far = 1e9
    is_group = jnp.logical_and(lane >= N_EXPERTS, lane < N_EXPERTS + N_EXPERT_GROUPS)
    gl = jnp.where(is_group, logits, -jnp.inf)
    gmax = jnp.max(gl, axis=-1, keepdims=True)
    gidx = jnp.min(jnp.where(gl == gmax, lane, far), axis=-1, keepdims=True) - N_EXPERTS
    g_w = 1.0 / jnp.sum(jnp.where(is_group, jnp.exp(logits - gmax), 0.0), axis=-1, keepdims=True)
    in_group = jnp.logical_and(lane < N_EXPERTS, jnp.floor(lane * (1.0 / EXPERTS_PER_GROUP)) == gidx)
    el = jnp.where(in_group, logits, -jnp.inf)
    emax = jnp.max(el, axis=-1, keepdims=True)
    ep = jnp.where(in_group, jnp.exp(logits - emax), 0.0)
    prob = jnp.where(in_group, ep / jnp.sum(ep, axis=-1, keepdims=True), -1.0)
    p1 = jnp.max(prob, axis=-1, keepdims=True)
    e0 = jnp.min(jnp.where(prob == p1, lane, far), axis=-1, keepdims=True)
    prob2 = jnp.where(lane == e0, -1.0, prob)
    p2 = jnp.max(prob2, axis=-1, keepdims=True)
    e1 = jnp.min(jnp.where(prob2 == p2, lane, far), axis=-1, keepdims=True)
    w0 = g_w * (p1 / (p1 + p2))
    w1 = g_w * (p2 / (p1 + p2))

    onehot = jnp.where(jnp.logical_or(lane == e0, lane == e1), 1.0, 0.0)
    prefix = _dot(ltri_ref[...], onehot.astype(_BF16))
    counts = jnp.sum(onehot, axis=0, keepdims=True)
    units = jnp.floor((counts + (RUN_ALIGN - 1)) * (1.0 / RUN_ALIGN))
    offs = _dot(jnp.broadcast_to(units, (8, 128)).astype(_BF16), utri_ref[...])[0:1, :] * RUN_ALIGN
    where_to = prefix + offs
    pos0 = jnp.sum(jnp.where(lane == e0, where_to, 0.0), axis=-1, keepdims=True)
    pos1 = jnp.sum(jnp.where(lane == e1, where_to, 0.0), axis=-1, keepdims=True)

    meta = jnp.where(lane == 0, e0,
           jnp.where(lane == 1, e1,
           jnp.where(lane == 2, w0,
           jnp.where(lane == 3, w1,
           jnp.where(lane == 4, pos0,
           jnp.where(lane == 5, pos1, 0.0))))))
    return meta, counts


def _mix_call(ya, yb, yc, yd, x, lw, kx, vx, seq_len):
    n = x.shape[0]
    tm = MIX_TILE
    tiles_per_seq = seq_len // tm

    def full(a):
        nd = a.ndim
        return pl.BlockSpec(a.shape, lambda i, _nd=nd: (0,) * _nd)

    def rows(width):
        return pl.BlockSpec((tm, width), lambda i: (i, 0))

    kvspec = pl.BlockSpec((None, XA_HEADS, MEM_LEN, 256), lambda i: (i // tiles_per_seq, 0, 0, 0))
    consts_a = [lw["g_mo"], lw["w_mo"], lw["g_xa"], lw["xa_w_q"], lw["g_xq"]]
    consts_b = [lw["xa_w_o"], lw["g_ffn"], lw["w_router_hi"], lw["w_router_lo"], lw["b_router"],
                lw["ltri"], lw["utri"]]
    ntiles = n // tm
    return pl.pallas_call(
        _mix_kernel,
        grid=(ntiles,),
        in_specs=[rows(256)] * 4 + [rows(D_MODEL)] + [full(a) for a in consts_a]
                 + [kvspec, kvspec] + [full(a) for a in consts_b],
        out_specs=[rows(D_MODEL), rows(D_MODEL), rows(128),
                   pl.BlockSpec((8 * (tm // TOKEN_TILE), 128), lambda i: (i, 0))],
        out_shape=[jax.ShapeDtypeStruct((n, D_MODEL), _F32), jax.ShapeDtypeStruct((n, D_MODEL), _BF16),
                   jax.ShapeDtypeStruct((n, 128), _F32),
                   jax.ShapeDtypeStruct((n // TOKEN_TILE * 8, 128), _F32)],
        compiler_params=_params(1),
        name="mix_xattn_router",
    )(ya, yb, yc, yd, x, *consts_a, kx, vx, *consts_b)


def _move_runs(tab_ref, t, live, local_ref, remote_ref, sems, *, to_remote, wait):
    for bit in range(RUN_BITS):
        rows = RUN_ALIGN << bit
        base = (t * RUN_BITS + bit) * PIECE_COLS
        count = jnp.where(live, tab_ref[base], 0)

        def one(p, carry, base=base, rows=rows, bit=bit):
            loc = pl.multiple_of(tab_ref[base + 1 + p], RUN_ALIGN)
            rem = pl.multiple_of(tab_ref[base + 1 + N_EXPERTS + p], RUN_ALIGN)
            lsl = local_ref.at[pl.ds(loc, rows)]
            rsl = remote_ref.at[pl.ds(rem, rows)]
            src, dst = (lsl, rsl) if to_remote else (rsl, lsl)
            copy = pltpu.make_async_copy(src, dst, sems.at[bit, p])
            if wait:
                copy.wait()
            else:
                copy.start()
            return carry

        lax.fori_loop(0, count, one, 0)


def _selection(meta, first, rows):
    r = first + lax.broadcasted_iota(jnp.int32, (TOKEN_TILE, rows), 1)
    pos0 = meta[:, 4:5].astype(jnp.int32)
    pos1 = meta[:, 5:6].astype(jnp.int32)
    return jnp.where(r == pos0, 1.0, 0.0).astype(_BF16), jnp.where(r == pos1, 1.0, 0.0).astype(_BF16)


def _zero_rows(count, row_of, rows, zero_ref, xs_ref, sems):
    def copy(i):
        dst = xs_ref.at[pl.ds(pl.multiple_of(row_of(i), RUN_ALIGN), rows)]
        return pltpu.make_async_copy(zero_ref.at[pl.ds(0, rows)], dst, sems.at[i % ZERO_RING])

    def issue(i, carry):
        @pl.when(i >= ZERO_RING)
        def _():
            copy(i - ZERO_RING).wait()
        copy(i).start()
        return carry

    def drain(i, carry):
        copy(i).wait()
        return carry

    lax.fori_loop(0, count, issue, 0)
    lax.fori_loop(jnp.maximum(count - ZERO_RING, 0), count, drain, 0)


def _dispatch_kernel(tab_ref, used_ref, ztab_ref, meta_ref, h3_ref, xs_ref, buf0_ref, buf1_ref, zero_ref,
                     sems, zero_sems, *, ntiles, n_blocks):
    t = pl.program_id(0)
    last = ntiles - 1
    bufs = (buf0_ref, buf1_ref)
    used = used_ref[jnp.minimum(t, last)]

    @pl.when(t == 0)
    def _():
        zero_ref[...] = jnp.zeros_like(zero_ref)
        for bit in range(TAIL_BITS):
            base = bit * (1 + N_EXPERTS)
            _zero_rows(ztab_ref[base], lambda i, base=base: ztab_ref[base + 1 + i], RUN_ALIGN << bit,
                       zero_ref, xs_ref, zero_sems.at[bit])
        live_blocks = ztab_ref[TAIL_BITS * (1 + N_EXPERTS)]
        _zero_rows(n_blocks - live_blocks, lambda i: (live_blocks + i) * MOE_BLOCK, MOE_BLOCK,
                   zero_ref, xs_ref, zero_sems.at[TAIL_BITS])

    def step(slot):
        mine, other = bufs[slot], bufs[1 - slot]
        _move_runs(tab_ref, jnp.clip(t - 2, 0, last), t >= 2, mine, xs_ref, sems.at[slot],
                   to_remote=True, wait=True)
        _move_runs(tab_ref, jnp.clip(t - 1, 0, last), jnp.logical_and(t >= 1, t <= ntiles), other, xs_ref,
                   sems.at[1 - slot], to_remote=True, wait=False)

        meta = meta_ref[...]
        lane = lax.broadcasted_iota(jnp.int32, (TOKEN_TILE, 128), 1)
        cols = []
        for k in range(2):
            w = meta[:, 2 + k:3 + k]
            hi = w.astype(_BF16).astype(_F32)
            mid = (w - hi).astype(_BF16).astype(_F32)
            lo = (w - hi) - mid
            cols.append(jnp.where(lane == 0, hi, jnp.where(lane == 1, mid, jnp.where(lane == 2, lo, 0.0)))
                        .astype(_BF16))

        def sort_rows(first, rows):
            sel0, sel1 = _selection(meta, first, rows)
            tn = (((0,), (0,)), ((), ()))
            mine[first:first + rows, 0:D_MODEL] = lax.dot_general(
                sel0 + sel1, h3_ref[...], tn, preferred_element_type=_F32).astype(_BF16)
            mine[first:first + rows, D_MODEL:XS_COLS] = (
                lax.dot_general(sel0, cols[0], tn, preferred_element_type=_F32)
                + lax.dot_general(sel1, cols[1], tn, preferred_element_type=_F32)).astype(_BF16)

        sort_rows(0, 2 * TOKEN_TILE)
        for first in range(2 * TOKEN_TILE, SORT_ROWS, SORT_CHUNK):
            pl.when(used > first)(functools.partial(sort_rows, first, SORT_CHUNK))

    for slot in range(2):
        pl.when(t % 2 == slot)(functools.partial(step, slot))


def _dispatch_call(tab, used, ztab, meta, h3):
    n = h3.shape[0]
    tm = TOKEN_TILE
    ntiles = n // tm
    xs_rows = _xs_rows(n)
    tile = lambda i, tab, used, ztab: (jnp.minimum(i, ntiles - 1), 0)
    grid_spec = pltpu.PrefetchScalarGridSpec(
        num_scalar_prefetch=3,
        grid=(ntiles + 2,),
        in_specs=[pl.BlockSpec((tm, 128), tile),
                  pl.BlockSpec((tm, D_MODEL), tile)],
        out_specs=pl.BlockSpec(memory_space=pl.ANY),
        scratch_shapes=[pltpu.VMEM((SORT_ROWS, XS_COLS), _BF16), pltpu.VMEM((SORT_ROWS, XS_COLS), _BF16),
                        pltpu.VMEM((MOE_BLOCK, XS_COLS), _BF16),
                        pltpu.SemaphoreType.DMA((2, RUN_BITS, N_EXPERTS)),
                        pltpu.SemaphoreType.DMA((TAIL_BITS + 1, ZERO_RING))],
    )
    return pl.pallas_call(
        functools.partial(_dispatch_kernel, ntiles=ntiles, n_blocks=xs_rows // MOE_BLOCK),
        grid_spec=grid_spec,
        out_shape=jax.ShapeDtypeStruct((xs_rows, XS_COLS), _BF16),
        compiler_params=_params(1),
        name="moe_dispatch",
    )(tab, used, ztab, meta, h3)


def _ffn_kernel(be_ref, nv_ref, xs_ref, wg_ref, wu_ref, wd_ref, ys_ref, wgu_ref, wdn_ref):
    i = pl.program_id(0)

    @pl.when(jnp.logical_or(i == 0, be_ref[i] != be_ref[jnp.maximum(i - 1, 0)]))
    def _():
        wgu_ref[:, 0:D_EXPERT] = wg_ref[...].astype(_BF16)
        wgu_ref[:, D_EXPERT:2 * D_EXPERT] = wu_ref[...].astype(_BF16)
        wdn_ref[...] = wd_ref[...].astype(_BF16)

    @pl.when(i < nv_ref[0])
    def _():
        xb = xs_ref[:, 0:D_MODEL]
        wcols = xs_ref[:, D_MODEL:XS_COLS].astype(_F32)
        wt = wcols[:, 0:1] + wcols[:, 1:2] + wcols[:, 2:3]
        gu = _dot(xb, wgu_ref[...])
        g = gu[:, 0:D_EXPERT]
        a = (g * jax.nn.sigmoid(g)) * gu[:, D_EXPERT:2 * D_EXPERT]
        y = _dot(a.astype(_BF16), wdn_ref[...])
        ys_ref[...] = (y * wt).astype(ys_ref.dtype)

    @pl.when(i >= nv_ref[0])
    def _():
        ys_ref[...] = jnp.zeros_like(ys_ref)


def _ffn_call(blk_expert, n_valid, xs, w_gate, w_up, w_down, layer):
    n_blocks = xs.shape[0] // MOE_BLOCK
    wspec = lambda shape: pl.BlockSpec((None, None) + shape, lambda i, be, nv: (layer, be[i], 0, 0))
    last_live = lambda i, be, nv: (jnp.minimum(i, nv[0] - 1), 0)
    grid_spec = pltpu.PrefetchScalarGridSpec(
        num_scalar_prefetch=2,
        grid=(n_blocks,),
        in_specs=[pl.BlockSpec((MOE_BLOCK, XS_COLS), last_live),
                  wspec((D_MODEL, D_EXPERT)), wspec((D_MODEL, D_EXPERT)), wspec((D_EXPERT, D_MODEL))],
        out_specs=pl.BlockSpec((MOE_BLOCK, D_MODEL), lambda i, be, nv: (i, 0)),
        scratch_shapes=[pltpu.VMEM((D_MODEL, 2 * D_EXPERT), _BF16), pltpu.VMEM((D_EXPERT, D_MODEL), _BF16)],
    )
    return pl.pallas_call(
        _ffn_kernel,
        grid_spec=grid_spec,
        out_shape=jax.ShapeDtypeStruct((xs.shape[0], D_MODEL), _BF16),
        compiler_params=_params(1),
        name="moe_experts",
    )(blk_expert, n_valid, xs, w_gate, w_up, w_down)


def _combine_kernel(tab_ref, used_ref, meta_ref, x2_ref, ys_ref, out_ref, buf0_ref, buf1_ref, sems,
                    *, ntiles):
    t = pl.program_id(0)
    last = ntiles - 1
    bufs = (buf0_ref, buf1_ref)
    used = used_ref[jnp.clip(t - 1, 0, last)]

    @pl.when(t == 0)
    def _():
        buf0_ref[...] = jnp.zeros_like(buf0_ref)
        buf1_ref[...] = jnp.zeros_like(buf1_ref)

    def step(slot):
        mine, other = bufs[slot], bufs[1 - slot]
        _move_runs(tab_ref, jnp.clip(t - 1, 0, last), t >= 1, other, ys_ref, sems.at[1 - slot],
                   to_remote=False, wait=True)
        _move_runs(tab_ref, jnp.minimum(t, last), t <= last, mine, ys_ref, sems.at[slot],
                   to_remote=False, wait=False)
        meta = meta_ref[...]

        def gathered(first, rows):
            sel0, sel1 = _selection(meta, first, rows)
            return _dot(sel0 + sel1, other[first:first + rows, :])

        out_ref[...] = x2_ref[...] + gathered(0, 2 * TOKEN_TILE)
        for first in range(2 * TOKEN_TILE, SORT_ROWS, SORT_CHUNK):
            @pl.when(used > first)
            def _(first=first):
                out_ref[...] += gathered(first, SORT_CHUNK)

    for slot in range(2):
        pl.when(t % 2 == slot)(functools.partial(step, slot))


def _combine_call(tab, used, meta, x2, ys):
    n = x2.shape[0]
    tm = TOKEN_TILE
    ntiles = n // tm
    tile = lambda i, tab, used: (jnp.maximum(i - 1, 0), 0)
    grid_spec = pltpu.PrefetchScalarGridSpec(
        num_scalar_prefetch=2,
        grid=(ntiles + 1,),
        in_specs=[pl.BlockSpec((tm, 128), tile),
                  pl.BlockSpec((tm, D_MODEL), tile),
                  pl.BlockSpec(memory_space=pl.ANY)],
        out_specs=pl.BlockSpec((tm, D_MODEL), tile),
        scratch_shapes=[pltpu.VMEM((SORT_ROWS, D_MODEL), _BF16), pltpu.VMEM((SORT_ROWS, D_MODEL), _BF16),
                        pltpu.SemaphoreType.DMA((2, RUN_BITS, N_EXPERTS))],
    )
    return pl.pallas_call(
        functools.partial(_combine_kernel, ntiles=ntiles),
        grid_spec=grid_spec,
        out_shape=jax.ShapeDtypeStruct((n, D_MODEL), _F32),
        compiler_params=_params(1),
        name="moe_combine",
    )(tab, used, meta, x2, ys)


def _moe_tables(cnt, n_tokens):
    ntiles = n_tokens // TOKEN_TILE
    counts = cnt.reshape(ntiles, 8, 128)[:, 0, :N_EXPERTS].astype(jnp.int32)
    units = (counts + RUN_ALIGN - 1) // RUN_ALIGN
    padded = units * RUN_ALIGN
    local = jnp.cumsum(padded, axis=1) - padded
    total = jnp.sum(padded, axis=0)
    total_blk = ((total + MOE_BLOCK - 1) // MOE_BLOCK) * MOE_BLOCK
    ends = jnp.cumsum(total_blk)
    starts = ends - total_blk
    remote = starts[None, :] + jnp.cumsum(padded, axis=0) - padded
    bits = jnp.arange(RUN_BITS, dtype=jnp.int32)[None, :, None]
    has = (units[:, None, :] >> bits) & 1
    done = (units[:, None, :] & ((1 << bits) - 1)) * RUN_ALIGN
    rank = jnp.cumsum(has, axis=2) - 1
    place = jnp.logical_and(rank[..., None] == jnp.arange(N_EXPERTS, dtype=jnp.int32), has[..., None] == 1)
    compact = lambda rows: jnp.sum(jnp.where(place, rows[..., None], 0), axis=2)
    tab = jnp.concatenate([jnp.sum(has, axis=2, keepdims=True), compact(local[:, None, :] + done),
                           compact(remote[:, None, :] + done)], axis=-1).reshape(-1).astype(jnp.int32)
    n_blocks = _xs_rows(n_tokens) // MOE_BLOCK
    blk_start = jnp.arange(n_blocks, dtype=jnp.int32) * MOE_BLOCK
    blk_expert = jnp.minimum(jnp.sum((ends[None, :] <= blk_start[:, None]).astype(jnp.int32), axis=1),
                             N_EXPERTS - 1)
    n_valid = (ends[-1] // MOE_BLOCK).astype(jnp.int32).reshape(1)
    used = jnp.sum(padded, axis=1).astype(jnp.int32)
    tail_units = (total_blk - total) // RUN_ALIGN
    tail_bits = jnp.arange(TAIL_BITS, dtype=jnp.int32)[:, None]
    tail_has = (tail_units[None, :] >> tail_bits) & 1
    tail_rows = starts + total + (tail_units[None, :] & ((1 << tail_bits) - 1)) * RUN_ALIGN
    tail_rank = jnp.cumsum(tail_has, axis=1) - 1
    tail_place = jnp.logical_and(tail_rank[..., None] == jnp.arange(N_EXPERTS, dtype=jnp.int32),
                                 tail_has[..., None] == 1)
    tail_list = jnp.sum(jnp.where(tail_place, tail_rows[..., None], 0), axis=1)
    ztab = jnp.concatenate([jnp.concatenate([jnp.sum(tail_has, axis=1, keepdims=True), tail_list],
                                            axis=1).reshape(-1), n_valid]).astype(jnp.int32)
    return tab, used, ztab, blk_expert, n_valid


def _xs_rows(n_tokens):
    ntiles = n_tokens // TOKEN_TILE
    worst = 2 * n_tokens + ntiles * N_EXPERTS * (RUN_ALIGN - 1) + N_EXPERTS * (MOE_BLOCK - 1)
    return ((worst + MOE_BLOCK - 1) // MOE_BLOCK) * MOE_BLOCK


def _pad_last(a, width):
    return jnp.pad(a, [(0, 0)] * (a.ndim - 1) + [(0, width - a.shape[-1])])


def _swap_mid_heads(a, axis):
    shape = a.shape
    a = a.reshape(shape[:axis] + (4, 64) + shape[axis + 1:])
    a = jnp.take(a, jnp.asarray([0, 2, 1, 3]), axis=axis)
    return a.reshape(shape)


def _layer_weights(l, p):
    w_in = p["w_in"][l]
    a, b = w_in[:, :A_COLS], w_in[:, A_COLS:A_COLS + B_COLS]
    c = w_in[:, A_COLS + B_COLS:A_COLS + B_COLS + C_COLS]
    d = w_in[:, A_COLS + B_COLS + C_COLS:]
    zeros = lambda w: jnp.zeros((D_MODEL, w), _F32)
    half = MLA_ROPE // 2
    k_rope = a[:, 384:416]
    k_rope_swap = jnp.concatenate([k_rope[:, half:], k_rope[:, :half]], axis=1)
    w_in_p = jnp.concatenate([
        a[:, :384], zeros(64), k_rope, zeros(32), zeros(64), k_rope_swap, zeros(32), b,
        _swap_mid_heads(c[:, :256], 1), c[:, 256:384], c[:, 384:512], d], axis=1).astype(_BF16)
    assert w_in_p.shape[1] == IN_COLS_PADDED

    w_uq = p["mla_w_uq"][l].reshape(MLA_Q_RANK, MLA_HEADS, MLA_QK_DIM)
    w_uq_swap = jnp.concatenate([jnp.zeros_like(w_uq[..., :MLA_NOPE]), w_uq[..., MLA_NOPE + half:],
                                 w_uq[..., MLA_NOPE:MLA_NOPE + half]], axis=-1)
    w_uq = jnp.concatenate([_pad_last(w_uq, HEAD_LANES).reshape(MLA_Q_RANK, -1),
                            _pad_last(w_uq_swap, HEAD_LANES).reshape(MLA_Q_RANK, -1)], axis=1).astype(_BF16)

    def swap_gain(g):
        return _pad_last(jnp.concatenate([jnp.zeros((MLA_NOPE,), _F32), g[MLA_NOPE + half:],
                                          g[MLA_NOPE:MLA_NOPE + half]]), HEAD_LANES)[None]
    w_ukv = p["mla_w_ukv"][l].reshape(MLA_KV_RANK, MLA_HEADS, MLA_NOPE + MLA_V)
    k_nope = _pad_last(w_ukv[..., :MLA_NOPE], HEAD_LANES).reshape(MLA_KV_RANK, MLA_HEADS * HEAD_LANES)
    v_part = w_ukv[..., MLA_NOPE:].reshape(MLA_KV_RANK, MLA_HEADS * MLA_V)
    w_ukv_p = jnp.concatenate([k_nope, v_part], axis=1).astype(_BF16)

    eye = jnp.eye(len(POOL_WINDOWS), dtype=_F32)
    w_pool = jnp.einsum("gcd,gh->gchd", p["pool_w"][l], eye).reshape(256, 256).astype(_BF16)

    g_mo = p["mix_out_norm"][l]
    g_mo = jnp.concatenate([g_mo[:512], _swap_mid_heads(g_mo[512:768], 0), g_mo[768:]])
    w_mo = p["w_mix_out"][l]
    w_mo = jnp.concatenate([w_mo[:512], _swap_mid_heads(w_mo[512:768], 0), w_mo[768:]], axis=0)

    w_router = _pad_last(jnp.concatenate([p["w_expert"][l], p["w_group"][l]], axis=1), 128)
    w_router_hi = w_router.astype(_BF16)
    w_router_lo = (w_router - w_router_hi.astype(_F32)).astype(_BF16)
    b_router = jnp.concatenate([p["b_expert"][l], p["b_group"][l]])
    tm = TOKEN_TILE
    ltri = (jnp.arange(tm)[None, :] < jnp.arange(tm)[:, None]).astype(_BF16)
    utri = (jnp.arange(128)[:, None] < jnp.arange(128)[None, :]).astype(_BF16)
    tile2 = lambda g: jnp.concatenate([g, g])[None]
    return dict(
        g_mix=p["norm_mix"][l][None], w_in=w_in_p,
        g_cq=p["mla_g_cq"][l][None], w_uq=w_uq, g_ckv=p["mla_g_ckv"][l][None], w_ukv=w_ukv_p,
        g_q=_pad_last(p["mla_g_q"][l], HEAD_LANES)[None], g_k=_pad_last(p["mla_g_k"][l], HEAD_LANES)[None],
        g_q_swap=swap_gain(p["mla_g_q"][l]), g_k_swap=swap_gain(p["mla_g_k"][l]),
        w_pool=w_pool, pool_scale=p["pool_scale"][l][None], conv_w=p["conv_w"][l],
        conv_b=p["conv_b"][l][None], g_sq=tile2(p["swa_g_q"][l]), g_sk=tile2(p["swa_g_k"][l]),
        sinks=jnp.take(p["swa_sinks"][l], jnp.asarray([0, 2, 1, 3])),
        g_mo=g_mo[None], w_mo=w_mo.astype(_BF16), g_xa=p["norm_xa"][l][None],
        xa_w_q=p["xa_w_q"][l].astype(_BF16), g_xq=jnp.tile(p["xa_g_q"][l], 4)[None],
        xa_w_o=p["xa_w_o"][l].astype(_BF16), g_ffn=p["norm_ffn"][l][None],
        w_router_hi=w_router_hi, w_router_lo=w_router_lo, b_router=_pad_last(b_router, 128)[None],
        ltri=ltri, utri=utri,
    )


def kernel(x, mem, positions, norm_mix, w_in, mla_g_cq, mla_w_uq, mla_g_ckv, mla_w_ukv, mla_g_q, mla_g_k, pool_w, pool_scale, swa_g_q, swa_g_k, swa_sinks, conv_w, conv_b, mix_out_norm, w_mix_out, norm_xa, norm_mem, xa_w_q, xa_w_kv, xa_g_q, xa_g_k, xa_w_o, norm_ffn, w_group, b_group, w_expert, b_expert, w_gate, w_up, w_down):
    p = dict(norm_mix=norm_mix, w_in=w_in, mla_g_cq=mla_g_cq, mla_w_uq=mla_w_uq, mla_g_ckv=mla_g_ckv,
             mla_w_ukv=mla_w_ukv, mla_g_q=mla_g_q, mla_g_k=mla_g_k, pool_w=pool_w, pool_scale=pool_scale,
             swa_g_q=swa_g_q, swa_g_k=swa_g_k, swa_sinks=swa_sinks, conv_w=conv_w, conv_b=conv_b,
             mix_out_norm=mix_out_norm, w_mix_out=w_mix_out, norm_xa=norm_xa, xa_w_q=xa_w_q,
             xa_g_q=xa_g_q, xa_w_o=xa_w_o, norm_ffn=norm_ffn, w_group=w_group, b_group=b_group,
             w_expert=w_expert, b_expert=b_expert)
    batch, seq_len, _ = x.shape
    depth = w_in.shape[0]
    n = batch * seq_len
    assert seq_len % MIX_TILE == 0 and MIX_TILE % TOKEN_TILE == 0
    assert seq_len % PROJ_TILE == 0 and seq_len % ATTN_Q_TILE == 0
    assert mem.shape[1] == MEM_LEN

    xf = x.reshape(n, D_MODEL)
    tabs = _rope_tables(positions)
    kx, vx = _memkv_call(mem.reshape(batch * MEM_LEN, D_MODEL), norm_mem[:, None, :],
                         xa_w_kv.astype(_BF16), jnp.tile(xa_g_k, (1, 4))[:, None, :], depth, batch)
    for l in range(depth):
        lw = _layer_weights(l, p)
        qm, km, vm, yb, yc, yd = _proj_call(xf, lw, tabs, seq_len)
        ya = _mla_attn_call(qm, km, vm, batch, seq_len)
        x2, h3, meta, cnt = _mix_call(ya, yb, yc, yd, xf, lw, kx[l], vx[l], seq_len)
        tab, used, ztab, blk_expert, n_valid = _moe_tables(cnt, n)
        xs = _dispatch_call(tab, used, ztab, meta, h3)
        ys = _ffn_call(blk_expert, n_valid, xs, w_gate, w_up, w_down, l)
        xf = _combine_call(tab, used, meta, x2, ys)
    return xf.reshape(batch, seq_len, D_MODEL)
```

```python
import functools

import jax
import jax.numpy as jnp
import numpy as np
from jax import lax
from jax.experimental import pallas as pl
from jax.experimental.pallas import tpu as pltpu

EPS = 1e-6
NEG_BIG = -1e30
LOG2_E = 1.4426950408889634
ROPE_THETA = 10000.0

D_MODEL = 1024
MEM_LEN = 256
GROUP_W = 256

MLA_HEADS = 4
MLA_Q_RANK = 256
MLA_KV_RANK = 128
MLA_NOPE = 64
MLA_ROPE = 32
MLA_QK_DIM = MLA_NOPE + MLA_ROPE
MLA_V = 64
HEAD_LANES = 128

POOL_WINDOWS = (2, 4, 8, 16)
POOL_GROUP = 64
POOL_HALO = 16

SWA_HEADS = 4
SWA_KV_HEADS = 2
SWA_HEAD_DIM = 64
SWA_WINDOW = 128

CONV_CH = 256

XA_HEADS = 4
XA_HEAD_DIM = 64

N_EXPERT_GROUPS = 4
EXPERTS_PER_GROUP = 8
N_EXPERTS = 32
D_EXPERT = 256
MOE_BLOCK = 1024

A_COLS = MLA_Q_RANK + MLA_KV_RANK + MLA_ROPE
B_COLS = GROUP_W
C_COLS = (SWA_HEADS + 2 * SWA_KV_HEADS) * SWA_HEAD_DIM
COL_CQ, COL_CKV, COL_KROPE, COL_KROPE_SWAP = 0, 256, 384, 512
COL_POOL, COL_SWA_Q, COL_SWA_K, COL_SWA_V, COL_CONV = 640, 896, 1152, 1280, 1408
IN_COLS_PADDED = COL_CONV + 3 * CONV_CH

TOKEN_TILE = 512
PROJ_TILE = 512
MIX_TILE = 1024
ATTN_TILE = 512
ATTN_Q_TILE = 1024
RUN_ALIGN = 16
RUN_BITS = 6
SORT_ROWS = 2 * TOKEN_TILE + N_EXPERTS * RUN_ALIGN
SORT_CHUNK = 256
TAIL_BITS = (MOE_BLOCK // RUN_ALIGN).bit_length() - 1
ZERO_RING = 8
PIECE_COLS = 1 + 2 * N_EXPERTS
XS_COLS = D_MODEL + 128
VMEM_LIMIT = 56 * 1024 * 1024

_F32 = jnp.float32
_BF16 = jnp.bfloat16


def _params(n_axes):
    return pltpu.CompilerParams(dimension_semantics=("arbitrary",) * n_axes,
                                vmem_limit_bytes=VMEM_LIMIT)


def _dot(a, b):
    return jnp.dot(a, b, preferred_element_type=_F32)


def _dot_nt(a, b):
    return lax.dot_general(a, b, (((1,), (1,)), ((), ())), preferred_element_type=_F32)


def _rms_factor(x, width):
    ss = jnp.sum(x * x, axis=-1, keepdims=True)
    return lax.rsqrt(ss * (1.0 / width) + EPS)


def _rms_scale(x, width):
    return x * _rms_factor(x, width)


def _pair_norm64(x, lane_lo):
    x2 = x * x
    s_all = jnp.sum(x2, axis=-1, keepdims=True)
    s_lo = jnp.sum(jnp.where(lane_lo, x2, 0.0), axis=-1, keepdims=True)
    ss = jnp.where(lane_lo, s_lo, s_all - s_lo)
    return x * lax.rsqrt(ss * (1.0 / 64.0) + EPS)


def _rope_kernel(pos_ref, freq_ref, c_ref, s_ref):
    ang = pos_ref[...] * freq_ref[...]
    c_ref[...] = jnp.cos(ang)
    s_ref[...] = jnp.sin(ang)


def _rope_tables(positions):
    half = MLA_ROPE // 2
    per_row = HEAD_LANES // half
    n = positions.size
    inv_freq = ROPE_THETA ** (-jnp.arange(half, dtype=_F32) / half)
    pos = jnp.repeat(positions.astype(_F32).reshape(n // per_row, per_row), half, axis=1)
    rows = n // per_row
    tile = min(rows, 1024)
    tab = pl.BlockSpec((tile, HEAD_LANES), lambda i: (i, 0))
    shp = jax.ShapeDtypeStruct((rows, HEAD_LANES), _F32)
    cos, sin = pl.pallas_call(
        _rope_kernel,
        grid=(rows // tile,),
        in_specs=[tab, pl.BlockSpec((1, HEAD_LANES), lambda i: (0, 0))],
        out_specs=[tab, tab],
        out_shape=[shp, shp],
        compiler_params=_params(1),
        name="rope_tables",
    )(pos, jnp.tile(inv_freq, per_row)[None])
    return cos.reshape(n, half), sin.reshape(n, half)


def _rotary_lanes(cos, sin):
    rows = cos.shape[0]
    pad = jnp.zeros((rows, HEAD_LANES - MLA_QK_DIM), _F32)
    c = jnp.concatenate([jnp.ones((rows, MLA_NOPE), _F32), cos, cos, pad], axis=1)
    s = jnp.concatenate([jnp.zeros((rows, MLA_NOPE), _F32), -sin, sin, pad], axis=1)
    return c, s


def _swa_tile(q, k, v, k_before, v_before, sink_ref, first_of_sequence):
    w = SWA_WINDOW
    oldest = jnp.where(first_of_sequence, 0, -w)
    lane_lo = lax.broadcasted_iota(jnp.int32, (1, 128), 1) < 64
    qpos = lax.broadcasted_iota(jnp.int32, (w, 2 * w), 0)
    kpos = lax.broadcasted_iota(jnp.int32, (w, 2 * w), 1) - w
    band = jnp.logical_and(kpos <= qpos, kpos > qpos - w)

    out_rows = []
    for jb in range(q.shape[0] // w):
        rs = slice(jb * w, (jb + 1) * w)
        if jb == 0:
            kprev, vprev = k_before, v_before
            visible = jnp.logical_and(band, kpos >= oldest)
        else:
            ps = slice((jb - 1) * w, jb * w)
            kprev, vprev = k[ps, :], v[ps, :]
            visible = band
        kk = jnp.concatenate([kprev, k[rs, :]], axis=0)
        vv = jnp.concatenate([vprev, v[rs, :]], axis=0)
        zero = jnp.zeros_like(kk)
        k_half = (jnp.where(lane_lo, kk, zero), jnp.where(lane_lo, zero, kk))
        pair_out = []
        for blk in range(2):
            qblk = q[rs, blk * 128:(blk + 1) * 128]
            outs = []
            for half in range(2):
                sink = sink_ref[2 * blk + half]
                s = _dot_nt(qblk, k_half[half])
                s = jnp.where(visible, s, NEG_BIG)
                m = jnp.maximum(jnp.max(s, axis=-1, keepdims=True), sink)
                p = jnp.exp(s - m)
                denom = jnp.sum(p, axis=-1, keepdims=True) + jnp.exp(sink - m)
                outs.append(_dot(p.astype(_BF16), vv) / denom)
            pair_out.append(jnp.where(lane_lo, outs[0], outs[1]))
        out_rows.append(jnp.concatenate(pair_out, axis=1))
    return jnp.concatenate(out_rows, axis=0)


def _proj_kernel(x_ref, gmix_ref, win_ref, gcq_ref, wuq_ref, gckv_ref, wukv_ref, gq_ref, gqs_ref,
                 gk_ref, gks_ref, c_ref, s_ref, wpool_ref, pscale_ref, convw_ref, convb_ref, gsq_ref,
                 gsk_ref, sink_ref, qm_ref, km_ref, vm_ref, yb_ref, yc_ref, yd_ref, halo_ref, swa_halo_ref,
                 *, tiles_per_seq):
    tm = PROJ_TILE
    i = pl.program_id(0)
    seq_tile = i % tiles_per_seq

    x = x_ref[...]
    h = _rms_scale(x, D_MODEL) * gmix_ref[...]
    u = _dot(h.astype(_BF16), win_ref[...])

    c, s = _rotary_lanes(c_ref[...], s_ref[...])
    hw = MLA_HEADS * HEAD_LANES

    cq = _rms_scale(u[:, COL_CQ:COL_CQ + MLA_Q_RANK], MLA_Q_RANK) * gcq_ref[...]
    q = _dot(cq.astype(_BF16), wuq_ref[...])
    gq = (gq_ref[...] * (MLA_QK_DIM ** -0.5 * LOG2_E)) * c
    gqs = (gqs_ref[...] * (MLA_QK_DIM ** -0.5 * LOG2_E)) * s
    for hd in range(MLA_HEADS):
        sl = slice(hd * HEAD_LANES, (hd + 1) * HEAD_LANES)
        xq = q[:, sl]
        r = _rms_factor(xq, MLA_QK_DIM)
        qm_ref[:, sl] = ((xq * r) * gq + (q[:, hw + sl.start:hw + sl.stop] * r) * gqs).astype(_BF16)

    ckv = _rms_scale(u[:, COL_CKV:COL_CKV + MLA_KV_RANK], MLA_KV_RANK) * gckv_ref[...]
    kv = _dot(ckv.astype(_BF16), wukv_ref[...])
    krope = u[:, COL_KROPE:COL_KROPE + HEAD_LANES]
    krope_swap = u[:, COL_KROPE_SWAP:COL_KROPE_SWAP + HEAD_LANES]
    gk = gk_ref[...] * c
    gks = gks_ref[...] * s
    for hd in range(MLA_HEADS):
        sl = slice(hd * HEAD_LANES, (hd + 1) * HEAD_LANES)
        xk = kv[:, sl] + krope
        r = _rms_factor(xk, MLA_QK_DIM)
        km_ref[:, sl] = ((xk * r) * gk + (krope_swap * r) * gks).astype(_BF16)
    vm_ref[...] = kv[:, hw:hw + MLA_HEADS * MLA_V].astype(_BF16)

    lane_lo = lax.broadcasted_iota(jnp.int32, (1, HEAD_LANES), 1) < 64
    gsq = gsq_ref[...] * (SWA_HEAD_DIM ** -0.5)
    qs = jnp.concatenate(
        [(_pair_norm64(u[:, COL_SWA_Q + blk * 128:COL_SWA_Q + (blk + 1) * 128], lane_lo) * gsq).astype(_BF16)
         for blk in range(2)], axis=1)
    ks = (_pair_norm64(u[:, COL_SWA_K:COL_SWA_K + 128], lane_lo) * gsk_ref[...]).astype(_BF16)
    vs = u[:, COL_SWA_V:COL_SWA_V + 128].astype(_BF16)

    @pl.when(seq_tile == 0)
    def _():
        halo_ref[...] = jnp.zeros_like(halo_ref)
        swa_halo_ref[...] = jnp.zeros_like(swa_halo_ref)

    swa_halo = swa_halo_ref[...].astype(_BF16)
    yc_ref[...] = _swa_tile(qs, ks, vs, swa_halo[:, 0:128], swa_halo[:, 128:256], sink_ref,
                            seq_tile == 0).astype(_BF16)
    swa_halo_ref[:, 0:128] = ks[tm - SWA_WINDOW:, :].astype(_F32)
    swa_halo_ref[:, 128:256] = vs[tm - SWA_WINDOW:, :].astype(_F32)

    halo = halo_ref[...]
    up = u[:, COL_POOL:COL_POOL + GROUP_W]
    ud = u[:, COL_CONV:COL_CONV + 3 * CONV_CH]

    b = jnp.concatenate([halo[:, 0:256], up], axis=0)
    w2 = b + pltpu.roll(b, 1, 0)
    w4 = w2 + pltpu.roll(w2, 2, 0)
    w8 = w4 + pltpu.roll(w4, 4, 0)
    w16 = w8 + pltpu.roll(w8, 8, 0)
    lane = lax.broadcasted_iota(jnp.int32, (1, 256), 1)
    win = jnp.where(lane < 64, w2, jnp.where(lane < 128, w4, jnp.where(lane < 192, w8, w16)))
    win = win[POOL_HALO:, :]
    width = jnp.where(lane < 64, 2, jnp.where(lane < 128, 4, jnp.where(lane < 192, 8, 16)))
    t = seq_tile * tm + lax.broadcasted_iota(jnp.int32, (tm, 1), 0)
    count = jnp.minimum(t + 1, width).astype(_F32)
    pooled = win / count - up
    yb_ref[...] = (_dot(pooled.astype(_BF16), wpool_ref[...]) * pscale_ref[...]).astype(_BF16)

    z = ud[:, 256:512] * ud[:, 512:768]
    zh = halo[:, 512:768] * halo[:, 768:1024]
    zb = jnp.concatenate([zh, z], axis=0)
    cw = convw_ref[...]
    conv = (pltpu.roll(zb, 2, 0)[POOL_HALO:, :] * cw[0:1, :]
            + pltpu.roll(zb, 1, 0)[POOL_HALO:, :] * cw[1:2, :]
            + z * cw[2:3, :])
    yd_ref[...] = (ud[:, 0:256] * (conv + convb_ref[...])).astype(_BF16)

    halo_ref[:, 0:256] = up[tm - POOL_HALO:, :]
    halo_ref[:, 256:1024] = ud[tm - POOL_HALO:, :]


def _proj_call(x, lw, tabs, seq_len):
    n = x.shape[0]
    tm = PROJ_TILE
    tiles_per_seq = seq_len // tm

    def full(a):
        nd = a.ndim
        return pl.BlockSpec(a.shape, lambda i, _nd=nd: (0,) * _nd)

    def rows(width):
        return pl.BlockSpec((tm, width), lambda i: (i, 0))

    consts = [lw["g_mix"], lw["w_in"], lw["g_cq"], lw["w_uq"], lw["g_ckv"], lw["w_ukv"],
              lw["g_q"], lw["g_q_swap"], lw["g_k"], lw["g_k_swap"]]
    consts2 = [lw["w_pool"], lw["pool_scale"], lw["conv_w"], lw["conv_b"], lw["g_sq"], lw["g_sk"]]
    out_widths = [512, 512, 256, 256, 256, 256]
    return pl.pallas_call(
        functools.partial(_proj_kernel, tiles_per_seq=tiles_per_seq),
        grid=(n // tm,),
        in_specs=[rows(D_MODEL)] + [full(a) for a in consts] + [rows(MLA_ROPE // 2)] * 2
                 + [full(a) for a in consts2] + [pl.BlockSpec(memory_space=pltpu.SMEM)],
        out_specs=[rows(w) for w in out_widths],
        out_shape=[jax.ShapeDtypeStruct((n, w), _BF16) for w in out_widths],
        scratch_shapes=[pltpu.VMEM((POOL_HALO, 1024), _F32), pltpu.VMEM((SWA_WINDOW, 256), _F32)],
        compiler_params=_params(1),
        name="proj_in",
    )(x, *consts, *tabs, *consts2, lw["sinks"])


def _mla_attn_kernel(q_ref, k_hbm, v_hbm, o_ref, kbuf, vbuf, sems, m_ref, l_ref, acc_ref, *, seq_len):
    tq, tk = ATTN_Q_TILE, ATTN_TILE
    ratio = tq // tk
    assert ratio == 2
    b = pl.program_id(0)
    qb = pl.program_id(1)
    row0 = b * seq_len

    def fetch(j, slot):
        rows = pl.ds(pl.multiple_of(row0 + j * tk, tk), tk)
        return (pltpu.make_async_copy(k_hbm.at[rows], kbuf.at[slot], sems.at[0, slot]),
                pltpu.make_async_copy(v_hbm.at[rows], vbuf.at[slot], sems.at[1, slot]))

    def start(j, slot):
        for cp in fetch(j, slot):
            cp.start()

    def wait(j, slot):
        for cp in fetch(j, slot):
            cp.wait()

    m_ref[...] = jnp.full_like(m_ref, NEG_BIG)
    l_ref[...] = jnp.zeros_like(l_ref)
    acc_ref[...] = jnp.zeros_like(acc_ref)
    lane_lo = lax.broadcasted_iota(jnp.int32, (1, HEAD_LANES), 1) < 64

    def accumulate(slot, rows, diagonal_from):
        nrows = rows.stop - rows.start
        if diagonal_from is not None:
            visible = (lax.broadcasted_iota(jnp.int32, (nrows, tk), 1)
                       <= lax.broadcasted_iota(jnp.int32, (nrows, tk), 0) + (rows.start - diagonal_from))
        for pair in range(MLA_HEADS // 2):
            vblk = vbuf[slot, :, pair * 128:(pair + 1) * 128]
            alphas = []
            pvs = []
            for sub in range(2):
                hd = 2 * pair + sub
                sl = slice(hd * HEAD_LANES, (hd + 1) * HEAD_LANES)
                s = _dot_nt(q_ref[rows, sl], kbuf[slot, :, sl])
                if diagonal_from is not None:
                    s = jnp.where(visible, s, NEG_BIG)
                m_prev = m_ref[hd, rows, :]
                m_new = jnp.maximum(m_prev, jnp.max(s, axis=-1, keepdims=True))
                alpha = jnp.exp2(m_prev - m_new)
                p = jnp.exp2(s - jnp.tile(m_new, (1, tk // HEAD_LANES)))
                l_ref[hd, rows, :] = alpha * l_ref[hd, rows, :] + jnp.sum(p, axis=-1, keepdims=True)
                m_ref[hd, rows, :] = m_new
                alphas.append(alpha)
                pvs.append(_dot(p.astype(_BF16), vblk))
            psl = slice(pair * 128, (pair + 1) * 128)
            acc_ref[rows, psl] = (acc_ref[rows, psl] * jnp.where(lane_lo, alphas[0], alphas[1])
                                  + jnp.where(lane_lo, pvs[0], pvs[1]))

    everything = slice(0, tq)
    start(0, 0)

    def two_tiles(pair_idx, carry):
        j = pair_idx * ratio
        wait(j, 0)
        start(j + 1, 1)
        accumulate(0, everything, None)
        wait(j + 1, 1)
        start(j + 2, 0)
        accumulate(1, everything, None)
        return carry

    lax.fori_loop(0, qb, two_tiles, 0)

    j = qb * ratio
    wait(j, 0)
    start(j + 1, 1)
    accumulate(0, everything, 0)
    wait(j + 1, 1)
    accumulate(1, slice(tk, tq), tk)

    for pair in range(MLA_HEADS // 2):
        psl = slice(pair * 128, (pair + 1) * 128)
        denom = jnp.where(lane_lo, l_ref[2 * pair], l_ref[2 * pair + 1])
        o_ref[:, psl] = (acc_ref[:, psl] / denom).astype(o_ref.dtype)


def _mla_attn_call(qm, km, vm, batch, seq_len):
    tq, tk = ATTN_Q_TILE, ATTN_TILE
    nq = seq_len // tq
    return pl.pallas_call(
        functools.partial(_mla_attn_kernel, seq_len=seq_len),
        grid=(batch, nq),
        in_specs=[pl.BlockSpec((tq, 512), lambda b, q: (b * nq + q, 0)),
                  pl.BlockSpec(memory_space=pl.ANY), pl.BlockSpec(memory_space=pl.ANY)],
        out_specs=pl.BlockSpec((tq, 256), lambda b, q: (b * nq + q, 0)),
        out_shape=jax.ShapeDtypeStruct((batch * seq_len, 256), _BF16),
        scratch_shapes=[pltpu.VMEM((2, tk, 512), _BF16), pltpu.VMEM((2, tk, 256), _BF16),
                        pltpu.SemaphoreType.DMA((2, 2)),
                        pltpu.VMEM((MLA_HEADS, tq, HEAD_LANES), _F32),
                        pltpu.VMEM((MLA_HEADS, tq, HEAD_LANES), _F32),
                        pltpu.VMEM((tq, 256), _F32)],
        compiler_params=_params(2),
        name="mla_attention",
    )(qm, km, vm)


def _memkv_kernel(mem_ref, gmem_ref, wkv_ref, gk_ref, k_ref, v_ref):
    m = _rms_scale(mem_ref[...], D_MODEL) * gmem_ref[...]
    kv = _dot(m.astype(_BF16), wkv_ref[...])
    lane_lo = lax.broadcasted_iota(jnp.int32, (1, 128), 1) < 64
    lane = lax.broadcasted_iota(jnp.int32, (1, 256), 1)
    k = jnp.concatenate([_pair_norm64(kv[:, 0:128], lane_lo), _pair_norm64(kv[:, 128:256], lane_lo)],
                        axis=1) * gk_ref[...]
    v = kv[:, 256:512]
    for hd in range(XA_HEADS):
        own = jnp.logical_and(lane >= hd * XA_HEAD_DIM, lane < (hd + 1) * XA_HEAD_DIM)
        k_ref[hd] = jnp.where(own, k, 0.0).astype(_BF16)
        v_ref[hd] = jnp.where(own, v, 0.0).astype(_BF16)


def _memkv_call(mem2d, g_mem, w_kv, g_k4, depth, batch):
    out = jax.ShapeDtypeStruct((depth, batch, XA_HEADS, MEM_LEN, 256), _BF16)
    ospec = pl.BlockSpec((None, None, XA_HEADS, MEM_LEN, 256), lambda l, b: (l, b, 0, 0, 0))
    return pl.pallas_call(
        _memkv_kernel,
        grid=(depth, batch),
        in_specs=[pl.BlockSpec((MEM_LEN, D_MODEL), lambda l, b: (b, 0)),
                  pl.BlockSpec((None, 1, D_MODEL), lambda l, b: (l, 0, 0)),
                  pl.BlockSpec((None, D_MODEL, 512), lambda l, b: (l, 0, 0)),
                  pl.BlockSpec((None, 1, 256), lambda l, b: (l, 0, 0))],
        out_specs=[ospec, ospec],
        out_shape=[out, out],
        compiler_params=_params(2),
        name="memory_kv",
    )(mem2d, g_mem, w_kv, g_k4)


def _mix_kernel(ya_ref, yb_ref, yc_ref, yd_ref, x_ref, gmo_ref, wmo_ref, gxa_ref, wq_ref, gxq_ref,
                kx_ref, vx_ref, wo_ref, gffn_ref, wrh_ref, wrl_ref, br_ref, ltri_ref, utri_ref,
                x2_ref, h3_ref, meta_ref, cnt_ref):
    tm = MIX_TILE
    gmo = gmo_ref[...]
    lane_lo = lax.broadcasted_iota(jnp.int32, (1, 128), 1) < 64

    def rows_to_logits(rs):
        parts = []
        for g, ref in enumerate((ya_ref, yb_ref, yc_ref, yd_ref)):
            yg = ref[rs, :].astype(_F32)
            parts.append((_rms_scale(yg, GROUP_W) * gmo[:, g * 256:(g + 1) * 256]).astype(_BF16))
        y = jnp.concatenate(parts, axis=1)
        x1 = x_ref[rs, :] + _dot(y, wmo_ref[...])

        h = (_rms_scale(x1, D_MODEL) * gxa_ref[...]).astype(_BF16)
        q = _dot(h, wq_ref[...])
        qn = jnp.concatenate([_pair_norm64(q[:, 0:128], lane_lo), _pair_norm64(q[:, 128:256], lane_lo)],
                             axis=1)
        qn = (qn * gxq_ref[...] * (XA_HEAD_DIM ** -0.5)).astype(_BF16)
        o = jnp.zeros((x1.shape[0], 256), _F32)
        for hd in range(XA_HEADS):
            s = _dot_nt(qn, kx_ref[hd])
            m = jnp.max(s, axis=-1, keepdims=True)
            p = jnp.exp(s - m)
            denom = jnp.sum(p, axis=-1, keepdims=True)
            o = o + _dot(p.astype(_BF16), vx_ref[hd]) / denom
        x2 = x1 + _dot(o.astype(_BF16), wo_ref[...])
        x2_ref[rs, :] = x2

        h3 = _rms_scale(x2, D_MODEL) * gffn_ref[...]
        h3_hi = h3.astype(_BF16)
        h3_lo = (h3 - h3_hi.astype(_F32)).astype(_BF16)
        h3_ref[rs, :] = h3_hi
        return (_dot(h3_hi, wrh_ref[...]) + (_dot(h3_hi, wrl_ref[...]) + _dot(h3_lo, wrh_ref[...]))
                + br_ref[...])

    all_logits = rows_to_logits(slice(0, tm))
    for st in range(tm // TOKEN_TILE):
        meta, counts = _route(all_logits[st * TOKEN_TILE:(st + 1) * TOKEN_TILE, :], ltri_ref, utri_ref)
        meta_ref[st * TOKEN_TILE:(st + 1) * TOKEN_TILE, :] = meta
        cnt_ref[st * 8:(st + 1) * 8, :] = jnp.broadcast_to(counts, (8, 128))


def _route(logits, ltri_ref, utri_ref):
    lane = lax.broadcasted_iota(jnp.int32, (TOKEN_TILE, 128), 1).astype(_F32)
    far = 1e9
    is_group = jnp.logical_and(lane >= N_EXPERTS, lane < N_EXPERTS + N_EXPERT_GROUPS)
    gl = jnp.where(is_group, logits, -jnp.inf)
    gmax = jnp.max(gl, axis=-1, keepdims=True)
    gidx = jnp.min(jnp.where(gl == gmax, lane, far), axis=-1, keepdims=True) - N_EXPERTS
    g_w = 1.0 / jnp.sum(jnp.where(is_group, jnp.exp(logits - gmax), 0.0), axis=-1, keepdims=True)
    in_group = jnp.logical_and(lane < N_EXPERTS, jnp.floor(lane * (1.0 / EXPERTS_PER_GROUP)) == gidx)
    el = jnp.where(in_group, logits, -jnp.inf)
    emax = jnp.max(el, axis=-1, keepdims=True)
    ep = jnp.where(in_group, jnp.exp(logits - emax), 0.0)
    prob = jnp.where(in_group, ep / jnp.sum(ep, axis=-1, keepdims=True), -1.0)
    p1 = jnp.max(prob, axis=-1, keepdims=True)
    e0 = jnp.min(jnp.where(prob == p1, lane, far), axis=-1, keepdims=True)
    prob2 = jnp.where(lane == e0, -1.0, prob)
    p2 = jnp.max(prob2, axis=-1, keepdims=True)
    e1 = jnp.min(jnp.where(prob2 == p2, lane, far), axis=-1, keepdims=True)
    w0 = g_w * (p1 / (p1 + p2))
    w1 = g_w * (p2 / (p1 + p2))

    onehot = jnp.where(jnp.logical_or(lane == e0, lane == e1), 1.0, 0.0)
    prefix = _dot(ltri_ref[...], onehot.astype(_BF16))
    counts = jnp.sum(onehot, axis=0, keepdims=True)
    units = jnp.floor((counts + (RUN_ALIGN - 1)) * (1.0 / RUN_ALIGN))
    offs = _dot(jnp.broadcast_to(units, (8, 128)).astype(_BF16), utri_ref[...])[0:1, :] * RUN_ALIGN
    where_to = prefix + offs
    pos0 = jnp.sum(jnp.where(lane == e0, where_to, 0.0), axis=-1, keepdims=True)
    pos1 = jnp.sum(jnp.where(lane == e1, where_to, 0.0), axis=-1, keepdims=True)

    meta = jnp.where(lane == 0, e0,
           jnp.where(lane == 1, e1,
           jnp.where(lane == 2, w0,
           jnp.where(lane == 3, w1,
           jnp.where(lane == 4, pos0,
           jnp.where(lane == 5, pos1, 0.0))))))
    return meta, counts


def _mix_call(ya, yb, yc, yd, x, lw, kx, vx, seq_len):
    n = x.shape[0]
    tm = MIX_TILE
    tiles_per_seq = seq_len // tm

    def full(a):
        nd = a.ndim
        return pl.BlockSpec(a.shape, lambda i, _nd=nd: (0,) * _nd)

    def rows(width):
        return pl.BlockSpec((tm, width), lambda i: (i, 0))

    kvspec = pl.BlockSpec((None, XA_HEADS, MEM_LEN, 256), lambda i: (i // tiles_per_seq, 0, 0, 0))
    consts_a = [lw["g_mo"], lw["w_mo"], lw["g_xa"], lw["xa_w_q"], lw["g_xq"]]
    consts_b = [lw["xa_w_o"], lw["g_ffn"], lw["w_router_hi"], lw["w_router_lo"], lw["b_router"],
                lw["ltri"], lw["utri"]]
    ntiles = n // tm
    return pl.pallas_call(
        _mix_kernel,
        grid=(ntiles,),
        in_specs=[rows(256)] * 4 + [rows(D_MODEL)] + [full(a) for a in consts_a]
                 + [kvspec, kvspec] + [full(a) for a in consts_b],
        out_specs=[rows(D_MODEL), rows(D_MODEL), rows(128),
                   pl.BlockSpec((8 * (tm // TOKEN_TILE), 128), lambda i: (i, 0))],
        out_shape=[jax.ShapeDtypeStruct((n, D_MODEL), _F32), jax.ShapeDtypeStruct((n, D_MODEL), _BF16),
                   jax.ShapeDtypeStruct((n, 128), _F32),
                   jax.ShapeDtypeStruct((n // TOKEN_TILE * 8, 128), _F32)],
        compiler_params=_params(1),
        name="mix_xattn_router",
    )(ya, yb, yc, yd, x, *consts_a, kx, vx, *consts_b)


def _move_runs(tab_ref, t, live, local_ref, remote_ref, sems, *, to_remote, wait):
    for bit in range(RUN_BITS):
        rows = RUN_ALIGN << bit
        base = (t * RUN_BITS + bit) * PIECE_COLS
        count = jnp.where(live, tab_ref[base], 0)

        def one(p, carry, base=base, rows=rows, bit=bit):
            loc = pl.multiple_of(tab_ref[base + 1 + p], RUN_ALIGN)
            rem = pl.multiple_of(tab_ref[base + 1 + N_EXPERTS + p], RUN_ALIGN)
            lsl = local_ref.at[pl.ds(loc, rows)]
            rsl = remote_ref.at[pl.ds(rem, rows)]
            src, dst = (lsl, rsl) if to_remote else (rsl, lsl)
            copy = pltpu.make_async_copy(src, dst, sems.at[bit, p])
            if wait:
                copy.wait()
            else:
                copy.start()
            return carry

        lax.fori_loop(0, count, one, 0)


def _selection(meta, first, rows):
    r = first + lax.broadcasted_iota(jnp.int32, (TOKEN_TILE, rows), 1)
    pos0 = meta[:, 4:5].astype(jnp.int32)
    pos1 = meta[:, 5:6].astype(jnp.int32)
    return jnp.where(r == pos0, 1.0, 0.0).astype(_BF16), jnp.where(r == pos1, 1.0, 0.0).astype(_BF16)


def _zero_rows(count, row_of, rows, zero_ref, xs_ref, sems):
    def copy(i):
        dst = xs_ref.at[pl.ds(pl.multiple_of(row_of(i), RUN_ALIGN), rows)]
        return pltpu.make_async_copy(zero_ref.at[pl.ds(0, rows)], dst, sems.at[i % ZERO_RING])

    def issue(i, carry):
        @pl.when(i >= ZERO_RING)
        def _():
            copy(i - ZERO_RING).wait()
        copy(i).start()
        return carry

    def drain(i, carry):
        copy(i).wait()
        return carry

    lax.fori_loop(0, count, issue, 0)
    lax.fori_loop(jnp.maximum(count - ZERO_RING, 0), count, drain, 0)


def _dispatch_kernel(tab_ref, used_ref, ztab_ref, meta_ref, h3_ref, xs_ref, buf0_ref, buf1_ref, zero_ref,
                     sems, zero_sems, *, ntiles, n_blocks):
    t = pl.program_id(0)
    last = ntiles - 1
    bufs = (buf0_ref, buf1_ref)
    used = used_ref[jnp.minimum(t, last)]

    @pl.when(t == 0)
    def _():
        zero_ref[...] = jnp.zeros_like(zero_ref)
        for bit in range(TAIL_BITS):
            base = bit * (1 + N_EXPERTS)
            _zero_rows(ztab_ref[base], lambda i, base=base: ztab_ref[base + 1 + i], RUN_ALIGN << bit,
                       zero_ref, xs_ref, zero_sems.at[bit])
        live_blocks = ztab_ref[TAIL_BITS * (1 + N_EXPERTS)]
        _zero_rows(n_blocks - live_blocks, lambda i: (live_blocks + i) * MOE_BLOCK, MOE_BLOCK,
                   zero_ref, xs_ref, zero_sems.at[TAIL_BITS])

    def step(slot):
        mine, other = bufs[slot], bufs[1 - slot]
        _move_runs(tab_ref, jnp.clip(t - 2, 0, last), t >= 2, mine, xs_ref, sems.at[slot],
                   to_remote=True, wait=True)
        _move_runs(tab_ref, jnp.clip(t - 1, 0, last), jnp.logical_and(t >= 1, t <= ntiles), other, xs_ref,
                   sems.at[1 - slot], to_remote=True, wait=False)

        meta = meta_ref[...]
        lane = lax.broadcasted_iota(jnp.int32, (TOKEN_TILE, 128), 1)
        cols = []
        for k in range(2):
            w = meta[:, 2 + k:3 + k]
            hi = w.astype(_BF16).astype(_F32)
            mid = (w - hi).astype(_BF16).astype(_F32)
            lo = (w - hi) - mid
            cols.append(jnp.where(lane == 0, hi, jnp.where(lane == 1, mid, jnp.where(lane == 2, lo, 0.0)))
                        .astype(_BF16))

        def sort_rows(first, rows):
            sel0, sel1 = _selection(meta, first, rows)
            tn = (((0,), (0,)), ((), ()))
            mine[first:first + rows, 0:D_MODEL] = lax.dot_general(
                sel0 + sel1, h3_ref[...], tn, preferred_element_type=_F32).astype(_BF16)
            mine[first:first + rows, D_MODEL:XS_COLS] = (
                lax.dot_general(sel0, cols[0], tn, preferred_element_type=_F32)
                + lax.dot_general(sel1, cols[1], tn, preferred_element_type=_F32)).astype(_BF16)

        sort_rows(0, 2 * TOKEN_TILE)
        for first in range(2 * TOKEN_TILE, SORT_ROWS, SORT_CHUNK):
            pl.when(used > first)(functools.partial(sort_rows, first, SORT_CHUNK))

    for slot in range(2):
        pl.when(t % 2 == slot)(functools.partial(step, slot))


def _dispatch_call(tab, used, ztab, meta, h3):
    n = h3.shape[0]
    tm = TOKEN_TILE
    ntiles = n // tm
    xs_rows = _xs_rows(n)
    tile = lambda i, tab, used, ztab: (jnp.minimum(i, ntiles - 1), 0)
    grid_spec = pltpu.PrefetchScalarGridSpec(
        num_scalar_prefetch=3,
        grid=(ntiles + 2,),
        in_specs=[pl.BlockSpec((tm, 128), tile),
                  pl.BlockSpec((tm, D_MODEL), tile)],
        out_specs=pl.BlockSpec(memory_space=pl.ANY),
        scratch_shapes=[pltpu.VMEM((SORT_ROWS, XS_COLS), _BF16), pltpu.VMEM((SORT_ROWS, XS_COLS), _BF16),
                        pltpu.VMEM((MOE_BLOCK, XS_COLS), _BF16),
                        pltpu.SemaphoreType.DMA((2, RUN_BITS, N_EXPERTS)),
                        pltpu.SemaphoreType.DMA((TAIL_BITS + 1, ZERO_RING))],
    )
    return pl.pallas_call(
        functools.partial(_dispatch_kernel, ntiles=ntiles, n_blocks=xs_rows // MOE_BLOCK),
        grid_spec=grid_spec,
        out_shape=jax.ShapeDtypeStruct((xs_rows, XS_COLS), _BF16),
        compiler_params=_params(1),
        name="moe_dispatch",
    )(tab, used, ztab, meta, h3)


def _ffn_kernel(be_ref, nv_ref, xs_ref, wg_ref, wu_ref, wd_ref, ys_ref, wgu_ref, wdn_ref):
    i = pl.program_id(0)

    @pl.when(jnp.logical_or(i == 0, be_ref[i] != be_ref[jnp.maximum(i - 1, 0)]))
    def _():
        wgu_ref[:, 0:D_EXPERT] = wg_ref[...].astype(_BF16)
        wgu_ref[:, D_EXPERT:2 * D_EXPERT] = wu_ref[...].astype(_BF16)
        wdn_ref[...] = wd_ref[...].astype(_BF16)

    @pl.when(i < nv_ref[0])
    def _():
        xb = xs_ref[:, 0:D_MODEL]
        wcols = xs_ref[:, D_MODEL:XS_COLS].astype(_F32)
        wt = wcols[:, 0:1] + wcols[:, 1:2] + wcols[:, 2:3]
        gu = _dot(xb, wgu_ref[...])
        g = gu[:, 0:D_EXPERT]
        a = (g * jax.nn.sigmoid(g)) * gu[:, D_EXPERT:2 * D_EXPERT]
        y = _dot(a.astype(_BF16), wdn_ref[...])
        ys_ref[...] = (y * wt).astype(ys_ref.dtype)

    @pl.when(i >= nv_ref[0])
    def _():
        ys_ref[...] = jnp.zeros_like(ys_ref)


def _ffn_call(blk_expert, n_valid, xs, w_gate, w_up, w_down, layer):
    n_blocks = xs.shape[0] // MOE_BLOCK
    wspec = lambda shape: pl.BlockSpec((None, None) + shape, lambda i, be, nv: (layer, be[i], 0, 0))
    last_live = lambda i, be, nv: (jnp.minimum(i, nv[0] - 1), 0)
    grid_spec = pltpu.PrefetchScalarGridSpec(
        num_scalar_prefetch=2,
        grid=(n_blocks,),
        in_specs=[pl.BlockSpec((MOE_BLOCK, XS_COLS), last_live),
                  wspec((D_MODEL, D_EXPERT)), wspec((D_MODEL, D_EXPERT)), wspec((D_EXPERT, D_MODEL))],
        out_specs=pl.BlockSpec((MOE_BLOCK, D_MODEL), lambda i, be, nv: (i, 0)),
        scratch_shapes=[pltpu.VMEM((D_MODEL, 2 * D_EXPERT), _BF16), pltpu.VMEM((D_EXPERT, D_MODEL), _BF16)],
    )
    return pl.pallas_call(
        _ffn_kernel,
        grid_spec=grid_spec,
        out_shape=jax.ShapeDtypeStruct((xs.shape[0], D_MODEL), _BF16),
        compiler_params=_params(1),
        name="moe_experts",
    )(blk_expert, n_valid, xs, w_gate, w_up, w_down)


def _combine_kernel(tab_ref, used_ref, meta_ref, x2_ref, ys_ref, out_ref, buf0_ref, buf1_ref, sems,
                    *, ntiles):
    t = pl.program_id(0)
    last = ntiles - 1
    bufs = (buf0_ref, buf1_ref)
    used = used_ref[jnp.clip(t - 1, 0, last)]

    @pl.when(t == 0)
    def _():
        buf0_ref[...] = jnp.zeros_like(buf0_ref)
        buf1_ref[...] = jnp.zeros_like(buf1_ref)

    def step(slot):
        mine, other = bufs[slot], bufs[1 - slot]
        _move_runs(tab_ref, jnp.clip(t - 1, 0, last), t >= 1, other, ys_ref, sems.at[1 - slot],
                   to_remote=False, wait=True)
        _move_runs(tab_ref, jnp.minimum(t, last), t <= last, mine, ys_ref, sems.at[slot],
                   to_remote=False, wait=False)
        meta = meta_ref[...]

        def gathered(first, rows):
            sel0, sel1 = _selection(meta, first, rows)
            return _dot(sel0 + sel1, other[first:first + rows, :])

        out_ref[...] = x2_ref[...] + gathered(0, 2 * TOKEN_TILE)
        for first in range(2 * TOKEN_TILE, SORT_ROWS, SORT_CHUNK):
            @pl.when(used > first)
            def _(first=first):
                out_ref[...] += gathered(first, SORT_CHUNK)

    for slot in range(2):
        pl.when(t % 2 == slot)(functools.partial(step, slot))


def _combine_call(tab, used, meta, x2, ys):
    n = x2.shape[0]
    tm = TOKEN_TILE
    ntiles = n // tm
    tile = lambda i, tab, used: (jnp.maximum(i - 1, 0), 0)
    grid_spec = pltpu.PrefetchScalarGridSpec(
        num_scalar_prefetch=2,
        grid=(ntiles + 1,),
        in_specs=[pl.BlockSpec((tm, 128), tile),
                  pl.BlockSpec((tm, D_MODEL), tile),
                  pl.BlockSpec(memory_space=pl.ANY)],
        out_specs=pl.BlockSpec((tm, D_MODEL), tile),
        scratch_shapes=[pltpu.VMEM((SORT_ROWS, D_MODEL), _BF16), pltpu.VMEM((SORT_ROWS, D_MODEL), _BF16),
                        pltpu.SemaphoreType.DMA((2, RUN_BITS, N_EXPERTS))],
    )
    return pl.pallas_call(
        functools.partial(_combine_kernel, ntiles=ntiles),
        grid_spec=grid_spec,
        out_shape=jax.ShapeDtypeStruct((n, D_MODEL), _F32),
        compiler_params=_params(1),
        name="moe_combine",
    )(tab, used, meta, x2, ys)


def _moe_tables(cnt, n_tokens):
    ntiles = n_tokens // TOKEN_TILE
    counts = cnt.reshape(ntiles, 8, 128)[:, 0, :N_EXPERTS].astype(jnp.int32)
    units = (counts + RUN_ALIGN - 1) // RUN_ALIGN
    padded = units * RUN_ALIGN
    local = jnp.cumsum(padded, axis=1) - padded
    total = jnp.sum(padded, axis=0)
    total_blk = ((total + MOE_BLOCK - 1) // MOE_BLOCK) * MOE_BLOCK
    ends = jnp.cumsum(total_blk)
    starts = ends - total_blk
    remote = starts[None, :] + jnp.cumsum(padded, axis=0) - padded
    bits = jnp.arange(RUN_BITS, dtype=jnp.int32)[None, :, None]
    has = (units[:, None, :] >> bits) & 1
    done = (units[:, None, :] & ((1 << bits) - 1)) * RUN_ALIGN
    rank = jnp.cumsum(has, axis=2) - 1
    place = jnp.logical_and(rank[..., None] == jnp.arange(N_EXPERTS, dtype=jnp.int32), has[..., None] == 1)
    compact = lambda rows: jnp.sum(jnp.where(place, rows[..., None], 0), axis=2)
    tab = jnp.concatenate([jnp.sum(has, axis=2, keepdims=True), compact(local[:, None, :] + done),
                           compact(remote[:, None, :] + done)], axis=-1).reshape(-1).astype(jnp.int32)
    n_blocks = _xs_rows(n_tokens) // MOE_BLOCK
    blk_start = jnp.arange(n_blocks, dtype=jnp.int32) * MOE_BLOCK
    blk_expert = jnp.minimum(jnp.sum((ends[None, :] <= blk_start[:, None]).astype(jnp.int32), axis=1),
                             N_EXPERTS - 1)
    n_valid = (ends[-1] // MOE_BLOCK).astype(jnp.int32).reshape(1)
    used = jnp.sum(padded, axis=1).astype(jnp.int32)
    tail_units = (total_blk - total) // RUN_ALIGN
    tail_bits = jnp.arange(TAIL_BITS, dtype=jnp.int32)[:, None]
    tail_has = (tail_units[None, :] >> tail_bits) & 1
    tail_rows = starts + total + (tail_units[None, :] & ((1 << tail_bits) - 1)) * RUN_ALIGN
    tail_rank = jnp.cumsum(tail_has, axis=1) - 1
    tail_place = jnp.logical_and(tail_rank[..., None] == jnp.arange(N_EXPERTS, dtype=jnp.int32),
                                 tail_has[..., None] == 1)
    tail_list = jnp.sum(jnp.where(tail_place, tail_rows[..., None], 0), axis=1)
    ztab = jnp.concatenate([jnp.concatenate([jnp.sum(tail_has, axis=1, keepdims=True), tail_list],
                                            axis=1).reshape(-1), n_valid]).astype(jnp.int32)
    return tab, used, ztab, blk_expert, n_valid


def _xs_rows(n_tokens):
    ntiles = n_tokens // TOKEN_TILE
    worst = 2 * n_tokens + ntiles * N_EXPERTS * (RUN_ALIGN - 1) + N_EXPERTS * (MOE_BLOCK - 1)
    return ((worst + MOE_BLOCK - 1) // MOE_BLOCK) * MOE_BLOCK


def _pad_last(a, width):
    return jnp.pad(a, [(0, 0)] * (a.ndim - 1) + [(0, width - a.shape[-1])])


def _swap_mid_heads(a, axis):
    shape = a.shape
    a = a.reshape(shape[:axis] + (4, 64) + shape[axis + 1:])
    a = jnp.take(a, jnp.asarray([0, 2, 1, 3]), axis=axis)
    return a.reshape(shape)


def _layer_weights(l, p):
    w_in = p["w_in"][l]
    a, b = w_in[:, :A_COLS], w_in[:, A_COLS:A_COLS + B_COLS]
    c = w_in[:, A_COLS + B_COLS:A_COLS + B_COLS + C_COLS]
    d = w_in[:, A_COLS + B_COLS + C_COLS:]
    zeros = lambda w: jnp.zeros((D_MODEL, w), _F32)
    half = MLA_ROPE // 2
    k_rope = a[:, 384:416]
    k_rope_swap = jnp.concatenate([k_rope[:, half:], k_rope[:, :half]], axis=1)
    w_in_p = jnp.concatenate([
        a[:, :384], zeros(64), k_rope, zeros(32), zeros(64), k_rope_swap, zeros(32), b,
        _swap_mid_heads(c[:, :256], 1), c[:, 256:384], c[:, 384:512], d], axis=1).astype(_BF16)
    assert w_in_p.shape[1] == IN_COLS_PADDED

    w_uq = p["mla_w_uq"][l].reshape(MLA_Q_RANK, MLA_HEADS, MLA_QK_DIM)
    w_uq_swap = jnp.concatenate([jnp.zeros_like(w_uq[..., :MLA_NOPE]), w_uq[..., MLA_NOPE + half:],
                                 w_uq[..., MLA_NOPE:MLA_NOPE + half]], axis=-1)
    w_uq = jnp.concatenate([_pad_last(w_uq, HEAD_LANES).reshape(MLA_Q_RANK, -1),
                            _pad_last(w_uq_swap, HEAD_LANES).reshape(MLA_Q_RANK, -1)], axis=1).astype(_BF16)

    def swap_gain(g):
        return _pad_last(jnp.concatenate([jnp.zeros((MLA_NOPE,), _F32), g[MLA_NOPE + half:],
                                          g[MLA_NOPE:MLA_NOPE + half]]), HEAD_LANES)[None]
    w_ukv = p["mla_w_ukv"][l].reshape(MLA_KV_RANK, MLA_HEADS, MLA_NOPE + MLA_V)
    k_nope = _pad_last(w_ukv[..., :MLA_NOPE], HEAD_LANES).reshape(MLA_KV_RANK, MLA_HEADS * HEAD_LANES)
    v_part = w_ukv[..., MLA_NOPE:].reshape(MLA_KV_RANK, MLA_HEADS * MLA_V)
    w_ukv_p = jnp.concatenate([k_nope, v_part], axis=1).astype(_BF16)

    eye = jnp.eye(len(POOL_WINDOWS), dtype=_F32)
    w_pool = jnp.einsum("gcd,gh->gchd", p["pool_w"][l], eye).reshape(256, 256).astype(_BF16)

    g_mo = p["mix_out_norm"][l]
    g_mo = jnp.concatenate([g_mo[:512], _swap_mid_heads(g_mo[512:768], 0), g_mo[768:]])
    w_mo = p["w_mix_out"][l]
    w_mo = jnp.concatenate([w_mo[:512], _swap_mid_heads(w_mo[512:768], 0), w_mo[768:]], axis=0)

    w_router = _pad_last(jnp.concatenate([p["w_expert"][l], p["w_group"][l]], axis=1), 128)
    w_router_hi = w_router.astype(_BF16)
    w_router_lo = (w_router - w_router_hi.astype(_F32)).astype(_BF16)
    b_router = jnp.concatenate([p["b_expert"][l], p["b_group"][l]])
    tm = TOKEN_TILE
    ltri = (jnp.arange(tm)[None, :] < jnp.arange(tm)[:, None]).astype(_BF16)
    utri = (jnp.arange(128)[:, None] < jnp.arange(128)[None, :]).astype(_BF16)
    tile2 = lambda g: jnp.concatenate([g, g])[None]
    return dict(
        g_mix=p["norm_mix"][l][None], w_in=w_in_p,
        g_cq=p["mla_g_cq"][l][None], w_uq=w_uq, g_ckv=p["mla_g_ckv"][l][None], w_ukv=w_ukv_p,
        g_q=_pad_last(p["mla_g_q"][l], HEAD_LANES)[None], g_k=_pad_last(p["mla_g_k"][l], HEAD_LANES)[None],
        g_q_swap=swap_gain(p["mla_g_q"][l]), g_k_swap=swap_gain(p["mla_g_k"][l]),
        w_pool=w_pool, pool_scale=p["pool_scale"][l][None], conv_w=p["conv_w"][l],
        conv_b=p["conv_b"][l][None], g_sq=tile2(p["swa_g_q"][l]), g_sk=tile2(p["swa_g_k"][l]),
        sinks=jnp.take(p["swa_sinks"][l], jnp.asarray([0, 2, 1, 3])),
        g_mo=g_mo[None], w_mo=w_mo.astype(_BF16), g_xa=p["norm_xa"][l][None],
        xa_w_q=p["xa_w_q"][l].astype(_BF16), g_xq=jnp.tile(p["xa_g_q"][l], 4)[None],
        xa_w_o=p["xa_w_o"][l].astype(_BF16), g_ffn=p["norm_ffn"][l][None],
        w_router_hi=w_router_hi, w_router_lo=w_router_lo, b_router=_pad_last(b_router, 128)[None],
        ltri=ltri, utri=utri,
    )


def kernel(x, mem, positions, norm_mix, w_in, mla_g_cq, mla_w_uq, mla_g_ckv, mla_w_ukv, mla_g_q, mla_g_k, pool_w, pool_scale, swa_g_q, swa_g_k, swa_sinks, conv_w, conv_b, mix_out_norm, w_mix_out, norm_xa, norm_mem, xa_w_q, xa_w_kv, xa_g_q, xa_g_k, xa_w_o, norm_ffn, w_group, b_group, w_expert, b_expert, w_gate, w_up, w_down):
    p = dict(norm_mix=norm_mix, w_in=w_in, mla_g_cq=mla_g_cq, mla_w_uq=mla_w_uq, mla_g_ckv=mla_g_ckv,
             mla_w_ukv=mla_w_ukv, mla_g_q=mla_g_q, mla_g_k=mla_g_k, pool_w=pool_w, pool_scale=pool_scale,
             swa_g_q=swa_g_q, swa_g_k=swa_g_k, swa_sinks=swa_sinks, conv_w=conv_w, conv_b=conv_b,
             mix_out_norm=mix_out_norm, w_mix_out=w_mix_out, norm_xa=norm_xa, xa_w_q=xa_w_q,
             xa_g_q=xa_g_q, xa_w_o=xa_w_o, norm_ffn=norm_ffn, w_group=w_group, b_group=b_group,
             w_expert=w_expert, b_expert=b_expert)
    batch, seq_len, _ = x.shape
    depth = w_in.shape[0]
    n = batch * seq_len
    assert seq_len % MIX_TILE == 0 and MIX_TILE % TOKEN_TILE == 0
    assert seq_len % PROJ_TILE == 0 and seq_len % ATTN_Q_TILE == 0
    assert mem.shape[1] == MEM_LEN

    xf = x.reshape(n, D_MODEL)
    tabs = _rope_tables(positions)
    kx, vx = _memkv_call(mem.reshape(batch * MEM_LEN, D_MODEL), norm_mem[:, None, :],
                         xa_w_kv.astype(_BF16), jnp.tile(xa_g_k, (1, 4))[:, None, :], depth, batch)
    for l in range(depth):
        lw = _layer_weights(l, p)
        qm, km, vm, yb, yc, yd = _proj_call(xf, lw, tabs, seq_len)
        ya = _mla_attn_call(qm, km, vm, batch, seq_len)
        x2, h3, meta, cnt = _mix_call(ya, yb, yc, yd, xf, lw, kx[l], vx[l], seq_len)
        tab, used, ztab, blk_expert, n_valid = _moe_tables(cnt, n)
        xs = _dispatch_call(tab, used, ztab, meta, h3)
        ys = _ffn_call(blk_expert, n_valid, xs, w_gate, w_up, w_down, l)
        xf = _combine_call(tab, used, meta, x2, ys)
    return xf.reshape(batch, seq_len, D_MODEL)
```

```python
import functools

import jax
import jax.numpy as jnp
import numpy as np
from jax import lax
from jax.experimental import pallas as pl
from jax.experimental.pallas import tpu as pltpu

EPS = 1e-6
NEG_BIG = -1e30
LOG2_E = 1.4426950408889634
ROPE_THETA = 10000.0

D_MODEL = 1024
MEM_LEN = 256
GROUP_W = 256

MLA_HEADS = 4
MLA_Q_RANK = 256
MLA_KV_RANK = 128
MLA_NOPE = 64
MLA_ROPE = 32
MLA_QK_DIM = MLA_NOPE + MLA_ROPE
MLA_V = 64
HEAD_LANES = 128

POOL_WINDOWS = (2, 4, 8, 16)
POOL_GROUP = 64
POOL_HALO = 16

SWA_HEADS = 4
SWA_KV_HEADS = 2
SWA_HEAD_DIM = 64
SWA_WINDOW = 128

CONV_CH = 256

XA_HEADS = 4
XA_HEAD_DIM = 64

N_EXPERT_GROUPS = 4
EXPERTS_PER_GROUP = 8
N_EXPERTS = 32
D_EXPERT = 256
MOE_BLOCK = 1024

A_COLS = MLA_Q_RANK + MLA_KV_RANK + MLA_ROPE
B_COLS = GROUP_W
C_COLS = (SWA_HEADS + 2 * SWA_KV_HEADS) * SWA_HEAD_DIM
COL_CQ, COL_CKV, COL_KROPE, COL_KROPE_SWAP = 0, 256, 384, 512
COL_POOL, COL_SWA_Q, COL_SWA_K, COL_SWA_V, COL_CONV = 640, 896, 1152, 1280, 1408
IN_COLS_PADDED = COL_CONV + 3 * CONV_CH

TOKEN_TILE = 512
PROJ_TILE = 512
MIX_TILE = 1024
ATTN_TILE = 512
ATTN_Q_TILE = 1024
RUN_ALIGN = 16
RUN_BITS = 6
SORT_ROWS = 2 * TOKEN_TILE + N_EXPERTS * RUN_ALIGN
SORT_CHUNK = 256
TAIL_BITS = (MOE_BLOCK // RUN_ALIGN).bit_length() - 1
ZERO_RING = 8
PIECE_COLS = 1 + 2 * N_EXPERTS
XS_COLS = D_MODEL + 128
VMEM_LIMIT = 56 * 1024 * 1024

_F32 = jnp.float32
_BF16 = jnp.bfloat16


def _params(n_axes):
    return pltpu.CompilerParams(dimension_semantics=("arbitrary",) * n_axes,
                                vmem_limit_bytes=VMEM_LIMIT)


def _dot(a, b):
    return jnp.dot(a, b, preferred_element_type=_F32)


def _dot_nt(a, b):
    return lax.dot_general(a, b, (((1,), (1,)), ((), ())), preferred_element_type=_F32)


def _rms_factor(x, width):
    ss = jnp.sum(x * x, axis=-1, keepdims=True)
    return lax.rsqrt(ss * (1.0 / width) + EPS)


def _rms_scale(x, width):
    return x * _rms_factor(x, width)


def _pair_norm64(x, lane_lo):
    x2 = x * x
    s_all = jnp.sum(x2, axis=-1, keepdims=True)
    s_lo = jnp.sum(jnp.where(lane_lo, x2, 0.0), axis=-1, keepdims=True)
    ss = jnp.where(lane_lo, s_lo, s_all - s_lo)
    return x * lax.rsqrt(ss * (1.0 / 64.0) + EPS)


def _rope_kernel(pos_ref, freq_ref, c_ref, s_ref):
    ang = pos_ref[...] * freq_ref[...]
    c_ref[...] = jnp.cos(ang)
    s_ref[...] = jnp.sin(ang)


def _rope_tables(positions):
    half = MLA_ROPE // 2
    per_row = HEAD_LANES // half
    n = positions.size
    inv_freq = ROPE_THETA ** (-jnp.arange(half, dtype=_F32) / half)
    pos = jnp.repeat(positions.astype(_F32).reshape(n // per_row, per_row), half, axis=1)
    rows = n // per_row
    tile = min(rows, 1024)
    tab = pl.BlockSpec((tile, HEAD_LANES), lambda i: (i, 0))
    shp = jax.ShapeDtypeStruct((rows, HEAD_LANES), _F32)
    cos, sin = pl.pallas_call(
        _rope_kernel,
        grid=(rows // tile,),
        in_specs=[tab, pl.BlockSpec((1, HEAD_LANES), lambda i: (0, 0))],
        out_specs=[tab, tab],
        out_shape=[shp, shp],
        compiler_params=_params(1),
        name="rope_tables",
    )(pos, jnp.tile(inv_freq, per_row)[None])
    return cos.reshape(n, half), sin.reshape(n, half)


def _rotary_lanes(cos, sin):
    rows = cos.shape[0]
    pad = jnp.zeros((rows, HEAD_LANES - MLA_QK_DIM), _F32)
    c = jnp.concatenate([jnp.ones((rows, MLA_NOPE), _F32), cos, cos, pad], axis=1)
    s = jnp.concatenate([jnp.zeros((rows, MLA_NOPE), _F32), -sin, sin, pad], axis=1)
    return c, s


def _swa_tile(q, k, v, k_before, v_before, sink_ref, first_of_sequence):
    w = SWA_WINDOW
    oldest = jnp.where(first_of_sequence, 0, -w)
    lane_lo = lax.broadcasted_iota(jnp.int32, (1, 128), 1) < 64
    qpos = lax.broadcasted_iota(jnp.int32, (w, 2 * w), 0)
    kpos = lax.broadcasted_iota(jnp.int32, (w, 2 * w), 1) - w
    band = jnp.logical_and(kpos <= qpos, kpos > qpos - w)

    out_rows = []
    for jb in range(q.shape[0] // w):
        rs = slice(jb * w, (jb + 1) * w)
        if jb == 0:
            kprev, vprev = k_before, v_before
            visible = jnp.logical_and(band, kpos >= oldest)
        else:
            ps = slice((jb - 1) * w, jb * w)
            kprev, vprev = k[ps, :], v[ps, :]
            visible = band
        kk = jnp.concatenate([kprev, k[rs, :]], axis=0)
        vv = jnp.concatenate([vprev, v[rs, :]], axis=0)
        zero = jnp.zeros_like(kk)
        k_half = (jnp.where(lane_lo, kk, zero), jnp.where(lane_lo, zero, kk))
        pair_out = []
        for blk in range(2):
            qblk = q[rs, blk * 128:(blk + 1) * 128]
            outs = []
            for half in range(2):
                sink = sink_ref[2 * blk + half]
                s = _dot_nt(qblk, k_half[half])
                s = jnp.where(visible, s, NEG_BIG)
                m = jnp.maximum(jnp.max(s, axis=-1, keepdims=True), sink)
                p = jnp.exp(s - m)
                denom = jnp.sum(p, axis=-1, keepdims=True) + jnp.exp(sink - m)
                outs.append(_dot(p.astype(_BF16), vv) / denom)
            pair_out.append(jnp.where(lane_lo, outs[0], outs[1]))
        out_rows.append(jnp.concatenate(pair_out, axis=1))
    return jnp.concatenate(out_rows, axis=0)


def _proj_kernel(x_ref, gmix_ref, win_ref, gcq_ref, wuq_ref, gckv_ref, wukv_ref, gq_ref, gqs_ref,
                 gk_ref, gks_ref, c_ref, s_ref, wpool_ref, pscale_ref, convw_ref, convb_ref, gsq_ref,
                 gsk_ref, sink_ref, qm_ref, km_ref, vm_ref, yb_ref, yc_ref, yd_ref, halo_ref, swa_halo_ref,
                 *, tiles_per_seq):
    tm = PROJ_TILE
    i = pl.program_id(0)
    seq_tile = i % tiles_per_seq

    x = x_ref[...]
    h = _rms_scale(x, D_MODEL) * gmix_ref[...]
    u = _dot(h.astype(_BF16), win_ref[...])

    c, s = _rotary_lanes(c_ref[...], s_ref[...])
    hw = MLA_HEADS * HEAD_LANES

    cq = _rms_scale(u[:, COL_CQ:COL_CQ + MLA_Q_RANK], MLA_Q_RANK) * gcq_ref[...]
    q = _dot(cq.astype(_BF16), wuq_ref[...])
    gq = (gq_ref[...] * (MLA_QK_DIM ** -0.5 * LOG2_E)) * c
    gqs = (gqs_ref[...] * (MLA_QK_DIM ** -0.5 * LOG2_E)) * s
    for hd in range(MLA_HEADS):
        sl = slice(hd * HEAD_LANES, (hd + 1) * HEAD_LANES)
        xq = q[:, sl]
        r = _rms_factor(xq, MLA_QK_DIM)
        qm_ref[:, sl] = ((xq * r) * gq + (q[:, hw + sl.start:hw + sl.stop] * r) * gqs).astype(_BF16)

    ckv = _rms_scale(u[:, COL_CKV:COL_CKV + MLA_KV_RANK], MLA_KV_RANK) * gckv_ref[...]
    kv = _dot(ckv.astype(_BF16), wukv_ref[...])
    krope = u[:, COL_KROPE:COL_KROPE + HEAD_LANES]
    krope_swap = u[:, COL_KROPE_SWAP:COL_KROPE_SWAP + HEAD_LANES]
    gk = gk_ref[...] * c
    gks = gks_ref[...] * s
    for hd in range(MLA_HEADS):
        sl = slice(hd * HEAD_LANES, (hd + 1) * HEAD_LANES)
        xk = kv[:, sl] + krope
        r = _rms_factor(xk, MLA_QK_DIM)
        km_ref[:, sl] = ((xk * r) * gk + (krope_swap * r) * gks).astype(_BF16)
    vm_ref[...] = kv[:, hw:hw + MLA_HEADS * MLA_V].astype(_BF16)

    lane_lo = lax.broadcasted_iota(jnp.int32, (1, HEAD_LANES), 1) < 64
    gsq = gsq_ref[...] * (SWA_HEAD_DIM ** -0.5)
    qs = jnp.concatenate(
        [(_pair_norm64(u[:, COL_SWA_Q + blk * 128:COL_SWA_Q + (blk + 1) * 128], lane_lo) * gsq).astype(_BF16)
         for blk in range(2)], axis=1)
    ks = (_pair_norm64(u[:, COL_SWA_K:COL_SWA_K + 128], lane_lo) * gsk_ref[...]).astype(_BF16)
    vs = u[:, COL_SWA_V:COL_SWA_V + 128].astype(_BF16)

    @pl.when(seq_tile == 0)
    def _():
        halo_ref[...] = jnp.zeros_like(halo_ref)
        swa_halo_ref[...] = jnp.zeros_like(swa_halo_ref)

    swa_halo = swa_halo_ref[...].astype(_BF16)
    yc_ref[...] = _swa_tile(qs, ks, vs, swa_halo[:, 0:128], swa_halo[:, 128:256], sink_ref,
                            seq_tile == 0).astype(_BF16)
    swa_halo_ref[:, 0:128] = ks[tm - SWA_WINDOW:, :].astype(_F32)
    swa_halo_ref[:, 128:256] = vs[tm - SWA_WINDOW:, :].astype(_F32)

    halo = halo_ref[...]
    up = u[:, COL_POOL:COL_POOL + GROUP_W]
    ud = u[:, COL_CONV:COL_CONV + 3 * CONV_CH]

    b = jnp.concatenate([halo[:, 0:256], up], axis=0)
    w2 = b + pltpu.roll(b, 1, 0)
    w4 = w2 + pltpu.roll(w2, 2, 0)
    w8 = w4 + pltpu.roll(w4, 4, 0)
    w16 = w8 + pltpu.roll(w8, 8, 0)
    lane = lax.broadcasted_iota(jnp.int32, (1, 256), 1)
    win = jnp.where(lane < 64, w2, jnp.where(lane < 128, w4, jnp.where(lane < 192, w8, w16)))
    win = win[POOL_HALO:, :]
    width = jnp.where(lane < 64, 2, jnp.where(lane < 128, 4, jnp.where(lane < 192, 8, 16)))
    t = seq_tile * tm + lax.broadcasted_iota(jnp.int32, (tm, 1), 0)
    count = jnp.minimum(t + 1, width).astype(_F32)
    pooled = win / count - up
    yb_ref[...] = (_dot(pooled.astype(_BF16), wpool_ref[...]) * pscale_ref[...]).astype(_BF16)

    z = ud[:, 256:512] * ud[:, 512:768]
    zh = halo[:, 512:768] * halo[:, 768:1024]
    zb = jnp.concatenate([zh, z], axis=0)
    cw = convw_ref[...]
    conv = (pltpu.roll(zb, 2, 0)[POOL_HALO:, :] * cw[0:1, :]
            + pltpu.roll(zb, 1, 0)[POOL_HALO:, :] * cw[1:2, :]
            + z * cw[2:3, :])
    yd_ref[...] = (ud[:, 0:256] * (conv + convb_ref[...])).astype(_BF16)

    halo_ref[:, 0:256] = up[tm - POOL_HALO:, :]
    halo_ref[:, 256:1024] = ud[tm - POOL_HALO:, :]


def _proj_call(x, lw, tabs, seq_len):
    n = x.shape[0]
    tm = PROJ_TILE
    tiles_per_seq = seq_len // tm

    def full(a):
        nd = a.ndim
        return pl.BlockSpec(a.shape, lambda i, _nd=nd: (0,) * _nd)

    def rows(width):
        return pl.BlockSpec((tm, width), lambda i: (i, 0))

    consts = [lw["g_mix"], lw["w_in"], lw["g_cq"], lw["w_uq"], lw["g_ckv"], lw["w_ukv"],
              lw["g_q"], lw["g_q_swap"], lw["g_k"], lw["g_k_swap"]]
    consts2 = [lw["w_pool"], lw["pool_scale"], lw["conv_w"], lw["conv_b"], lw["g_sq"], lw["g_sk"]]
    out_widths = [512, 512, 256, 256, 256, 256]
    return pl.pallas_call(
        functools.partial(_proj_kernel, tiles_per_seq=tiles_per_seq),
        grid=(n // tm,),
        in_specs=[rows(D_MODEL)] + [full(a) for a in consts] + [rows(MLA_ROPE // 2)] * 2
                 + [full(a) for a in consts2] + [pl.BlockSpec(memory_space=pltpu.SMEM)],
        out_specs=[rows(w) for w in out_widths],
        out_shape=[jax.ShapeDtypeStruct((n, w), _BF16) for w in out_widths],
        scratch_shapes=[pltpu.VMEM((POOL_HALO, 1024), _F32), pltpu.VMEM((SWA_WINDOW, 256), _F32)],
        compiler_params=_params(1),
        name="proj_in",
    )(x, *consts, *tabs, *consts2, lw["sinks"])


def _mla_attn_kernel(q_ref, k_hbm, v_hbm, o_ref, kbuf, vbuf, sems, m_ref, l_ref, acc_ref, *, seq_len):
    tq, tk = ATTN_Q_TILE, ATTN_TILE
    assert tq == 2 * tk
    b = pl.program_id(0)
    qb = pl.program_id(1)
    row0 = b * seq_len

    def fetch(j, slot):
        rows = pl.ds(pl.multiple_of(row0 + j * tq, tq), tq)
        return (pltpu.make_async_copy(k_hbm.at[rows], kbuf.at[slot], sems.at[0, slot]),
                pltpu.make_async_copy(v_hbm.at[rows], vbuf.at[slot], sems.at[1, slot]))

    def start(j, slot):
        for cp in fetch(j, slot):
            cp.start()

    def wait(j, slot):
        for cp in fetch(j, slot):
            cp.wait()

    m_ref[...] = jnp.full_like(m_ref, NEG_BIG)
    l_ref[...] = jnp.zeros_like(l_ref)
    acc_ref[...] = jnp.zeros_like(acc_ref)
    lane_lo = lax.broadcasted_iota(jnp.int32, (1, HEAD_LANES), 1) < 64

    def accumulate(slot, half, rows, diagonal_from):
        nrows = rows.stop - rows.start
        keys = slice(half * tk, (half + 1) * tk)
        if diagonal_from is not None:
            visible = (lax.broadcasted_iota(jnp.int32, (nrows, tk), 1)
                       <= lax.broadcasted_iota(jnp.int32, (nrows, tk), 0) + (rows.start - diagonal_from))
        for pair in range(MLA_HEADS // 2):
            vblk = vbuf[slot, keys, pair * 128:(pair + 1) * 128]
            alphas = []
            pvs = []
            for sub in range(2):
                hd = 2 * pair + sub
                sl = slice(hd * HEAD_LANES, (hd + 1) * HEAD_LANES)
                s = _dot_nt(q_ref[rows, sl], kbuf[slot, keys, sl])
                if diagonal_from is not None:
                    s = jnp.where(visible, s, NEG_BIG)
                m_prev = m_ref[hd, rows, :]
                m_new = jnp.maximum(m_prev, jnp.max(s, axis=-1, keepdims=True))
                alpha = jnp.exp2(m_prev - m_new)
                p = jnp.exp2(s - jnp.tile(m_new, (1, tk // HEAD_LANES)))
                l_ref[hd, rows, :] = alpha * l_ref[hd, rows, :] + jnp.sum(p, axis=-1, keepdims=True)
                m_ref[hd, rows, :] = m_new
                alphas.append(alpha)
                pvs.append(_dot(p.astype(_BF16), vblk))
            psl = slice(pair * 128, (pair + 1) * 128)
            acc_ref[rows, psl] = (acc_ref[rows, psl] * jnp.where(lane_lo, alphas[0], alphas[1])
                                  + jnp.where(lane_lo, pvs[0], pvs[1]))

    everything = slice(0, tq)
    start(0, 0)

    def earlier_group(j, carry):
        slot = j % 2
        wait(j, slot)
        start(j + 1, 1 - slot)
        accumulate(slot, 0, everything, None)
        accumulate(slot, 1, everything, None)
        return carry

    lax.fori_loop(0, qb, earlier_group, 0)

    slot = qb % 2
    wait(qb, slot)
    accumulate(slot, 0, everything, 0)
    accumulate(slot, 1, slice(tk, tq), tk)

    for pair in range(MLA_HEADS // 2):
        psl = slice(pair * 128, (pair + 1) * 128)
        denom = jnp.where(lane_lo, l_ref[2 * pair], l_ref[2 * pair + 1])
        o_ref[:, psl] = (acc_ref[:, psl] / denom).astype(o_ref.dtype)


def _mla_attn_call(qm, km, vm, batch, seq_len):
    tq, tk = ATTN_Q_TILE, ATTN_TILE
    nq = seq_len // tq
    return pl.pallas_call(
        functools.partial(_mla_attn_kernel, seq_len=seq_len),
        grid=(batch, nq),
        in_specs=[pl.BlockSpec((tq, 512), lambda b, q: (b * nq + q, 0)),
                  pl.BlockSpec(memory_space=pl.ANY), pl.BlockSpec(memory_space=pl.ANY)],
        out_specs=pl.BlockSpec((tq, 256), lambda b, q: (b * nq + q, 0)),
        out_shape=jax.ShapeDtypeStruct((batch * seq_len, 256), _BF16),
        scratch_shapes=[pltpu.VMEM((2, tq, 512), _BF16), pltpu.VMEM((2, tq, 256), _BF16),
                        pltpu.SemaphoreType.DMA((2, 2)),
                        pltpu.VMEM((MLA_HEADS, tq, HEAD_LANES), _F32),
                        pltpu.VMEM((MLA_HEADS, tq, HEAD_LANES), _F32),
                        pltpu.VMEM((tq, 256), _F32)],
        compiler_params=_params(2),
        name="mla_attention",
    )(qm, km, vm)


def _memkv_kernel(mem_ref, gmem_ref, wkv_ref, gk_ref, k_ref, v_ref):
    m = _rms_scale(mem_ref[...], D_MODEL) * gmem_ref[...]
    kv = _dot(m.astype(_BF16), wkv_ref[...])
    lane_lo = lax.broadcasted_iota(jnp.int32, (1, 128), 1) < 64
    lane = lax.broadcasted_iota(jnp.int32, (1, 256), 1)
    k = jnp.concatenate([_pair_norm64(kv[:, 0:128], lane_lo), _pair_norm64(kv[:, 128:256], lane_lo)],
                        axis=1) * gk_ref[...]
    v = kv[:, 256:512]
    for hd in range(XA_HEADS):
        own = jnp.logical_and(lane >= hd * XA_HEAD_DIM, lane < (hd + 1) * XA_HEAD_DIM)
        k_ref[hd] = jnp.where(own, k, 0.0).astype(_BF16)
        v_ref[hd] = jnp.where(own, v, 0.0).astype(_BF16)


def _memkv_call(mem2d, g_mem, w_kv, g_k4, depth, batch):
    out = jax.ShapeDtypeStruct((depth, batch, XA_HEADS, MEM_LEN, 256), _BF16)
    ospec = pl.BlockSpec((None, None, XA_HEADS, MEM_LEN, 256), lambda l, b: (l, b, 0, 0, 0))
    return pl.pallas_call(
        _memkv_kernel,
        grid=(depth, batch),
        in_specs=[pl.BlockSpec((MEM_LEN, D_MODEL), lambda l, b: (b, 0)),
                  pl.BlockSpec((None, 1, D_MODEL), lambda l, b: (l, 0, 0)),
                  pl.BlockSpec((None, D_MODEL, 512), lambda l, b: (l, 0, 0)),
                  pl.BlockSpec((None, 1, 256), lambda l, b: (l, 0, 0))],
        out_specs=[ospec, ospec],
        out_shape=[out, out],
        compiler_params=_params(2),
        name="memory_kv",
    )(mem2d, g_mem, w_kv, g_k4)


def _mix_kernel(ya_ref, yb_ref, yc_ref, yd_ref, x_ref, gmo_ref, wmo_ref, gxa_ref, wq_ref, gxq_ref,
                kx_ref, vx_ref, wo_ref, gffn_ref, wrh_ref, wrl_ref, br_ref, ltri_ref, utri_ref,
                x2_ref, h3_ref, meta_ref, cnt_ref):
    tm = MIX_TILE
    gmo = gmo_ref[...]
    lane_lo = lax.broadcasted_iota(jnp.int32, (1, 128), 1) < 64

    def rows_to_logits(rs):
        parts = []
        for g, ref in enumerate((ya_ref, yb_ref, yc_ref, yd_ref)):
            yg = ref[rs, :].astype(_F32)
            parts.append((_rms_scale(yg, GROUP_W) * gmo[:, g * 256:(g + 1) * 256]).astype(_BF16))
        y = jnp.concatenate(parts, axis=1)
        x1 = x_ref[rs, :] + _dot(y, wmo_ref[...])

        h = (_rms_scale(x1, D_MODEL) * gxa_ref[...]).astype(_BF16)
        q = _dot(h, wq_ref[...])
        qn = jnp.concatenate([_pair_norm64(q[:, 0:128], lane_lo), _pair_norm64(q[:, 128:256], lane_lo)],
                             axis=1)
        qn = (qn * gxq_ref[...] * (XA_HEAD_DIM ** -0.5)).astype(_BF16)
        o = jnp.zeros((x1.shape[0], 256), _F32)
        for hd in range(XA_HEADS):
            s = _dot_nt(qn, kx_ref[hd])
            m = jnp.max(s, axis=-1, keepdims=True)
            p = jnp.exp(s - m)
            denom = jnp.sum(p, axis=-1, keepdims=True)
            o = o + _dot(p.astype(_BF16), vx_ref[hd]) / denom
        x2 = x1 + _dot(o.astype(_BF16), wo_ref[...])
        x2_ref[rs, :] = x2

        h3 = _rms_scale(x2, D_MODEL) * gffn_ref[...]
        h3_hi = h3.astype(_BF16)
        h3_lo = (h3 - h3_hi.astype(_F32)).astype(_BF16)
        h3_ref[rs, :] = h3_hi
        return (_dot(h3_hi, wrh_ref[...]) + (_dot(h3_hi, wrl_ref[...]) + _dot(h3_lo, wrh_ref[...]))
                + br_ref[...])

    all_logits = rows_to_logits(slice(0, tm))
    for st in range(tm // TOKEN_TILE):
        meta, counts = _route(all_logits[st * TOKEN_TILE:(st + 1) * TOKEN_TILE, :], ltri_ref, utri_ref)
        meta_ref[st * TOKEN_TILE:(st + 1) * TOKEN_TILE, :] = meta
        cnt_ref[st * 8:(st + 1) * 8, :] = jnp.broadcast_to(counts, (8, 128))


def _route(logits, ltri_ref, utri_ref):
    lane = lax.broadcasted_iota(jnp.int32, (TOKEN_TILE, 128), 1).astype(_F32)
    far = 1e9
    is_group = jnp.logical_and(lane >= N_EXPERTS, lane < N_EXPERTS + N_EXPERT_GROUPS)
    gl = jnp.where(is_group, logits, -jnp.inf)
    gmax = jnp.max(gl, axis=-1, keepdims=True)
    gidx = jnp.min(jnp.where(gl == gmax, lane, far), axis=-1, keepdims=True) - N_EXPERTS
    g_w = 1.0 / jnp.sum(jnp.where(is_group, jnp.exp(logits - gmax), 0.0), axis=-1, keepdims=True)
    in_group = jnp.logical_and(lane < N_EXPERTS, jnp.floor(lane * (1.0 / EXPERTS_PER_GROUP)) == gidx)
    el = jnp.where(in_group, logits, -jnp.inf)
    emax = jnp.max(el, axis=-1, keepdims=True)
    ep = jnp.where(in_group, jnp.exp(logits - emax), 0.0)
    prob = jnp.where(in_group, ep / jnp.sum(ep, axis=-1, keepdims=True), -1.0)
    p1 = jnp.max(prob, axis=-1, keepdims=True)
    e0 = jnp.min(jnp.where(prob == p1, lane, far), axis=-1, keepdims=True)
    prob2 = jnp.where(lane == e0, -1.0, prob)
    p2 = jnp.max(prob2, axis=-1, keepdims=True)
    e1 = jnp.min(jnp.where(prob2 == p2, lane, far), axis=-1, keepdims=True)
    w0 = g_w * (p1 / (p1 + p2))
    w1 = g_w * (p2 / (p1 + p2))

    onehot = jnp.where(jnp.logical_or(lane == e0, lane == e1), 1.0, 0.0)
    prefix = _dot(ltri_ref[...], onehot.astype(_BF16))
    counts = jnp.sum(onehot, axis=0, keepdims=True)
    units = jnp.floor((counts + (RUN_ALIGN - 1)) * (1.0 / RUN_ALIGN))
    offs = _dot(jnp.broadcast_to(units, (8, 128)).astype(_BF16), utri_ref[...])[0:1, :] * RUN_ALIGN
    where_to = prefix + offs
    pos0 = jnp.sum(jnp.where(lane == e0, where_to, 0.0), axis=-1, keepdims=True)
    pos1 = jnp.sum(jnp.where(lane == e1, where_to, 0.0), axis=-1, keepdims=True)

    meta = jnp.where(lane == 0, e0,
           jnp.where(lane == 1, e1,
           jnp.where(lane == 2, w0,
           jnp.where(lane == 3, w1,
           jnp.where(lane == 4, pos0,
           jnp.where(lane == 5, pos1, 0.0))))))
    return meta, counts


def _mix_call(ya, yb, yc, yd, x, lw, kx, vx, seq_len):
    n = x.shape[0]
    tm = MIX_TILE
    tiles_per_seq = seq_len // tm

    def full(a):
        nd = a.ndim
        return pl.BlockSpec(a.shape, lambda i, _nd=nd: (0,) * _nd)

    def rows(width):
        return pl.BlockSpec((tm, width), lambda i: (i, 0))

    kvspec = pl.BlockSpec((None, XA_HEADS, MEM_LEN, 256), lambda i: (i // tiles_per_seq, 0, 0, 0))
    consts_a = [lw["g_mo"], lw["w_mo"], lw["g_xa"], lw["xa_w_q"], lw["g_xq"]]
    consts_b = [lw["xa_w_o"], lw["g_ffn"], lw["w_router_hi"], lw["w_router_lo"], lw["b_router"],
                lw["ltri"], lw["utri"]]
    ntiles = n // tm
    return pl.pallas_call(
        _mix_kernel,
        grid=(ntiles,),
        in_specs=[rows(256)] * 4 + [rows(D_MODEL)] + [full(a) for a in consts_a]
                 + [kvspec, kvspec] + [full(a) for a in consts_b],
        out_specs=[rows(D_MODEL), rows(D_MODEL), rows(128),
                   pl.BlockSpec((8 * (tm // TOKEN_TILE), 128), lambda i: (i, 0))],
        out_shape=[jax.ShapeDtypeStruct((n, D_MODEL), _F32), jax.ShapeDtypeStruct((n, D_MODEL), _BF16),
                   jax.ShapeDtypeStruct((n, 128), _F32),
                   jax.ShapeDtypeStruct((n // TOKEN_TILE * 8, 128), _F32)],
        compiler_params=_params(1),
        name="mix_xattn_router",
    )(ya, yb, yc, yd, x, *consts_a, kx, vx, *consts_b)


def _move_runs(tab_ref, t, live, local_ref, remote_ref, sems, *, to_remote, wait):
    for bit in range(RUN_BITS):
        rows = RUN_ALIGN << bit
        base = (t * RUN_BITS + bit) * PIECE_COLS
        count = jnp.where(live, tab_ref[base], 0)

        def one(p, carry, base=base, rows=rows, bit=bit):
            loc = pl.multiple_of(tab_ref[base + 1 + p], RUN_ALIGN)
            rem = pl.multiple_of(tab_ref[base + 1 + N_EXPERTS + p], RUN_ALIGN)
            lsl = local_ref.at[pl.ds(loc, rows)]
            rsl = remote_ref.at[pl.ds(rem, rows)]
            src, dst = (lsl, rsl) if to_remote else (rsl, lsl)
            copy = pltpu.make_async_copy(src, dst, sems.at[bit, p])
            if wait:
                copy.wait()
            else:
                copy.start()
            return carry

        lax.fori_loop(0, count, one, 0)


def _selection(meta, first, rows):
    r = first + lax.broadcasted_iota(jnp.int32, (TOKEN_TILE, rows), 1)
    pos0 = meta[:, 4:5].astype(jnp.int32)
    pos1 = meta[:, 5:6].astype(jnp.int32)
    return jnp.where(r == pos0, 1.0, 0.0).astype(_BF16), jnp.where(r == pos1, 1.0, 0.0).astype(_BF16)


def _zero_rows(count, row_of, rows, zero_ref, xs_ref, sems):
    def copy(i):
        dst = xs_ref.at[pl.ds(pl.multiple_of(row_of(i), RUN_ALIGN), rows)]
        return pltpu.make_async_copy(zero_ref.at[pl.ds(0, rows)], dst, sems.at[i % ZERO_RING])

    def issue(i, carry):
        @pl.when(i >= ZERO_RING)
        def _():
            copy(i - ZERO_RING).wait()
        copy(i).start()
        return carry

    def drain(i, carry):
        copy(i).wait()
        return carry

    lax.fori_loop(0, count, issue, 0)
    lax.fori_loop(jnp.maximum(count - ZERO_RING, 0), count, drain, 0)


def _dispatch_kernel(tab_ref, used_ref, ztab_ref, meta_ref, h3_ref, xs_ref, buf0_ref, buf1_ref, zero_ref,
                     sems, zero_sems, free_sems, *, ntiles, n_blocks):
    t = pl.program_id(0)
    last = ntiles - 1
    bufs = (buf0_ref, buf1_ref)
    used = used_ref[jnp.minimum(t, last)]

    @pl.when(t == 0)
    def _():
        zero_ref[...] = jnp.zeros_like(zero_ref)
        for bit in range(TAIL_BITS):
            base = bit * (1 + N_EXPERTS)
            _zero_rows(ztab_ref[base], lambda i, base=base: ztab_ref[base + 1 + i], RUN_ALIGN << bit,
                       zero_ref, xs_ref, zero_sems.at[bit])

    live_blocks = ztab_ref[TAIL_BITS * (1 + N_EXPERTS)]
    per_step = -(-n_blocks // (ntiles + 1))

    def free_block_copy(step, j):
        blk = live_blocks + step * per_step + j
        dst = xs_ref.at[pl.ds(pl.multiple_of(blk * MOE_BLOCK, MOE_BLOCK), MOE_BLOCK)]
        return blk < n_blocks, pltpu.make_async_copy(zero_ref, dst, free_sems.at[step % 2, j])

    for j in range(per_step):
        exists, copy = free_block_copy(t - 1, j)
        pl.when(jnp.logical_and(t >= 1, exists))(copy.wait)
    for j in range(per_step):
        exists, copy = free_block_copy(t, j)
        pl.when(jnp.logical_and(t <= ntiles, exists))(copy.start)

    def step(slot):
        mine, other = bufs[slot], bufs[1 - slot]
        _move_runs(tab_ref, jnp.clip(t - 2, 0, last), t >= 2, mine, xs_ref, sems.at[slot],
                   to_remote=True, wait=True)
        _move_runs(tab_ref, jnp.clip(t - 1, 0, last), jnp.logical_and(t >= 1, t <= ntiles), other, xs_ref,
                   sems.at[1 - slot], to_remote=True, wait=False)

        meta = meta_ref[...]
        lane = lax.broadcasted_iota(jnp.int32, (TOKEN_TILE, 128), 1)
        cols = []
        for k in range(2):
            w = meta[:, 2 + k:3 + k]
            hi = w.astype(_BF16).astype(_F32)
            mid = (w - hi).astype(_BF16).astype(_F32)
            lo = (w - hi) - mid
            cols.append(jnp.where(lane == 0, hi, jnp.where(lane == 1, mid, jnp.where(lane == 2, lo, 0.0)))
                        .astype(_BF16))

        def sort_rows(first, rows):
            sel0, sel1 = _selection(meta, first, rows)
            tn = (((0,), (0,)), ((), ()))
            mine[first:first + rows, 0:D_MODEL] = lax.dot_general(
                sel0 + sel1, h3_ref[...], tn, preferred_element_type=_F32).astype(_BF16)
            mine[first:first + rows, D_MODEL:XS_COLS] = (
                lax.dot_general(sel0, cols[0], tn, preferred_element_type=_F32)
                + lax.dot_general(sel1, cols[1], tn, preferred_element_type=_F32)).astype(_BF16)

        sort_rows(0, 2 * TOKEN_TILE)
        for first in range(2 * TOKEN_TILE, SORT_ROWS, SORT_CHUNK):
            pl.when(used > first)(functools.partial(sort_rows, first, SORT_CHUNK))

    for slot in range(2):
        pl.when(t % 2 == slot)(functools.partial(step, slot))


def _dispatch_call(tab, used, ztab, meta, h3):
    n = h3.shape[0]
    tm = TOKEN_TILE
    ntiles = n // tm
    xs_rows = _xs_rows(n)
    tile = lambda i, tab, used, ztab: (jnp.minimum(i, ntiles - 1), 0)
    grid_spec = pltpu.PrefetchScalarGridSpec(
        num_scalar_prefetch=3,
        grid=(ntiles + 2,),
        in_specs=[pl.BlockSpec((tm, 128), tile),
                  pl.BlockSpec((tm, D_MODEL), tile)],
        out_specs=pl.BlockSpec(memory_space=pl.ANY),
        scratch_shapes=[pltpu.VMEM((SORT_ROWS, XS_COLS), _BF16), pltpu.VMEM((SORT_ROWS, XS_COLS), _BF16),
                        pltpu.VMEM((MOE_BLOCK, XS_COLS), _BF16),
                        pltpu.SemaphoreType.DMA((2, RUN_BITS, N_EXPERTS)),
                        pltpu.SemaphoreType.DMA((TAIL_BITS, ZERO_RING)),
                        pltpu.SemaphoreType.DMA((2, -(-(xs_rows // MOE_BLOCK) // (ntiles + 1))))],
    )
    return pl.pallas_call(
        functools.partial(_dispatch_kernel, ntiles=ntiles, n_blocks=xs_rows // MOE_BLOCK),
        grid_spec=grid_spec,
        out_shape=jax.ShapeDtypeStruct((xs_rows, XS_COLS), _BF16),
        compiler_params=_params(1),
        name="moe_dispatch",
    )(tab, used, ztab, meta, h3)


def _ffn_kernel(be_ref, nv_ref, xs_ref, wg_ref, wu_ref, wd_ref, ys_ref, wgu_ref, wdn_ref):
    i = pl.program_id(0)

    @pl.when(jnp.logical_or(i == 0, be_ref[i] != be_ref[jnp.maximum(i - 1, 0)]))
    def _():
        wgu_ref[:, 0:D_EXPERT] = wg_ref[...].astype(_BF16)
        wgu_ref[:, D_EXPERT:2 * D_EXPERT] = wu_ref[...].astype(_BF16)
        wdn_ref[...] = wd_ref[...].astype(_BF16)

    @pl.when(i < nv_ref[0])
    def _():
        xb = xs_ref[:, 0:D_MODEL]
        wcols = xs_ref[:, D_MODEL:XS_COLS].astype(_F32)
        wt = wcols[:, 0:1] + wcols[:, 1:2] + wcols[:, 2:3]
        gu = _dot(xb, wgu_ref[...])
        g = gu[:, 0:D_EXPERT]
        a = (g * jax.nn.sigmoid(g)) * gu[:, D_EXPERT:2 * D_EXPERT]
        y = _dot(a.astype(_BF16), wdn_ref[...])
        ys_ref[...] = (y * wt).astype(ys_ref.dtype)

    @pl.when(i >= nv_ref[0])
    def _():
        ys_ref[...] = jnp.zeros_like(ys_ref)


def _ffn_call(blk_expert, n_valid, xs, w_gate, w_up, w_down, layer):
    n_blocks = xs.shape[0] // MOE_BLOCK
    wspec = lambda shape: pl.BlockSpec((None, None) + shape, lambda i, be, nv: (layer, be[i], 0, 0))
    last_live = lambda i, be, nv: (jnp.minimum(i, nv[0] - 1), 0)
    grid_spec = pltpu.PrefetchScalarGridSpec(
        num_scalar_prefetch=2,
        grid=(n_blocks,),
        in_specs=[pl.BlockSpec((MOE_BLOCK, XS_COLS), last_live),
                  wspec((D_MODEL, D_EXPERT)), wspec((D_MODEL, D_EXPERT)), wspec((D_EXPERT, D_MODEL))],
        out_specs=pl.BlockSpec((MOE_BLOCK, D_MODEL), lambda i, be, nv: (i, 0)),
        scratch_shapes=[pltpu.VMEM((D_MODEL, 2 * D_EXPERT), _BF16), pltpu.VMEM((D_EXPERT, D_MODEL), _BF16)],
    )
    return pl.pallas_call(
        _ffn_kernel,
        grid_spec=grid_spec,
        out_shape=jax.ShapeDtypeStruct((xs.shape[0], D_MODEL), _BF16),
        compiler_params=_params(1),
        name="moe_experts",
    )(blk_expert, n_valid, xs, w_gate, w_up, w_down)


def _combine_kernel(tab_ref, used_ref, meta_ref, x2_ref, ys_ref, out_ref, buf0_ref, buf1_ref, sems,
                    *, ntiles):
    t = pl.program_id(0)
    last = ntiles - 1
    bufs = (buf0_ref, buf1_ref)
    used = used_ref[jnp.clip(t - 1, 0, last)]

    @pl.when(t == 0)
    def _():
        buf0_ref[...] = jnp.zeros_like(buf0_ref)
        buf1_ref[...] = jnp.zeros_like(buf1_ref)

    def step(slot):
        mine, other = bufs[slot], bufs[1 - slot]
        _move_runs(tab_ref, jnp.clip(t - 1, 0, last), t >= 1, other, ys_ref, sems.at[1 - slot],
                   to_remote=False, wait=True)
        _move_runs(tab_ref, jnp.minimum(t, last), t <= last, mine, ys_ref, sems.at[slot],
                   to_remote=False, wait=False)
        meta = meta_ref[...]

        def gathered(first, rows):
            sel0, sel1 = _selection(meta, first, rows)
            return _dot(sel0 + sel1, other[first:first + rows, :])

        out_ref[...] = x2_ref[...] + gathered(0, 2 * TOKEN_TILE)
        for first in range(2 * TOKEN_TILE, SORT_ROWS, SORT_CHUNK):
            @pl.when(used > first)
            def _(first=first):
                out_ref[...] += gathered(first, SORT_CHUNK)

    for slot in range(2):
        pl.when(t % 2 == slot)(functools.partial(step, slot))


def _combine_call(tab, used, meta, x2, ys):
    n = x2.shape[0]
    tm = TOKEN_TILE
    ntiles = n // tm
    tile = lambda i, tab, used: (jnp.maximum(i - 1, 0), 0)
    grid_spec = pltpu.PrefetchScalarGridSpec(
        num_scalar_prefetch=2,
        grid=(ntiles + 1,),
        in_specs=[pl.BlockSpec((tm, 128), tile),
                  pl.BlockSpec((tm, D_MODEL), tile),
                  pl.BlockSpec(memory_space=pl.ANY)],
        out_specs=pl.BlockSpec((tm, D_MODEL), tile),
        scratch_shapes=[pltpu.VMEM((SORT_ROWS, D_MODEL), _BF16), pltpu.VMEM((SORT_ROWS, D_MODEL), _BF16),
                        pltpu.SemaphoreType.DMA((2, RUN_BITS, N_EXPERTS))],
    )
    return pl.pallas_call(
        functools.partial(_combine_kernel, ntiles=ntiles),
        grid_spec=grid_spec,
        out_shape=jax.ShapeDtypeStruct((n, D_MODEL), _F32),
        compiler_params=_params(1),
        name="moe_combine",
    )(tab, used, meta, x2, ys)


def _moe_tables(cnt, n_tokens):
    ntiles = n_tokens // TOKEN_TILE
    counts = cnt.reshape(ntiles, 8, 128)[:, 0, :N_EXPERTS].astype(jnp.int32)
    units = (counts + RUN_ALIGN - 1) // RUN_ALIGN
    padded = units * RUN_ALIGN
    local = jnp.cumsum(padded, axis=1) - padded
    total = jnp.sum(padded, axis=0)
    total_blk = ((total + MOE_BLOCK - 1) // MOE_BLOCK) * MOE_BLOCK
    ends = jnp.cumsum(total_blk)
    starts = ends - total_blk
    remote = starts[None, :] + jnp.cumsum(padded, axis=0) - padded
    bits = jnp.arange(RUN_BITS, dtype=jnp.int32)[None, :, None]
    has = (units[:, None, :] >> bits) & 1
    done = (units[:, None, :] & ((1 << bits) - 1)) * RUN_ALIGN
    rank = jnp.cumsum(has, axis=2) - 1
    place = jnp.logical_and(rank[..., None] == jnp.arange(N_EXPERTS, dtype=jnp.int32), has[..., None] == 1)
    compact = lambda rows: jnp.sum(jnp.where(place, rows[..., None], 0), axis=2)
    tab = jnp.concatenate([jnp.sum(has, axis=2, keepdims=True), compact(local[:, None, :] + done),
                           compact(remote[:, None, :] + done)], axis=-1).reshape(-1).astype(jnp.int32)
    n_blocks = _xs_rows(n_tokens) // MOE_BLOCK
    blk_start = jnp.arange(n_blocks, dtype=jnp.int32) * MOE_BLOCK
    blk_expert = jnp.minimum(jnp.sum((ends[None, :] <= blk_start[:, None]).astype(jnp.int32), axis=1),
                             N_EXPERTS - 1)
    n_valid = (ends[-1] // MOE_BLOCK).astype(jnp.int32).reshape(1)
    used = jnp.sum(padded, axis=1).astype(jnp.int32)
    tail_units = (total_blk - total) // RUN_ALIGN
    tail_bits = jnp.arange(TAIL_BITS, dtype=jnp.int32)[:, None]
    tail_has = (tail_units[None, :] >> tail_bits) & 1
    tail_rows = starts + total + (tail_units[None, :] & ((1 << tail_bits) - 1)) * RUN_ALIGN
    tail_rank = jnp.cumsum(tail_has, axis=1) - 1
    tail_place = jnp.logical_and(tail_rank[..., None] == jnp.arange(N_EXPERTS, dtype=jnp.int32),
                                 tail_has[..., None] == 1)
    tail_list = jnp.sum(jnp.where(tail_place, tail_rows[..., None], 0), axis=1)
    ztab = jnp.concatenate([jnp.concatenate([jnp.sum(tail_has, axis=1, keepdims=True), tail_list],
                                            axis=1).reshape(-1), n_valid]).astype(jnp.int32)
    return tab, used, ztab, blk_expert, n_valid


def _xs_rows(n_tokens):
    ntiles = n_tokens // TOKEN_TILE
    worst = 2 * n_tokens + ntiles * N_EXPERTS * (RUN_ALIGN - 1) + N_EXPERTS * (MOE_BLOCK - 1)
    return ((worst + MOE_BLOCK - 1) // MOE_BLOCK) * MOE_BLOCK


def _pad_last(a, width):
    return jnp.pad(a, [(0, 0)] * (a.ndim - 1) + [(0, width - a.shape[-1])])


def _swap_mid_heads(a, axis):
    shape = a.shape
    a = a.reshape(shape[:axis] + (4, 64) + shape[axis + 1:])
    a = jnp.take(a, jnp.asarray([0, 2, 1, 3]), axis=axis)
    return a.reshape(shape)


def _layer_weights(l, p):
    w_in = p["w_in"][l]
    a, b = w_in[:, :A_COLS], w_in[:, A_COLS:A_COLS + B_COLS]
    c = w_in[:, A_COLS + B_COLS:A_COLS + B_COLS + C_COLS]
    d = w_in[:, A_COLS + B_COLS + C_COLS:]
    zeros = lambda w: jnp.zeros((D_MODEL, w), _F32)
    half = MLA_ROPE // 2
    k_rope = a[:, 384:416]
    k_rope_swap = jnp.concatenate([k_rope[:, half:], k_rope[:, :half]], axis=1)
    w_in_p = jnp.concatenate([
        a[:, :384], zeros(64), k_rope, zeros(32), zeros(64), k_rope_swap, zeros(32), b,
        _swap_mid_heads(c[:, :256], 1), c[:, 256:384], c[:, 384:512], d], axis=1).astype(_BF16)
    assert w_in_p.shape[1] == IN_COLS_PADDED

    w_uq = p["mla_w_uq"][l].reshape(MLA_Q_RANK, MLA_HEADS, MLA_QK_DIM)
    w_uq_swap = jnp.concatenate([jnp.zeros_like(w_uq[..., :MLA_NOPE]), w_uq[..., MLA_NOPE + half:],
                                 w_uq[..., MLA_NOPE:MLA_NOPE + half]], axis=-1)
    w_uq = jnp.concatenate([_pad_last(w_uq, HEAD_LANES).reshape(MLA_Q_RANK, -1),
                            _pad_last(w_uq_swap, HEAD_LANES).reshape(MLA_Q_RANK, -1)], axis=1).astype(_BF16)

    def swap_gain(g):
        return _pad_last(jnp.concatenate([jnp.zeros((MLA_NOPE,), _F32), g[MLA_NOPE + half:],
                                          g[MLA_NOPE:MLA_NOPE + half]]), HEAD_LANES)[None]
    w_ukv = p["mla_w_ukv"][l].reshape(MLA_KV_RANK, MLA_HEADS, MLA_NOPE + MLA_V)
    k_nope = _pad_last(w_ukv[..., :MLA_NOPE], HEAD_LANES).reshape(MLA_KV_RANK, MLA_HEADS * HEAD_LANES)
    v_part = w_ukv[..., MLA_NOPE:].reshape(MLA_KV_RANK, MLA_HEADS * MLA_V)
    w_ukv_p = jnp.concatenate([k_nope, v_part], axis=1).astype(_BF16)

    eye = jnp.eye(len(POOL_WINDOWS), dtype=_F32)
    w_pool = jnp.einsum("gcd,gh->gchd", p["pool_w"][l], eye).reshape(256, 256).astype(_BF16)

    g_mo = p["mix_out_norm"][l]
    g_mo = jnp.concatenate([g_mo[:512], _swap_mid_heads(g_mo[512:768], 0), g_mo[768:]])
    w_mo = p["w_mix_out"][l]
    w_mo = jnp.concatenate([w_mo[:512], _swap_mid_heads(w_mo[512:768], 0), w_mo[768:]], axis=0)

    w_router = _pad_last(jnp.concatenate([p["w_expert"][l], p["w_group"][l]], axis=1), 128)
    w_router_hi = w_router.astype(_BF16)
    w_router_lo = (w_router - w_router_hi.astype(_F32)).astype(_BF16)
    b_router = jnp.concatenate([p["b_expert"][l], p["b_group"][l]])
    tm = TOKEN_TILE
    ltri = (jnp.arange(tm)[None, :] < jnp.arange(tm)[:, None]).astype(_BF16)
    utri = (jnp.arange(128)[:, None] < jnp.arange(128)[None, :]).astype(_BF16)
    tile2 = lambda g: jnp.concatenate([g, g])[None]
    return dict(
        g_mix=p["norm_mix"][l][None], w_in=w_in_p,
        g_cq=p["mla_g_cq"][l][None], w_uq=w_uq, g_ckv=p["mla_g_ckv"][l][None], w_ukv=w_ukv_p,
        g_q=_pad_last(p["mla_g_q"][l], HEAD_LANES)[None], g_k=_pad_last(p["mla_g_k"][l], HEAD_LANES)[None],
        g_q_swap=swap_gain(p["mla_g_q"][l]), g_k_swap=swap_gain(p["mla_g_k"][l]),
        w_pool=w_pool, pool_scale=p["pool_scale"][l][None], conv_w=p["conv_w"][l],
        conv_b=p["conv_b"][l][None], g_sq=tile2(p["swa_g_q"][l]), g_sk=tile2(p["swa_g_k"][l]),
        sinks=jnp.take(p["swa_sinks"][l], jnp.asarray([0, 2, 1, 3])),
        g_mo=g_mo[None], w_mo=w_mo.astype(_BF16), g_xa=p["norm_xa"][l][None],
        xa_w_q=p["xa_w_q"][l].astype(_BF16), g_xq=jnp.tile(p["xa_g_q"][l], 4)[None],
        xa_w_o=p["xa_w_o"][l].astype(_BF16), g_ffn=p["norm_ffn"][l][None],
        w_router_hi=w_router_hi, w_router_lo=w_router_lo, b_router=_pad_last(b_router, 128)[None],
        ltri=ltri, utri=utri,
    )


def kernel(x, mem, positions, norm_mix, w_in, mla_g_cq, mla_w_uq, mla_g_ckv, mla_w_ukv, mla_g_q, mla_g_k, pool_w, pool_scale, swa_g_q, swa_g_k, swa_sinks, conv_w, conv_b, mix_out_norm, w_mix_out, norm_xa, norm_mem, xa_w_q, xa_w_kv, xa_g_q, xa_g_k, xa_w_o, norm_ffn, w_group, b_group, w_expert, b_expert, w_gate, w_up, w_down):
    p = dict(norm_mix=norm_mix, w_in=w_in, mla_g_cq=mla_g_cq, mla_w_uq=mla_w_uq, mla_g_ckv=mla_g_ckv,
             mla_w_ukv=mla_w_ukv, mla_g_q=mla_g_q, mla_g_k=mla_g_k, pool_w=pool_w, pool_scale=pool_scale,
             swa_g_q=swa_g_q, swa_g_k=swa_g_k, swa_sinks=swa_sinks, conv_w=conv_w, conv_b=conv_b,
             mix_out_norm=mix_out_norm, w_mix_out=w_mix_out, norm_xa=norm_xa, xa_w_q=xa_w_q,
             xa_g_q=xa_g_q, xa_w_o=xa_w_o, norm_ffn=norm_ffn, w_group=w_group, b_group=b_group,
             w_expert=w_expert, b_expert=b_expert)
    batch, seq_len, _ = x.shape
    depth = w_in.shape[0]
    n = batch * seq_len
    assert seq_len % MIX_TILE == 0 and MIX_TILE % TOKEN_TILE == 0
    assert seq_len % PROJ_TILE == 0 and seq_len % ATTN_Q_TILE == 0
    assert mem.shape[1] == MEM_LEN

    xf = x.reshape(n, D_MODEL)
    tabs = _rope_tables(positions)
    kx, vx = _memkv_call(mem.reshape(batch * MEM_LEN, D_MODEL), norm_mem[:, None, :],
                         xa_w_kv.astype(_BF16), jnp.tile(xa_g_k, (1, 4))[:, None, :], depth, batch)
    for l in range(depth):
        lw = _layer_weights(l, p)
        qm, km, vm, yb, yc, yd = _proj_call(xf, lw, tabs, seq_len)
        ya = _mla_attn_call(qm, km, vm, batch, seq_len)
        x2, h3, meta, cnt = _mix_call(ya, yb, yc, yd, xf, lw, kx[l], vx[l], seq_len)
        tab, used, ztab, blk_expert, n_valid = _moe_tables(cnt, n)
        xs = _dispatch_call(tab, used, ztab, meta, h3)
        ys = _ffn_call(blk_expert, n_valid, xs, w_gate, w_up, w_down, l)
        xf = _combine_call(tab, used, meta, x2, ys)
    return xf.reshape(batch, seq_len, D_MODEL)
```

```python
import functools

import jax
import jax.numpy as jnp
import numpy as np
from jax import lax
from jax.experimental import pallas as pl
from jax.experimental.pallas import tpu as pltpu

EPS = 1e-6
NEG_BIG = -1e30
LOG2_E = 1.4426950408889634
ROPE_THETA = 10000.0

D_MODEL = 1024
MEM_LEN = 256
GROUP_W = 256

MLA_HEADS = 4
MLA_Q_RANK = 256
MLA_KV_RANK = 128
MLA_NOPE = 64
MLA_ROPE = 32
MLA_QK_DIM = MLA_NOPE + MLA_ROPE
MLA_V = 64
HEAD_LANES = 128

POOL_WINDOWS = (2, 4, 8, 16)
POOL_GROUP = 64
POOL_HALO = 16

SWA_HEADS = 4
SWA_KV_HEADS = 2
SWA_HEAD_DIM = 64
SWA_WINDOW = 128

CONV_CH = 256

XA_HEADS = 4
XA_HEAD_DIM = 64

N_EXPERT_GROUPS = 4
EXPERTS_PER_GROUP = 8
N_EXPERTS = 32
D_EXPERT = 256
MOE_BLOCK = 1024

A_COLS = MLA_Q_RANK + MLA_KV_RANK + MLA_ROPE
B_COLS = GROUP_W
C_COLS = (SWA_HEADS + 2 * SWA_KV_HEADS) * SWA_HEAD_DIM
COL_CQ, COL_CKV, COL_KROPE, COL_KROPE_SWAP = 0, 256, 384, 512
COL_POOL, COL_SWA_Q, COL_SWA_K, COL_SWA_V, COL_CONV = 640, 896, 1152, 1280, 1408
IN_COLS_PADDED = COL_CONV + 3 * CONV_CH

TOKEN_TILE = 512
PROJ_TILE = 512
MIX_TILE = 1024
ATTN_TILE = 512
ATTN_Q_TILE = 1024
RUN_ALIGN = 16
RUN_BITS = 6
SORT_ROWS = 2 * TOKEN_TILE + N_EXPERTS * RUN_ALIGN
SORT_CHUNK = 256
SORT_ALWAYS = 2 * TOKEN_TILE + SORT_CHUNK
TAIL_BITS = (MOE_BLOCK // RUN_ALIGN).bit_length() - 1
ZERO_RING = 8
PIECE_COLS = 1 + 2 * N_EXPERTS
XS_COLS = D_MODEL + 128
VMEM_LIMIT = 56 * 1024 * 1024

_F32 = jnp.float32
_BF16 = jnp.bfloat16


def _params(n_axes):
    return pltpu.CompilerParams(dimension_semantics=("arbitrary",) * n_axes,
                                vmem_limit_bytes=VMEM_LIMIT)


def _dot(a, b):
    return jnp.dot(a, b, preferred_element_type=_F32)


def _dot_nt(a, b):
    return lax.dot_general(a, b, (((1,), (1,)), ((), ())), preferred_element_type=_F32)


def _rms_factor(x, width):
    ss = jnp.sum(x * x, axis=-1, keepdims=True)
    return lax.rsqrt(ss * (1.0 / width) + EPS)


def _rms_scale(x, width):
    return x * _rms_factor(x, width)


def _pair_norm64(x, lane_lo):
    x2 = x * x
    s_all = jnp.sum(x2, axis=-1, keepdims=True)
    s_lo = jnp.sum(jnp.where(lane_lo, x2, 0.0), axis=-1, keepdims=True)
    ss = jnp.where(lane_lo, s_lo, s_all - s_lo)
    return x * lax.rsqrt(ss * (1.0 / 64.0) + EPS)


def _rope_kernel(pos_ref, freq_ref, c_ref, s_ref):
    ang = pos_ref[...] * freq_ref[...]
    c_ref[...] = jnp.cos(ang)
    s_ref[...] = jnp.sin(ang)


def _rope_tables(positions):
    half = MLA_ROPE // 2
    per_row = HEAD_LANES // half
    n = positions.size
    inv_freq = ROPE_THETA ** (-jnp.arange(half, dtype=_F32) / half)
    pos = jnp.repeat(positions.astype(_F32).reshape(n // per_row, per_row), half, axis=1)
    rows = n // per_row
    tile = min(rows, 1024)
    tab = pl.BlockSpec((tile, HEAD_LANES), lambda i: (i, 0))
    shp = jax.ShapeDtypeStruct((rows, HEAD_LANES), _F32)
    cos, sin = pl.pallas_call(
        _rope_kernel,
        grid=(rows // tile,),
        in_specs=[tab, pl.BlockSpec((1, HEAD_LANES), lambda i: (0, 0))],
        out_specs=[tab, tab],
        out_shape=[shp, shp],
        compiler_params=_params(1),
        name="rope_tables",
    )(pos, jnp.tile(inv_freq, per_row)[None])
    return cos.reshape(n, half), sin.reshape(n, half)


def _rotary_lanes(cos, sin):
    rows = cos.shape[0]
    pad = jnp.zeros((rows, HEAD_LANES - MLA_QK_DIM), _F32)
    c = jnp.concatenate([jnp.ones((rows, MLA_NOPE), _F32), cos, cos, pad], axis=1)
    s = jnp.concatenate([jnp.zeros((rows, MLA_NOPE), _F32), -sin, sin, pad], axis=1)
    return c, s


def _swa_tile(q, k, v, k_before, v_before, sink_ref, first_of_sequence):
    w = SWA_WINDOW
    oldest = jnp.where(first_of_sequence, 0, -w)
    lane_lo = lax.broadcasted_iota(jnp.int32, (1, 128), 1) < 64
    qpos = lax.broadcasted_iota(jnp.int32, (w, 2 * w), 0)
    kpos = lax.broadcasted_iota(jnp.int32, (w, 2 * w), 1) - w
    band = jnp.logical_and(kpos <= qpos, kpos > qpos - w)

    out_rows = []
    for jb in range(q.shape[0] // w):
        rs = slice(jb * w, (jb + 1) * w)
        if jb == 0:
            kprev, vprev = k_before, v_before
            visible = jnp.logical_and(band, kpos >= oldest)
        else:
            ps = slice((jb - 1) * w, jb * w)
            kprev, vprev = k[ps, :], v[ps, :]
            visible = band
        kk = jnp.concatenate([kprev, k[rs, :]], axis=0)
        vv = jnp.concatenate([vprev, v[rs, :]], axis=0)
        zero = jnp.zeros_like(kk)
        k_half = (jnp.where(lane_lo, kk, zero), jnp.where(lane_lo, zero, kk))
        pair_out = []
        for blk in range(2):
            qblk = q[rs, blk * 128:(blk + 1) * 128]
            outs = []
            for half in range(2):
                sink = sink_ref[2 * blk + half]
                s = _dot_nt(qblk, k_half[half])
                s = jnp.where(visible, s, NEG_BIG)
                m = jnp.maximum(jnp.max(s, axis=-1, keepdims=True), sink)
                p = jnp.exp(s - m)
                denom = jnp.sum(p, axis=-1, keepdims=True) + jnp.exp(sink - m)
                outs.append(_dot(p.astype(_BF16), vv) / denom)
            pair_out.append(jnp.where(lane_lo, outs[0], outs[1]))
        out_rows.append(jnp.concatenate(pair_out, axis=1))
    return jnp.concatenate(out_rows, axis=0)


def _proj_kernel(x_ref, gmix_ref, win_ref, gcq_ref, wuq_ref, gckv_ref, wukv_ref, gq_ref, gqs_ref,
                 gk_ref, gks_ref, c_ref, s_ref, wpool_ref, pscale_ref, convw_ref, convb_ref, gsq_ref,
                 gsk_ref, sink_ref, qm_ref, km_ref, vm_ref, yb_ref, yc_ref, yd_ref, halo_ref, swa_halo_ref,
                 *, tiles_per_seq):
    tm = PROJ_TILE
    i = pl.program_id(0)
    seq_tile = i % tiles_per_seq

    @pl.when(i == 0)
    def _():
        halo_ref[...] = jnp.zeros_like(halo_ref)
        swa_halo_ref[...] = jnp.zeros_like(swa_halo_ref)

    x = x_ref[...]
    h = _rms_scale(x, D_MODEL) * gmix_ref[...]
    u = _dot(h.astype(_BF16), win_ref[...])

    c, s = _rotary_lanes(c_ref[...], s_ref[...])
    hw = MLA_HEADS * HEAD_LANES

    cq = _rms_scale(u[:, COL_CQ:COL_CQ + MLA_Q_RANK], MLA_Q_RANK) * gcq_ref[...]
    q = _dot(cq.astype(_BF16), wuq_ref[...])
    gq = (gq_ref[...] * (MLA_QK_DIM ** -0.5 * LOG2_E)) * c
    gqs = (gqs_ref[...] * (MLA_QK_DIM ** -0.5 * LOG2_E)) * s
    for hd in range(MLA_HEADS):
        sl = slice(hd * HEAD_LANES, (hd + 1) * HEAD_LANES)
        xq = q[:, sl]
        r = _rms_factor(xq, MLA_QK_DIM)
        qm_ref[:, sl] = ((xq * r) * gq + (q[:, hw + sl.start:hw + sl.stop] * r) * gqs).astype(_BF16)

    ckv = _rms_scale(u[:, COL_CKV:COL_CKV + MLA_KV_RANK], MLA_KV_RANK) * gckv_ref[...]
    kv = _dot(ckv.astype(_BF16), wukv_ref[...])
    krope = u[:, COL_KROPE:COL_KROPE + HEAD_LANES]
    krope_swap = u[:, COL_KROPE_SWAP:COL_KROPE_SWAP + HEAD_LANES]
    gk = gk_ref[...] * c
    gks = gks_ref[...] * s
    for hd in range(MLA_HEADS):
        sl = slice(hd * HEAD_LANES, (hd + 1) * HEAD_LANES)
        xk = kv[:, sl] + krope
        r = _rms_factor(xk, MLA_QK_DIM)
        km_ref[:, sl] = ((xk * r) * gk + (krope_swap * r) * gks).astype(_BF16)
    vm_ref[...] = kv[:, hw:hw + MLA_HEADS * MLA_V].astype(_BF16)

    lane_lo = lax.broadcasted_iota(jnp.int32, (1, HEAD_LANES), 1) < 64
    gsq = gsq_ref[...] * (SWA_HEAD_DIM ** -0.5)
    qs = jnp.concatenate(
        [(_pair_norm64(u[:, COL_SWA_Q + blk * 128:COL_SWA_Q + (blk + 1) * 128], lane_lo) * gsq).astype(_BF16)
         for blk in range(2)], axis=1)
    ks = (_pair_norm64(u[:, COL_SWA_K:COL_SWA_K + 128], lane_lo) * gsk_ref[...]).astype(_BF16)
    vs = u[:, COL_SWA_V:COL_SWA_V + 128].astype(_BF16)

    swa_halo = swa_halo_ref[...].astype(_BF16)
    yc_ref[...] = _swa_tile(qs, ks, vs, swa_halo[:, 0:128], swa_halo[:, 128:256], sink_ref,
                            seq_tile == 0).astype(_BF16)
    swa_halo_ref[:, 0:128] = ks[tm - SWA_WINDOW:, :].astype(_F32)
    swa_halo_ref[:, 128:256] = vs[tm - SWA_WINDOW:, :].astype(_F32)

    halo = jnp.where(jnp.broadcast_to(seq_tile, halo_ref.shape) == 0, 0.0, halo_ref[...])
    up = u[:, COL_POOL:COL_POOL + GROUP_W]
    ud = u[:, COL_CONV:COL_CONV + 3 * CONV_CH]

    b = jnp.concatenate([halo[:, 0:256], up], axis=0)
    w2 = b + pltpu.roll(b, 1, 0)
    w4 = w2 + pltpu.roll(w2, 2, 0)
    w8 = w4 + pltpu.roll(w4, 4, 0)
    w16 = w8 + pltpu.roll(w8, 8, 0)
    lane = lax.broadcasted_iota(jnp.int32, (1, 256), 1)
    win = jnp.where(lane < 64, w2, jnp.where(lane < 128, w4, jnp.where(lane < 192, w8, w16)))
    win = win[POOL_HALO:, :]
    width = jnp.where(lane < 64, 2, jnp.where(lane < 128, 4, jnp.where(lane < 192, 8, 16)))
    t = seq_tile * tm + lax.broadcasted_iota(jnp.int32, (tm, 1), 0)
    count = jnp.minimum(t + 1, width).astype(_F32)
    pooled = win / count - up
    yb_ref[...] = (_dot(pooled.astype(_BF16), wpool_ref[...]) * pscale_ref[...]).astype(_BF16)

    z = ud[:, 256:512] * ud[:, 512:768]
    zh = halo[:, 512:768] * halo[:, 768:1024]
    zb = jnp.concatenate([zh, z], axis=0)
    cw = convw_ref[...]
    conv = (pltpu.roll(zb, 2, 0)[POOL_HALO:, :] * cw[0:1, :]
            + pltpu.roll(zb, 1, 0)[POOL_HALO:, :] * cw[1:2, :]
            + z * cw[2:3, :])
    yd_ref[...] = (ud[:, 0:256] * (conv + convb_ref[...])).astype(_BF16)

    halo_ref[:, 0:256] = up[tm - POOL_HALO:, :]
    halo_ref[:, 256:1024] = ud[tm - POOL_HALO:, :]


def _proj_call(x, lw, tabs, seq_len):
    n = x.shape[0]
    tm = PROJ_TILE
    tiles_per_seq = seq_len // tm

    def full(a):
        nd = a.ndim
        return pl.BlockSpec(a.shape, lambda i, _nd=nd: (0,) * _nd)

    def rows(width):
        return pl.BlockSpec((tm, width), lambda i: (i, 0))

    consts = [lw["g_mix"], lw["w_in"], lw["g_cq"], lw["w_uq"], lw["g_ckv"], lw["w_ukv"],
              lw["g_q"], lw["g_q_swap"], lw["g_k"], lw["g_k_swap"]]
    consts2 = [lw["w_pool"], lw["pool_scale"], lw["conv_w"], lw["conv_b"], lw["g_sq"], lw["g_sk"]]
    out_widths = [512, 512, 256, 256, 256, 256]
    return pl.pallas_call(
        functools.partial(_proj_kernel, tiles_per_seq=tiles_per_seq),
        grid=(n // tm,),
        in_specs=[rows(D_MODEL)] + [full(a) for a in consts] + [rows(MLA_ROPE // 2)] * 2
                 + [full(a) for a in consts2] + [pl.BlockSpec(memory_space=pltpu.SMEM)],
        out_specs=[rows(w) for w in out_widths],
        out_shape=[jax.ShapeDtypeStruct((n, w), _BF16) for w in out_widths],
        scratch_shapes=[pltpu.VMEM((POOL_HALO, 1024), _F32), pltpu.VMEM((SWA_WINDOW, 256), _F32)],
        compiler_params=_params(1),
        name="proj_in",
    )(x, *consts, *tabs, *consts2, lw["sinks"])


def _mla_attn_kernel(q_ref, k_hbm, v_hbm, o_ref, kbuf, vbuf, sems, m_ref, l_ref, acc_ref, *, seq_len):
    tq, tk = ATTN_Q_TILE, ATTN_TILE
    assert tq == 2 * tk
    b = pl.program_id(0)
    qb = pl.program_id(1)
    row0 = b * seq_len

    def fetch(j, slot):
        rows = pl.ds(pl.multiple_of(row0 + j * tq, tq), tq)
        return (pltpu.make_async_copy(k_hbm.at[rows], kbuf.at[slot], sems.at[0, slot]),
                pltpu.make_async_copy(v_hbm.at[rows], vbuf.at[slot], sems.at[1, slot]))

    def start(j, slot):
        for cp in fetch(j, slot):
            cp.start()

    def wait(j, slot):
        for cp in fetch(j, slot):
            cp.wait()

    m_ref[...] = jnp.full_like(m_ref, NEG_BIG)
    l_ref[...] = jnp.zeros_like(l_ref)
    acc_ref[...] = jnp.zeros_like(acc_ref)
    lane_lo = lax.broadcasted_iota(jnp.int32, (1, HEAD_LANES), 1) < 64

    def accumulate(slot, half, rows, diagonal_from):
        nrows = rows.stop - rows.start
        keys = slice(half * tk, (half + 1) * tk)
        if diagonal_from is not None:
            visible = (lax.broadcasted_iota(jnp.int32, (nrows, tk), 1)
                       <= lax.broadcasted_iota(jnp.int32, (nrows, tk), 0) + (rows.start - diagonal_from))
        for pair in range(MLA_HEADS // 2):
            vblk = vbuf[slot, keys, pair * 128:(pair + 1) * 128]
            alphas = []
            pvs = []
            for sub in range(2):
                hd = 2 * pair + sub
                sl = slice(hd * HEAD_LANES, (hd + 1) * HEAD_LANES)
                s = _dot_nt(q_ref[rows, sl], kbuf[slot, keys, sl])
                if diagonal_from is not None:
                    s = jnp.where(visible, s, NEG_BIG)
                m_prev = m_ref[hd, rows, :]
                m_new = jnp.maximum(m_prev, jnp.max(s, axis=-1, keepdims=True))
                alpha = jnp.exp2(m_prev - m_new)
                p = jnp.exp2(s - jnp.tile(m_new, (1, tk // HEAD_LANES)))
                l_ref[hd, rows, :] = alpha * l_ref[hd, rows, :] + jnp.sum(p, axis=-1, keepdims=True)
                m_ref[hd, rows, :] = m_new
                alphas.append(alpha)
                pvs.append(_dot(p.astype(_BF16), vblk))
            psl = slice(pair * 128, (pair + 1) * 128)
            acc_ref[rows, psl] = (acc_ref[rows, psl] * jnp.where(lane_lo, alphas[0], alphas[1])
                                  + jnp.where(lane_lo, pvs[0], pvs[1]))

    everything = slice(0, tq)
    start(0, 0)

    def earlier_group(j, carry):
        slot = j % 2
        wait(j, slot)
        start(j + 1, 1 - slot)
        accumulate(slot, 0, everything, None)
        accumulate(slot, 1, everything, None)
        return carry

    lax.fori_loop(0, qb, earlier_group, 0)

    slot = qb % 2
    wait(qb, slot)
    accumulate(slot, 0, everything, 0)
    accumulate(slot, 1, slice(tk, tq), tk)

    for pair in range(MLA_HEADS // 2):
        psl = slice(pair * 128, (pair + 1) * 128)
        denom = jnp.where(lane_lo, l_ref[2 * pair], l_ref[2 * pair + 1])
        o_ref[:, psl] = (acc_ref[:, psl] / denom).astype(o_ref.dtype)


def _mla_attn_call(qm, km, vm, batch, seq_len):
    tq, tk = ATTN_Q_TILE, ATTN_TILE
    nq = seq_len // tq
    return pl.pallas_call(
        functools.partial(_mla_attn_kernel, seq_len=seq_len),
        grid=(batch, nq),
        in_specs=[pl.BlockSpec((tq, 512), lambda b, q: (b * nq + q, 0)),
                  pl.BlockSpec(memory_space=pl.ANY), pl.BlockSpec(memory_space=pl.ANY)],
        out_specs=pl.BlockSpec((tq, 256), lambda b, q: (b * nq + q, 0)),
        out_shape=jax.ShapeDtypeStruct((batch * seq_len, 256), _BF16),
        scratch_shapes=[pltpu.VMEM((2, tq, 512), _BF16), pltpu.VMEM((2, tq, 256), _BF16),
                        pltpu.SemaphoreType.DMA((2, 2)),
                        pltpu.VMEM((MLA_HEADS, tq, HEAD_LANES), _F32),
                        pltpu.VMEM((MLA_HEADS, tq, HEAD_LANES), _F32),
                        pltpu.VMEM((tq, 256), _F32)],
        compiler_params=_params(2),
        name="mla_attention",
    )(qm, km, vm)


def _memkv_kernel(mem_ref, gmem_ref, wkv_ref, gk_ref, k_ref, v_ref):
    m = _rms_scale(mem_ref[...], D_MODEL) * gmem_ref[...]
    kv = _dot(m.astype(_BF16), wkv_ref[...])
    lane_lo = lax.broadcasted_iota(jnp.int32, (1, 128), 1) < 64
    lane = lax.broadcasted_iota(jnp.int32, (1, 256), 1)
    k = jnp.concatenate([_pair_norm64(kv[:, 0:128], lane_lo), _pair_norm64(kv[:, 128:256], lane_lo)],
                        axis=1) * gk_ref[...]
    v = kv[:, 256:512]
    for hd in range(XA_HEADS):
        own = jnp.logical_and(lane >= hd * XA_HEAD_DIM, lane < (hd + 1) * XA_HEAD_DIM)
        k_ref[hd] = jnp.where(own, k, 0.0).astype(_BF16)
        v_ref[hd] = jnp.where(own, v, 0.0).astype(_BF16)


def _memkv_call(mem2d, g_mem, w_kv, g_k4, depth, batch):
    out = jax.ShapeDtypeStruct((depth, batch, XA_HEADS, MEM_LEN, 256), _BF16)
    ospec = pl.BlockSpec((None, None, XA_HEADS, MEM_LEN, 256), lambda l, b: (l, b, 0, 0, 0))
    return pl.pallas_call(
        _memkv_kernel,
        grid=(depth, batch),
        in_specs=[pl.BlockSpec((MEM_LEN, D_MODEL), lambda l, b: (b, 0)),
                  pl.BlockSpec((None, 1, D_MODEL), lambda l, b: (l, 0, 0)),
                  pl.BlockSpec((None, D_MODEL, 512), lambda l, b: (l, 0, 0)),
                  pl.BlockSpec((None, 1, 256), lambda l, b: (l, 0, 0))],
        out_specs=[ospec, ospec],
        out_shape=[out, out],
        compiler_params=_params(2),
        name="memory_kv",
    )(mem2d, g_mem, w_kv, g_k4)


def _mix_kernel(ya_ref, yb_ref, yc_ref, yd_ref, x_ref, gmo_ref, wmo_ref, gxa_ref, wq_ref, gxq_ref,
                kx_ref, vx_ref, wo_ref, gffn_ref, wrh_ref, wrl_ref, br_ref, ltri_ref, utri_ref,
                x2_ref, h3_ref, meta_ref, cnt_ref):
    tm = MIX_TILE
    gmo = gmo_ref[...]
    lane_lo = lax.broadcasted_iota(jnp.int32, (1, 128), 1) < 64

    def rows_to_logits(rs):
        parts = []
        for g, ref in enumerate((ya_ref, yb_ref, yc_ref, yd_ref)):
            yg = ref[rs, :].astype(_F32)
            parts.append((_rms_scale(yg, GROUP_W) * gmo[:, g * 256:(g + 1) * 256]).astype(_BF16))
        y = jnp.concatenate(parts, axis=1)
        x1 = x_ref[rs, :] + _dot(y, wmo_ref[...])

        h = (_rms_scale(x1, D_MODEL) * gxa_ref[...]).astype(_BF16)
        q = _dot(h, wq_ref[...])
        qn = jnp.concatenate([_pair_norm64(q[:, 0:128], lane_lo), _pair_norm64(q[:, 128:256], lane_lo)],
                             axis=1)
        qn = (qn * gxq_ref[...] * (XA_HEAD_DIM ** -0.5)).astype(_BF16)
        o = jnp.zeros((x1.shape[0], 256), _F32)
        for hd in range(XA_HEADS):
            s = _dot_nt(qn, kx_ref[hd])
            m = jnp.max(s, axis=-1, keepdims=True)
            p = jnp.exp(s - m)
            denom = jnp.sum(p, axis=-1, keepdims=True)
            o = o + _dot(p.astype(_BF16), vx_ref[hd]) / denom
        x2 = x1 + _dot(o.astype(_BF16), wo_ref[...])
        x2_ref[rs, :] = x2

        h3 = _rms_scale(x2, D_MODEL) * gffn_ref[...]
        h3_hi = h3.astype(_BF16)
        h3_lo = (h3 - h3_hi.astype(_F32)).astype(_BF16)
        h3_ref[rs, :] = h3_hi
        return (_dot(h3_hi, wrh_ref[...]) + (_dot(h3_hi, wrl_ref[...]) + _dot(h3_lo, wrh_ref[...]))
                + br_ref[...])

    all_logits = rows_to_logits(slice(0, tm))
    for st in range(tm // TOKEN_TILE):
        meta, counts = _route(all_logits[st * TOKEN_TILE:(st + 1) * TOKEN_TILE, :], ltri_ref, utri_ref)
        meta_ref[st * TOKEN_TILE:(st + 1) * TOKEN_TILE, :] = meta
        cnt_ref[st * 8:(st + 1) * 8, :] = jnp.broadcast_to(counts, (8, 128))


def _route(logits, ltri_ref, utri_ref):
    lane = lax.broadcasted_iota(jnp.int32, (TOKEN_TILE, 128), 1).astype(_F32)
    far = 1e9
    is_group = jnp.logical_and(lane >= N_EXPERTS, lane < N_EXPERTS + N_EXPERT_GROUPS)
    gl = jnp.where(is_group, logits, -jnp.inf)
    gmax = jnp.max(gl, axis=-1, keepdims=True)
    gidx = jnp.min(jnp.where(gl == gmax, lane, far), axis=-1, keepdims=True) - N_EXPERTS
    g_w = 1.0 / jnp.sum(jnp.where(is_group, jnp.exp(logits - gmax), 0.0), axis=-1, keepdims=True)
    in_group = jnp.logical_and(lane < N_EXPERTS, jnp.floor(lane * (1.0 / EXPERTS_PER_GROUP)) == gidx)
    el = jnp.where(in_group, logits, -jnp.inf)
    emax = jnp.max(el, axis=-1, keepdims=True)
    ep = jnp.where(in_group, jnp.exp(logits - emax), 0.0)
    prob = jnp.where(in_group, ep / jnp.sum(ep, axis=-1, keepdims=True), -1.0)
    p1 = jnp.max(prob, axis=-1, keepdims=True)
    e0 = jnp.min(jnp.where(prob == p1, lane, far), axis=-1, keepdims=True)
    prob2 = jnp.where(lane == e0, -1.0, prob)
    p2 = jnp.max(prob2, axis=-1, keepdims=True)
    e1 = jnp.min(jnp.where(prob2 == p2, lane, far), axis=-1, keepdims=True)
    w0 = g_w * (p1 / (p1 + p2))
    w1 = g_w * (p2 / (p1 + p2))

    onehot = jnp.where(jnp.logical_or(lane == e0, lane == e1), 1.0, 0.0)
    prefix = _dot(ltri_ref[...], onehot.astype(_BF16))
    counts = jnp.sum(onehot, axis=0, keepdims=True)
    units = jnp.floor((counts + (RUN_ALIGN - 1)) * (1.0 / RUN_ALIGN))
    offs = _dot(jnp.broadcast_to(units, (8, 128)).astype(_BF16), utri_ref[...])[0:1, :] * RUN_ALIGN
    where_to = prefix + offs
    pos0 = jnp.sum(jnp.where(lane == e0, where_to, 0.0), axis=-1, keepdims=True)
    pos1 = jnp.sum(jnp.where(lane == e1, where_to, 0.0), axis=-1, keepdims=True)

    meta = jnp.where(lane == 0, e0,
           jnp.where(lane == 1, e1,
           jnp.where(lane == 2, w0,
           jnp.where(lane == 3, w1,
           jnp.where(lane == 4, pos0,
           jnp.where(lane == 5, pos1, 0.0))))))
    return meta, counts


def _mix_call(ya, yb, yc, yd, x, lw, kx, vx, seq_len):
    n = x.shape[0]
    tm = MIX_TILE
    tiles_per_seq = seq_len // tm

    def full(a):
        nd = a.ndim
        return pl.BlockSpec(a.shape, lambda i, _nd=nd: (0,) * _nd)

    def rows(width):
        return pl.BlockSpec((tm, width), lambda i: (i, 0))

    kvspec = pl.BlockSpec((None, XA_HEADS, MEM_LEN, 256), lambda i: (i // tiles_per_seq, 0, 0, 0))
    consts_a = [lw["g_mo"], lw["w_mo"], lw["g_xa"], lw["xa_w_q"], lw["g_xq"]]
    consts_b = [lw["xa_w_o"], lw["g_ffn"], lw["w_router_hi"], lw["w_router_lo"], lw["b_router"],
                lw["ltri"], lw["utri"]]
    ntiles = n // tm
    return pl.pallas_call(
        _mix_kernel,
        grid=(ntiles,),
        in_specs=[rows(256)] * 4 + [rows(D_MODEL)] + [full(a) for a in consts_a]
                 + [kvspec, kvspec] + [full(a) for a in consts_b],
        out_specs=[rows(D_MODEL), rows(D_MODEL), rows(128),
                   pl.BlockSpec((8 * (tm // TOKEN_TILE), 128), lambda i: (i, 0))],
        out_shape=[jax.ShapeDtypeStruct((n, D_MODEL), _F32), jax.ShapeDtypeStruct((n, D_MODEL), _BF16),
                   jax.ShapeDtypeStruct((n, 128), _F32),
                   jax.ShapeDtypeStruct((n // TOKEN_TILE * 8, 128), _F32)],
        compiler_params=_params(1),
        name="mix_xattn_router",
    )(ya, yb, yc, yd, x, *consts_a, kx, vx, *consts_b)


def _move_runs(tab_ref, t, live, local_ref, remote_ref, sems, *, to_remote, wait):
    for bit in range(RUN_BITS):
        rows = RUN_ALIGN << bit
        base = (t * RUN_BITS + bit) * PIECE_COLS
        count = jnp.where(live, tab_ref[base], 0)

        def one(p, carry, base=base, rows=rows, bit=bit):
            loc = pl.multiple_of(tab_ref[base + 1 + p], RUN_ALIGN)
            rem = pl.multiple_of(tab_ref[base + 1 + N_EXPERTS + p], RUN_ALIGN)
            lsl = local_ref.at[pl.ds(loc, rows)]
            rsl = remote_ref.at[pl.ds(rem, rows)]
            src, dst = (lsl, rsl) if to_remote else (rsl, lsl)
            copy = pltpu.make_async_copy(src, dst, sems.at[bit, p])
            if wait:
                copy.wait()
            else:
                copy.start()
            return carry

        lax.fori_loop(0, count, one, 0)


def _selection(meta, first, rows):
    r = first + lax.broadcasted_iota(jnp.int32, (TOKEN_TILE, rows), 1)
    pos0 = meta[:, 4:5].astype(jnp.int32)
    pos1 = meta[:, 5:6].astype(jnp.int32)
    return jnp.where(r == pos0, 1.0, 0.0).astype(_BF16), jnp.where(r == pos1, 1.0, 0.0).astype(_BF16)


def _zero_rows(count, row_of, rows, zero_ref, xs_ref, sems):
    def copy(i):
        dst = xs_ref.at[pl.ds(pl.multiple_of(row_of(i), RUN_ALIGN), rows)]
        return pltpu.make_async_copy(zero_ref.at[pl.ds(0, rows)], dst, sems.at[i % ZERO_RING])

    def issue(i, carry):
        @pl.when(i >= ZERO_RING)
        def _():
            copy(i - ZERO_RING).wait()
        copy(i).start()
        return carry

    def drain(i, carry):
        copy(i).wait()
        return carry

    lax.fori_loop(0, count, issue, 0)
    lax.fori_loop(jnp.maximum(count - ZERO_RING, 0), count, drain, 0)


def _dispatch_kernel(tab_ref, used_ref, ztab_ref, meta_ref, h3_ref, xs_ref, buf0_ref, buf1_ref, zero_ref,
                     sems, zero_sems, free_sems, *, ntiles, n_blocks):
    t = pl.program_id(0)
    last = ntiles - 1
    bufs = (buf0_ref, buf1_ref)
    used = used_ref[jnp.minimum(t, last)]

    @pl.when(t == 0)
    def _():
        zero_ref[...] = jnp.zeros_like(zero_ref)
        for bit in range(TAIL_BITS):
            base = bit * (1 + N_EXPERTS)
            _zero_rows(ztab_ref[base], lambda i, base=base: ztab_ref[base + 1 + i], RUN_ALIGN << bit,
                       zero_ref, xs_ref, zero_sems.at[bit])

    live_blocks = ztab_ref[TAIL_BITS * (1 + N_EXPERTS)]
    per_step = -(-n_blocks // (ntiles + 1))

    def free_block_copy(step, j):
        blk = live_blocks + step * per_step + j
        dst = xs_ref.at[pl.ds(pl.multiple_of(blk * MOE_BLOCK, MOE_BLOCK), MOE_BLOCK)]
        return blk < n_blocks, pltpu.make_async_copy(zero_ref, dst, free_sems.at[step % 2, j])

    for j in range(per_step):
        exists, copy = free_block_copy(t - 1, j)
        pl.when(jnp.logical_and(t >= 1, exists))(copy.wait)
    for j in range(per_step):
        exists, copy = free_block_copy(t, j)
        pl.when(jnp.logical_and(t <= ntiles, exists))(copy.start)

    def step(slot):
        mine, other = bufs[slot], bufs[1 - slot]
        _move_runs(tab_ref, jnp.clip(t - 2, 0, last), t >= 2, mine, xs_ref, sems.at[slot],
                   to_remote=True, wait=True)
        _move_runs(tab_ref, jnp.clip(t - 1, 0, last), jnp.logical_and(t >= 1, t <= ntiles), other, xs_ref,
                   sems.at[1 - slot], to_remote=True, wait=False)

        meta = meta_ref[...]
        lane = lax.broadcasted_iota(jnp.int32, (TOKEN_TILE, 128), 1)
        cols = []
        for k in range(2):
            w = meta[:, 2 + k:3 + k]
            hi = w.astype(_BF16).astype(_F32)
            mid = (w - hi).astype(_BF16).astype(_F32)
            lo = (w - hi) - mid
            cols.append(jnp.where(lane == 0, hi, jnp.where(lane == 1, mid, jnp.where(lane == 2, lo, 0.0)))
                        .astype(_BF16))

        def sort_rows(first, rows):
            sel0, sel1 = _selection(meta, first, rows)
            tn = (((0,), (0,)), ((), ()))
            mine[first:first + rows, 0:D_MODEL] = lax.dot_general(
                sel0 + sel1, h3_ref[...], tn, preferred_element_type=_F32).astype(_BF16)
            mine[first:first + rows, D_MODEL:XS_COLS] = (
                lax.dot_general(sel0, cols[0], tn, preferred_element_type=_F32)
                + lax.dot_general(sel1, cols[1], tn, preferred_element_type=_F32)).astype(_BF16)

        sort_rows(0, SORT_ALWAYS)
        for first in range(SORT_ALWAYS, SORT_ROWS, SORT_CHUNK):
            pl.when(used > first)(functools.partial(sort_rows, first, SORT_CHUNK))

    for slot in range(2):
        pl.when(t % 2 == slot)(functools.partial(step, slot))


def _dispatch_call(tab, used, ztab, meta, h3):
    n = h3.shape[0]
    tm = TOKEN_TILE
    ntiles = n // tm
    xs_rows = _xs_rows(n)
    tile = lambda i, tab, used, ztab: (jnp.minimum(i, ntiles - 1), 0)
    grid_spec = pltpu.PrefetchScalarGridSpec(
        num_scalar_prefetch=3,
        grid=(ntiles + 2,),
        in_specs=[pl.BlockSpec((tm, 128), tile),
                  pl.BlockSpec((tm, D_MODEL), tile)],
        out_specs=pl.BlockSpec(memory_space=pl.ANY),
        scratch_shapes=[pltpu.VMEM((SORT_ROWS, XS_COLS), _BF16), pltpu.VMEM((SORT_ROWS, XS_COLS), _BF16),
                        pltpu.VMEM((MOE_BLOCK, XS_COLS), _BF16),
                        pltpu.SemaphoreType.DMA((2, RUN_BITS, N_EXPERTS)),
                        pltpu.SemaphoreType.DMA((TAIL_BITS, ZERO_RING)),
                        pltpu.SemaphoreType.DMA((2, -(-(xs_rows // MOE_BLOCK) // (ntiles + 1))))],
    )
    return pl.pallas_call(
        functools.partial(_dispatch_kernel, ntiles=ntiles, n_blocks=xs_rows // MOE_BLOCK),
        grid_spec=grid_spec,
        out_shape=jax.ShapeDtypeStruct((xs_rows, XS_COLS), _BF16),
        compiler_params=_params(1),
        name="moe_dispatch",
    )(tab, used, ztab, meta, h3)


def _ffn_kernel(be_ref, nv_ref, xs_ref, wg_ref, wu_ref, wd_ref, ys_ref, wgu_ref, wdn_ref):
    i = pl.program_id(0)

    @pl.when(jnp.logical_or(i == 0, be_ref[i] != be_ref[jnp.maximum(i - 1, 0)]))
    def _():
        wgu_ref[:, 0:D_EXPERT] = wg_ref[...].astype(_BF16)
        wgu_ref[:, D_EXPERT:2 * D_EXPERT] = wu_ref[...].astype(_BF16)
        wdn_ref[...] = wd_ref[...].astype(_BF16)

    @pl.when(i < nv_ref[0])
    def _():
        xb = xs_ref[:, 0:D_MODEL]
        wcols = xs_ref[:, D_MODEL:XS_COLS].astype(_F32)
        wt = wcols[:, 0:1] + wcols[:, 1:2] + wcols[:, 2:3]
        gu = _dot(xb, wgu_ref[...])
        g = gu[:, 0:D_EXPERT]
        a = (g * jax.nn.sigmoid(g)) * gu[:, D_EXPERT:2 * D_EXPERT]
        y = _dot(a.astype(_BF16), wdn_ref[...])
        ys_ref[...] = (y * wt).astype(ys_ref.dtype)

    @pl.when(i >= nv_ref[0])
    def _():
        ys_ref[...] = jnp.zeros_like(ys_ref)


def _ffn_call(blk_expert, n_valid, xs, w_gate, w_up, w_down, layer):
    n_blocks = xs.shape[0] // MOE_BLOCK
    wspec = lambda shape: pl.BlockSpec((None, None) + shape, lambda i, be, nv: (layer, be[i], 0, 0))
    last_live = lambda i, be, nv: (jnp.minimum(i, nv[0] - 1), 0)
    grid_spec = pltpu.PrefetchScalarGridSpec(
        num_scalar_prefetch=2,
        grid=(n_blocks,),
        in_specs=[pl.BlockSpec((MOE_BLOCK, XS_COLS), last_live),
                  wspec((D_MODEL, D_EXPERT)), wspec((D_MODEL, D_EXPERT)), wspec((D_EXPERT, D_MODEL))],
        out_specs=pl.BlockSpec((MOE_BLOCK, D_MODEL), lambda i, be, nv: (i, 0)),
        scratch_shapes=[pltpu.VMEM((D_MODEL, 2 * D_EXPERT), _BF16), pltpu.VMEM((D_EXPERT, D_MODEL), _BF16)],
    )
    return pl.pallas_call(
        _ffn_kernel,
        grid_spec=grid_spec,
        out_shape=jax.ShapeDtypeStruct((xs.shape[0], D_MODEL), _BF16),
        compiler_params=_params(1),
        name="moe_experts",
    )(blk_expert, n_valid, xs, w_gate, w_up, w_down)


def _combine_kernel(tab_ref, used_ref, meta_ref, x2_ref, ys_ref, out_ref, buf0_ref, buf1_ref, sems,
                    *, ntiles):
    t = pl.program_id(0)
    last = ntiles - 1
    bufs = (buf0_ref, buf1_ref)
    used = used_ref[jnp.clip(t - 1, 0, last)]

    @pl.when(t == 0)
    def _():
        buf0_ref[...] = jnp.zeros_like(buf0_ref)
        buf1_ref[...] = jnp.zeros_like(buf1_ref)

    def step(slot):
        mine, other = bufs[slot], bufs[1 - slot]
        _move_runs(tab_ref, jnp.clip(t - 1, 0, last), t >= 1, other, ys_ref, sems.at[1 - slot],
                   to_remote=False, wait=True)
        _move_runs(tab_ref, jnp.minimum(t, last), t <= last, mine, ys_ref, sems.at[slot],
                   to_remote=False, wait=False)
        meta = meta_ref[...]

        def gathered(first, rows):
            sel0, sel1 = _selection(meta, first, rows)
            return _dot(sel0 + sel1, other[first:first + rows, :])

        out_ref[...] = x2_ref[...] + gathered(0, SORT_ALWAYS)
        for first in range(SORT_ALWAYS, SORT_ROWS, SORT_CHUNK):
            @pl.when(used > first)
            def _(first=first):
                out_ref[...] += gathered(first, SORT_CHUNK)

    for slot in range(2):
        pl.when(t % 2 == slot)(functools.partial(step, slot))


def _combine_call(tab, used, meta, x2, ys):
    n = x2.shape[0]
    tm = TOKEN_TILE
    ntiles = n // tm
    tile = lambda i, tab, used: (jnp.maximum(i - 1, 0), 0)
    grid_spec = pltpu.PrefetchScalarGridSpec(
        num_scalar_prefetch=2,
        grid=(ntiles + 1,),
        in_specs=[pl.BlockSpec((tm, 128), tile),
                  pl.BlockSpec((tm, D_MODEL), tile),
                  pl.BlockSpec(memory_space=pl.ANY)],
        out_specs=pl.BlockSpec((tm, D_MODEL), tile),
        scratch_shapes=[pltpu.VMEM((SORT_ROWS, D_MODEL), _BF16), pltpu.VMEM((SORT_ROWS, D_MODEL), _BF16),
                        pltpu.SemaphoreType.DMA((2, RUN_BITS, N_EXPERTS))],
    )
    return pl.pallas_call(
        functools.partial(_combine_kernel, ntiles=ntiles),
        grid_spec=grid_spec,
        out_shape=jax.ShapeDtypeStruct((n, D_MODEL), _F32),
        compiler_params=_params(1),
        name="moe_combine",
    )(tab, used, meta, x2, ys)


def _moe_tables(cnt, n_tokens):
    ntiles = n_tokens // TOKEN_TILE
    counts = cnt.reshape(ntiles, 8, 128)[:, 0, :N_EXPERTS].astype(jnp.int32)
    units = (counts + RUN_ALIGN - 1) // RUN_ALIGN
    padded = units * RUN_ALIGN
    local = jnp.cumsum(padded, axis=1) - padded
    total = jnp.sum(padded, axis=0)
    total_blk = ((total + MOE_BLOCK - 1) // MOE_BLOCK) * MOE_BLOCK
    ends = jnp.cumsum(total_blk)
    starts = ends - total_blk
    remote = starts[None, :] + jnp.cumsum(padded, axis=0) - padded
    bits = jnp.arange(RUN_BITS, dtype=jnp.int32)[None, :, None]
    has = (units[:, None, :] >> bits) & 1
    done = (units[:, None, :] & ((1 << bits) - 1)) * RUN_ALIGN
    rank = jnp.cumsum(has, axis=2) - 1
    place = jnp.logical_and(rank[..., None] == jnp.arange(N_EXPERTS, dtype=jnp.int32), has[..., None] == 1)
    compact = lambda rows: jnp.sum(jnp.where(place, rows[..., None], 0), axis=2)
    tab = jnp.concatenate([jnp.sum(has, axis=2, keepdims=True), compact(local[:, None, :] + done),
                           compact(remote[:, None, :] + done)], axis=-1).reshape(-1).astype(jnp.int32)
    n_blocks = _xs_rows(n_tokens) // MOE_BLOCK
    blk_start = jnp.arange(n_blocks, dtype=jnp.int32) * MOE_BLOCK
    blk_expert = jnp.minimum(jnp.sum((ends[None, :] <= blk_start[:, None]).astype(jnp.int32), axis=1),
                             N_EXPERTS - 1)
    n_valid = (ends[-1] // MOE_BLOCK).astype(jnp.int32).reshape(1)
    used = jnp.sum(padded, axis=1).astype(jnp.int32)
    tail_units = (total_blk - total) // RUN_ALIGN
    tail_bits = jnp.arange(TAIL_BITS, dtype=jnp.int32)[:, None]
    tail_has = (tail_units[None, :] >> tail_bits) & 1
    tail_rows = starts + total + (tail_units[None, :] & ((1 << tail_bits) - 1)) * RUN_ALIGN
    tail_rank = jnp.cumsum(tail_has, axis=1) - 1
    tail_place = jnp.logical_and(tail_rank[..., None] == jnp.arange(N_EXPERTS, dtype=jnp.int32),
                                 tail_has[..., None] == 1)
    tail_list = jnp.sum(jnp.where(tail_place, tail_rows[..., None], 0), axis=1)
    ztab = jnp.concatenate([jnp.concatenate([jnp.sum(tail_has, axis=1, keepdims=True), tail_list],
                                            axis=1).reshape(-1), n_valid]).astype(jnp.int32)
    return tab, used, ztab, blk_expert, n_valid


def _xs_rows(n_tokens):
    ntiles = n_tokens // TOKEN_TILE
    worst = 2 * n_tokens + ntiles * N_EXPERTS * (RUN_ALIGN - 1) + N_EXPERTS * (MOE_BLOCK - 1)
    return ((worst + MOE_BLOCK - 1) // MOE_BLOCK) * MOE_BLOCK


def _pad_last(a, width):
    return jnp.pad(a, [(0, 0)] * (a.ndim - 1) + [(0, width - a.shape[-1])])


def _swap_mid_heads(a, axis):
    shape = a.shape
    a = a.reshape(shape[:axis] + (4, 64) + shape[axis + 1:])
    a = jnp.take(a, jnp.asarray([0, 2, 1, 3]), axis=axis)
    return a.reshape(shape)


def _layer_weights(l, p):
    w_in = p["w_in"][l]
    a, b = w_in[:, :A_COLS], w_in[:, A_COLS:A_COLS + B_COLS]
    c = w_in[:, A_COLS + B_COLS:A_COLS + B_COLS + C_COLS]
    d = w_in[:, A_COLS + B_COLS + C_COLS:]
    zeros = lambda w: jnp.zeros((D_MODEL, w), _F32)
    half = MLA_ROPE // 2
    k_rope = a[:, 384:416]
    k_rope_swap = jnp.concatenate([k_rope[:, half:], k_rope[:, :half]], axis=1)
    w_in_p = jnp.concatenate([
        a[:, :384], zeros(64), k_rope, zeros(32), zeros(64), k_rope_swap, zeros(32), b,
        _swap_mid_heads(c[:, :256], 1), c[:, 256:384], c[:, 384:512], d], axis=1).astype(_BF16)
    assert w_in_p.shape[1] == IN_COLS_PADDED

    w_uq = p["mla_w_uq"][l].reshape(MLA_Q_RANK, MLA_HEADS, MLA_QK_DIM)
    w_uq_swap = jnp.concatenate([jnp.zeros_like(w_uq[..., :MLA_NOPE]), w_uq[..., MLA_NOPE + half:],
                                 w_uq[..., MLA_NOPE:MLA_NOPE + half]], axis=-1)
    w_uq = jnp.concatenate([_pad_last(w_uq, HEAD_LANES).reshape(MLA_Q_RANK, -1),
                            _pad_last(w_uq_swap, HEAD_LANES).reshape(MLA_Q_RANK, -1)], axis=1).astype(_BF16)

    def swap_gain(g):
        return _pad_last(jnp.concatenate([jnp.zeros((MLA_NOPE,), _F32), g[MLA_NOPE + half:],
                                          g[MLA_NOPE:MLA_NOPE + half]]), HEAD_LANES)[None]
    w_ukv = p["mla_w_ukv"][l].reshape(MLA_KV_RANK, MLA_HEADS, MLA_NOPE + MLA_V)
    k_nope = _pad_last(w_ukv[..., :MLA_NOPE], HEAD_LANES).reshape(MLA_KV_RANK, MLA_HEADS * HEAD_LANES)
    v_part = w_ukv[..., MLA_NOPE:].reshape(MLA_KV_RANK, MLA_HEADS * MLA_V)
    w_ukv_p = jnp.concatenate([k_nope, v_part], axis=1).astype(_BF16)

    eye = jnp.eye(len(POOL_WINDOWS), dtype=_F32)
    w_pool = jnp.einsum("gcd,gh->gchd", p["pool_w"][l], eye).reshape(256, 256).astype(_BF16)

    g_mo = p["mix_out_norm"][l]
    g_mo = jnp.concatenate([g_mo[:512], _swap_mid_heads(g_mo[512:768], 0), g_mo[768:]])
    w_mo = p["w_mix_out"][l]
    w_mo = jnp.concatenate([w_mo[:512], _swap_mid_heads(w_mo[512:768], 0), w_mo[768:]], axis=0)

    w_router = _pad_last(jnp.concatenate([p["w_expert"][l], p["w_group"][l]], axis=1), 128)
    w_router_hi = w_router.astype(_BF16)
    w_router_lo = (w_router - w_router_hi.astype(_F32)).astype(_BF16)
    b_router = jnp.concatenate([p["b_expert"][l], p["b_group"][l]])
    tm = TOKEN_TILE
    ltri = (jnp.arange(tm)[None, :] < jnp.arange(tm)[:, None]).astype(_BF16)
    utri = (jnp.arange(128)[:, None] < jnp.arange(128)[None, :]).astype(_BF16)
    tile2 = lambda g: jnp.concatenate([g, g])[None]
    return dict(
        g_mix=p["norm_mix"][l][None], w_in=w_in_p,
        g_cq=p["mla_g_cq"][l][None], w_uq=w_uq, g_ckv=p["mla_g_ckv"][l][None], w_ukv=w_ukv_p,
        g_q=_pad_last(p["mla_g_q"][l], HEAD_LANES)[None], g_k=_pad_last(p["mla_g_k"][l], HEAD_LANES)[None],
        g_q_swap=swap_gain(p["mla_g_q"][l]), g_k_swap=swap_gain(p["mla_g_k"][l]),
        w_pool=w_pool, pool_scale=p["pool_scale"][l][None], conv_w=p["conv_w"][l],
        conv_b=p["conv_b"][l][None], g_sq=tile2(p["swa_g_q"][l]), g_sk=tile2(p["swa_g_k"][l]),
        sinks=jnp.take(p["swa_sinks"][l], jnp.asarray([0, 2, 1, 3])),
        g_mo=g_mo[None], w_mo=w_mo.astype(_BF16), g_xa=p["norm_xa"][l][None],
        xa_w_q=p["xa_w_q"][l].astype(_BF16), g_xq=jnp.tile(p["xa_g_q"][l], 4)[None],
        xa_w_o=p["xa_w_o"][l].astype(_BF16), g_ffn=p["norm_ffn"][l][None],
        w_router_hi=w_router_hi, w_router_lo=w_router_lo, b_router=_pad_last(b_router, 128)[None],
        ltri=ltri, utri=utri,
    )


def kernel(x, mem, positions, norm_mix, w_in, mla_g_cq, mla_w_uq, mla_g_ckv, mla_w_ukv, mla_g_q, mla_g_k, pool_w, pool_scale, swa_g_q, swa_g_k, swa_sinks, conv_w, conv_b, mix_out_norm, w_mix_out, norm_xa, norm_mem, xa_w_q, xa_w_kv, xa_g_q, xa_g_k, xa_w_o, norm_ffn, w_group, b_group, w_expert, b_expert, w_gate, w_up, w_down):
    p = dict(norm_mix=norm_mix, w_in=w_in, mla_g_cq=mla_g_cq, mla_w_uq=mla_w_uq, mla_g_ckv=mla_g_ckv,
             mla_w_ukv=mla_w_ukv, mla_g_q=mla_g_q, mla_g_k=mla_g_k, pool_w=pool_w, pool_scale=pool_scale,
             swa_g_q=swa_g_q, swa_g_k=swa_g_k, swa_sinks=swa_sinks, conv_w=conv_w, conv_b=conv_b,
             mix_out_norm=mix_out_norm, w_mix_out=w_mix_out, norm_xa=norm_xa, xa_w_q=xa_w_q,
             xa_g_q=xa_g_q, xa_w_o=xa_w_o, norm_ffn=norm_ffn, w_group=w_group, b_group=b_group,
             w_expert=w_expert, b_expert=b_expert)
    batch, seq_len, _ = x.shape
    depth = w_in.shape[0]
    n = batch * seq_len
    assert seq_len % MIX_TILE == 0 and MIX_TILE % TOKEN_TILE == 0
    assert seq_len % PROJ_TILE == 0 and seq_len % ATTN_Q_TILE == 0
    assert mem.shape[1] == MEM_LEN

    xf = x.reshape(n, D_MODEL)
    tabs = _rope_tables(positions)
    kx, vx = _memkv_call(mem.reshape(batch * MEM_LEN, D_MODEL), norm_mem[:, None, :],
                         xa_w_kv.astype(_BF16), jnp.tile(xa_g_k, (1, 4))[:, None, :], depth, batch)
    for l in range(depth):
        lw = _layer_weights(l, p)
        qm, km, vm, yb, yc, yd = _proj_call(xf, lw, tabs, seq_len)
        ya = _mla_attn_call(qm, km, vm, batch, seq_len)
        x2, h3, meta, cnt = _mix_call(ya, yb, yc, yd, xf, lw, kx[l], vx[l], seq_len)
        tab, used, ztab, blk_expert, n_valid = _moe_tables(cnt, n)
        xs = _dispatch_call(tab, used, ztab, meta, h3)
        ys = _ffn_call(blk_expert, n_valid, xs, w_gate, w_up, w_down, l)
        xf = _combine_call(tab, used, meta, x2, ys)
    return xf.reshape(batch, seq_len, D_MODEL)
```

```python
import functools

import jax
import jax.numpy as jnp
from jax import lax
from jax.experimental import pallas as pl
from jax.experimental.pallas import tpu as pltpu

EPS = 1e-6
NEG_BIG = -1e30
LOG2_E = 1.4426950408889634
ROPE_THETA = 10000.0

D_MODEL = 1024
MEM_LEN = 256
GROUP_W = 256

MLA_HEADS = 4
MLA_Q_RANK = 256
MLA_KV_RANK = 128
MLA_NOPE = 64
MLA_ROPE = 32
MLA_QK_DIM = MLA_NOPE + MLA_ROPE
MLA_V = 64
HEAD_LANES = 128

POOL_WINDOWS = (2, 4, 8, 16)
POOL_HALO = 16

SWA_HEADS = 4
SWA_KV_HEADS = 2
SWA_HEAD_DIM = 64
SWA_WINDOW = 128

CONV_CH = 256

XA_HEADS = 4
XA_HEAD_DIM = 64

N_EXPERT_GROUPS = 4
EXPERTS_PER_GROUP = 8
N_EXPERTS = 32
D_EXPERT = 256
MOE_BLOCK = 1024

A_COLS = MLA_Q_RANK + MLA_KV_RANK + MLA_ROPE
B_COLS = GROUP_W
C_COLS = (SWA_HEADS + 2 * SWA_KV_HEADS) * SWA_HEAD_DIM
COL_CQ, COL_CKV, COL_KROPE, COL_KROPE_SWAP = 0, 256, 384, 512
COL_POOL, COL_SWA_Q, COL_SWA_K, COL_SWA_V, COL_CONV = 640, 896, 1152, 1280, 1408
IN_COLS_PADDED = COL_CONV + 3 * CONV_CH

TOKEN_TILE = 512
PROJ_TILE = 512
MIX_TILE = 1024
ATTN_TILE = 512
ATTN_Q_TILE = 1024
RUN_ALIGN = 16
RUN_BITS = 6
SORT_ROWS = 2 * TOKEN_TILE + N_EXPERTS * RUN_ALIGN
SORT_CHUNK = 256
SORT_ALWAYS = 2 * TOKEN_TILE + SORT_CHUNK
TAIL_BITS = (MOE_BLOCK // RUN_ALIGN).bit_length() - 1
ZERO_RING = 8
PIECE_COLS = 1 + 2 * N_EXPERTS
XS_COLS = D_MODEL + 128
VMEM_LIMIT = 56 * 1024 * 1024

_F32 = jnp.float32
_BF16 = jnp.bfloat16


def _params(n_axes):
    return pltpu.CompilerParams(dimension_semantics=("arbitrary",) * n_axes,
                                vmem_limit_bytes=VMEM_LIMIT)


def _dot(a, b):
    return jnp.dot(a, b, preferred_element_type=_F32)


def _dot_nt(a, b):
    return lax.dot_general(a, b, (((1,), (1,)), ((), ())), preferred_element_type=_F32)


def _rms_factor(x, width):
    ss = jnp.sum(x * x, axis=-1, keepdims=True)
    return lax.rsqrt(ss * (1.0 / width) + EPS)


def _rms_scale(x, width):
    return x * _rms_factor(x, width)


def _pair_norm64(x, lane_lo):
    x2 = x * x
    s_all = jnp.sum(x2, axis=-1, keepdims=True)
    s_lo = jnp.sum(jnp.where(lane_lo, x2, 0.0), axis=-1, keepdims=True)
    ss = jnp.where(lane_lo, s_lo, s_all - s_lo)
    return x * lax.rsqrt(ss * (1.0 / 64.0) + EPS)


def _rope_kernel(pos_ref, freq_ref, c_ref, s_ref):
    ang = pos_ref[...] * freq_ref[...]
    c_ref[...] = jnp.cos(ang)
    s_ref[...] = jnp.sin(ang)


def _rope_tables(positions):
    half = MLA_ROPE // 2
    per_row = HEAD_LANES // half
    n = positions.size
    inv_freq = ROPE_THETA ** (-jnp.arange(half, dtype=_F32) / half)
    pos = jnp.repeat(positions.astype(_F32).reshape(n // per_row, per_row), half, axis=1)
    rows = n // per_row
    tile = min(rows, 1024)
    tab = pl.BlockSpec((tile, HEAD_LANES), lambda i: (i, 0))
    shp = jax.ShapeDtypeStruct((rows, HEAD_LANES), _F32)
    cos, sin = pl.pallas_call(
        _rope_kernel,
        grid=(rows // tile,),
        in_specs=[tab, pl.BlockSpec((1, HEAD_LANES), lambda i: (0, 0))],
        out_specs=[tab, tab],
        out_shape=[shp, shp],
        compiler_params=_params(1),
        name="rope_tables",
    )(pos, jnp.tile(inv_freq, per_row)[None])
    return cos.reshape(n, half), sin.reshape(n, half)


def _rotary_lanes(cos, sin):
    rows = cos.shape[0]
    pad = jnp.zeros((rows, HEAD_LANES - MLA_QK_DIM), _F32)
    c = jnp.concatenate([jnp.ones((rows, MLA_NOPE), _F32), cos, cos, pad], axis=1)
    s = jnp.concatenate([jnp.zeros((rows, MLA_NOPE), _F32), -sin, sin, pad], axis=1)
    return c, s


def _swa_tile(q, k, v, k_before, v_before, sink_ref, first_of_sequence):
    w = SWA_WINDOW
    oldest = jnp.where(first_of_sequence, 0, -w)
    lane_lo = lax.broadcasted_iota(jnp.int32, (1, 128), 1) < 64
    qpos = lax.broadcasted_iota(jnp.int32, (w, 2 * w), 0)
    kpos = lax.broadcasted_iota(jnp.int32, (w, 2 * w), 1) - w
    band = jnp.logical_and(kpos <= qpos, kpos > qpos - w)

    out_rows = []
    for jb in range(q.shape[0] // w):
        rs = slice(jb * w, (jb + 1) * w)
        if jb == 0:
            kprev, vprev = k_before, v_before
            visible = jnp.logical_and(band, kpos >= oldest)
        else:
            ps = slice((jb - 1) * w, jb * w)
            kprev, vprev = k[ps, :], v[ps, :]
            visible = band
        kk = jnp.concatenate([kprev, k[rs, :]], axis=0)
        vv = jnp.concatenate([vprev, v[rs, :]], axis=0)
        zero = jnp.zeros_like(kk)
        k_half = (jnp.where(lane_lo, kk, zero), jnp.where(lane_lo, zero, kk))
        pair_out = []
        for blk in range(2):
            qblk = q[rs, blk * 128:(blk + 1) * 128]
            outs = []
            for half in range(2):
                sink = sink_ref[2 * blk + half]
                s = _dot_nt(qblk, k_half[half])
                s = jnp.where(visible, s, NEG_BIG)
                m = jnp.maximum(jnp.max(s, axis=-1, keepdims=True), sink)
                p = jnp.exp(s - m)
                denom = jnp.sum(p, axis=-1, keepdims=True) + jnp.exp(sink - m)
                outs.append(_dot(p.astype(_BF16), vv) / denom)
            pair_out.append(jnp.where(lane_lo, outs[0], outs[1]))
        out_rows.append(jnp.concatenate(pair_out, axis=1))
    return jnp.concatenate(out_rows, axis=0)


def _proj_kernel(x_ref, gmix_ref, win_ref, gcq_ref, wuq_ref, gckv_ref, wukv_ref, gq_ref, gqs_ref,
                 gk_ref, gks_ref, c_ref, s_ref, wpool_ref, pscale_ref, convw_ref, convb_ref, gsq_ref,
                 gsk_ref, sink_ref, qm_ref, km_ref, vm_ref, yb_ref, yc_ref, yd_ref, halo_ref, swa_halo_ref,
                 *, tiles_per_seq):
    tm = PROJ_TILE
    i = pl.program_id(0)
    seq_tile = i % tiles_per_seq

    @pl.when(i == 0)
    def _():
        halo_ref[...] = jnp.zeros_like(halo_ref)
        swa_halo_ref[...] = jnp.zeros_like(swa_halo_ref)

    x = x_ref[...]
    h = _rms_scale(x, D_MODEL) * gmix_ref[...]
    u = _dot(h.astype(_BF16), win_ref[...])

    c, s = _rotary_lanes(c_ref[...], s_ref[...])
    hw = MLA_HEADS * HEAD_LANES

    cq = _rms_scale(u[:, COL_CQ:COL_CQ + MLA_Q_RANK], MLA_Q_RANK) * gcq_ref[...]
    q = _dot(cq.astype(_BF16), wuq_ref[...])
    gq = (gq_ref[...] * (MLA_QK_DIM ** -0.5 * LOG2_E)) * c
    gqs = (gqs_ref[...] * (MLA_QK_DIM ** -0.5 * LOG2_E)) * s
    for hd in range(MLA_HEADS):
        sl = slice(hd * HEAD_LANES, (hd + 1) * HEAD_LANES)
        xq = q[:, sl]
        r = _rms_factor(xq, MLA_QK_DIM)
        qm_ref[:, sl] = ((xq * r) * gq + (q[:, hw + sl.start:hw + sl.stop] * r) * gqs).astype(_BF16)

    ckv = _rms_scale(u[:, COL_CKV:COL_CKV + MLA_KV_RANK], MLA_KV_RANK) * gckv_ref[...]
    kv = _dot(ckv.astype(_BF16), wukv_ref[...])
    krope = u[:, COL_KROPE:COL_KROPE + HEAD_LANES]
    krope_swap = u[:, COL_KROPE_SWAP:COL_KROPE_SWAP + HEAD_LANES]
    gk = gk_ref[...] * c
    gks = gks_ref[...] * s
    for hd in range(MLA_HEADS):
        sl = slice(hd * HEAD_LANES, (hd + 1) * HEAD_LANES)
        xk = kv[:, sl] + krope
        r = _rms_factor(xk, MLA_QK_DIM)
        km_ref[:, sl] = ((xk * r) * gk + (krope_swap * r) * gks).astype(_BF16)
    vm_ref[...] = kv[:, hw:hw + MLA_HEADS * MLA_V].astype(_BF16)

    lane_lo = lax.broadcasted_iota(jnp.int32, (1, HEAD_LANES), 1) < 64
    gsq = gsq_ref[...] * (SWA_HEAD_DIM ** -0.5)
    qs = jnp.concatenate(
        [(_pair_norm64(u[:, COL_SWA_Q + blk * 128:COL_SWA_Q + (blk + 1) * 128], lane_lo) * gsq).astype(_BF16)
         for blk in range(2)], axis=1)
    ks = (_pair_norm64(u[:, COL_SWA_K:COL_SWA_K + 128], lane_lo) * gsk_ref[...]).astype(_BF16)
    vs = u[:, COL_SWA_V:COL_SWA_V + 128].astype(_BF16)

    swa_halo = swa_halo_ref[...].astype(_BF16)
    yc_ref[...] = _swa_tile(qs, ks, vs, swa_halo[:, 0:128], swa_halo[:, 128:256], sink_ref,
                            seq_tile == 0).astype(_BF16)
    swa_halo_ref[:, 0:128] = ks[tm - SWA_WINDOW:, :].astype(_F32)
    swa_halo_ref[:, 128:256] = vs[tm - SWA_WINDOW:, :].astype(_F32)

    halo = jnp.where(jnp.broadcast_to(seq_tile, halo_ref.shape) == 0, 0.0, halo_ref[...])
    up = u[:, COL_POOL:COL_POOL + GROUP_W]
    ud = u[:, COL_CONV:COL_CONV + 3 * CONV_CH]

    b = jnp.concatenate([halo[:, 0:256], up], axis=0)
    w2 = b + pltpu.roll(b, 1, 0)
    w4 = w2 + pltpu.roll(w2, 2, 0)
    w8 = w4 + pltpu.roll(w4, 4, 0)
    w16 = w8 + pltpu.roll(w8, 8, 0)
    lane = lax.broadcasted_iota(jnp.int32, (1, 256), 1)
    win = jnp.where(lane < 64, w2, jnp.where(lane < 128, w4, jnp.where(lane < 192, w8, w16)))
    win = win[POOL_HALO:, :]
    width = jnp.where(lane < 64, 2, jnp.where(lane < 128, 4, jnp.where(lane < 192, 8, 16)))
    t = seq_tile * tm + lax.broadcasted_iota(jnp.int32, (tm, 1), 0)
    count = jnp.minimum(t + 1, width).astype(_F32)
    pooled = win / count - up
    yb_ref[...] = (_dot(pooled.astype(_BF16), wpool_ref[...]) * pscale_ref[...]).astype(_BF16)

    z = ud[:, 256:512] * ud[:, 512:768]
    zh = halo[:, 512:768] * halo[:, 768:1024]
    zb = jnp.concatenate([zh, z], axis=0)
    cw = convw_ref[...]
    conv = (pltpu.roll(zb, 2, 0)[POOL_HALO:, :] * cw[0:1, :]
            + pltpu.roll(zb, 1, 0)[POOL_HALO:, :] * cw[1:2, :]
            + z * cw[2:3, :])
    yd_ref[...] = (ud[:, 0:256] * (conv + convb_ref[...])).astype(_BF16)

    halo_ref[:, 0:256] = up[tm - POOL_HALO:, :]
    halo_ref[:, 256:1024] = ud[tm - POOL_HALO:, :]


def _proj_call(x, lw, tabs, seq_len):
    n = x.shape[0]
    tm = PROJ_TILE
    tiles_per_seq = seq_len // tm

    def full(a):
        nd = a.ndim
        return pl.BlockSpec(a.shape, lambda i, _nd=nd: (0,) * _nd)

    def rows(width):
        return pl.BlockSpec((tm, width), lambda i: (i, 0))

    consts = [lw["g_mix"], lw["w_in"], lw["g_cq"], lw["w_uq"], lw["g_ckv"], lw["w_ukv"],
              lw["g_q"], lw["g_q_swap"], lw["g_k"], lw["g_k_swap"]]
    consts2 = [lw["w_pool"], lw["pool_scale"], lw["conv_w"], lw["conv_b"], lw["g_sq"], lw["g_sk"]]
    out_widths = [512, 512, 256, 256, 256, 256]
    return pl.pallas_call(
        functools.partial(_proj_kernel, tiles_per_seq=tiles_per_seq),
        grid=(n // tm,),
        in_specs=[rows(D_MODEL)] + [full(a) for a in consts] + [rows(MLA_ROPE // 2)] * 2
                 + [full(a) for a in consts2] + [pl.BlockSpec(memory_space=pltpu.SMEM)],
        out_specs=[rows(w) for w in out_widths],
        out_shape=[jax.ShapeDtypeStruct((n, w), _BF16) for w in out_widths],
        scratch_shapes=[pltpu.VMEM((POOL_HALO, 1024), _F32), pltpu.VMEM((SWA_WINDOW, 256), _F32)],
        compiler_params=_params(1),
        name="proj_in",
    )(x, *consts, *tabs, *consts2, lw["sinks"])


def _mla_attn_kernel(q_ref, k_hbm, v_hbm, o_ref, kbuf, vbuf, sems, m_ref, l_ref, acc_ref, *, seq_len):
    tq, tk = ATTN_Q_TILE, ATTN_TILE
    assert tq == 2 * tk
    b = pl.program_id(0)
    qb = pl.program_id(1)
    row0 = b * seq_len

    def fetch(j, slot):
        rows = pl.ds(pl.multiple_of(row0 + j * tq, tq), tq)
        return (pltpu.make_async_copy(k_hbm.at[rows], kbuf.at[slot], sems.at[0, slot]),
                pltpu.make_async_copy(v_hbm.at[rows], vbuf.at[slot], sems.at[1, slot]))

    def start(j, slot):
        for cp in fetch(j, slot):
            cp.start()

    def wait(j, slot):
        for cp in fetch(j, slot):
            cp.wait()

    m_ref[...] = jnp.full_like(m_ref, NEG_BIG)
    l_ref[...] = jnp.zeros_like(l_ref)
    acc_ref[...] = jnp.zeros_like(acc_ref)
    lane_lo = lax.broadcasted_iota(jnp.int32, (1, HEAD_LANES), 1) < 64

    def accumulate(slot, half, rows, diagonal_from):
        nrows = rows.stop - rows.start
        keys = slice(half * tk, (half + 1) * tk)
        if diagonal_from is not None:
            visible = (lax.broadcasted_iota(jnp.int32, (nrows, tk), 1)
                       <= lax.broadcasted_iota(jnp.int32, (nrows, tk), 0) + (rows.start - diagonal_from))
        for pair in range(MLA_HEADS // 2):
            vblk = vbuf[slot, keys, pair * 128:(pair + 1) * 128]
            alphas = []
            pvs = []
            for sub in range(2):
                hd = 2 * pair + sub
                sl = slice(hd * HEAD_LANES, (hd + 1) * HEAD_LANES)
                s = _dot_nt(q_ref[rows, sl], kbuf[slot, keys, sl])
                if diagonal_from is not None:
                    s = jnp.where(visible, s, NEG_BIG)
                m_prev = m_ref[hd, rows, :]
                m_new = jnp.maximum(m_prev, jnp.max(s, axis=-1, keepdims=True))
                alpha = jnp.exp2(m_prev - m_new)
                p = jnp.exp2(s - jnp.tile(m_new, (1, tk // HEAD_LANES)))
                l_ref[hd, rows, :] = alpha * l_ref[hd, rows, :] + jnp.sum(p, axis=-1, keepdims=True)
                m_ref[hd, rows, :] = m_new
                alphas.append(alpha)
                pvs.append(_dot(p.astype(_BF16), vblk))
            psl = slice(pair * 128, (pair + 1) * 128)
            acc_ref[rows, psl] = (acc_ref[rows, psl] * jnp.where(lane_lo, alphas[0], alphas[1])
                                  + jnp.where(lane_lo, pvs[0], pvs[1]))

    everything = slice(0, tq)
    start(0, 0)

    def earlier_group(j, carry):
        slot = j % 2
        wait(j, slot)
        start(j + 1, 1 - slot)
        accumulate(slot, 0, everything, None)
        accumulate(slot, 1, everything, None)
        return carry

    lax.fori_loop(0, qb, earlier_group, 0)

    slot = qb % 2
    wait(qb, slot)
    accumulate(slot, 0, everything, 0)
    accumulate(slot, 1, slice(tk, tq), tk)

    for pair in range(MLA_HEADS // 2):
        psl = slice(pair * 128, (pair + 1) * 128)
        denom = jnp.where(lane_lo, l_ref[2 * pair], l_ref[2 * pair + 1])
        o_ref[:, psl] = (acc_ref[:, psl] / denom).astype(o_ref.dtype)


def _mla_attn_call(qm, km, vm, batch, seq_len):
    tq, tk = ATTN_Q_TILE, ATTN_TILE
    nq = seq_len // tq
    return pl.pallas_call(
        functools.partial(_mla_attn_kernel, seq_len=seq_len),
        grid=(batch, nq),
        in_specs=[pl.BlockSpec((tq, 512), lambda b, q: (b * nq + q, 0)),
                  pl.BlockSpec(memory_space=pl.ANY), pl.BlockSpec(memory_space=pl.ANY)],
        out_specs=pl.BlockSpec((tq, 256), lambda b, q: (b * nq + q, 0)),
        out_shape=jax.ShapeDtypeStruct((batch * seq_len, 256), _BF16),
        scratch_shapes=[pltpu.VMEM((2, tq, 512), _BF16), pltpu.VMEM((2, tq, 256), _BF16),
                        pltpu.SemaphoreType.DMA((2, 2)),
                        pltpu.VMEM((MLA_HEADS, tq, HEAD_LANES), _F32),
                        pltpu.VMEM((MLA_HEADS, tq, HEAD_LANES), _F32),
                        pltpu.VMEM((tq, 256), _F32)],
        compiler_params=_params(2),
        name="mla_attention",
    )(qm, km, vm)


def _memkv_kernel(mem_ref, gmem_ref, wkv_ref, gk_ref, k_ref, v_ref):
    m = _rms_scale(mem_ref[...], D_MODEL) * gmem_ref[...]
    kv = _dot(m.astype(_BF16), wkv_ref[...])
    lane_lo = lax.broadcasted_iota(jnp.int32, (1, 128), 1) < 64
    lane = lax.broadcasted_iota(jnp.int32, (1, 256), 1)
    k = jnp.concatenate([_pair_norm64(kv[:, 0:128], lane_lo), _pair_norm64(kv[:, 128:256], lane_lo)],
                        axis=1) * gk_ref[...]
    v = kv[:, 256:512]
    for hd in range(XA_HEADS):
        own = jnp.logical_and(lane >= hd * XA_HEAD_DIM, lane < (hd + 1) * XA_HEAD_DIM)
        k_ref[hd] = jnp.where(own, k, 0.0).astype(_BF16)
        v_ref[hd] = jnp.where(own, v, 0.0).astype(_BF16)


def _memkv_call(mem2d, g_mem, w_kv, g_k4, depth, batch):
    out = jax.ShapeDtypeStruct((depth, batch, XA_HEADS, MEM_LEN, 256), _BF16)
    ospec = pl.BlockSpec((None, None, XA_HEADS, MEM_LEN, 256), lambda l, b: (l, b, 0, 0, 0))
    return pl.pallas_call(
        _memkv_kernel,
        grid=(depth, batch),
        in_specs=[pl.BlockSpec((MEM_LEN, D_MODEL), lambda l, b: (b, 0)),
                  pl.BlockSpec((None, 1, D_MODEL), lambda l, b: (l, 0, 0)),
                  pl.BlockSpec((None, D_MODEL, 512), lambda l, b: (l, 0, 0)),
                  pl.BlockSpec((None, 1, 256), lambda l, b: (l, 0, 0))],
        out_specs=[ospec, ospec],
        out_shape=[out, out],
        compiler_params=_params(2),
        name="memory_kv",
    )(mem2d, g_mem, w_kv, g_k4)


def _mix_kernel(ya_ref, yb_ref, yc_ref, yd_ref, x_ref, gmo_ref, wmo_ref, gxa_ref, wq_ref, gxq_ref,
                kx_ref, vx_ref, wo_ref, gffn_ref, wrh_ref, wrl_ref, br_ref, ltri_ref, utri_ref,
                x2_ref, h3_ref, meta_ref, cnt_ref):
    tm = MIX_TILE
    gmo = gmo_ref[...]
    lane_lo = lax.broadcasted_iota(jnp.int32, (1, 128), 1) < 64

    def rows_to_logits(rs):
        parts = []
        for g, ref in enumerate((ya_ref, yb_ref, yc_ref, yd_ref)):
            yg = ref[rs, :].astype(_F32)
            parts.append((_rms_scale(yg, GROUP_W) * gmo[:, g * 256:(g + 1) * 256]).astype(_BF16))
        y = jnp.concatenate(parts, axis=1)
        x1 = x_ref[rs, :] + _dot(y, wmo_ref[...])

        h = (_rms_scale(x1, D_MODEL) * gxa_ref[...]).astype(_BF16)
        q = _dot(h, wq_ref[...])
        qn = jnp.concatenate([_pair_norm64(q[:, 0:128], lane_lo), _pair_norm64(q[:, 128:256], lane_lo)],
                             axis=1)
        qn = (qn * gxq_ref[...] * (XA_HEAD_DIM ** -0.5)).astype(_BF16)
        o = jnp.zeros((x1.shape[0], 256), _F32)
        for hd in range(XA_HEADS):
            s = _dot_nt(qn, kx_ref[hd])
            m = jnp.max(s, axis=-1, keepdims=True)
            p = jnp.exp(s - m)
            denom = jnp.sum(p, axis=-1, keepdims=True)
            o = o + _dot(p.astype(_BF16), vx_ref[hd]) / denom
        x2 = x1 + _dot(o.astype(_BF16), wo_ref[...])
        x2_ref[rs, :] = x2

        h3 = _rms_scale(x2, D_MODEL) * gffn_ref[...]
        h3_hi = h3.astype(_BF16)
        h3_lo = (h3 - h3_hi.astype(_F32)).astype(_BF16)
        h3_ref[rs, :] = h3_hi
        return (_dot(h3_hi, wrh_ref[...]) + (_dot(h3_hi, wrl_ref[...]) + _dot(h3_lo, wrh_ref[...]))
                + br_ref[...])

    all_logits = rows_to_logits(slice(0, tm))
    for st in range(tm // TOKEN_TILE):
        meta, counts = _route(all_logits[st * TOKEN_TILE:(st + 1) * TOKEN_TILE, :], ltri_ref, utri_ref)
        meta_ref[st * TOKEN_TILE:(st + 1) * TOKEN_TILE, :] = meta
        cnt_ref[st * 8:(st + 1) * 8, :] = jnp.broadcast_to(counts, (8, 128))


def _route(logits, ltri_ref, utri_ref):
    lane = lax.broadcasted_iota(jnp.int32, (TOKEN_TILE, 128), 1).astype(_F32)
    far = 1e9
    is_group = jnp.logical_and(lane >= N_EXPERTS, lane < N_EXPERTS + N_EXPERT_GROUPS)
    gl = jnp.where(is_group, logits, -jnp.inf)
    gmax = jnp.max(gl, axis=-1, keepdims=True)
    gidx = jnp.min(jnp.where(gl == gmax, lane, far), axis=-1, keepdims=True) - N_EXPERTS
    g_w = 1.0 / jnp.sum(jnp.where(is_group, jnp.exp(logits - gmax), 0.0), axis=-1, keepdims=True)
    in_group = jnp.logical_and(lane < N_EXPERTS, jnp.floor(lane * (1.0 / EXPERTS_PER_GROUP)) == gidx)
    el = jnp.where(in_group, logits, -jnp.inf)
    emax = jnp.max(el, axis=-1, keepdims=True)
    ep = jnp.where(in_group, jnp.exp(logits - emax), 0.0)
    prob = jnp.where(in_group, ep / jnp.sum(ep, axis=-1, keepdims=True), -1.0)
    p1 = jnp.max(prob, axis=-1, keepdims=True)
    e0 = jnp.min(jnp.where(prob == p1, lane, far), axis=-1, keepdims=True)
    prob2 = jnp.where(lane == e0, -1.0, prob)
    p2 = jnp.max(prob2, axis=-1, keepdims=True)
    e1 = jnp.min(jnp.where(prob2 == p2, lane, far), axis=-1, keepdims=True)
    w0 = g_w * (p1 / (p1 + p2))
    w1 = g_w * (p2 / (p1 + p2))

    onehot = jnp.where(jnp.logical_or(lane == e0, lane == e1), 1.0, 0.0)
    prefix = _dot(ltri_ref[...], onehot.astype(_BF16))
    counts = jnp.sum(onehot, axis=0, keepdims=True)
    units = jnp.floor((counts + (RUN_ALIGN - 1)) * (1.0 / RUN_ALIGN))
    offs = _dot(jnp.broadcast_to(units, (8, 128)).astype(_BF16), utri_ref[...])[0:1, :] * RUN_ALIGN
    where_to = prefix + offs
    pos0 = jnp.sum(jnp.where(lane == e0, where_to, 0.0), axis=-1, keepdims=True)
    pos1 = jnp.sum(jnp.where(lane == e1, where_to, 0.0), axis=-1, keepdims=True)

    meta = jnp.where(lane == 0, e0,
           jnp.where(lane == 1, e1,
           jnp.where(lane == 2, w0,
           jnp.where(lane == 3, w1,
           jnp.where(lane == 4, pos0,
           jnp.where(lane == 5, pos1, 0.0))))))
    return meta, counts


def _mix_call(ya, yb, yc, yd, x, lw, kx, vx, seq_len):
    n = x.shape[0]
    tm = MIX_TILE
    tiles_per_seq = seq_len // tm

    def full(a):
        nd = a.ndim
        return pl.BlockSpec(a.shape, lambda i, _nd=nd: (0,) * _nd)

    def rows(width):
        return pl.BlockSpec((tm, width), lambda i: (i, 0))

    kvspec = pl.BlockSpec((None, XA_HEADS, MEM_LEN, 256), lambda i: (i // tiles_per_seq, 0, 0, 0))
    consts_a = [lw["g_mo"], lw["w_mo"], lw["g_xa"], lw["xa_w_q"], lw["g_xq"]]
    consts_b = [lw["xa_w_o"], lw["g_ffn"], lw["w_router_hi"], lw["w_router_lo"], lw["b_router"],
                lw["ltri"], lw["utri"]]
    ntiles = n // tm
    return pl.pallas_call(
        _mix_kernel,
        grid=(ntiles,),
        in_specs=[rows(256)] * 4 + [rows(D_MODEL)] + [full(a) for a in consts_a]
                 + [kvspec, kvspec] + [full(a) for a in consts_b],
        out_specs=[rows(D_MODEL), rows(D_MODEL), rows(128),
                   pl.BlockSpec((8 * (tm // TOKEN_TILE), 128), lambda i: (i, 0))],
        out_shape=[jax.ShapeDtypeStruct((n, D_MODEL), _F32), jax.ShapeDtypeStruct((n, D_MODEL), _BF16),
                   jax.ShapeDtypeStruct((n, 128), _F32),
                   jax.ShapeDtypeStruct((n // TOKEN_TILE * 8, 128), _F32)],
        compiler_params=_params(1),
        name="mix_xattn_router",
    )(ya, yb, yc, yd, x, *consts_a, kx, vx, *consts_b)


def _move_runs(tab_ref, t, live, local_ref, remote_ref, sems, *, to_remote, wait):
    for bit in range(RUN_BITS):
        rows = RUN_ALIGN << bit
        base = (t * RUN_BITS + bit) * PIECE_COLS
        count = jnp.where(live, tab_ref[base], 0)

        def one(p, carry, base=base, rows=rows, bit=bit):
            loc = pl.multiple_of(tab_ref[base + 1 + p], RUN_ALIGN)
            rem = pl.multiple_of(tab_ref[base + 1 + N_EXPERTS + p], RUN_ALIGN)
            lsl = local_ref.at[pl.ds(loc, rows)]
            rsl = remote_ref.at[pl.ds(rem, rows)]
            src, dst = (lsl, rsl) if to_remote else (rsl, lsl)
            copy = pltpu.make_async_copy(src, dst, sems.at[bit, p])
            if wait:
                copy.wait()
            else:
                copy.start()
            return carry

        lax.fori_loop(0, count, one, 0)


def _selection(meta, first, rows):
    r = first + lax.broadcasted_iota(jnp.int32, (TOKEN_TILE, rows), 1)
    pos0 = meta[:, 4:5].astype(jnp.int32)
    pos1 = meta[:, 5:6].astype(jnp.int32)
    return jnp.where(r == pos0, 1.0, 0.0).astype(_BF16), jnp.where(r == pos1, 1.0, 0.0).astype(_BF16)


def _zero_rows(count, row_of, rows, zero_ref, xs_ref, sems):
    def copy(i):
        dst = xs_ref.at[pl.ds(pl.multiple_of(row_of(i), RUN_ALIGN), rows)]
        return pltpu.make_async_copy(zero_ref.at[pl.ds(0, rows)], dst, sems.at[i % ZERO_RING])

    def issue(i, carry):
        @pl.when(i >= ZERO_RING)
        def _():
            copy(i - ZERO_RING).wait()
        copy(i).start()
        return carry

    def drain(i, carry):
        copy(i).wait()
        return carry

    lax.fori_loop(0, count, issue, 0)
    lax.fori_loop(jnp.maximum(count - ZERO_RING, 0), count, drain, 0)


def _dispatch_kernel(tab_ref, used_ref, ztab_ref, meta_ref, h3_ref, xs_ref, buf0_ref, buf1_ref, zero_ref,
                     sems, zero_sems, free_sems, *, ntiles, n_blocks):
    t = pl.program_id(0)
    last = ntiles - 1
    bufs = (buf0_ref, buf1_ref)
    used = used_ref[jnp.minimum(t, last)]

    @pl.when(t == 0)
    def _():
        zero_ref[...] = jnp.zeros_like(zero_ref)
        for bit in range(TAIL_BITS):
            base = bit * (1 + N_EXPERTS)
            _zero_rows(ztab_ref[base], lambda i, base=base: ztab_ref[base + 1 + i], RUN_ALIGN << bit,
                       zero_ref, xs_ref, zero_sems.at[bit])

    live_blocks = ztab_ref[TAIL_BITS * (1 + N_EXPERTS)]
    per_step = -(-n_blocks // (ntiles + 1))

    def free_block_copy(step, j):
        blk = live_blocks + step * per_step + j
        dst = xs_ref.at[pl.ds(pl.multiple_of(blk * MOE_BLOCK, MOE_BLOCK), MOE_BLOCK)]
        return blk < n_blocks, pltpu.make_async_copy(zero_ref, dst, free_sems.at[step % 2, j])

    for j in range(per_step):
        exists, copy = free_block_copy(t - 1, j)
        pl.when(jnp.logical_and(t >= 1, exists))(copy.wait)
    for j in range(per_step):
        exists, copy = free_block_copy(t, j)
        pl.when(jnp.logical_and(t <= ntiles, exists))(copy.start)

    def step(slot):
        mine, other = bufs[slot], bufs[1 - slot]
        _move_runs(tab_ref, jnp.clip(t - 2, 0, last), t >= 2, mine, xs_ref, sems.at[slot],
                   to_remote=True, wait=True)
        _move_runs(tab_ref, jnp.clip(t - 1, 0, last), jnp.logical_and(t >= 1, t <= ntiles), other, xs_ref,
                   sems.at[1 - slot], to_remote=True, wait=False)

        meta = meta_ref[...]
        lane = lax.broadcasted_iota(jnp.int32, (TOKEN_TILE, 128), 1)
        cols = []
        for k in range(2):
            w = meta[:, 2 + k:3 + k]
            hi = w.astype(_BF16).astype(_F32)
            mid = (w - hi).astype(_BF16).astype(_F32)
            lo = (w - hi) - mid
            cols.append(jnp.where(lane == 0, hi, jnp.where(lane == 1, mid, jnp.where(lane == 2, lo, 0.0)))
                        .astype(_BF16))

        def sort_rows(first, rows):
            sel0, sel1 = _selection(meta, first, rows)
            tn = (((0,), (0,)), ((), ()))
            mine[first:first + rows, 0:D_MODEL] = lax.dot_general(
                sel0 + sel1, h3_ref[...], tn, preferred_element_type=_F32).astype(_BF16)
            mine[first:first + rows, D_MODEL:XS_COLS] = (
                lax.dot_general(sel0, cols[0], tn, preferred_element_type=_F32)
                + lax.dot_general(sel1, cols[1], tn, preferred_element_type=_F32)).astype(_BF16)

        sort_rows(0, SORT_ALWAYS)
        for first in range(SORT_ALWAYS, SORT_ROWS, SORT_CHUNK):
            pl.when(used > first)(functools.partial(sort_rows, first, SORT_CHUNK))

    for slot in range(2):
        pl.when(t % 2 == slot)(functools.partial(step, slot))


def _dispatch_call(tab, used, ztab, meta, h3):
    n = h3.shape[0]
    tm = TOKEN_TILE
    ntiles = n // tm
    xs_rows = _xs_rows(n)
    tile = lambda i, tab, used, ztab: (jnp.minimum(i, ntiles - 1), 0)
    grid_spec = pltpu.PrefetchScalarGridSpec(
        num_scalar_prefetch=3,
        grid=(ntiles + 2,),
        in_specs=[pl.BlockSpec((tm, 128), tile),
                  pl.BlockSpec((tm, D_MODEL), tile)],
        out_specs=pl.BlockSpec(memory_space=pl.ANY),
        scratch_shapes=[pltpu.VMEM((SORT_ROWS, XS_COLS), _BF16), pltpu.VMEM((SORT_ROWS, XS_COLS), _BF16),
                        pltpu.VMEM((MOE_BLOCK, XS_COLS), _BF16),
                        pltpu.SemaphoreType.DMA((2, RUN_BITS, N_EXPERTS)),
                        pltpu.SemaphoreType.DMA((TAIL_BITS, ZERO_RING)),
                        pltpu.SemaphoreType.DMA((2, -(-(xs_rows // MOE_BLOCK) // (ntiles + 1))))],
    )
    return pl.pallas_call(
        functools.partial(_dispatch_kernel, ntiles=ntiles, n_blocks=xs_rows // MOE_BLOCK),
        grid_spec=grid_spec,
        out_shape=jax.ShapeDtypeStruct((xs_rows, XS_COLS), _BF16),
        compiler_params=_params(1),
        name="moe_dispatch",
    )(tab, used, ztab, meta, h3)


def _ffn_kernel(be_ref, nv_ref, xs_ref, wg_ref, wu_ref, wd_ref, ys_ref, wgu_ref, wdn_ref):
    i = pl.program_id(0)

    @pl.when(jnp.logical_or(i == 0, be_ref[i] != be_ref[jnp.maximum(i - 1, 0)]))
    def _():
        wgu_ref[:, 0:D_EXPERT] = wg_ref[...].astype(_BF16)
        wgu_ref[:, D_EXPERT:2 * D_EXPERT] = wu_ref[...].astype(_BF16)
        wdn_ref[...] = wd_ref[...].astype(_BF16)

    @pl.when(i < nv_ref[0])
    def _():
        xb = xs_ref[:, 0:D_MODEL]
        wcols = xs_ref[:, D_MODEL:XS_COLS].astype(_F32)
        wt = wcols[:, 0:1] + wcols[:, 1:2] + wcols[:, 2:3]
        gu = _dot(xb, wgu_ref[...])
        g = gu[:, 0:D_EXPERT]
        a = (g * jax.nn.sigmoid(g)) * gu[:, D_EXPERT:2 * D_EXPERT]
        y = _dot(a.astype(_BF16), wdn_ref[...])
        ys_ref[...] = (y * wt).astype(ys_ref.dtype)

    @pl.when(i >= nv_ref[0])
    def _():
        ys_ref[...] = jnp.zeros_like(ys_ref)


def _ffn_call(blk_expert, n_valid, xs, w_gate, w_up, w_down, layer):
    n_blocks = xs.shape[0] // MOE_BLOCK
    wspec = lambda shape: pl.BlockSpec((None, None) + shape, lambda i, be, nv: (layer, be[i], 0, 0))
    last_live = lambda i, be, nv: (jnp.minimum(i, nv[0] - 1), 0)
    grid_spec = pltpu.PrefetchScalarGridSpec(
        num_scalar_prefetch=2,
        grid=(n_blocks,),
        in_specs=[pl.BlockSpec((MOE_BLOCK, XS_COLS), last_live),
                  wspec((D_MODEL, D_EXPERT)), wspec((D_MODEL, D_EXPERT)), wspec((D_EXPERT, D_MODEL))],
        out_specs=pl.BlockSpec((MOE_BLOCK, D_MODEL), lambda i, be, nv: (i, 0)),
        scratch_shapes=[pltpu.VMEM((D_MODEL, 2 * D_EXPERT), _BF16), pltpu.VMEM((D_EXPERT, D_MODEL), _BF16)],
    )
    return pl.pallas_call(
        _ffn_kernel,
        grid_spec=grid_spec,
        out_shape=jax.ShapeDtypeStruct((xs.shape[0], D_MODEL), _BF16),
        compiler_params=_params(1),
        name="moe_experts",
    )(blk_expert, n_valid, xs, w_gate, w_up, w_down)


def _combine_kernel(tab_ref, used_ref, meta_ref, x2_ref, ys_ref, out_ref, buf0_ref, buf1_ref, sems,
                    *, ntiles):
    t = pl.program_id(0)
    last = ntiles - 1
    bufs = (buf0_ref, buf1_ref)
    used = used_ref[jnp.clip(t - 1, 0, last)]

    @pl.when(t == 0)
    def _():
        buf0_ref[...] = jnp.zeros_like(buf0_ref)
        buf1_ref[...] = jnp.zeros_like(buf1_ref)

    def step(slot):
        mine, other = bufs[slot], bufs[1 - slot]
        _move_runs(tab_ref, jnp.clip(t - 1, 0, last), t >= 1, other, ys_ref, sems.at[1 - slot],
                   to_remote=False, wait=True)
        _move_runs(tab_ref, jnp.minimum(t, last), t <= last, mine, ys_ref, sems.at[slot],
                   to_remote=False, wait=False)
        meta = meta_ref[...]

        def gathered(first, rows):
            sel0, sel1 = _selection(meta, first, rows)
            return _dot(sel0 + sel1, other[first:first + rows, :])

        out_ref[...] = x2_ref[...] + gathered(0, SORT_ALWAYS)
        for first in range(SORT_ALWAYS, SORT_ROWS, SORT_CHUNK):
            @pl.when(used > first)
            def _(first=first):
                out_ref[...] += gathered(first, SORT_CHUNK)

    for slot in range(2):
        pl.when(t % 2 == slot)(functools.partial(step, slot))


def _combine_call(tab, used, meta, x2, ys):
    n = x2.shape[0]
    tm = TOKEN_TILE
    ntiles = n // tm
    tile = lambda i, tab, used: (jnp.maximum(i - 1, 0), 0)
    grid_spec = pltpu.PrefetchScalarGridSpec(
        num_scalar_prefetch=2,
        grid=(ntiles + 1,),
        in_specs=[pl.BlockSpec((tm, 128), tile),
                  pl.BlockSpec((tm, D_MODEL), tile),
                  pl.BlockSpec(memory_space=pl.ANY)],
        out_specs=pl.BlockSpec((tm, D_MODEL), tile),
        scratch_shapes=[pltpu.VMEM((SORT_ROWS, D_MODEL), _BF16), pltpu.VMEM((SORT_ROWS, D_MODEL), _BF16),
                        pltpu.SemaphoreType.DMA((2, RUN_BITS, N_EXPERTS))],
    )
    return pl.pallas_call(
        functools.partial(_combine_kernel, ntiles=ntiles),
        grid_spec=grid_spec,
        out_shape=jax.ShapeDtypeStruct((n, D_MODEL), _F32),
        compiler_params=_params(1),
        name="moe_combine",
    )(tab, used, meta, x2, ys)


def _moe_tables(cnt, n_tokens):
    ntiles = n_tokens // TOKEN_TILE
    counts = cnt.reshape(ntiles, 8, 128)[:, 0, :N_EXPERTS].astype(jnp.int32)
    units = (counts + RUN_ALIGN - 1) // RUN_ALIGN
    padded = units * RUN_ALIGN
    local = jnp.cumsum(padded, axis=1) - padded
    total = jnp.sum(padded, axis=0)
    total_blk = ((total + MOE_BLOCK - 1) // MOE_BLOCK) * MOE_BLOCK
    ends = jnp.cumsum(total_blk)
    starts = ends - total_blk
    remote = starts[None, :] + jnp.cumsum(padded, axis=0) - padded
    bits = jnp.arange(RUN_BITS, dtype=jnp.int32)[None, :, None]
    has = (units[:, None, :] >> bits) & 1
    done = (units[:, None, :] & ((1 << bits) - 1)) * RUN_ALIGN
    rank = jnp.cumsum(has, axis=2) - 1
    place = jnp.logical_and(rank[..., None] == jnp.arange(N_EXPERTS, dtype=jnp.int32), has[..., None] == 1)
    compact = lambda rows: jnp.sum(jnp.where(place, rows[..., None], 0), axis=2)
    tab = jnp.concatenate([jnp.sum(has, axis=2, keepdims=True), compact(local[:, None, :] + done),
                           compact(remote[:, None, :] + done)], axis=-1).reshape(-1).astype(jnp.int32)
    n_blocks = _xs_rows(n_tokens) // MOE_BLOCK
    blk_start = jnp.arange(n_blocks, dtype=jnp.int32) * MOE_BLOCK
    blk_expert = jnp.minimum(jnp.sum((ends[None, :] <= blk_start[:, None]).astype(jnp.int32), axis=1),
                             N_EXPERTS - 1)
    n_valid = (ends[-1] // MOE_BLOCK).astype(jnp.int32).reshape(1)
    used = jnp.sum(padded, axis=1).astype(jnp.int32)
    tail_units = (total_blk - total) // RUN_ALIGN
    tail_bits = jnp.arange(TAIL_BITS, dtype=jnp.int32)[:, None]
    tail_has = (tail_units[None, :] >> tail_bits) & 1
    tail_rows = starts + total + (tail_units[None, :] & ((1 << tail_bits) - 1)) * RUN_ALIGN
    tail_rank = jnp.cumsum(tail_has, axis=1) - 1
    tail_place = jnp.logical_and(tail_rank[..., None] == jnp.arange(N_EXPERTS, dtype=jnp.int32),
                                 tail_has[..., None] == 1)
    tail_list = jnp.sum(jnp.where(tail_place, tail_rows[..., None], 0), axis=1)
    ztab = jnp.concatenate([jnp.concatenate([jnp.sum(tail_has, axis=1, keepdims=True), tail_list],
                                            axis=1).reshape(-1), n_valid]).astype(jnp.int32)
    return tab, used, ztab, blk_expert, n_valid


def _xs_rows(n_tokens):
    ntiles = n_tokens // TOKEN_TILE
    worst = 2 * n_tokens + ntiles * N_EXPERTS * (RUN_ALIGN - 1) + N_EXPERTS * (MOE_BLOCK - 1)
    return ((worst + MOE_BLOCK - 1) // MOE_BLOCK) * MOE_BLOCK


def _pad_last(a, width):
    return jnp.pad(a, [(0, 0)] * (a.ndim - 1) + [(0, width - a.shape[-1])])


def _swap_mid_heads(a, axis):
    shape = a.shape
    a = a.reshape(shape[:axis] + (4, 64) + shape[axis + 1:])
    a = jnp.take(a, jnp.asarray([0, 2, 1, 3]), axis=axis)
    return a.reshape(shape)


def _layer_weights(l, p):
    w_in = p["w_in"][l]
    a, b = w_in[:, :A_COLS], w_in[:, A_COLS:A_COLS + B_COLS]
    c = w_in[:, A_COLS + B_COLS:A_COLS + B_COLS + C_COLS]
    d = w_in[:, A_COLS + B_COLS + C_COLS:]
    zeros = lambda w: jnp.zeros((D_MODEL, w), _F32)
    half = MLA_ROPE // 2
    k_rope = a[:, 384:416]
    k_rope_swap = jnp.concatenate([k_rope[:, half:], k_rope[:, :half]], axis=1)
    w_in_p = jnp.concatenate([
        a[:, :384], zeros(64), k_rope, zeros(32), zeros(64), k_rope_swap, zeros(32), b,
        _swap_mid_heads(c[:, :256], 1), c[:, 256:384], c[:, 384:512], d], axis=1).astype(_BF16)
    assert w_in_p.shape[1] == IN_COLS_PADDED

    w_uq = p["mla_w_uq"][l].reshape(MLA_Q_RANK, MLA_HEADS, MLA_QK_DIM)
    w_uq_swap = jnp.concatenate([jnp.zeros_like(w_uq[..., :MLA_NOPE]), w_uq[..., MLA_NOPE + half:],
                                 w_uq[..., MLA_NOPE:MLA_NOPE + half]], axis=-1)
    w_uq = jnp.concatenate([_pad_last(w_uq, HEAD_LANES).reshape(MLA_Q_RANK, -1),
                            _pad_last(w_uq_swap, HEAD_LANES).reshape(MLA_Q_RANK, -1)], axis=1).astype(_BF16)

    def swap_gain(g):
        return _pad_last(jnp.concatenate([jnp.zeros((MLA_NOPE,), _F32), g[MLA_NOPE + half:],
                                          g[MLA_NOPE:MLA_NOPE + half]]), HEAD_LANES)[None]
    w_ukv = p["mla_w_ukv"][l].reshape(MLA_KV_RANK, MLA_HEADS, MLA_NOPE + MLA_V)
    k_nope = _pad_last(w_ukv[..., :MLA_NOPE], HEAD_LANES).reshape(MLA_KV_RANK, MLA_HEADS * HEAD_LANES)
    v_part = w_ukv[..., MLA_NOPE:].reshape(MLA_KV_RANK, MLA_HEADS * MLA_V)
    w_ukv_p = jnp.concatenate([k_nope, v_part], axis=1).astype(_BF16)

    eye = jnp.eye(len(POOL_WINDOWS), dtype=_F32)
    w_pool = jnp.einsum("gcd,gh->gchd", p["pool_w"][l], eye).reshape(256, 256).astype(_BF16)

    g_mo = p["mix_out_norm"][l]
    g_mo = jnp.concatenate([g_mo[:512], _swap_mid_heads(g_mo[512:768], 0), g_mo[768:]])
    w_mo = p["w_mix_out"][l]
    w_mo = jnp.concatenate([w_mo[:512], _swap_mid_heads(w_mo[512:768], 0), w_mo[768:]], axis=0)

    w_router = _pad_last(jnp.concatenate([p["w_expert"][l], p["w_group"][l]], axis=1), 128)
    w_router_hi = w_router.astype(_BF16)
    w_router_lo = (w_router - w_router_hi.astype(_F32)).astype(_BF16)
    b_router = jnp.concatenate([p["b_expert"][l], p["b_group"][l]])
    tm = TOKEN_TILE
    ltri = (jnp.arange(tm)[None, :] < jnp.arange(tm)[:, None]).astype(_BF16)
    utri = (jnp.arange(128)[:, None] < jnp.arange(128)[None, :]).astype(_BF16)
    tile2 = lambda g: jnp.concatenate([g, g])[None]
    return dict(
        g_mix=p["norm_mix"][l][None], w_in=w_in_p,
        g_cq=p["mla_g_cq"][l][None], w_uq=w_uq, g_ckv=p["mla_g_ckv"][l][None], w_ukv=w_ukv_p,
        g_q=_pad_last(p["mla_g_q"][l], HEAD_LANES)[None], g_k=_pad_last(p["mla_g_k"][l], HEAD_LANES)[None],
        g_q_swap=swap_gain(p["mla_g_q"][l]), g_k_swap=swap_gain(p["mla_g_k"][l]),
        w_pool=w_pool, pool_scale=p["pool_scale"][l][None], conv_w=p["conv_w"][l],
        conv_b=p["conv_b"][l][None], g_sq=tile2(p["swa_g_q"][l]), g_sk=tile2(p["swa_g_k"][l]),
        sinks=jnp.take(p["swa_sinks"][l], jnp.asarray([0, 2, 1, 3])),
        g_mo=g_mo[None], w_mo=w_mo.astype(_BF16), g_xa=p["norm_xa"][l][None],
        xa_w_q=p["xa_w_q"][l].astype(_BF16), g_xq=jnp.tile(p["xa_g_q"][l], 4)[None],
        xa_w_o=p["xa_w_o"][l].astype(_BF16), g_ffn=p["norm_ffn"][l][None],
        w_router_hi=w_router_hi, w_router_lo=w_router_lo, b_router=_pad_last(b_router, 128)[None],
        ltri=ltri, utri=utri,
    )


def kernel(x, mem, positions, norm_mix, w_in, mla_g_cq, mla_w_uq, mla_g_ckv, mla_w_ukv, mla_g_q, mla_g_k, pool_w, pool_scale, swa_g_q, swa_g_k, swa_sinks, conv_w, conv_b, mix_out_norm, w_mix_out, norm_xa, norm_mem, xa_w_q, xa_w_kv, xa_g_q, xa_g_k, xa_w_o, norm_ffn, w_group, b_group, w_expert, b_expert, w_gate, w_up, w_down):
    p = dict(norm_mix=norm_mix, w_in=w_in, mla_g_cq=mla_g_cq, mla_w_uq=mla_w_uq, mla_g_ckv=mla_g_ckv,
             mla_w_ukv=mla_w_ukv, mla_g_q=mla_g_q, mla_g_k=mla_g_k, pool_w=pool_w, pool_scale=pool_scale,
             swa_g_q=swa_g_q, swa_g_k=swa_g_k, swa_sinks=swa_sinks, conv_w=conv_w, conv_b=conv_b,
             mix_out_norm=mix_out_norm, w_mix_out=w_mix_out, norm_xa=norm_xa, xa_w_q=xa_w_q,
             xa_g_q=xa_g_q, xa_w_o=xa_w_o, norm_ffn=norm_ffn, w_group=w_group, b_group=b_group,
             w_expert=w_expert, b_expert=b_expert)
    batch, seq_len, _ = x.shape
    depth = w_in.shape[0]
    n = batch * seq_len
    assert seq_len % MIX_TILE == 0 and MIX_TILE % TOKEN_TILE == 0
    assert seq_len % PROJ_TILE == 0 and seq_len % ATTN_Q_TILE == 0
    assert mem.shape[1] == MEM_LEN

    xf = x.reshape(n, D_MODEL)
    tabs = _rope_tables(positions)
    kx, vx = _memkv_call(mem.reshape(batch * MEM_LEN, D_MODEL), norm_mem[:, None, :],
                         xa_w_kv.astype(_BF16), jnp.tile(xa_g_k, (1, 4))[:, None, :], depth, batch)
    for l in range(depth):
        lw = _layer_weights(l, p)
        qm, km, vm, yb, yc, yd = _proj_call(xf, lw, tabs, seq_len)
        ya = _mla_attn_call(qm, km, vm, batch, seq_len)
        x2, h3, meta, cnt = _mix_call(ya, yb, yc, yd, xf, lw, kx[l], vx[l], seq_len)
        tab, used, ztab, blk_expert, n_valid = _moe_tables(cnt, n)
        xs = _dispatch_call(tab, used, ztab, meta, h3)
        ys = _ffn_call(blk_expert, n_valid, xs, w_gate, w_up, w_down, l)
        xf = _combine_call(tab, used, meta, x2, ys)
    return xf.reshape(batch, seq_len, D_MODEL)
```

```python
import functools

import jax
import jax.numpy as jnp
from jax import lax
from jax.experimental import pallas as pl
from jax.experimental.pallas import tpu as pltpu

EPS = 1e-6
NEG_BIG = -1e30
LOG2_E = 1.4426950408889634
ROPE_THETA = 10000.0

D_MODEL = 1024
MEM_LEN = 256
GROUP_W = 256

MLA_HEADS = 4
MLA_Q_RANK = 256
MLA_KV_RANK = 128
MLA_NOPE = 64
MLA_ROPE = 32
MLA_QK_DIM = MLA_NOPE + MLA_ROPE
MLA_V = 64
HEAD_LANES = 128

POOL_WINDOWS = (2, 4, 8, 16)
POOL_HALO = 16

SWA_HEADS = 4
SWA_KV_HEADS = 2
SWA_HEAD_DIM = 64
SWA_WINDOW = 128

CONV_CH = 256

XA_HEADS = 4
XA_HEAD_DIM = 64

N_EXPERT_GROUPS = 4
EXPERTS_PER_GROUP = 8
N_EXPERTS = 32
D_EXPERT = 256
MOE_BLOCK = 1024

A_COLS = MLA_Q_RANK + MLA_KV_RANK + MLA_ROPE
B_COLS = GROUP_W
C_COLS = (SWA_HEADS + 2 * SWA_KV_HEADS) * SWA_HEAD_DIM
COL_CQ, COL_CKV, COL_KROPE, COL_KROPE_SWAP = 0, 256, 384, 512
COL_POOL, COL_SWA_Q, COL_SWA_K, COL_SWA_V, COL_CONV = 640, 896, 1152, 1280, 1408
IN_COLS_PADDED = COL_CONV + 3 * CONV_CH

TOKEN_TILE = 512
PROJ_TILE = 512
MIX_TILE = 1024
ATTN_TILE = 512
ATTN_Q_TILE = 1024
RUN_ALIGN = 16
RUN_BITS = 6
SORT_ROWS = 2 * TOKEN_TILE + N_EXPERTS * RUN_ALIGN
SORT_CHUNK = 256
SORT_ALWAYS = 2 * TOKEN_TILE + SORT_CHUNK
TAIL_BITS = (MOE_BLOCK // RUN_ALIGN).bit_length() - 1
ZERO_RING = 8
PIECE_COLS = 1 + 2 * N_EXPERTS
XS_COLS = D_MODEL + 128
VMEM_LIMIT = 56 * 1024 * 1024

_F32 = jnp.float32
_BF16 = jnp.bfloat16


def _params(n_axes):
    return pltpu.CompilerParams(dimension_semantics=("arbitrary",) * n_axes,
                                vmem_limit_bytes=VMEM_LIMIT)


def _dot(a, b):
    return jnp.dot(a, b, preferred_element_type=_F32)


def _dot_nt(a, b):
    return lax.dot_general(a, b, (((1,), (1,)), ((), ())), preferred_element_type=_F32)


def _rms_factor(x, width):
    ss = jnp.sum(x * x, axis=-1, keepdims=True)
    return lax.rsqrt(ss * (1.0 / width) + EPS)


def _rms_scale(x, width):
    return x * _rms_factor(x, width)


def _pair_norm64(x, lane_lo):
    x2 = x * x
    s_all = jnp.sum(x2, axis=-1, keepdims=True)
    s_lo = jnp.sum(jnp.where(lane_lo, x2, 0.0), axis=-1, keepdims=True)
    ss = jnp.where(lane_lo, s_lo, s_all - s_lo)
    return x * lax.rsqrt(ss * (1.0 / 64.0) + EPS)


def _rope_kernel(pos_ref, freq_ref, c_ref, s_ref):
    ang = pos_ref[...] * freq_ref[...]
    c_ref[...] = jnp.cos(ang)
    s_ref[...] = jnp.sin(ang)


def _rope_tables(positions):
    half = MLA_ROPE // 2
    per_row = HEAD_LANES // half
    n = positions.size
    inv_freq = ROPE_THETA ** (-jnp.arange(half, dtype=_F32) / half)
    pos = jnp.repeat(positions.astype(_F32).reshape(n // per_row, per_row), half, axis=1)
    rows = n // per_row
    tile = min(rows, 1024)
    tab = pl.BlockSpec((tile, HEAD_LANES), lambda i: (i, 0))
    shp = jax.ShapeDtypeStruct((rows, HEAD_LANES), _F32)
    cos, sin = pl.pallas_call(
        _rope_kernel,
        grid=(rows // tile,),
        in_specs=[tab, pl.BlockSpec((1, HEAD_LANES), lambda i: (0, 0))],
        out_specs=[tab, tab],
        out_shape=[shp, shp],
        compiler_params=_params(1),
        name="rope_tables",
    )(pos, jnp.tile(inv_freq, per_row)[None])
    return cos.reshape(n, half), sin.reshape(n, half)


def _rotary_lanes(cos, sin):
    rows = cos.shape[0]
    pad = jnp.zeros((rows, HEAD_LANES - MLA_QK_DIM), _F32)
    c = jnp.concatenate([jnp.ones((rows, MLA_NOPE), _F32), cos, cos, pad], axis=1)
    s = jnp.concatenate([jnp.zeros((rows, MLA_NOPE), _F32), -sin, sin, pad], axis=1)
    return c, s


def _swa_tile(q, k, v, k_before, v_before, sink_ref, first_of_sequence):
    w = SWA_WINDOW
    oldest = jnp.where(first_of_sequence, 0, -w)
    lane_lo = lax.broadcasted_iota(jnp.int32, (1, 128), 1) < 64
    qpos = lax.broadcasted_iota(jnp.int32, (w, 2 * w), 0)
    kpos = lax.broadcasted_iota(jnp.int32, (w, 2 * w), 1) - w
    band = jnp.logical_and(kpos <= qpos, kpos > qpos - w)

    out_rows = []
    for jb in range(q.shape[0] // w):
        rs = slice(jb * w, (jb + 1) * w)
        if jb == 0:
            kprev, vprev = k_before, v_before
            visible = jnp.logical_and(band, kpos >= oldest)
        else:
            ps = slice((jb - 1) * w, jb * w)
            kprev, vprev = k[ps, :], v[ps, :]
            visible = band
        kk = jnp.concatenate([kprev, k[rs, :]], axis=0)
        vv = jnp.concatenate([vprev, v[rs, :]], axis=0)
        zero = jnp.zeros_like(kk)
        k_half = (jnp.where(lane_lo, kk, zero), jnp.where(lane_lo, zero, kk))
        q_both = jnp.concatenate([q[rs, 0:128], q[rs, 128:256]], axis=0)
        seen = jnp.concatenate([visible, visible], axis=0)
        first_pair = lax.broadcasted_iota(jnp.int32, (2 * w, 1), 0) < w
        outs = []
        for half in range(2):
            sink = jnp.where(first_pair, sink_ref[half], sink_ref[2 + half])
            s = _dot_nt(q_both, k_half[half])
            s = jnp.where(seen, s, NEG_BIG)
            m = jnp.maximum(jnp.max(s, axis=-1, keepdims=True), sink)
            p = jnp.exp(s - m)
            denom = jnp.sum(p, axis=-1, keepdims=True) + jnp.exp(sink - m)
            outs.append(_dot(p.astype(_BF16), vv) / denom)
        o_both = jnp.where(lane_lo, outs[0], outs[1])
        out_rows.append(jnp.concatenate([o_both[0:w, :], o_both[w:2 * w, :]], axis=1))
    return jnp.concatenate(out_rows, axis=0)


def _proj_kernel(x_ref, gmix_ref, win_ref, gcq_ref, wuq_ref, gckv_ref, wukv_ref, gq_ref, gqs_ref,
                 gk_ref, gks_ref, c_ref, s_ref, wpool_ref, pscale_ref, convw_ref, convb_ref, gsq_ref,
                 gsk_ref, sink_ref, qm_ref, km_ref, vm_ref, yb_ref, yc_ref, yd_ref, halo_ref, swa_halo_ref,
                 *, tiles_per_seq):
    tm = PROJ_TILE
    i = pl.program_id(0)
    seq_tile = i % tiles_per_seq

    @pl.when(i == 0)
    def _():
        halo_ref[...] = jnp.zeros_like(halo_ref)
        swa_halo_ref[...] = jnp.zeros_like(swa_halo_ref)

    x = x_ref[...]
    h = _rms_scale(x, D_MODEL) * gmix_ref[...]
    u = _dot(h.astype(_BF16), win_ref[...])

    c, s = _rotary_lanes(c_ref[...], s_ref[...])
    hw = MLA_HEADS * HEAD_LANES

    cq = _rms_scale(u[:, COL_CQ:COL_CQ + MLA_Q_RANK], MLA_Q_RANK) * gcq_ref[...]
    q = _dot(cq.astype(_BF16), wuq_ref[...])
    gq = (gq_ref[...] * (MLA_QK_DIM ** -0.5 * LOG2_E)) * c
    gqs = (gqs_ref[...] * (MLA_QK_DIM ** -0.5 * LOG2_E)) * s
    for hd in range(MLA_HEADS):
        sl = slice(hd * HEAD_LANES, (hd + 1) * HEAD_LANES)
        xq = q[:, sl]
        r = _rms_factor(xq, MLA_QK_DIM)
        qm_ref[:, sl] = ((xq * r) * gq + (q[:, hw + sl.start:hw + sl.stop] * r) * gqs).astype(_BF16)

    ckv = _rms_scale(u[:, COL_CKV:COL_CKV + MLA_KV_RANK], MLA_KV_RANK) * gckv_ref[...]
    kv = _dot(ckv.astype(_BF16), wukv_ref[...])
    krope = u[:, COL_KROPE:COL_KROPE + HEAD_LANES]
    krope_swap = u[:, COL_KROPE_SWAP:COL_KROPE_SWAP + HEAD_LANES]
    gk = gk_ref[...] * c
    gks = gks_ref[...] * s
    for hd in range(MLA_HEADS):
        sl = slice(hd * HEAD_LANES, (hd + 1) * HEAD_LANES)
        xk = kv[:, sl] + krope
        r = _rms_factor(xk, MLA_QK_DIM)
        km_ref[:, sl] = ((xk * r) * gk + (krope_swap * r) * gks).astype(_BF16)
    vm_ref[...] = kv[:, hw:hw + MLA_HEADS * MLA_V].astype(_BF16)

    lane_lo = lax.broadcasted_iota(jnp.int32, (1, HEAD_LANES), 1) < 64
    gsq = gsq_ref[...] * (SWA_HEAD_DIM ** -0.5)
    qs = jnp.concatenate(
        [(_pair_norm64(u[:, COL_SWA_Q + blk * 128:COL_SWA_Q + (blk + 1) * 128], lane_lo) * gsq).astype(_BF16)
         for blk in range(2)], axis=1)
    ks = (_pair_norm64(u[:, COL_SWA_K:COL_SWA_K + 128], lane_lo) * gsk_ref[...]).astype(_BF16)
    vs = u[:, COL_SWA_V:COL_SWA_V + 128].astype(_BF16)

    swa_halo = swa_halo_ref[...].astype(_BF16)
    yc_ref[...] = _swa_tile(qs, ks, vs, swa_halo[:, 0:128], swa_halo[:, 128:256], sink_ref,
                            seq_tile == 0).astype(_BF16)
    swa_halo_ref[:, 0:128] = ks[tm - SWA_WINDOW:, :].astype(_F32)
    swa_halo_ref[:, 128:256] = vs[tm - SWA_WINDOW:, :].astype(_F32)

    halo = jnp.where(jnp.broadcast_to(seq_tile, halo_ref.shape) == 0, 0.0, halo_ref[...])
    up = u[:, COL_POOL:COL_POOL + GROUP_W]
    ud = u[:, COL_CONV:COL_CONV + 3 * CONV_CH]

    b = jnp.concatenate([halo[:, 0:256], up], axis=0)
    w2 = b + pltpu.roll(b, 1, 0)
    w4 = w2 + pltpu.roll(w2, 2, 0)
    w8 = w4 + pltpu.roll(w4, 4, 0)
    w16 = w8 + pltpu.roll(w8, 8, 0)
    lane = lax.broadcasted_iota(jnp.int32, (1, 256), 1)
    win = jnp.where(lane < 64, w2, jnp.where(lane < 128, w4, jnp.where(lane < 192, w8, w16)))
    win = win[POOL_HALO:, :]
    width = jnp.where(lane < 64, 2, jnp.where(lane < 128, 4, jnp.where(lane < 192, 8, 16)))
    t = seq_tile * tm + lax.broadcasted_iota(jnp.int32, (tm, 1), 0)
    count = jnp.minimum(t + 1, width).astype(_F32)
    pooled = win / count - up
    yb_ref[...] = (_dot(pooled.astype(_BF16), wpool_ref[...]) * pscale_ref[...]).astype(_BF16)

    z = ud[:, 256:512] * ud[:, 512:768]
    zh = halo[:, 512:768] * halo[:, 768:1024]
    zb = jnp.concatenate([zh, z], axis=0)
    cw = convw_ref[...]
    conv = (pltpu.roll(zb, 2, 0)[POOL_HALO:, :] * cw[0:1, :]
            + pltpu.roll(zb, 1, 0)[POOL_HALO:, :] * cw[1:2, :]
            + z * cw[2:3, :])
    yd_ref[...] = (ud[:, 0:256] * (conv + convb_ref[...])).astype(_BF16)

    halo_ref[:, 0:256] = up[tm - POOL_HALO:, :]
    halo_ref[:, 256:1024] = ud[tm - POOL_HALO:, :]


def _proj_call(x, lw, tabs, seq_len):
    n = x.shape[0]
    tm = PROJ_TILE
    tiles_per_seq = seq_len // tm

    def full(a):
        nd = a.ndim
        return pl.BlockSpec(a.shape, lambda i, _nd=nd: (0,) * _nd)

    def rows(width):
        return pl.BlockSpec((tm, width), lambda i: (i, 0))

    consts = [lw["g_mix"], lw["w_in"], lw["g_cq"], lw["w_uq"], lw["g_ckv"], lw["w_ukv"],
              lw["g_q"], lw["g_q_swap"], lw["g_k"], lw["g_k_swap"]]
    consts2 = [lw["w_pool"], lw["pool_scale"], lw["conv_w"], lw["conv_b"], lw["g_sq"], lw["g_sk"]]
    out_widths = [512, 512, 256, 256, 256, 256]
    return pl.pallas_call(
        functools.partial(_proj_kernel, tiles_per_seq=tiles_per_seq),
        grid=(n // tm,),
        in_specs=[rows(D_MODEL)] + [full(a) for a in consts] + [rows(MLA_ROPE // 2)] * 2
                 + [full(a) for a in consts2] + [pl.BlockSpec(memory_space=pltpu.SMEM)],
        out_specs=[rows(w) for w in out_widths],
        out_shape=[jax.ShapeDtypeStruct((n, w), _BF16) for w in out_widths],
        scratch_shapes=[pltpu.VMEM((POOL_HALO, 1024), _F32), pltpu.VMEM((SWA_WINDOW, 256), _F32)],
        compiler_params=_params(1),
        name="proj_in",
    )(x, *consts, *tabs, *consts2, lw["sinks"])


def _mla_attn_kernel(q_ref, k_hbm, v_hbm, o_ref, kbuf, vbuf, sems, m_ref, l_ref, acc_ref, *, seq_len):
    tq, tk = ATTN_Q_TILE, ATTN_TILE
    assert tq == 2 * tk
    b = pl.program_id(0)
    qb = pl.program_id(1)
    row0 = b * seq_len

    def fetch(j, slot):
        rows = pl.ds(pl.multiple_of(row0 + j * tq, tq), tq)
        return (pltpu.make_async_copy(k_hbm.at[rows], kbuf.at[slot], sems.at[0, slot]),
                pltpu.make_async_copy(v_hbm.at[rows], vbuf.at[slot], sems.at[1, slot]))

    def start(j, slot):
        for cp in fetch(j, slot):
            cp.start()

    def wait(j, slot):
        for cp in fetch(j, slot):
            cp.wait()

    m_ref[...] = jnp.full_like(m_ref, NEG_BIG)
    l_ref[...] = jnp.zeros_like(l_ref)
    acc_ref[...] = jnp.zeros_like(acc_ref)
    lane_lo = lax.broadcasted_iota(jnp.int32, (1, HEAD_LANES), 1) < 64

    def accumulate(slot, half, rows, diagonal_from):
        nrows = rows.stop - rows.start
        keys = slice(half * tk, (half + 1) * tk)
        if diagonal_from is not None:
            visible = (lax.broadcasted_iota(jnp.int32, (nrows, tk), 1)
                       <= lax.broadcasted_iota(jnp.int32, (nrows, tk), 0) + (rows.start - diagonal_from))
        for pair in range(MLA_HEADS // 2):
            vblk = vbuf[slot, keys, pair * 128:(pair + 1) * 128]
            alphas = []
            pvs = []
            for sub in range(2):
                hd = 2 * pair + sub
                sl = slice(hd * HEAD_LANES, (hd + 1) * HEAD_LANES)
                s = _dot_nt(q_ref[rows, sl], kbuf[slot, keys, sl])
                if diagonal_from is not None:
                    s = jnp.where(visible, s, NEG_BIG)
                m_prev = m_ref[hd, rows, :]
                m_new = jnp.maximum(m_prev, jnp.max(s, axis=-1, keepdims=True))
                alpha = jnp.exp2(m_prev - m_new)
                p = jnp.exp2(s - jnp.tile(m_new, (1, tk // HEAD_LANES)))
                l_ref[hd, rows, :] = alpha * l_ref[hd, rows, :] + jnp.sum(p, axis=-1, keepdims=True)
                m_ref[hd, rows, :] = m_new
                alphas.append(alpha)
                pvs.append(_dot(p.astype(_BF16), vblk))
            psl = slice(pair * 128, (pair + 1) * 128)
            acc_ref[rows, psl] = (acc_ref[rows, psl] * jnp.where(lane_lo, alphas[0], alphas[1])
                                  + jnp.where(lane_lo, pvs[0], pvs[1]))

    everything = slice(0, tq)
    start(0, 0)

    def earlier_group(j, carry):
        slot = j % 2
        wait(j, slot)
        start(j + 1, 1 - slot)
        accumulate(slot, 0, everything, None)
        accumulate(slot, 1, everything, None)
        return carry

    lax.fori_loop(0, qb, earlier_group, 0)

    slot = qb % 2
    wait(qb, slot)
    accumulate(slot, 0, everything, 0)
    accumulate(slot, 1, slice(tk, tq), tk)

    for pair in range(MLA_HEADS // 2):
        psl = slice(pair * 128, (pair + 1) * 128)
        denom = jnp.where(lane_lo, l_ref[2 * pair], l_ref[2 * pair + 1])
        o_ref[:, psl] = (acc_ref[:, psl] / denom).astype(o_ref.dtype)


def _mla_attn_call(qm, km, vm, batch, seq_len):
    tq, tk = ATTN_Q_TILE, ATTN_TILE
    nq = seq_len // tq
    return pl.pallas_call(
        functools.partial(_mla_attn_kernel, seq_len=seq_len),
        grid=(batch, nq),
        in_specs=[pl.BlockSpec((tq, 512), lambda b, q: (b * nq + q, 0)),
                  pl.BlockSpec(memory_space=pl.ANY), pl.BlockSpec(memory_space=pl.ANY)],
        out_specs=pl.BlockSpec((tq, 256), lambda b, q: (b * nq + q, 0)),
        out_shape=jax.ShapeDtypeStruct((batch * seq_len, 256), _BF16),
        scratch_shapes=[pltpu.VMEM((2, tq, 512), _BF16), pltpu.VMEM((2, tq, 256), _BF16),
                        pltpu.SemaphoreType.DMA((2, 2)),
                        pltpu.VMEM((MLA_HEADS, tq, HEAD_LANES), _F32),
                        pltpu.VMEM((MLA_HEADS, tq, HEAD_LANES), _F32),
                        pltpu.VMEM((tq, 256), _F32)],
        compiler_params=_params(2),
        name="mla_attention",
    )(qm, km, vm)


def _memkv_kernel(mem_ref, gmem_ref, wkv_ref, gk_ref, k_ref, v_ref):
    m = _rms_scale(mem_ref[...], D_MODEL) * gmem_ref[...]
    kv = _dot(m.astype(_BF16), wkv_ref[...])
    lane_lo = lax.broadcasted_iota(jnp.int32, (1, 128), 1) < 64
    lane = lax.broadcasted_iota(jnp.int32, (1, 256), 1)
    k = jnp.concatenate([_pair_norm64(kv[:, 0:128], lane_lo), _pair_norm64(kv[:, 128:256], lane_lo)],
                        axis=1) * gk_ref[...]
    v = kv[:, 256:512]
    for hd in range(XA_HEADS):
        own = jnp.logical_and(lane >= hd * XA_HEAD_DIM, lane < (hd + 1) * XA_HEAD_DIM)
        k_ref[hd] = jnp.where(own, k, 0.0).astype(_BF16)
        v_ref[hd] = jnp.where(own, v, 0.0).astype(_BF16)


def _memkv_call(mem2d, g_mem, w_kv, g_k4, depth, batch):
    out = jax.ShapeDtypeStruct((depth, batch, XA_HEADS, MEM_LEN, 256), _BF16)
    ospec = pl.BlockSpec((None, None, XA_HEADS, MEM_LEN, 256), lambda l, b: (l, b, 0, 0, 0))
    return pl.pallas_call(
        _memkv_kernel,
        grid=(depth, batch),
        in_specs=[pl.BlockSpec((MEM_LEN, D_MODEL), lambda l, b: (b, 0)),
                  pl.BlockSpec((None, 1, D_MODEL), lambda l, b: (l, 0, 0)),
                  pl.BlockSpec((None, D_MODEL, 512), lambda l, b: (l, 0, 0)),
                  pl.BlockSpec((None, 1, 256), lambda l, b: (l, 0, 0))],
        out_specs=[ospec, ospec],
        out_shape=[out, out],
        compiler_params=_params(2),
        name="memory_kv",
    )(mem2d, g_mem, w_kv, g_k4)


def _mix_kernel(ya_ref, yb_ref, yc_ref, yd_ref, x_ref, gmo_ref, wmo_ref, gxa_ref, wq_ref, gxq_ref,
                kx_ref, vx_ref, wo_ref, gffn_ref, wrh_ref, wrl_ref, br_ref, ltri_ref, utri_ref,
                x2_ref, h3_ref, meta_ref, cnt_ref):
    tm = MIX_TILE
    gmo = gmo_ref[...]
    lane_lo = lax.broadcasted_iota(jnp.int32, (1, 128), 1) < 64

    def rows_to_logits(rs):
        parts = []
        for g, ref in enumerate((ya_ref, yb_ref, yc_ref, yd_ref)):
            yg = ref[rs, :].astype(_F32)
            parts.append((_rms_scale(yg, GROUP_W) * gmo[:, g * 256:(g + 1) * 256]).astype(_BF16))
        y = jnp.concatenate(parts, axis=1)
        x1 = x_ref[rs, :] + _dot(y, wmo_ref[...])

        h = (_rms_scale(x1, D_MODEL) * gxa_ref[...]).astype(_BF16)
        q = _dot(h, wq_ref[...])
        qn = jnp.concatenate([_pair_norm64(q[:, 0:128], lane_lo), _pair_norm64(q[:, 128:256], lane_lo)],
                             axis=1)
        qn = (qn * gxq_ref[...] * (XA_HEAD_DIM ** -0.5)).astype(_BF16)
        o = jnp.zeros((x1.shape[0], 256), _F32)
        for hd in range(XA_HEADS):
            s = _dot_nt(qn, kx_ref[hd])
            m = jnp.max(s, axis=-1, keepdims=True)
            p = jnp.exp(s - m)
            denom = jnp.sum(p, axis=-1, keepdims=True)
            o = o + _dot(p.astype(_BF16), vx_ref[hd]) / denom
        x2 = x1 + _dot(o.astype(_BF16), wo_ref[...])
        x2_ref[rs, :] = x2

        h3 = _rms_scale(x2, D_MODEL) * gffn_ref[...]
        h3_hi = h3.astype(_BF16)
        h3_lo = (h3 - h3_hi.astype(_F32)).astype(_BF16)
        h3_ref[rs, :] = h3_hi
        return (_dot(h3_hi, wrh_ref[...]) + (_dot(h3_hi, wrl_ref[...]) + _dot(h3_lo, wrh_ref[...]))
                + br_ref[...])

    all_logits = rows_to_logits(slice(0, tm))
    for st in range(tm // TOKEN_TILE):
        meta, counts = _route(all_logits[st * TOKEN_TILE:(st + 1) * TOKEN_TILE, :], ltri_ref, utri_ref)
        meta_ref[st * TOKEN_TILE:(st + 1) * TOKEN_TILE, :] = meta
        cnt_ref[st * 8:(st + 1) * 8, :] = jnp.broadcast_to(counts, (8, 128))


def _route(logits, ltri_ref, utri_ref):
    lane = lax.broadcasted_iota(jnp.int32, (TOKEN_TILE, 128), 1).astype(_F32)
    far = 1e9
    is_group = jnp.logical_and(lane >= N_EXPERTS, lane < N_EXPERTS + N_EXPERT_GROUPS)
    gl = jnp.where(is_group, logits, -jnp.inf)
    gmax = jnp.max(gl, axis=-1, keepdims=True)
    gidx = jnp.min(jnp.where(gl == gmax, lane, far), axis=-1, keepdims=True) - N_EXPERTS
    g_w = 1.0 / jnp.sum(jnp.where(is_group, jnp.exp(logits - gmax), 0.0), axis=-1, keepdims=True)
    in_group = jnp.logical_and(lane < N_EXPERTS, jnp.floor(lane * (1.0 / EXPERTS_PER_GROUP)) == gidx)
    el = jnp.where(in_group, logits, -jnp.inf)
    emax = jnp.max(el, axis=-1, keepdims=True)
    ep = jnp.where(in_group, jnp.exp(logits - emax), 0.0)
    prob = jnp.where(in_group, ep / jnp.sum(ep, axis=-1, keepdims=True), -1.0)
    p1 = jnp.max(prob, axis=-1, keepdims=True)
    e0 = jnp.min(jnp.where(prob == p1, lane, far), axis=-1, keepdims=True)
    prob2 = jnp.where(lane == e0, -1.0, prob)
    p2 = jnp.max(prob2, axis=-1, keepdims=True)
    e1 = jnp.min(jnp.where(prob2 == p2, lane, far), axis=-1, keepdims=True)
    w0 = g_w * (p1 / (p1 + p2))
    w1 = g_w * (p2 / (p1 + p2))

    onehot = jnp.where(jnp.logical_or(lane == e0, lane == e1), 1.0, 0.0)
    prefix = _dot(ltri_ref[...], onehot.astype(_BF16))
    counts = jnp.sum(onehot, axis=0, keepdims=True)
    units = jnp.floor((counts + (RUN_ALIGN - 1)) * (1.0 / RUN_ALIGN))
    offs = _dot(jnp.broadcast_to(units, (8, 128)).astype(_BF16), utri_ref[...])[0:1, :] * RUN_ALIGN
    where_to = prefix + offs
    pos0 = jnp.sum(jnp.where(lane == e0, where_to, 0.0), axis=-1, keepdims=True)
    pos1 = jnp.sum(jnp.where(lane == e1, where_to, 0.0), axis=-1, keepdims=True)

    meta = jnp.where(lane == 0, e0,
           jnp.where(lane == 1, e1,
           jnp.where(lane == 2, w0,
           jnp.where(lane == 3, w1,
           jnp.where(lane == 4, pos0,
           jnp.where(lane == 5, pos1, 0.0))))))
    return meta, counts


def _mix_call(ya, yb, yc, yd, x, lw, kx, vx, seq_len):
    n = x.shape[0]
    tm = MIX_TILE
    tiles_per_seq = seq_len // tm

    def full(a):
        nd = a.ndim
        return pl.BlockSpec(a.shape, lambda i, _nd=nd: (0,) * _nd)

    def rows(width):
        return pl.BlockSpec((tm, width), lambda i: (i, 0))

    kvspec = pl.BlockSpec((None, XA_HEADS, MEM_LEN, 256), lambda i: (i // tiles_per_seq, 0, 0, 0))
    consts_a = [lw["g_mo"], lw["w_mo"], lw["g_xa"], lw["xa_w_q"], lw["g_xq"]]
    consts_b = [lw["xa_w_o"], lw["g_ffn"], lw["w_router_hi"], lw["w_router_lo"], lw["b_router"],
                lw["ltri"], lw["utri"]]
    ntiles = n // tm
    return pl.pallas_call(
        _mix_kernel,
        grid=(ntiles,),
        in_specs=[rows(256)] * 4 + [rows(D_MODEL)] + [full(a) for a in consts_a]
                 + [kvspec, kvspec] + [full(a) for a in consts_b],
        out_specs=[rows(D_MODEL), rows(D_MODEL), rows(128),
                   pl.BlockSpec((8 * (tm // TOKEN_TILE), 128), lambda i: (i, 0))],
        out_shape=[jax.ShapeDtypeStruct((n, D_MODEL), _F32), jax.ShapeDtypeStruct((n, D_MODEL), _BF16),
                   jax.ShapeDtypeStruct((n, 128), _F32),
                   jax.ShapeDtypeStruct((n // TOKEN_TILE * 8, 128), _F32)],
        compiler_params=_params(1),
        name="mix_xattn_router",
    )(ya, yb, yc, yd, x, *consts_a, kx, vx, *consts_b)


def _move_runs(tab_ref, t, live, local_ref, remote_ref, sems, *, to_remote, wait):
    for bit in range(RUN_BITS):
        rows = RUN_ALIGN << bit
        base = (t * RUN_BITS + bit) * PIECE_COLS
        count = jnp.where(live, tab_ref[base], 0)

        def one(p, carry, base=base, rows=rows, bit=bit):
            loc = pl.multiple_of(tab_ref[base + 1 + p], RUN_ALIGN)
            rem = pl.multiple_of(tab_ref[base + 1 + N_EXPERTS + p], RUN_ALIGN)
            lsl = local_ref.at[pl.ds(loc, rows)]
            rsl = remote_ref.at[pl.ds(rem, rows)]
            src, dst = (lsl, rsl) if to_remote else (rsl, lsl)
            copy = pltpu.make_async_copy(src, dst, sems.at[bit, p])
            if wait:
                copy.wait()
            else:
                copy.start()
            return carry

        lax.fori_loop(0, count, one, 0)


def _selection(meta, first, rows):
    r = first + lax.broadcasted_iota(jnp.int32, (TOKEN_TILE, rows), 1)
    pos0 = meta[:, 4:5].astype(jnp.int32)
    pos1 = meta[:, 5:6].astype(jnp.int32)
    return jnp.where(r == pos0, 1.0, 0.0).astype(_BF16), jnp.where(r == pos1, 1.0, 0.0).astype(_BF16)


def _zero_rows(count, row_of, rows, zero_ref, xs_ref, sems):
    def copy(i):
        dst = xs_ref.at[pl.ds(pl.multiple_of(row_of(i), RUN_ALIGN), rows)]
        return pltpu.make_async_copy(zero_ref.at[pl.ds(0, rows)], dst, sems.at[i % ZERO_RING])

    def issue(i, carry):
        @pl.when(i >= ZERO_RING)
        def _():
            copy(i - ZERO_RING).wait()
        copy(i).start()
        return carry

    def drain(i, carry):
        copy(i).wait()
        return carry

    lax.fori_loop(0, count, issue, 0)
    lax.fori_loop(jnp.maximum(count - ZERO_RING, 0), count, drain, 0)


def _dispatch_kernel(tab_ref, used_ref, ztab_ref, meta_ref, h3_ref, xs_ref, buf0_ref, buf1_ref, zero_ref,
                     sems, zero_sems, free_sems, *, ntiles, n_blocks):
    t = pl.program_id(0)
    last = ntiles - 1
    bufs = (buf0_ref, buf1_ref)
    used = used_ref[jnp.minimum(t, last)]

    @pl.when(t == 0)
    def _():
        zero_ref[...] = jnp.zeros_like(zero_ref)
        for bit in range(TAIL_BITS):
            base = bit * (1 + N_EXPERTS)
            _zero_rows(ztab_ref[base], lambda i, base=base: ztab_ref[base + 1 + i], RUN_ALIGN << bit,
                       zero_ref, xs_ref, zero_sems.at[bit])

    live_blocks = ztab_ref[TAIL_BITS * (1 + N_EXPERTS)]
    per_step = -(-n_blocks // (ntiles + 1))

    def free_block_copy(step, j):
        blk = live_blocks + step * per_step + j
        dst = xs_ref.at[pl.ds(pl.multiple_of(blk * MOE_BLOCK, MOE_BLOCK), MOE_BLOCK)]
        return blk < n_blocks, pltpu.make_async_copy(zero_ref, dst, free_sems.at[step % 2, j])

    for j in range(per_step):
        exists, copy = free_block_copy(t - 1, j)
        pl.when(jnp.logical_and(t >= 1, exists))(copy.wait)
    for j in range(per_step):
        exists, copy = free_block_copy(t, j)
        pl.when(jnp.logical_and(t <= ntiles, exists))(copy.start)

    def step(slot):
        mine, other = bufs[slot], bufs[1 - slot]
        _move_runs(tab_ref, jnp.clip(t - 2, 0, last), t >= 2, mine, xs_ref, sems.at[slot],
                   to_remote=True, wait=True)
        _move_runs(tab_ref, jnp.clip(t - 1, 0, last), jnp.logical_and(t >= 1, t <= ntiles), other, xs_ref,
                   sems.at[1 - slot], to_remote=True, wait=False)

        meta = meta_ref[...]
        lane = lax.broadcasted_iota(jnp.int32, (TOKEN_TILE, 128), 1)
        cols = []
        for k in range(2):
            w = meta[:, 2 + k:3 + k]
            hi = w.astype(_BF16).astype(_F32)
            mid = (w - hi).astype(_BF16).astype(_F32)
            lo = (w - hi) - mid
            cols.append(jnp.where(lane == 0, hi, jnp.where(lane == 1, mid, jnp.where(lane == 2, lo, 0.0)))
                        .astype(_BF16))

        def sort_rows(first, rows):
            sel0, sel1 = _selection(meta, first, rows)
            tn = (((0,), (0,)), ((), ()))
            mine[first:first + rows, 0:D_MODEL] = lax.dot_general(
                sel0 + sel1, h3_ref[...], tn, preferred_element_type=_F32).astype(_BF16)
            mine[first:first + rows, D_MODEL:XS_COLS] = (
                lax.dot_general(sel0, cols[0], tn, preferred_element_type=_F32)
                + lax.dot_general(sel1, cols[1], tn, preferred_element_type=_F32)).astype(_BF16)

        sort_rows(0, SORT_ALWAYS)
        for first in range(SORT_ALWAYS, SORT_ROWS, SORT_CHUNK):
            pl.when(used > first)(functools.partial(sort_rows, first, SORT_CHUNK))

    for slot in range(2):
        pl.when(t % 2 == slot)(functools.partial(step, slot))


def _dispatch_call(tab, used, ztab, meta, h3):
    n = h3.shape[0]
    tm = TOKEN_TILE
    ntiles = n // tm
    xs_rows = _xs_rows(n)
    tile = lambda i, tab, used, ztab: (jnp.minimum(i, ntiles - 1), 0)
    grid_spec = pltpu.PrefetchScalarGridSpec(
        num_scalar_prefetch=3,
        grid=(ntiles + 2,),
        in_specs=[pl.BlockSpec((tm, 128), tile),
                  pl.BlockSpec((tm, D_MODEL), tile)],
        out_specs=pl.BlockSpec(memory_space=pl.ANY),
        scratch_shapes=[pltpu.VMEM((SORT_ROWS, XS_COLS), _BF16), pltpu.VMEM((SORT_ROWS, XS_COLS), _BF16),
                        pltpu.VMEM((MOE_BLOCK, XS_COLS), _BF16),
                        pltpu.SemaphoreType.DMA((2, RUN_BITS, N_EXPERTS)),
                        pltpu.SemaphoreType.DMA((TAIL_BITS, ZERO_RING)),
                        pltpu.SemaphoreType.DMA((2, -(-(xs_rows // MOE_BLOCK) // (ntiles + 1))))],
    )
    return pl.pallas_call(
        functools.partial(_dispatch_kernel, ntiles=ntiles, n_blocks=xs_rows // MOE_BLOCK),
        grid_spec=grid_spec,
        out_shape=jax.ShapeDtypeStruct((xs_rows, XS_COLS), _BF16),
        compiler_params=_params(1),
        name="moe_dispatch",
    )(tab, used, ztab, meta, h3)


def _ffn_kernel(be_ref, nv_ref, xs_ref, wg_ref, wu_ref, wd_ref, ys_ref, wgu_ref, wdn_ref):
    i = pl.program_id(0)

    @pl.when(jnp.logical_or(i == 0, be_ref[i] != be_ref[jnp.maximum(i - 1, 0)]))
    def _():
        wgu_ref[:, 0:D_EXPERT] = wg_ref[...].astype(_BF16)
        wgu_ref[:, D_EXPERT:2 * D_EXPERT] = wu_ref[...].astype(_BF16)
        wdn_ref[...] = wd_ref[...].astype(_BF16)

    @pl.when(i < nv_ref[0])
    def _():
        xb = xs_ref[:, 0:D_MODEL]
        wcols = xs_ref[:, D_MODEL:XS_COLS].astype(_F32)
        wt = wcols[:, 0:1] + wcols[:, 1:2] + wcols[:, 2:3]
        gu = _dot(xb, wgu_ref[...])
        g = gu[:, 0:D_EXPERT]
        a = (g * jax.nn.sigmoid(g)) * gu[:, D_EXPERT:2 * D_EXPERT]
        y = _dot(a.astype(_BF16), wdn_ref[...])
        ys_ref[...] = (y * wt).astype(ys_ref.dtype)

    @pl.when(i >= nv_ref[0])
    def _():
        ys_ref[...] = jnp.zeros_like(ys_ref)


def _ffn_call(blk_expert, n_valid, xs, w_gate, w_up, w_down, layer):
    n_blocks = xs.shape[0] // MOE_BLOCK
    wspec = lambda shape: pl.BlockSpec((None, None) + shape, lambda i, be, nv: (layer, be[i], 0, 0))
    last_live = lambda i, be, nv: (jnp.minimum(i, nv[0] - 1), 0)
    grid_spec = pltpu.PrefetchScalarGridSpec(
        num_scalar_prefetch=2,
        grid=(n_blocks,),
        in_specs=[pl.BlockSpec((MOE_BLOCK, XS_COLS), last_live),
                  wspec((D_MODEL, D_EXPERT)), wspec((D_MODEL, D_EXPERT)), wspec((D_EXPERT, D_MODEL))],
        out_specs=pl.BlockSpec((MOE_BLOCK, D_MODEL), lambda i, be, nv: (i, 0)),
        scratch_shapes=[pltpu.VMEM((D_MODEL, 2 * D_EXPERT), _BF16), pltpu.VMEM((D_EXPERT, D_MODEL), _BF16)],
    )
    return pl.pallas_call(
        _ffn_kernel,
        grid_spec=grid_spec,
        out_shape=jax.ShapeDtypeStruct((xs.shape[0], D_MODEL), _BF16),
        compiler_params=_params(1),
        name="moe_experts",
    )(blk_expert, n_valid, xs, w_gate, w_up, w_down)


def _combine_kernel(tab_ref, used_ref, meta_ref, x2_ref, ys_ref, out_ref, buf0_ref, buf1_ref, sems,
                    *, ntiles):
    t = pl.program_id(0)
    last = ntiles - 1
    bufs = (buf0_ref, buf1_ref)
    used = used_ref[jnp.clip(t - 1, 0, last)]

    @pl.when(t == 0)
    def _():
        buf0_ref[...] = jnp.zeros_like(buf0_ref)
        buf1_ref[...] = jnp.zeros_like(buf1_ref)

    def step(slot):
        mine, other = bufs[slot], bufs[1 - slot]
        _move_runs(tab_ref, jnp.clip(t - 1, 0, last), t >= 1, other, ys_ref, sems.at[1 - slot],
                   to_remote=False, wait=True)
        _move_runs(tab_ref, jnp.minimum(t, last), t <= last, mine, ys_ref, sems.at[slot],
                   to_remote=False, wait=False)
        meta = meta_ref[...]

        def gathered(first, rows):
            sel0, sel1 = _selection(meta, first, rows)
            return _dot(sel0 + sel1, other[first:first + rows, :])

        out_ref[...] = x2_ref[...] + gathered(0, SORT_ALWAYS)
        for first in range(SORT_ALWAYS, SORT_ROWS, SORT_CHUNK):
            @pl.when(used > first)
            def _(first=first):
                out_ref[...] += gathered(first, SORT_CHUNK)

    for slot in range(2):
        pl.when(t % 2 == slot)(functools.partial(step, slot))


def _combine_call(tab, used, meta, x2, ys):
    n = x2.shape[0]
    tm = TOKEN_TILE
    ntiles = n // tm
    tile = lambda i, tab, used: (jnp.maximum(i - 1, 0), 0)
    grid_spec = pltpu.PrefetchScalarGridSpec(
        num_scalar_prefetch=2,
        grid=(ntiles + 1,),
        in_specs=[pl.BlockSpec((tm, 128), tile),
                  pl.BlockSpec((tm, D_MODEL), tile),
                  pl.BlockSpec(memory_space=pl.ANY)],
        out_specs=pl.BlockSpec((tm, D_MODEL), tile),
        scratch_shapes=[pltpu.VMEM((SORT_ROWS, D_MODEL), _BF16), pltpu.VMEM((SORT_ROWS, D_MODEL), _BF16),
                        pltpu.SemaphoreType.DMA((2, RUN_BITS, N_EXPERTS))],
    )
    return pl.pallas_call(
        functools.partial(_combine_kernel, ntiles=ntiles),
        grid_spec=grid_spec,
        out_shape=jax.ShapeDtypeStruct((n, D_MODEL), _F32),
        compiler_params=_params(1),
        name="moe_combine",
    )(tab, used, meta, x2, ys)


def _moe_tables(cnt, n_tokens):
    ntiles = n_tokens // TOKEN_TILE
    counts = cnt.reshape(ntiles, 8, 128)[:, 0, :N_EXPERTS].astype(jnp.int32)
    units = (counts + RUN_ALIGN - 1) // RUN_ALIGN
    padded = units * RUN_ALIGN
    local = jnp.cumsum(padded, axis=1) - padded
    total = jnp.sum(padded, axis=0)
    total_blk = ((total + MOE_BLOCK - 1) // MOE_BLOCK) * MOE_BLOCK
    ends = jnp.cumsum(total_blk)
    starts = ends - total_blk
    remote = starts[None, :] + jnp.cumsum(padded, axis=0) - padded
    bits = jnp.arange(RUN_BITS, dtype=jnp.int32)[None, :, None]
    has = (units[:, None, :] >> bits) & 1
    done = (units[:, None, :] & ((1 << bits) - 1)) * RUN_ALIGN
    rank = jnp.cumsum(has, axis=2) - 1
    place = jnp.logical_and(rank[..., None] == jnp.arange(N_EXPERTS, dtype=jnp.int32), has[..., None] == 1)
    compact = lambda rows: jnp.sum(jnp.where(place, rows[..., None], 0), axis=2)
    tab = jnp.concatenate([jnp.sum(has, axis=2, keepdims=True), compact(local[:, None, :] + done),
                           compact(remote[:, None, :] + done)], axis=-1).reshape(-1).astype(jnp.int32)
    n_blocks = _xs_rows(n_tokens) // MOE_BLOCK
    blk_start = jnp.arange(n_blocks, dtype=jnp.int32) * MOE_BLOCK
    blk_expert = jnp.minimum(jnp.sum((ends[None, :] <= blk_start[:, None]).astype(jnp.int32), axis=1),
                             N_EXPERTS - 1)
    n_valid = (ends[-1] // MOE_BLOCK).astype(jnp.int32).reshape(1)
    used = jnp.sum(padded, axis=1).astype(jnp.int32)
    tail_units = (total_blk - total) // RUN_ALIGN
    tail_bits = jnp.arange(TAIL_BITS, dtype=jnp.int32)[:, None]
    tail_has = (tail_units[None, :] >> tail_bits) & 1
    tail_rows = starts + total + (tail_units[None, :] & ((1 << tail_bits) - 1)) * RUN_ALIGN
    tail_rank = jnp.cumsum(tail_has, axis=1) - 1
    tail_place = jnp.logical_and(tail_rank[..., None] == jnp.arange(N_EXPERTS, dtype=jnp.int32),
                                 tail_has[..., None] == 1)
    tail_list = jnp.sum(jnp.where(tail_place, tail_rows[..., None], 0), axis=1)
    ztab = jnp.concatenate([jnp.concatenate([jnp.sum(tail_has, axis=1, keepdims=True), tail_list],
                                            axis=1).reshape(-1), n_valid]).astype(jnp.int32)
    return tab, used, ztab, blk_expert, n_valid


def _xs_rows(n_tokens):
    ntiles = n_tokens // TOKEN_TILE
    worst = 2 * n_tokens + ntiles * N_EXPERTS * (RUN_ALIGN - 1) + N_EXPERTS * (MOE_BLOCK - 1)
    return ((worst + MOE_BLOCK - 1) // MOE_BLOCK) * MOE_BLOCK


def _pad_last(a, width):
    return jnp.pad(a, [(0, 0)] * (a.ndim - 1) + [(0, width - a.shape[-1])])


def _swap_mid_heads(a, axis):
    shape = a.shape
    a = a.reshape(shape[:axis] + (4, 64) + shape[axis + 1:])
    a = jnp.take(a, jnp.asarray([0, 2, 1, 3]), axis=axis)
    return a.reshape(shape)


def _layer_weights(l, p):
    w_in = p["w_in"][l]
    a, b = w_in[:, :A_COLS], w_in[:, A_COLS:A_COLS + B_COLS]
    c = w_in[:, A_COLS + B_COLS:A_COLS + B_COLS + C_COLS]
    d = w_in[:, A_COLS + B_COLS + C_COLS:]
    zeros = lambda w: jnp.zeros((D_MODEL, w), _F32)
    half = MLA_ROPE // 2
    k_rope = a[:, 384:416]
    k_rope_swap = jnp.concatenate([k_rope[:, half:], k_rope[:, :half]], axis=1)
    w_in_p = jnp.concatenate([
        a[:, :384], zeros(64), k_rope, zeros(32), zeros(64), k_rope_swap, zeros(32), b,
        _swap_mid_heads(c[:, :256], 1), c[:, 256:384], c[:, 384:512], d], axis=1).astype(_BF16)
    assert w_in_p.shape[1] == IN_COLS_PADDED

    w_uq = p["mla_w_uq"][l].reshape(MLA_Q_RANK, MLA_HEADS, MLA_QK_DIM)
    w_uq_swap = jnp.concatenate([jnp.zeros_like(w_uq[..., :MLA_NOPE]), w_uq[..., MLA_NOPE + half:],
                                 w_uq[..., MLA_NOPE:MLA_NOPE + half]], axis=-1)
    w_uq = jnp.concatenate([_pad_last(w_uq, HEAD_LANES).reshape(MLA_Q_RANK, -1),
                            _pad_last(w_uq_swap, HEAD_LANES).reshape(MLA_Q_RANK, -1)], axis=1).astype(_BF16)

    def swap_gain(g):
        return _pad_last(jnp.concatenate([jnp.zeros((MLA_NOPE,), _F32), g[MLA_NOPE + half:],
                                          g[MLA_NOPE:MLA_NOPE + half]]), HEAD_LANES)[None]
    w_ukv = p["mla_w_ukv"][l].reshape(MLA_KV_RANK, MLA_HEADS, MLA_NOPE + MLA_V)
    k_nope = _pad_last(w_ukv[..., :MLA_NOPE], HEAD_LANES).reshape(MLA_KV_RANK, MLA_HEADS * HEAD_LANES)
    v_part = w_ukv[..., MLA_NOPE:].reshape(MLA_KV_RANK, MLA_HEADS * MLA_V)
    w_ukv_p = jnp.concatenate([k_nope, v_part], axis=1).astype(_BF16)

    eye = jnp.eye(len(POOL_WINDOWS), dtype=_F32)
    w_pool = jnp.einsum("gcd,gh->gchd", p["pool_w"][l], eye).reshape(256, 256).astype(_BF16)

    g_mo = p["mix_out_norm"][l]
    g_mo = jnp.concatenate([g_mo[:512], _swap_mid_heads(g_mo[512:768], 0), g_mo[768:]])
    w_mo = p["w_mix_out"][l]
    w_mo = jnp.concatenate([w_mo[:512], _swap_mid_heads(w_mo[512:768], 0), w_mo[768:]], axis=0)

    w_router = _pad_last(jnp.concatenate([p["w_expert"][l], p["w_group"][l]], axis=1), 128)
    w_router_hi = w_router.astype(_BF16)
    w_router_lo = (w_router - w_router_hi.astype(_F32)).astype(_BF16)
    b_router = jnp.concatenate([p["b_expert"][l], p["b_group"][l]])
    tm = TOKEN_TILE
    ltri = (jnp.arange(tm)[None, :] < jnp.arange(tm)[:, None]).astype(_BF16)
    utri = (jnp.arange(128)[:, None] < jnp.arange(128)[None, :]).astype(_BF16)
    tile2 = lambda g: jnp.concatenate([g, g])[None]
    return dict(
        g_mix=p["norm_mix"][l][None], w_in=w_in_p,
        g_cq=p["mla_g_cq"][l][None], w_uq=w_uq, g_ckv=p["mla_g_ckv"][l][None], w_ukv=w_ukv_p,
        g_q=_pad_last(p["mla_g_q"][l], HEAD_LANES)[None], g_k=_pad_last(p["mla_g_k"][l], HEAD_LANES)[None],
        g_q_swap=swap_gain(p["mla_g_q"][l]), g_k_swap=swap_gain(p["mla_g_k"][l]),
        w_pool=w_pool, pool_scale=p["pool_scale"][l][None], conv_w=p["conv_w"][l],
        conv_b=p["conv_b"][l][None], g_sq=tile2(p["swa_g_q"][l]), g_sk=tile2(p["swa_g_k"][l]),
        sinks=jnp.take(p["swa_sinks"][l], jnp.asarray([0, 2, 1, 3])),
        g_mo=g_mo[None], w_mo=w_mo.astype(_BF16), g_xa=p["norm_xa"][l][None],
        xa_w_q=p["xa_w_q"][l].astype(_BF16), g_xq=jnp.tile(p["xa_g_q"][l], 4)[None],
        xa_w_o=p["xa_w_o"][l].astype(_BF16), g_ffn=p["norm_ffn"][l][None],
        w_router_hi=w_router_hi, w_router_lo=w_router_lo, b_router=_pad_last(b_router, 128)[None],
        ltri=ltri, utri=utri,
    )


def kernel(x, mem, positions, norm_mix, w_in, mla_g_cq, mla_w_uq, mla_g_ckv, mla_w_ukv, mla_g_q, mla_g_k, pool_w, pool_scale, swa_g_q, swa_g_k, swa_sinks, conv_w, conv_b, mix_out_norm, w_mix_out, norm_xa, norm_mem, xa_w_q, xa_w_kv, xa_g_q, xa_g_k, xa_w_o, norm_ffn, w_group, b_group, w_expert, b_expert, w_gate, w_up, w_down):
    p = dict(norm_mix=norm_mix, w_in=w_in, mla_g_cq=mla_g_cq, mla_w_uq=mla_w_uq, mla_g_ckv=mla_g_ckv,
             mla_w_ukv=mla_w_ukv, mla_g_q=mla_g_q, mla_g_k=mla_g_k, pool_w=pool_w, pool_scale=pool_scale,
             swa_g_q=swa_g_q, swa_g_k=swa_g_k, swa_sinks=swa_sinks, conv_w=conv_w, conv_b=conv_b,
             mix_out_norm=mix_out_norm, w_mix_out=w_mix_out, norm_xa=norm_xa, xa_w_q=xa_w_q,
             xa_g_q=xa_g_q, xa_w_o=xa_w_o, norm_ffn=norm_ffn, w_group=w_group, b_group=b_group,
             w_expert=w_expert, b_expert=b_expert)
    batch, seq_len, _ = x.shape
    depth = w_in.shape[0]
    n = batch * seq_len
    assert seq_len % MIX_TILE == 0 and MIX_TILE % TOKEN_TILE == 0
    assert seq_len % PROJ_TILE == 0 and seq_len % ATTN_Q_TILE == 0
    assert mem.shape[1] == MEM_LEN

    xf = x.reshape(n, D_MODEL)
    tabs = _rope_tables(positions)
    kx, vx = _memkv_call(mem.reshape(batch * MEM_LEN, D_MODEL), norm_mem[:, None, :],
                         xa_w_kv.astype(_BF16), jnp.tile(xa_g_k, (1, 4))[:, None, :], depth, batch)
    for l in range(depth):
        lw = _layer_weights(l, p)
        qm, km, vm, yb, yc, yd = _proj_call(xf, lw, tabs, seq_len)
        ya = _mla_attn_call(qm, km, vm, batch, seq_len)
        x2, h3, meta, cnt = _mix_call(ya, yb, yc, yd, xf, lw, kx[l], vx[l], seq_len)
        tab, used, ztab, blk_expert, n_valid = _moe_tables(cnt, n)
        xs = _dispatch_call(tab, used, ztab, meta, h3)
        ys = _ffn_call(blk_expert, n_valid, xs, w_gate, w_up, w_down, l)
        xf = _combine_call(tab, used, meta, x2, ys)
    return xf.reshape(batch, seq_len, D_MODEL)
```

```python
import functools

import jax
import jax.numpy as jnp
from jax import lax
from jax.experimental import pallas as pl
from jax.experimental.pallas import tpu as pltpu

EPS = 1e-6
NEG_BIG = -1e30
LOG2_E = 1.4426950408889634
ROPE_THETA = 10000.0

D_MODEL = 1024
MEM_LEN = 256
GROUP_W = 256

MLA_HEADS = 4
MLA_Q_RANK = 256
MLA_KV_RANK = 128
MLA_NOPE = 64
MLA_ROPE = 32
MLA_QK_DIM = MLA_NOPE + MLA_ROPE
MLA_V = 64
HEAD_LANES = 128

POOL_WINDOWS = (2, 4, 8, 16)
POOL_HALO = 16

SWA_HEADS = 4
SWA_KV_HEADS = 2
SWA_HEAD_DIM = 64
SWA_WINDOW = 128

CONV_CH = 256

XA_HEADS = 4
XA_HEAD_DIM = 64

N_EXPERT_GROUPS = 4
EXPERTS_PER_GROUP = 8
N_EXPERTS = 32
D_EXPERT = 256
MOE_BLOCK = 1024

A_COLS = MLA_Q_RANK + MLA_KV_RANK + MLA_ROPE
B_COLS = GROUP_W
C_COLS = (SWA_HEADS + 2 * SWA_KV_HEADS) * SWA_HEAD_DIM
COL_CQ, COL_CKV, COL_KROPE, COL_KROPE_SWAP = 0, 256, 384, 512
COL_POOL, COL_SWA_Q, COL_SWA_K, COL_SWA_V, COL_CONV = 640, 896, 1152, 1280, 1408
IN_COLS_PADDED = COL_CONV + 3 * CONV_CH

TOKEN_TILE = 512
PROJ_TILE = 512
MIX_TILE = 1024
ATTN_TILE = 512
ATTN_Q_TILE = 1024
RUN_ALIGN = 16
RUN_BITS = 6
SORT_ROWS = 2 * TOKEN_TILE + N_EXPERTS * RUN_ALIGN
SORT_CHUNK = 256
SORT_ALWAYS = 2 * TOKEN_TILE + SORT_CHUNK
TAIL_BITS = (MOE_BLOCK // RUN_ALIGN).bit_length() - 1
ZERO_RING = 8
PIECE_COLS = 1 + 2 * N_EXPERTS
XS_COLS = D_MODEL + 128
VMEM_LIMIT = 56 * 1024 * 1024

_F32 = jnp.float32
_BF16 = jnp.bfloat16


def _params(n_axes):
    return pltpu.CompilerParams(dimension_semantics=("arbitrary",) * n_axes,
                                vmem_limit_bytes=VMEM_LIMIT)


def _dot(a, b):
    return jnp.dot(a, b, preferred_element_type=_F32)


def _dot_nt(a, b):
    return lax.dot_general(a, b, (((1,), (1,)), ((), ())), preferred_element_type=_F32)


def _rms_factor(x, width):
    ss = jnp.sum(x * x, axis=-1, keepdims=True)
    return lax.rsqrt(ss * (1.0 / width) + EPS)


def _rms_scale(x, width):
    return x * _rms_factor(x, width)


def _pair_norm64(x, lane_lo):
    x2 = x * x
    s_all = jnp.sum(x2, axis=-1, keepdims=True)
    s_lo = jnp.sum(jnp.where(lane_lo, x2, 0.0), axis=-1, keepdims=True)
    ss = jnp.where(lane_lo, s_lo, s_all - s_lo)
    return x * lax.rsqrt(ss * (1.0 / 64.0) + EPS)


def _rope_kernel(pos_ref, freq_ref, c_ref, s_ref):
    ang = pos_ref[...] * freq_ref[...]
    c_ref[...] = jnp.cos(ang)
    s_ref[...] = jnp.sin(ang)


def _rope_tables(positions):
    half = MLA_ROPE // 2
    per_row = HEAD_LANES // half
    n = positions.size
    inv_freq = ROPE_THETA ** (-jnp.arange(half, dtype=_F32) / half)
    pos = jnp.repeat(positions.astype(_F32).reshape(n // per_row, per_row), half, axis=1)
    rows = n // per_row
    tile = min(rows, 1024)
    tab = pl.BlockSpec((tile, HEAD_LANES), lambda i: (i, 0))
    shp = jax.ShapeDtypeStruct((rows, HEAD_LANES), _F32)
    cos, sin = pl.pallas_call(
        _rope_kernel,
        grid=(rows // tile,),
        in_specs=[tab, pl.BlockSpec((1, HEAD_LANES), lambda i: (0, 0))],
        out_specs=[tab, tab],
        out_shape=[shp, shp],
        compiler_params=_params(1),
        name="rope_tables",
    )(pos, jnp.tile(inv_freq, per_row)[None])
    return cos.reshape(n, half), sin.reshape(n, half)


def _rotary_lanes(cos, sin):
    rows = cos.shape[0]
    pad = jnp.zeros((rows, HEAD_LANES - MLA_QK_DIM), _F32)
    c = jnp.concatenate([jnp.ones((rows, MLA_NOPE), _F32), cos, cos, pad], axis=1)
    s = jnp.concatenate([jnp.zeros((rows, MLA_NOPE), _F32), -sin, sin, pad], axis=1)
    return c, s


def _swa_tile(q, k, v, k_before, v_before, sink_ref, first_of_sequence):
    w = SWA_WINDOW
    oldest = jnp.where(first_of_sequence, 0, -w)
    lane_lo = lax.broadcasted_iota(jnp.int32, (1, 128), 1) < 64
    qpos = lax.broadcasted_iota(jnp.int32, (w, 2 * w), 0)
    kpos = lax.broadcasted_iota(jnp.int32, (w, 2 * w), 1) - w
    band = jnp.logical_and(kpos <= qpos, kpos > qpos - w)

    out_rows = []
    for jb in range(q.shape[0] // w):
        rs = slice(jb * w, (jb + 1) * w)
        if jb == 0:
            kprev, vprev = k_before, v_before
            visible = jnp.logical_and(band, kpos >= oldest)
        else:
            ps = slice((jb - 1) * w, jb * w)
            kprev, vprev = k[ps, :], v[ps, :]
            visible = band
        kk = jnp.concatenate([kprev, k[rs, :]], axis=0)
        vv = jnp.concatenate([vprev, v[rs, :]], axis=0)
        zero = jnp.zeros_like(kk)
        k_half = (jnp.where(lane_lo, kk, zero), jnp.where(lane_lo, zero, kk))
        q_both = jnp.concatenate([q[rs, 0:128], q[rs, 128:256]], axis=0)
        seen = jnp.concatenate([visible, visible], axis=0)
        first_pair = lax.broadcasted_iota(jnp.int32, (2 * w, 1), 0) < w
        outs = []
        for half in range(2):
            sink = jnp.where(first_pair, sink_ref[half], sink_ref[2 + half])
            s = _dot_nt(q_both, k_half[half])
            s = jnp.where(seen, s, NEG_BIG)
            m = jnp.maximum(jnp.max(s, axis=-1, keepdims=True), sink)
            p = jnp.exp(s - m)
            denom = jnp.sum(p, axis=-1, keepdims=True) + jnp.exp(sink - m)
            outs.append(_dot(p.astype(_BF16), vv) / denom)
        o_both = jnp.where(lane_lo, outs[0], outs[1])
        out_rows.append(jnp.concatenate([o_both[0:w, :], o_both[w:2 * w, :]], axis=1))
    return jnp.concatenate(out_rows, axis=0)


def _proj_kernel(x_ref, gmix_ref, win_ref, gcq_ref, wuq_ref, gckv_ref, wukv_ref, gq_ref, gqs_ref,
                 gk_ref, gks_ref, c_ref, s_ref, wpool_ref, pscale_ref, convw_ref, convb_ref, gsq_ref,
                 gsk_ref, sink_ref, qm_ref, km_ref, vm_ref, yb_ref, yc_ref, yd_ref, halo_ref, swa_halo_ref,
                 *, tiles_per_seq):
    tm = PROJ_TILE
    i = pl.program_id(0)
    seq_tile = i % tiles_per_seq

    @pl.when(i == 0)
    def _():
        halo_ref[...] = jnp.zeros_like(halo_ref)
        swa_halo_ref[...] = jnp.zeros_like(swa_halo_ref)

    x = x_ref[...]
    h = _rms_scale(x, D_MODEL) * gmix_ref[...]
    u = _dot(h.astype(_BF16), win_ref[...])

    c, s = _rotary_lanes(c_ref[...], s_ref[...])
    hw = MLA_HEADS * HEAD_LANES

    cq = _rms_scale(u[:, COL_CQ:COL_CQ + MLA_Q_RANK], MLA_Q_RANK) * gcq_ref[...]
    q = _dot(cq.astype(_BF16), wuq_ref[...])
    gq = (gq_ref[...] * (MLA_QK_DIM ** -0.5 * LOG2_E)) * c
    gqs = (gqs_ref[...] * (MLA_QK_DIM ** -0.5 * LOG2_E)) * s
    for hd in range(MLA_HEADS):
        sl = slice(hd * HEAD_LANES, (hd + 1) * HEAD_LANES)
        xq = q[:, sl]
        r = _rms_factor(xq, MLA_QK_DIM)
        qm_ref[:, sl] = ((xq * r) * gq + (q[:, hw + sl.start:hw + sl.stop] * r) * gqs).astype(_BF16)

    ckv = _rms_scale(u[:, COL_CKV:COL_CKV + MLA_KV_RANK], MLA_KV_RANK) * gckv_ref[...]
    kv = _dot(ckv.astype(_BF16), wukv_ref[...])
    krope = u[:, COL_KROPE:COL_KROPE + HEAD_LANES]
    krope_swap = u[:, COL_KROPE_SWAP:COL_KROPE_SWAP + HEAD_LANES]
    gk = gk_ref[...] * c
    gks = gks_ref[...] * s
    for hd in range(MLA_HEADS):
        sl = slice(hd * HEAD_LANES, (hd + 1) * HEAD_LANES)
        xk = kv[:, sl] + krope
        r = _rms_factor(xk, MLA_QK_DIM)
        km_ref[:, sl] = ((xk * r) * gk + (krope_swap * r) * gks).astype(_BF16)
    vm_ref[...] = kv[:, hw:hw + MLA_HEADS * MLA_V].astype(_BF16)

    lane_lo = lax.broadcasted_iota(jnp.int32, (1, HEAD_LANES), 1) < 64
    gsq = gsq_ref[...] * (SWA_HEAD_DIM ** -0.5)
    qs = jnp.concatenate(
        [(_pair_norm64(u[:, COL_SWA_Q + blk * 128:COL_SWA_Q + (blk + 1) * 128], lane_lo) * gsq).astype(_BF16)
         for blk in range(2)], axis=1)
    ks = (_pair_norm64(u[:, COL_SWA_K:COL_SWA_K + 128], lane_lo) * gsk_ref[...]).astype(_BF16)
    vs = u[:, COL_SWA_V:COL_SWA_V + 128].astype(_BF16)

    swa_halo = swa_halo_ref[...].astype(_BF16)
    yc_ref[...] = _swa_tile(qs, ks, vs, swa_halo[:, 0:128], swa_halo[:, 128:256], sink_ref,
                            seq_tile == 0).astype(_BF16)
    swa_halo_ref[:, 0:128] = ks[tm - SWA_WINDOW:, :].astype(_F32)
    swa_halo_ref[:, 128:256] = vs[tm - SWA_WINDOW:, :].astype(_F32)

    halo = jnp.where(jnp.broadcast_to(seq_tile, halo_ref.shape) == 0, 0.0, halo_ref[...])
    up = u[:, COL_POOL:COL_POOL + GROUP_W]
    ud = u[:, COL_CONV:COL_CONV + 3 * CONV_CH]

    b = jnp.concatenate([halo[:, 0:256], up], axis=0)
    w2 = b + pltpu.roll(b, 1, 0)
    w4 = w2 + pltpu.roll(w2, 2, 0)
    w8 = w4 + pltpu.roll(w4, 4, 0)
    w16 = w8 + pltpu.roll(w8, 8, 0)
    lane = lax.broadcasted_iota(jnp.int32, (1, 256), 1)
    win = jnp.where(lane < 64, w2, jnp.where(lane < 128, w4, jnp.where(lane < 192, w8, w16)))
    win = win[POOL_HALO:, :]
    width = jnp.where(lane < 64, 2, jnp.where(lane < 128, 4, jnp.where(lane < 192, 8, 16)))
    t = seq_tile * tm + lax.broadcasted_iota(jnp.int32, (tm, 1), 0)
    count = jnp.minimum(t + 1, width).astype(_F32)
    pooled = win / count - up
    yb_ref[...] = (_dot(pooled.astype(_BF16), wpool_ref[...]) * pscale_ref[...]).astype(_BF16)

    z = ud[:, 256:512] * ud[:, 512:768]
    zh = halo[:, 512:768] * halo[:, 768:1024]
    zb = jnp.concatenate([zh, z], axis=0)
    cw = convw_ref[...]
    conv = (pltpu.roll(zb, 2, 0)[POOL_HALO:, :] * cw[0:1, :]
            + pltpu.roll(zb, 1, 0)[POOL_HALO:, :] * cw[1:2, :]
            + z * cw[2:3, :])
    yd_ref[...] = (ud[:, 0:256] * (conv + convb_ref[...])).astype(_BF16)

    halo_ref[:, 0:256] = up[tm - POOL_HALO:, :]
    halo_ref[:, 256:1024] = ud[tm - POOL_HALO:, :]


def _proj_call(x, lw, tabs, seq_len):
    n = x.shape[0]
    tm = PROJ_TILE
    tiles_per_seq = seq_len // tm

    def full(a):
        nd = a.ndim
        return pl.BlockSpec(a.shape, lambda i, _nd=nd: (0,) * _nd)

    def rows(width):
        return pl.BlockSpec((tm, width), lambda i: (i, 0))

    consts = [lw["g_mix"], lw["w_in"], lw["g_cq"], lw["w_uq"], lw["g_ckv"], lw["w_ukv"],
              lw["g_q"], lw["g_q_swap"], lw["g_k"], lw["g_k_swap"]]
    consts2 = [lw["w_pool"], lw["pool_scale"], lw["conv_w"], lw["conv_b"], lw["g_sq"], lw["g_sk"]]
    out_widths = [512, 512, 256, 256, 256, 256]
    return pl.pallas_call(
        functools.partial(_proj_kernel, tiles_per_seq=tiles_per_seq),
        grid=(n // tm,),
        in_specs=[rows(D_MODEL)] + [full(a) for a in consts] + [rows(MLA_ROPE // 2)] * 2
                 + [full(a) for a in consts2] + [pl.BlockSpec(memory_space=pltpu.SMEM)],
        out_specs=[rows(w) for w in out_widths],
        out_shape=[jax.ShapeDtypeStruct((n, w), _BF16) for w in out_widths],
        scratch_shapes=[pltpu.VMEM((POOL_HALO, 1024), _F32), pltpu.VMEM((SWA_WINDOW, 256), _F32)],
        compiler_params=_params(1),
        name="proj_in",
    )(x, *consts, *tabs, *consts2, lw["sinks"])


def _mla_attn_kernel(q_ref, k_hbm, v_hbm, o_ref, kbuf, vbuf, sems, m_ref, l_ref, acc_ref, *, seq_len):
    tq, tk = ATTN_Q_TILE, ATTN_TILE
    assert tq == 2 * tk
    b = pl.program_id(0)
    qb = pl.program_id(1)
    row0 = b * seq_len

    def fetch(j, slot):
        rows = pl.ds(pl.multiple_of(row0 + j * tq, tq), tq)
        return (pltpu.make_async_copy(k_hbm.at[rows], kbuf.at[slot], sems.at[0, slot]),
                pltpu.make_async_copy(v_hbm.at[rows], vbuf.at[slot], sems.at[1, slot]))

    def start(j, slot):
        for cp in fetch(j, slot):
            cp.start()

    def wait(j, slot):
        for cp in fetch(j, slot):
            cp.wait()

    m_ref[...] = jnp.full_like(m_ref, NEG_BIG)
    l_ref[...] = jnp.zeros_like(l_ref)
    acc_ref[...] = jnp.zeros_like(acc_ref)
    lane_lo = lax.broadcasted_iota(jnp.int32, (1, HEAD_LANES), 1) < 64

    def accumulate(slot, half, rows, diagonal_from):
        nrows = rows.stop - rows.start
        keys = slice(half * tk, (half + 1) * tk)
        if diagonal_from is not None:
            visible = (lax.broadcasted_iota(jnp.int32, (nrows, tk), 1)
                       <= lax.broadcasted_iota(jnp.int32, (nrows, tk), 0) + (rows.start - diagonal_from))
        for pair in range(MLA_HEADS // 2):
            vblk = vbuf[slot, keys, pair * 128:(pair + 1) * 128]
            alphas = []
            pvs = []
            for sub in range(2):
                hd = 2 * pair + sub
                sl = slice(hd * HEAD_LANES, (hd + 1) * HEAD_LANES)
                s = _dot_nt(q_ref[rows, sl], kbuf[slot, keys, sl])
                if diagonal_from is not None:
                    s = jnp.where(visible, s, NEG_BIG)
                m_prev = m_ref[hd, rows, :]
                m_new = jnp.maximum(m_prev, jnp.max(s, axis=-1, keepdims=True))
                alpha = jnp.exp2(m_prev - m_new)
                p = jnp.exp2(s - jnp.tile(m_new, (1, tk // HEAD_LANES)))
                l_ref[hd, rows, :] = alpha * l_ref[hd, rows, :] + jnp.sum(p, axis=-1, keepdims=True)
                m_ref[hd, rows, :] = m_new
                alphas.append(alpha)
                pvs.append(_dot(p.astype(_BF16), vblk))
            psl = slice(pair * 128, (pair + 1) * 128)
            acc_ref[rows, psl] = (acc_ref[rows, psl] * jnp.where(lane_lo, alphas[0], alphas[1])
                                  + jnp.where(lane_lo, pvs[0], pvs[1]))

    everything = slice(0, tq)
    start(0, 0)

    def earlier_group(j, carry):
        slot = j % 2
        wait(j, slot)
        start(j + 1, 1 - slot)
        accumulate(slot, 0, everything, None)
        accumulate(slot, 1, everything, None)
        return carry

    lax.fori_loop(0, qb, earlier_group, 0)

    slot = qb % 2
    wait(qb, slot)
    accumulate(slot, 0, everything, 0)
    accumulate(slot, 1, slice(tk, tq), tk)

    for pair in range(MLA_HEADS // 2):
        psl = slice(pair * 128, (pair + 1) * 128)
        denom = jnp.where(lane_lo, l_ref[2 * pair], l_ref[2 * pair + 1])
        o_ref[:, psl] = (acc_ref[:, psl] / denom).astype(o_ref.dtype)


def _mla_attn_call(qm, km, vm, batch, seq_len):
    tq, tk = ATTN_Q_TILE, ATTN_TILE
    nq = seq_len // tq
    return pl.pallas_call(
        functools.partial(_mla_attn_kernel, seq_len=seq_len),
        grid=(batch, nq),
        in_specs=[pl.BlockSpec((tq, 512), lambda b, q: (b * nq + q, 0)),
                  pl.BlockSpec(memory_space=pl.ANY), pl.BlockSpec(memory_space=pl.ANY)],
        out_specs=pl.BlockSpec((tq, 256), lambda b, q: (b * nq + q, 0)),
        out_shape=jax.ShapeDtypeStruct((batch * seq_len, 256), _BF16),
        scratch_shapes=[pltpu.VMEM((2, tq, 512), _BF16), pltpu.VMEM((2, tq, 256), _BF16),
                        pltpu.SemaphoreType.DMA((2, 2)),
                        pltpu.VMEM((MLA_HEADS, tq, HEAD_LANES), _F32),
                        pltpu.VMEM((MLA_HEADS, tq, HEAD_LANES), _F32),
                        pltpu.VMEM((tq, 256), _F32)],
        compiler_params=_params(2),
        name="mla_attention",
    )(qm, km, vm)


def _memkv_kernel(mem_ref, gmem_ref, wkv_ref, gk_ref, k_ref, v_ref):
    m = _rms_scale(mem_ref[...], D_MODEL) * gmem_ref[...]
    kv = _dot(m.astype(_BF16), wkv_ref[...])
    lane_lo = lax.broadcasted_iota(jnp.int32, (1, 128), 1) < 64
    lane = lax.broadcasted_iota(jnp.int32, (1, 256), 1)
    k = jnp.concatenate([_pair_norm64(kv[:, 0:128], lane_lo), _pair_norm64(kv[:, 128:256], lane_lo)],
                        axis=1) * gk_ref[...]
    v = kv[:, 256:512]
    for hd in range(XA_HEADS):
        own = jnp.logical_and(lane >= hd * XA_HEAD_DIM, lane < (hd + 1) * XA_HEAD_DIM)
        k_ref[hd] = jnp.where(own, k, 0.0).astype(_BF16)
        v_ref[hd] = jnp.where(own, v, 0.0).astype(_BF16)


def _memkv_call(mem2d, g_mem, w_kv, g_k4, depth, batch):
    out = jax.ShapeDtypeStruct((depth, batch, XA_HEADS, MEM_LEN, 256), _BF16)
    ospec = pl.BlockSpec((None, None, XA_HEADS, MEM_LEN, 256), lambda l, b: (l, b, 0, 0, 0))
    return pl.pallas_call(
        _memkv_kernel,
        grid=(depth, batch),
        in_specs=[pl.BlockSpec((MEM_LEN, D_MODEL), lambda l, b: (b, 0)),
                  pl.BlockSpec((None, 1, D_MODEL), lambda l, b: (l, 0, 0)),
                  pl.BlockSpec((None, D_MODEL, 512), lambda l, b: (l, 0, 0)),
                  pl.BlockSpec((None, 1, 256), lambda l, b: (l, 0, 0))],
        out_specs=[ospec, ospec],
        out_shape=[out, out],
        compiler_params=_params(2),
        name="memory_kv",
    )(mem2d, g_mem, w_kv, g_k4)


def _mix_kernel(ya_ref, yb_ref, yc_ref, yd_ref, x_ref, gmo_ref, wmo_ref, gxa_ref, wq_ref, gxq_ref,
                kx_ref, vx_ref, wo_ref, gffn_ref, wr_ref, br_ref, ltri_ref, utri_ref,
                x2_ref, h3_ref, meta_ref, cnt_ref):
    tm = MIX_TILE
    gmo = gmo_ref[...]
    lane_lo = lax.broadcasted_iota(jnp.int32, (1, 128), 1) < 64

    def rows_to_logits(rs):
        parts = []
        for g, ref in enumerate((ya_ref, yb_ref, yc_ref, yd_ref)):
            yg = ref[rs, :].astype(_F32)
            parts.append((_rms_scale(yg, GROUP_W) * gmo[:, g * 256:(g + 1) * 256]).astype(_BF16))
        y = jnp.concatenate(parts, axis=1)
        x1 = x_ref[rs, :] + _dot(y, wmo_ref[...])

        h = (_rms_scale(x1, D_MODEL) * gxa_ref[...]).astype(_BF16)
        q = _dot(h, wq_ref[...])
        qn = jnp.concatenate([_pair_norm64(q[:, 0:128], lane_lo), _pair_norm64(q[:, 128:256], lane_lo)],
                             axis=1)
        qn = (qn * gxq_ref[...] * (XA_HEAD_DIM ** -0.5)).astype(_BF16)
        o = jnp.zeros((x1.shape[0], 256), _F32)
        for hd in range(XA_HEADS):
            s = _dot_nt(qn, kx_ref[hd])
            m = jnp.max(s, axis=-1, keepdims=True)
            p = jnp.exp(s - m)
            denom = jnp.sum(p, axis=-1, keepdims=True)
            o = o + _dot(p.astype(_BF16), vx_ref[hd]) / denom
        x2 = x1 + _dot(o.astype(_BF16), wo_ref[...])
        x2_ref[rs, :] = x2

        h3 = _rms_scale(x2, D_MODEL) * gffn_ref[...]
        h3_hi = h3.astype(_BF16)
        h3_lo = (h3 - h3_hi.astype(_F32)).astype(_BF16)
        h3_ref[rs, :] = h3_hi
        by_hi = _dot(h3_hi, wr_ref[...])
        return (by_hi[:, 0:128] + (by_hi[:, 128:256] + _dot(h3_lo, wr_ref[:, 0:128]))
                + br_ref[...])

    all_logits = rows_to_logits(slice(0, tm))
    for st in range(tm // TOKEN_TILE):
        meta, counts = _route(all_logits[st * TOKEN_TILE:(st + 1) * TOKEN_TILE, :], ltri_ref, utri_ref)
        meta_ref[st * TOKEN_TILE:(st + 1) * TOKEN_TILE, :] = meta
        cnt_ref[st * 8:(st + 1) * 8, :] = jnp.broadcast_to(counts, (8, 128))


def _route(logits, ltri_ref, utri_ref):
    lane = lax.broadcasted_iota(jnp.int32, (TOKEN_TILE, 128), 1).astype(_F32)
    far = 1e9
    is_group = jnp.logical_and(lane >= N_EXPERTS, lane < N_EXPERTS + N_EXPERT_GROUPS)
    gl = jnp.where(is_group, logits, -jnp.inf)
    gmax = jnp.max(gl, axis=-1, keepdims=True)
    gidx = jnp.min(jnp.where(gl == gmax, lane, far), axis=-1, keepdims=True) - N_EXPERTS
    g_w = 1.0 / jnp.sum(jnp.where(is_group, jnp.exp(logits - gmax), 0.0), axis=-1, keepdims=True)
    in_group = jnp.logical_and(lane < N_EXPERTS, jnp.floor(lane * (1.0 / EXPERTS_PER_GROUP)) == gidx)
    el = jnp.where(in_group, logits, -jnp.inf)
    emax = jnp.max(el, axis=-1, keepdims=True)
    ep = jnp.where(in_group, jnp.exp(logits - emax), 0.0)
    prob = jnp.where(in_group, ep / jnp.sum(ep, axis=-1, keepdims=True), -1.0)
    p1 = jnp.max(prob, axis=-1, keepdims=True)
    e0 = jnp.min(jnp.where(prob == p1, lane, far), axis=-1, keepdims=True)
    prob2 = jnp.where(lane == e0, -1.0, prob)
    p2 = jnp.max(prob2, axis=-1, keepdims=True)
    e1 = jnp.min(jnp.where(prob2 == p2, lane, far), axis=-1, keepdims=True)
    w0 = g_w * (p1 / (p1 + p2))
    w1 = g_w * (p2 / (p1 + p2))

    onehot = jnp.where(jnp.logical_or(lane == e0, lane == e1), 1.0, 0.0)
    prefix = _dot(ltri_ref[...], onehot.astype(_BF16))
    counts = jnp.sum(onehot, axis=0, keepdims=True)
    units = jnp.floor((counts + (RUN_ALIGN - 1)) * (1.0 / RUN_ALIGN))
    offs = _dot(jnp.broadcast_to(units, (8, 128)).astype(_BF16), utri_ref[...])[0:1, :] * RUN_ALIGN
    where_to = prefix + offs
    pos0 = jnp.sum(jnp.where(lane == e0, where_to, 0.0), axis=-1, keepdims=True)
    pos1 = jnp.sum(jnp.where(lane == e1, where_to, 0.0), axis=-1, keepdims=True)

    meta = jnp.where(lane == 0, e0,
           jnp.where(lane == 1, e1,
           jnp.where(lane == 2, w0,
           jnp.where(lane == 3, w1,
           jnp.where(lane == 4, pos0,
           jnp.where(lane == 5, pos1, 0.0))))))
    return meta, counts


def _mix_call(ya, yb, yc, yd, x, lw, kx, vx, seq_len):
    n = x.shape[0]
    tm = MIX_TILE
    tiles_per_seq = seq_len // tm

    def full(a):
        nd = a.ndim
        return pl.BlockSpec(a.shape, lambda i, _nd=nd: (0,) * _nd)

    def rows(width):
        return pl.BlockSpec((tm, width), lambda i: (i, 0))

    kvspec = pl.BlockSpec((None, XA_HEADS, MEM_LEN, 256), lambda i: (i // tiles_per_seq, 0, 0, 0))
    consts_a = [lw["g_mo"], lw["w_mo"], lw["g_xa"], lw["xa_w_q"], lw["g_xq"]]
    consts_b = [lw["xa_w_o"], lw["g_ffn"], lw["w_router_hi_lo"], lw["b_router"],
                lw["ltri"], lw["utri"]]
    ntiles = n // tm
    return pl.pallas_call(
        _mix_kernel,
        grid=(ntiles,),
        in_specs=[rows(256)] * 4 + [rows(D_MODEL)] + [full(a) for a in consts_a]
                 + [kvspec, kvspec] + [full(a) for a in consts_b],
        out_specs=[rows(D_MODEL), rows(D_MODEL), rows(128),
                   pl.BlockSpec((8 * (tm // TOKEN_TILE), 128), lambda i: (i, 0))],
        out_shape=[jax.ShapeDtypeStruct((n, D_MODEL), _F32), jax.ShapeDtypeStruct((n, D_MODEL), _BF16),
                   jax.ShapeDtypeStruct((n, 128), _F32),
                   jax.ShapeDtypeStruct((n // TOKEN_TILE * 8, 128), _F32)],
        compiler_params=_params(1),
        name="mix_xattn_router",
    )(ya, yb, yc, yd, x, *consts_a, kx, vx, *consts_b)


def _move_runs(tab_ref, t, live, local_ref, remote_ref, sems, *, to_remote, wait):
    for bit in range(RUN_BITS):
        rows = RUN_ALIGN << bit
        base = (t * RUN_BITS + bit) * PIECE_COLS
        count = jnp.where(live, tab_ref[base], 0)

        def one(p, carry, base=base, rows=rows, bit=bit):
            loc = pl.multiple_of(tab_ref[base + 1 + p], RUN_ALIGN)
            rem = pl.multiple_of(tab_ref[base + 1 + N_EXPERTS + p], RUN_ALIGN)
            lsl = local_ref.at[pl.ds(loc, rows)]
            rsl = remote_ref.at[pl.ds(rem, rows)]
            src, dst = (lsl, rsl) if to_remote else (rsl, lsl)
            copy = pltpu.make_async_copy(src, dst, sems.at[bit, p])
            if wait:
                copy.wait()
            else:
                copy.start()
            return carry

        lax.fori_loop(0, count, one, 0)


def _selection(meta, first, rows):
    r = first + lax.broadcasted_iota(jnp.int32, (TOKEN_TILE, rows), 1)
    pos0 = meta[:, 4:5].astype(jnp.int32)
    pos1 = meta[:, 5:6].astype(jnp.int32)
    return jnp.where(r == pos0, 1.0, 0.0).astype(_BF16), jnp.where(r == pos1, 1.0, 0.0).astype(_BF16)


def _zero_rows(count, row_of, rows, zero_ref, xs_ref, sems):
    def copy(i):
        dst = xs_ref.at[pl.ds(pl.multiple_of(row_of(i), RUN_ALIGN), rows)]
        return pltpu.make_async_copy(zero_ref.at[pl.ds(0, rows)], dst, sems.at[i % ZERO_RING])

    def issue(i, carry):
        @pl.when(i >= ZERO_RING)
        def _():
            copy(i - ZERO_RING).wait()
        copy(i).start()
        return carry

    def drain(i, carry):
        copy(i).wait()
        return carry

    lax.fori_loop(0, count, issue, 0)
    lax.fori_loop(jnp.maximum(count - ZERO_RING, 0), count, drain, 0)


def _dispatch_kernel(tab_ref, used_ref, ztab_ref, meta_ref, h3_ref, xs_ref, buf0_ref, buf1_ref, zero_ref,
                     sems, zero_sems, free_sems, *, ntiles, n_blocks):
    t = pl.program_id(0)
    last = ntiles - 1
    bufs = (buf0_ref, buf1_ref)
    used = used_ref[jnp.minimum(t, last)]

    @pl.when(t == 0)
    def _():
        zero_ref[...] = jnp.zeros_like(zero_ref)
        for bit in range(TAIL_BITS):
            base = bit * (1 + N_EXPERTS)
            _zero_rows(ztab_ref[base], lambda i, base=base: ztab_ref[base + 1 + i], RUN_ALIGN << bit,
                       zero_ref, xs_ref, zero_sems.at[bit])

    live_blocks = ztab_ref[TAIL_BITS * (1 + N_EXPERTS)]
    per_step = -(-n_blocks // (ntiles + 1))

    def free_block_copy(step, j):
        blk = live_blocks + step * per_step + j
        dst = xs_ref.at[pl.ds(pl.multiple_of(blk * MOE_BLOCK, MOE_BLOCK), MOE_BLOCK)]
        return blk < n_blocks, pltpu.make_async_copy(zero_ref, dst, free_sems.at[step % 2, j])

    for j in range(per_step):
        exists, copy = free_block_copy(t - 1, j)
        pl.when(jnp.logical_and(t >= 1, exists))(copy.wait)
    for j in range(per_step):
        exists, copy = free_block_copy(t, j)
        pl.when(jnp.logical_and(t <= ntiles, exists))(copy.start)

    def step(slot):
        mine, other = bufs[slot], bufs[1 - slot]
        _move_runs(tab_ref, jnp.clip(t - 2, 0, last), t >= 2, mine, xs_ref, sems.at[slot],
                   to_remote=True, wait=True)
        _move_runs(tab_ref, jnp.clip(t - 1, 0, last), jnp.logical_and(t >= 1, t <= ntiles), other, xs_ref,
                   sems.at[1 - slot], to_remote=True, wait=False)

        meta = meta_ref[...]
        lane = lax.broadcasted_iota(jnp.int32, (TOKEN_TILE, 128), 1)
        cols = []
        for k in range(2):
            w = meta[:, 2 + k:3 + k]
            hi = w.astype(_BF16).astype(_F32)
            mid = (w - hi).astype(_BF16).astype(_F32)
            lo = (w - hi) - mid
            cols.append(jnp.where(lane == 0, hi, jnp.where(lane == 1, mid, jnp.where(lane == 2, lo, 0.0)))
                        .astype(_BF16))

        def sort_rows(first, rows):
            sel0, sel1 = _selection(meta, first, rows)
            tn = (((0,), (0,)), ((), ()))
            mine[first:first + rows, 0:D_MODEL] = lax.dot_general(
                sel0 + sel1, h3_ref[...], tn, preferred_element_type=_F32).astype(_BF16)
            mine[first:first + rows, D_MODEL:XS_COLS] = (
                lax.dot_general(sel0, cols[0], tn, preferred_element_type=_F32)
                + lax.dot_general(sel1, cols[1], tn, preferred_element_type=_F32)).astype(_BF16)

        sort_rows(0, SORT_ALWAYS)
        for first in range(SORT_ALWAYS, SORT_ROWS, SORT_CHUNK):
            pl.when(used > first)(functools.partial(sort_rows, first, SORT_CHUNK))

    for slot in range(2):
        pl.when(t % 2 == slot)(functools.partial(step, slot))


def _dispatch_call(tab, used, ztab, meta, h3):
    n = h3.shape[0]
    tm = TOKEN_TILE
    ntiles = n // tm
    xs_rows = _xs_rows(n)
    tile = lambda i, tab, used, ztab: (jnp.minimum(i, ntiles - 1), 0)
    grid_spec = pltpu.PrefetchScalarGridSpec(
        num_scalar_prefetch=3,
        grid=(ntiles + 2,),
        in_specs=[pl.BlockSpec((tm, 128), tile),
                  pl.BlockSpec((tm, D_MODEL), tile)],
        out_specs=pl.BlockSpec(memory_space=pl.ANY),
        scratch_shapes=[pltpu.VMEM((SORT_ROWS, XS_COLS), _BF16), pltpu.VMEM((SORT_ROWS, XS_COLS), _BF16),
                        pltpu.VMEM((MOE_BLOCK, XS_COLS), _BF16),
                        pltpu.SemaphoreType.DMA((2, RUN_BITS, N_EXPERTS)),
                        pltpu.SemaphoreType.DMA((TAIL_BITS, ZERO_RING)),
                        pltpu.SemaphoreType.DMA((2, -(-(xs_rows // MOE_BLOCK) // (ntiles + 1))))],
    )
    return pl.pallas_call(
        functools.partial(_dispatch_kernel, ntiles=ntiles, n_blocks=xs_rows // MOE_BLOCK),
        grid_spec=grid_spec,
        out_shape=jax.ShapeDtypeStruct((xs_rows, XS_COLS), _BF16),
        compiler_params=_params(1),
        name="moe_dispatch",
    )(tab, used, ztab, meta, h3)


def _ffn_kernel(be_ref, nv_ref, xs_ref, wg_ref, wu_ref, wd_ref, ys_ref, wgu_ref, wdn_ref):
    i = pl.program_id(0)

    @pl.when(jnp.logical_or(i == 0, be_ref[i] != be_ref[jnp.maximum(i - 1, 0)]))
    def _():
        wgu_ref[:, 0:D_EXPERT] = wg_ref[...].astype(_BF16)
        wgu_ref[:, D_EXPERT:2 * D_EXPERT] = wu_ref[...].astype(_BF16)
        wdn_ref[...] = wd_ref[...].astype(_BF16)

    @pl.when(i < nv_ref[0])
    def _():
        xb = xs_ref[:, 0:D_MODEL]
        wcols = xs_ref[:, D_MODEL:XS_COLS].astype(_F32)
        wt = wcols[:, 0:1] + wcols[:, 1:2] + wcols[:, 2:3]
        gu = _dot(xb, wgu_ref[...])
        g = gu[:, 0:D_EXPERT]
        a = (g * jax.nn.sigmoid(g)) * gu[:, D_EXPERT:2 * D_EXPERT]
        y = _dot(a.astype(_BF16), wdn_ref[...])
        ys_ref[...] = (y * wt).astype(ys_ref.dtype)

    @pl.when(i >= nv_ref[0])
    def _():
        ys_ref[...] = jnp.zeros_like(ys_ref)


def _ffn_call(blk_expert, n_valid, xs, w_gate, w_up, w_down, layer):
    n_blocks = xs.shape[0] // MOE_BLOCK
    wspec = lambda shape: pl.BlockSpec((None, None) + shape, lambda i, be, nv: (layer, be[i], 0, 0))
    last_live = lambda i, be, nv: (jnp.minimum(i, nv[0] - 1), 0)
    grid_spec = pltpu.PrefetchScalarGridSpec(
        num_scalar_prefetch=2,
        grid=(n_blocks,),
        in_specs=[pl.BlockSpec((MOE_BLOCK, XS_COLS), last_live),
                  wspec((D_MODEL, D_EXPERT)), wspec((D_MODEL, D_EXPERT)), wspec((D_EXPERT, D_MODEL))],
        out_specs=pl.BlockSpec((MOE_BLOCK, D_MODEL), lambda i, be, nv: (i, 0)),
        scratch_shapes=[pltpu.VMEM((D_MODEL, 2 * D_EXPERT), _BF16), pltpu.VMEM((D_EXPERT, D_MODEL), _BF16)],
    )
    return pl.pallas_call(
        _ffn_kernel,
        grid_spec=grid_spec,
        out_shape=jax.ShapeDtypeStruct((xs.shape[0], D_MODEL), _BF16),
        compiler_params=_params(1),
        name="moe_experts",
    )(blk_expert, n_valid, xs, w_gate, w_up, w_down)


def _combine_kernel(tab_ref, used_ref, meta_ref, x2_ref, ys_ref, out_ref, buf0_ref, buf1_ref, sems,
                    *, ntiles):
    t = pl.program_id(0)
    last = ntiles - 1
    bufs = (buf0_ref, buf1_ref)
    used = used_ref[jnp.clip(t - 1, 0, last)]

    @pl.when(t == 0)
    def _():
        buf0_ref[...] = jnp.zeros_like(buf0_ref)
        buf1_ref[...] = jnp.zeros_like(buf1_ref)

    def step(slot):
        mine, other = bufs[slot], bufs[1 - slot]
        _move_runs(tab_ref, jnp.clip(t - 1, 0, last), t >= 1, other, ys_ref, sems.at[1 - slot],
                   to_remote=False, wait=True)
        _move_runs(tab_ref, jnp.minimum(t, last), t <= last, mine, ys_ref, sems.at[slot],
                   to_remote=False, wait=False)
        meta = meta_ref[...]

        def gathered(first, rows):
            sel0, sel1 = _selection(meta, first, rows)
            return _dot(sel0 + sel1, other[first:first + rows, :])

        out_ref[...] = x2_ref[...] + gathered(0, SORT_ALWAYS)
        for first in range(SORT_ALWAYS, SORT_ROWS, SORT_CHUNK):
            @pl.when(used > first)
            def _(first=first):
                out_ref[...] += gathered(first, SORT_CHUNK)

    for slot in range(2):
        pl.when(t % 2 == slot)(functools.partial(step, slot))


def _combine_call(tab, used, meta, x2, ys):
    n = x2.shape[0]
    tm = TOKEN_TILE
    ntiles = n // tm
    tile = lambda i, tab, used: (jnp.maximum(i - 1, 0), 0)
    grid_spec = pltpu.PrefetchScalarGridSpec(
        num_scalar_prefetch=2,
        grid=(ntiles + 1,),
        in_specs=[pl.BlockSpec((tm, 128), tile),
                  pl.BlockSpec((tm, D_MODEL), tile),
                  pl.BlockSpec(memory_space=pl.ANY)],
        out_specs=pl.BlockSpec((tm, D_MODEL), tile),
        scratch_shapes=[pltpu.VMEM((SORT_ROWS, D_MODEL), _BF16), pltpu.VMEM((SORT_ROWS, D_MODEL), _BF16),
                        pltpu.SemaphoreType.DMA((2, RUN_BITS, N_EXPERTS))],
    )
    return pl.pallas_call(
        functools.partial(_combine_kernel, ntiles=ntiles),
        grid_spec=grid_spec,
        out_shape=jax.ShapeDtypeStruct((n, D_MODEL), _F32),
        compiler_params=_params(1),
        name="moe_combine",
    )(tab, used, meta, x2, ys)


def _moe_tables(cnt, n_tokens):
    ntiles = n_tokens // TOKEN_TILE
    counts = cnt.reshape(ntiles, 8, 128)[:, 0, :N_EXPERTS].astype(jnp.int32)
    units = (counts + RUN_ALIGN - 1) // RUN_ALIGN
    padded = units * RUN_ALIGN
    local = jnp.cumsum(padded, axis=1) - padded
    total = jnp.sum(padded, axis=0)
    total_blk = ((total + MOE_BLOCK - 1) // MOE_BLOCK) * MOE_BLOCK
    ends = jnp.cumsum(total_blk)
    starts = ends - total_blk
    remote = starts[None, :] + jnp.cumsum(padded, axis=0) - padded
    bits = jnp.arange(RUN_BITS, dtype=jnp.int32)[None, :, None]
    has = (units[:, None, :] >> bits) & 1
    done = (units[:, None, :] & ((1 << bits) - 1)) * RUN_ALIGN
    rank = jnp.cumsum(has, axis=2) - 1
    place = jnp.logical_and(rank[..., None] == jnp.arange(N_EXPERTS, dtype=jnp.int32), has[..., None] == 1)
    compact = lambda rows: jnp.sum(jnp.where(place, rows[..., None], 0), axis=2)
    tab = jnp.concatenate([jnp.sum(has, axis=2, keepdims=True), compact(local[:, None, :] + done),
                           compact(remote[:, None, :] + done)], axis=-1).reshape(-1).astype(jnp.int32)
    n_blocks = _xs_rows(n_tokens) // MOE_BLOCK
    blk_start = jnp.arange(n_blocks, dtype=jnp.int32) * MOE_BLOCK
    blk_expert = jnp.minimum(jnp.sum((ends[None, :] <= blk_start[:, None]).astype(jnp.int32), axis=1),
                             N_EXPERTS - 1)
    n_valid = (ends[-1] // MOE_BLOCK).astype(jnp.int32).reshape(1)
    used = jnp.sum(padded, axis=1).astype(jnp.int32)
    tail_units = (total_blk - total) // RUN_ALIGN
    tail_bits = jnp.arange(TAIL_BITS, dtype=jnp.int32)[:, None]
    tail_has = (tail_units[None, :] >> tail_bits) & 1
    tail_rows = starts + total + (tail_units[None, :] & ((1 << tail_bits) - 1)) * RUN_ALIGN
    tail_rank = jnp.cumsum(tail_has, axis=1) - 1
    tail_place = jnp.logical_and(tail_rank[..., None] == jnp.arange(N_EXPERTS, dtype=jnp.int32),
                                 tail_has[..., None] == 1)
    tail_list = jnp.sum(jnp.where(tail_place, tail_rows[..., None], 0), axis=1)
    ztab = jnp.concatenate([jnp.concatenate([jnp.sum(tail_has, axis=1, keepdims=True), tail_list],
                                            axis=1).reshape(-1), n_valid]).astype(jnp.int32)
    return tab, used, ztab, blk_expert, n_valid


def _xs_rows(n_tokens):
    ntiles = n_tokens // TOKEN_TILE
    worst = 2 * n_tokens + ntiles * N_EXPERTS * (RUN_ALIGN - 1) + N_EXPERTS * (MOE_BLOCK - 1)
    return ((worst + MOE_BLOCK - 1) // MOE_BLOCK) * MOE_BLOCK


def _pad_last(a, width):
    return jnp.pad(a, [(0, 0)] * (a.ndim - 1) + [(0, width - a.shape[-1])])


def _swap_mid_heads(a, axis):
    shape = a.shape
    a = a.reshape(shape[:axis] + (4, 64) + shape[axis + 1:])
    a = jnp.take(a, jnp.asarray([0, 2, 1, 3]), axis=axis)
    return a.reshape(shape)


def _layer_weights(l, p):
    w_in = p["w_in"][l]
    a, b = w_in[:, :A_COLS], w_in[:, A_COLS:A_COLS + B_COLS]
    c = w_in[:, A_COLS + B_COLS:A_COLS + B_COLS + C_COLS]
    d = w_in[:, A_COLS + B_COLS + C_COLS:]
    zeros = lambda w: jnp.zeros((D_MODEL, w), _F32)
    half = MLA_ROPE // 2
    k_rope = a[:, 384:416]
    k_rope_swap = jnp.concatenate([k_rope[:, half:], k_rope[:, :half]], axis=1)
    w_in_p = jnp.concatenate([
        a[:, :384], zeros(64), k_rope, zeros(32), zeros(64), k_rope_swap, zeros(32), b,
        _swap_mid_heads(c[:, :256], 1), c[:, 256:384], c[:, 384:512], d], axis=1).astype(_BF16)
    assert w_in_p.shape[1] == IN_COLS_PADDED

    w_uq = p["mla_w_uq"][l].reshape(MLA_Q_RANK, MLA_HEADS, MLA_QK_DIM)
    w_uq_swap = jnp.concatenate([jnp.zeros_like(w_uq[..., :MLA_NOPE]), w_uq[..., MLA_NOPE + half:],
                                 w_uq[..., MLA_NOPE:MLA_NOPE + half]], axis=-1)
    w_uq = jnp.concatenate([_pad_last(w_uq, HEAD_LANES).reshape(MLA_Q_RANK, -1),
                            _pad_last(w_uq_swap, HEAD_LANES).reshape(MLA_Q_RANK, -1)], axis=1).astype(_BF16)

    def swap_gain(g):
        return _pad_last(jnp.concatenate([jnp.zeros((MLA_NOPE,), _F32), g[MLA_NOPE + half:],
                                          g[MLA_NOPE:MLA_NOPE + half]]), HEAD_LANES)[None]
    w_ukv = p["mla_w_ukv"][l].reshape(MLA_KV_RANK, MLA_HEADS, MLA_NOPE + MLA_V)
    k_nope = _pad_last(w_ukv[..., :MLA_NOPE], HEAD_LANES).reshape(MLA_KV_RANK, MLA_HEADS * HEAD_LANES)
    v_part = w_ukv[..., MLA_NOPE:].reshape(MLA_KV_RANK, MLA_HEADS * MLA_V)
    w_ukv_p = jnp.concatenate([k_nope, v_part], axis=1).astype(_BF16)

    eye = jnp.eye(len(POOL_WINDOWS), dtype=_F32)
    w_pool = jnp.einsum("gcd,gh->gchd", p["pool_w"][l], eye).reshape(256, 256).astype(_BF16)

    g_mo = p["mix_out_norm"][l]
    g_mo = jnp.concatenate([g_mo[:512], _swap_mid_heads(g_mo[512:768], 0), g_mo[768:]])
    w_mo = p["w_mix_out"][l]
    w_mo = jnp.concatenate([w_mo[:512], _swap_mid_heads(w_mo[512:768], 0), w_mo[768:]], axis=0)

    w_router = _pad_last(jnp.concatenate([p["w_expert"][l], p["w_group"][l]], axis=1), 128)
    w_router_hi = w_router.astype(_BF16)
    w_router_lo = (w_router - w_router_hi.astype(_F32)).astype(_BF16)
    b_router = jnp.concatenate([p["b_expert"][l], p["b_group"][l]])
    tm = TOKEN_TILE
    ltri = (jnp.arange(tm)[None, :] < jnp.arange(tm)[:, None]).astype(_BF16)
    utri = (jnp.arange(128)[:, None] < jnp.arange(128)[None, :]).astype(_BF16)
    tile2 = lambda g: jnp.concatenate([g, g])[None]
    return dict(
        g_mix=p["norm_mix"][l][None], w_in=w_in_p,
        g_cq=p["mla_g_cq"][l][None], w_uq=w_uq, g_ckv=p["mla_g_ckv"][l][None], w_ukv=w_ukv_p,
        g_q=_pad_last(p["mla_g_q"][l], HEAD_LANES)[None], g_k=_pad_last(p["mla_g_k"][l], HEAD_LANES)[None],
        g_q_swap=swap_gain(p["mla_g_q"][l]), g_k_swap=swap_gain(p["mla_g_k"][l]),
        w_pool=w_pool, pool_scale=p["pool_scale"][l][None], conv_w=p["conv_w"][l],
        conv_b=p["conv_b"][l][None], g_sq=tile2(p["swa_g_q"][l]), g_sk=tile2(p["swa_g_k"][l]),
        sinks=jnp.take(p["swa_sinks"][l], jnp.asarray([0, 2, 1, 3])),
        g_mo=g_mo[None], w_mo=w_mo.astype(_BF16), g_xa=p["norm_xa"][l][None],
        xa_w_q=p["xa_w_q"][l].astype(_BF16), g_xq=jnp.tile(p["xa_g_q"][l], 4)[None],
        xa_w_o=p["xa_w_o"][l].astype(_BF16), g_ffn=p["norm_ffn"][l][None],
        w_router_hi_lo=jnp.concatenate([w_router_hi, w_router_lo], axis=1), b_router=_pad_last(b_router, 128)[None],
        ltri=ltri, utri=utri,
    )


def kernel(x, mem, positions, norm_mix, w_in, mla_g_cq, mla_w_uq, mla_g_ckv, mla_w_ukv, mla_g_q, mla_g_k, pool_w, pool_scale, swa_g_q, swa_g_k, swa_sinks, conv_w, conv_b, mix_out_norm, w_mix_out, norm_xa, norm_mem, xa_w_q, xa_w_kv, xa_g_q, xa_g_k, xa_w_o, norm_ffn, w_group, b_group, w_expert, b_expert, w_gate, w_up, w_down):
    p = dict(norm_mix=norm_mix, w_in=w_in, mla_g_cq=mla_g_cq, mla_w_uq=mla_w_uq, mla_g_ckv=mla_g_ckv,
             mla_w_ukv=mla_w_ukv, mla_g_q=mla_g_q, mla_g_k=mla_g_k, pool_w=pool_w, pool_scale=pool_scale,
             swa_g_q=swa_g_q, swa_g_k=swa_g_k, swa_sinks=swa_sinks, conv_w=conv_w, conv_b=conv_b,
             mix_out_norm=mix_out_norm, w_mix_out=w_mix_out, norm_xa=norm_xa, xa_w_q=xa_w_q,
             xa_g_q=xa_g_q, xa_w_o=xa_w_o, norm_ffn=norm_ffn, w_group=w_group, b_group=b_group,
             w_expert=w_expert, b_expert=b_expert)
    batch, seq_len, _ = x.shape
    depth = w_in.shape[0]
    n = batch * seq_len
    assert seq_len % MIX_TILE == 0 and MIX_TILE % TOKEN_TILE == 0
    assert seq_len % PROJ_TILE == 0 and seq_len % ATTN_Q_TILE == 0
    assert mem.shape[1] == MEM_LEN

    xf = x.reshape(n, D_MODEL)
    tabs = _rope_tables(positions)
    kx, vx = _memkv_call(mem.reshape(batch * MEM_LEN, D_MODEL), norm_mem[:, None, :],
                         xa_w_kv.astype(_BF16), jnp.tile(xa_g_k, (1, 4))[:, None, :], depth, batch)
    for l in range(depth):
        lw = _layer_weights(l, p)
        qm, km, vm, yb, yc, yd = _proj_call(xf, lw, tabs, seq_len)
        ya = _mla_attn_call(qm, km, vm, batch, seq_len)
        x2, h3, meta, cnt = _mix_call(ya, yb, yc, yd, xf, lw, kx[l], vx[l], seq_len)
        tab, used, ztab, blk_expert, n_valid = _moe_tables(cnt, n)
        xs = _dispatch_call(tab, used, ztab, meta, h3)
        ys = _ffn_call(blk_expert, n_valid, xs, w_gate, w_up, w_down, l)
        xf = _combine_call(tab, used, meta, x2, ys)
    return xf.reshape(batch, seq_len, D_MODEL)
```

```python
import functools

import jax
import jax.numpy as jnp
from jax import lax
from jax.experimental import pallas as pl
from jax.experimental.pallas import tpu as pltpu

EPS = 1e-6
NEG_BIG = -1e30
LOG2_E = 1.4426950408889634
ROPE_THETA = 10000.0

D_MODEL = 1024
MEM_LEN = 256
GROUP_W = 256

MLA_HEADS = 4
MLA_Q_RANK = 256
MLA_KV_RANK = 128
MLA_NOPE = 64
MLA_ROPE = 32
MLA_QK_DIM = MLA_NOPE + MLA_ROPE
MLA_V = 64
HEAD_LANES = 128

POOL_WINDOWS = (2, 4, 8, 16)
POOL_HALO = 16

SWA_HEADS = 4
SWA_KV_HEADS = 2
SWA_HEAD_DIM = 64
SWA_WINDOW = 128

CONV_CH = 256

XA_HEADS = 4
XA_HEAD_DIM = 64

N_EXPERT_GROUPS = 4
EXPERTS_PER_GROUP = 8
N_EXPERTS = 32
D_EXPERT = 256
MOE_BLOCK = 1024

A_COLS = MLA_Q_RANK + MLA_KV_RANK + MLA_ROPE
B_COLS = GROUP_W
C_COLS = (SWA_HEADS + 2 * SWA_KV_HEADS) * SWA_HEAD_DIM
COL_CQ, COL_CKV, COL_KROPE, COL_KROPE_SWAP = 0, 256, 384, 512
COL_POOL, COL_SWA_Q, COL_SWA_K, COL_SWA_V, COL_CONV = 640, 896, 1152, 1280, 1408
IN_COLS_PADDED = COL_CONV + 3 * CONV_CH

TOKEN_TILE = 512
PROJ_TILE = 512
MIX_TILE = 1024
ATTN_TILE = 512
ATTN_Q_TILE = 1024
RUN_ALIGN = 16
RUN_BITS = 6
SORT_ROWS = 2 * TOKEN_TILE + N_EXPERTS * RUN_ALIGN
SORT_CHUNK = 256
SORT_ALWAYS = 2 * TOKEN_TILE + SORT_CHUNK
TAIL_BITS = (MOE_BLOCK // RUN_ALIGN).bit_length() - 1
ZERO_RING = 8
N_DMA_PRIORITIES = 2
PIECE_COLS = 1 + 2 * N_EXPERTS
XS_COLS = D_MODEL + 128
VMEM_LIMIT = 56 * 1024 * 1024

_F32 = jnp.float32
_BF16 = jnp.bfloat16


def _params(n_axes):
    return pltpu.CompilerParams(dimension_semantics=("arbitrary",) * n_axes,
                                vmem_limit_bytes=VMEM_LIMIT)


def _dot(a, b):
    return jnp.dot(a, b, preferred_element_type=_F32)


def _dot_nt(a, b):
    return lax.dot_general(a, b, (((1,), (1,)), ((), ())), preferred_element_type=_F32)


def _rms_factor(x, width):
    ss = jnp.sum(x * x, axis=-1, keepdims=True)
    return lax.rsqrt(ss * (1.0 / width) + EPS)


def _rms_scale(x, width):
    return x * _rms_factor(x, width)


def _pair_norm64(x, lane_lo):
    x2 = x * x
    s_all = jnp.sum(x2, axis=-1, keepdims=True)
    s_lo = jnp.sum(jnp.where(lane_lo, x2, 0.0), axis=-1, keepdims=True)
    ss = jnp.where(lane_lo, s_lo, s_all - s_lo)
    return x * lax.rsqrt(ss * (1.0 / 64.0) + EPS)


def _rope_kernel(pos_ref, freq_ref, c_ref, s_ref):
    ang = pos_ref[...] * freq_ref[...]
    c_ref[...] = jnp.cos(ang)
    s_ref[...] = jnp.sin(ang)


def _rope_tables(positions):
    half = MLA_ROPE // 2
    per_row = HEAD_LANES // half
    n = positions.size
    inv_freq = ROPE_THETA ** (-jnp.arange(half, dtype=_F32) / half)
    pos = jnp.repeat(positions.astype(_F32).reshape(n // per_row, per_row), half, axis=1)
    rows = n // per_row
    tile = min(rows, 1024)
    tab = pl.BlockSpec((tile, HEAD_LANES), lambda i: (i, 0))
    shp = jax.ShapeDtypeStruct((rows, HEAD_LANES), _F32)
    cos, sin = pl.pallas_call(
        _rope_kernel,
        grid=(rows // tile,),
        in_specs=[tab, pl.BlockSpec((1, HEAD_LANES), lambda i: (0, 0))],
        out_specs=[tab, tab],
        out_shape=[shp, shp],
        compiler_params=_params(1),
        name="rope_tables",
    )(pos, jnp.tile(inv_freq, per_row)[None])
    return cos.reshape(n, half), sin.reshape(n, half)


def _rotary_lanes(cos, sin):
    rows = cos.shape[0]
    pad = jnp.zeros((rows, HEAD_LANES - MLA_QK_DIM), _F32)
    c = jnp.concatenate([jnp.ones((rows, MLA_NOPE), _F32), cos, cos, pad], axis=1)
    s = jnp.concatenate([jnp.zeros((rows, MLA_NOPE), _F32), -sin, sin, pad], axis=1)
    return c, s


def _swa_tile(q, k, v, k_before, v_before, sink_ref, first_of_sequence):
    w = SWA_WINDOW
    oldest = jnp.where(first_of_sequence, 0, -w)
    lane_lo = lax.broadcasted_iota(jnp.int32, (1, 128), 1) < 64
    qpos = lax.broadcasted_iota(jnp.int32, (w, 2 * w), 0)
    kpos = lax.broadcasted_iota(jnp.int32, (w, 2 * w), 1) - w
    band = jnp.logical_and(kpos <= qpos, kpos > qpos - w)

    out_rows = []
    for jb in range(q.shape[0] // w):
        rs = slice(jb * w, (jb + 1) * w)
        if jb == 0:
            kprev, vprev = k_before, v_before
            visible = jnp.logical_and(band, kpos >= oldest)
        else:
            ps = slice((jb - 1) * w, jb * w)
            kprev, vprev = k[ps, :], v[ps, :]
            visible = band
        kk = jnp.concatenate([kprev, k[rs, :]], axis=0)
        vv = jnp.concatenate([vprev, v[rs, :]], axis=0)
        zero = jnp.zeros_like(kk)
        k_half = (jnp.where(lane_lo, kk, zero), jnp.where(lane_lo, zero, kk))
        q_both = jnp.concatenate([q[rs, 0:128], q[rs, 128:256]], axis=0)
        seen = jnp.concatenate([visible, visible], axis=0)
        first_pair = lax.broadcasted_iota(jnp.int32, (2 * w, 1), 0) < w
        outs = []
        for half in range(2):
            sink = jnp.where(first_pair, sink_ref[half], sink_ref[2 + half])
            s = _dot_nt(q_both, k_half[half])
            s = jnp.where(seen, s, NEG_BIG)
            m = jnp.maximum(jnp.max(s, axis=-1, keepdims=True), sink)
            p = jnp.exp(s - m)
            denom = jnp.sum(p, axis=-1, keepdims=True) + jnp.exp(sink - m)
            outs.append(_dot(p.astype(_BF16), vv) / denom)
        o_both = jnp.where(lane_lo, outs[0], outs[1])
        out_rows.append(jnp.concatenate([o_both[0:w, :], o_both[w:2 * w, :]], axis=1))
    return jnp.concatenate(out_rows, axis=0)


def _proj_kernel(x_ref, gmix_ref, win_ref, gcq_ref, wuq_ref, gckv_ref, wukv_ref, gq_ref, gqs_ref,
                 gk_ref, gks_ref, c_ref, s_ref, wpool_ref, pscale_ref, convw_ref, convb_ref, gsq_ref,
                 gsk_ref, sink_ref, qm_ref, km_ref, vm_ref, yb_ref, yc_ref, yd_ref, halo_ref, swa_halo_ref,
                 *, tiles_per_seq):
    tm = PROJ_TILE
    i = pl.program_id(0)
    seq_tile = i % tiles_per_seq

    @pl.when(i == 0)
    def _():
        halo_ref[...] = jnp.zeros_like(halo_ref)
        swa_halo_ref[...] = jnp.zeros_like(swa_halo_ref)

    x = x_ref[...]
    h = _rms_scale(x, D_MODEL) * gmix_ref[...]
    u = _dot(h.astype(_BF16), win_ref[...])

    c, s = _rotary_lanes(c_ref[...], s_ref[...])
    hw = MLA_HEADS * HEAD_LANES

    cq = _rms_scale(u[:, COL_CQ:COL_CQ + MLA_Q_RANK], MLA_Q_RANK) * gcq_ref[...]
    q = _dot(cq.astype(_BF16), wuq_ref[...])
    gq = (gq_ref[...] * (MLA_QK_DIM ** -0.5 * LOG2_E)) * c
    gqs = (gqs_ref[...] * (MLA_QK_DIM ** -0.5 * LOG2_E)) * s
    for hd in range(MLA_HEADS):
        sl = slice(hd * HEAD_LANES, (hd + 1) * HEAD_LANES)
        xq = q[:, sl]
        r = _rms_factor(xq, MLA_QK_DIM)
        qm_ref[:, sl] = ((xq * r) * gq + (q[:, hw + sl.start:hw + sl.stop] * r) * gqs).astype(_BF16)

    ckv = _rms_scale(u[:, COL_CKV:COL_CKV + MLA_KV_RANK], MLA_KV_RANK) * gckv_ref[...]
    kv = _dot(ckv.astype(_BF16), wukv_ref[...])
    krope = u[:, COL_KROPE:COL_KROPE + HEAD_LANES]
    krope_swap = u[:, COL_KROPE_SWAP:COL_KROPE_SWAP + HEAD_LANES]
    gk = gk_ref[...] * c
    gks = gks_ref[...] * s
    for hd in range(MLA_HEADS):
        sl = slice(hd * HEAD_LANES, (hd + 1) * HEAD_LANES)
        xk = kv[:, sl] + krope
        r = _rms_factor(xk, MLA_QK_DIM)
        km_ref[:, sl] = ((xk * r) * gk + (krope_swap * r) * gks).astype(_BF16)
    vm_ref[...] = kv[:, hw:hw + MLA_HEADS * MLA_V].astype(_BF16)

    lane_lo = lax.broadcasted_iota(jnp.int32, (1, HEAD_LANES), 1) < 64
    gsq = gsq_ref[...] * (SWA_HEAD_DIM ** -0.5)
    qs = jnp.concatenate(
        [(_pair_norm64(u[:, COL_SWA_Q + blk * 128:COL_SWA_Q + (blk + 1) * 128], lane_lo) * gsq).astype(_BF16)
         for blk in range(2)], axis=1)
    ks = (_pair_norm64(u[:, COL_SWA_K:COL_SWA_K + 128], lane_lo) * gsk_ref[...]).astype(_BF16)
    vs = u[:, COL_SWA_V:COL_SWA_V + 128].astype(_BF16)

    swa_halo = swa_halo_ref[...].astype(_BF16)
    yc_ref[...] = _swa_tile(qs, ks, vs, swa_halo[:, 0:128], swa_halo[:, 128:256], sink_ref,
                            seq_tile == 0).astype(_BF16)
    swa_halo_ref[:, 0:128] = ks[tm - SWA_WINDOW:, :].astype(_F32)
    swa_halo_ref[:, 128:256] = vs[tm - SWA_WINDOW:, :].astype(_F32)

    halo = jnp.where(jnp.broadcast_to(seq_tile, halo_ref.shape) == 0, 0.0, halo_ref[...])
    up = u[:, COL_POOL:COL_POOL + GROUP_W]
    ud = u[:, COL_CONV:COL_CONV + 3 * CONV_CH]

    b = jnp.concatenate([halo[:, 0:256], up], axis=0)
    w2 = b + pltpu.roll(b, 1, 0)
    w4 = w2 + pltpu.roll(w2, 2, 0)
    w8 = w4 + pltpu.roll(w4, 4, 0)
    w16 = w8 + pltpu.roll(w8, 8, 0)
    lane = lax.broadcasted_iota(jnp.int32, (1, 256), 1)
    win = jnp.where(lane < 64, w2, jnp.where(lane < 128, w4, jnp.where(lane < 192, w8, w16)))
    win = win[POOL_HALO:, :]
    width = jnp.where(lane < 64, 2, jnp.where(lane < 128, 4, jnp.where(lane < 192, 8, 16)))
    t = seq_tile * tm + lax.broadcasted_iota(jnp.int32, (tm, 1), 0)
    count = jnp.minimum(t + 1, width).astype(_F32)
    pooled = win / count - up
    yb_ref[...] = (_dot(pooled.astype(_BF16), wpool_ref[...]) * pscale_ref[...]).astype(_BF16)

    z = ud[:, 256:512] * ud[:, 512:768]
    zh = halo[:, 512:768] * halo[:, 768:1024]
    zb = jnp.concatenate([zh, z], axis=0)
    cw = convw_ref[...]
    conv = (pltpu.roll(zb, 2, 0)[POOL_HALO:, :] * cw[0:1, :]
            + pltpu.roll(zb, 1, 0)[POOL_HALO:, :] * cw[1:2, :]
            + z * cw[2:3, :])
    yd_ref[...] = (ud[:, 0:256] * (conv + convb_ref[...])).astype(_BF16)

    halo_ref[:, 0:256] = up[tm - POOL_HALO:, :]
    halo_ref[:, 256:1024] = ud[tm - POOL_HALO:, :]


def _proj_call(x, lw, tabs, seq_len):
    n = x.shape[0]
    tm = PROJ_TILE
    tiles_per_seq = seq_len // tm

    def full(a):
        nd = a.ndim
        return pl.BlockSpec(a.shape, lambda i, _nd=nd: (0,) * _nd)

    def rows(width):
        return pl.BlockSpec((tm, width), lambda i: (i, 0))

    consts = [lw["g_mix"], lw["w_in"], lw["g_cq"], lw["w_uq"], lw["g_ckv"], lw["w_ukv"],
              lw["g_q"], lw["g_q_swap"], lw["g_k"], lw["g_k_swap"]]
    consts2 = [lw["w_pool"], lw["pool_scale"], lw["conv_w"], lw["conv_b"], lw["g_sq"], lw["g_sk"]]
    out_widths = [512, 512, 256, 256, 256, 256]
    return pl.pallas_call(
        functools.partial(_proj_kernel, tiles_per_seq=tiles_per_seq),
        grid=(n // tm,),
        in_specs=[rows(D_MODEL)] + [full(a) for a in consts] + [rows(MLA_ROPE // 2)] * 2
                 + [full(a) for a in consts2] + [pl.BlockSpec(memory_space=pltpu.SMEM)],
        out_specs=[rows(w) for w in out_widths],
        out_shape=[jax.ShapeDtypeStruct((n, w), _BF16) for w in out_widths],
        scratch_shapes=[pltpu.VMEM((POOL_HALO, 1024), _F32), pltpu.VMEM((SWA_WINDOW, 256), _F32)],
        compiler_params=_params(1),
        name="proj_in",
    )(x, *consts, *tabs, *consts2, lw["sinks"])


def _mla_attn_kernel(q_ref, k_hbm, v_hbm, o_ref, kbuf, vbuf, sems, m_ref, l_ref, acc_ref, *, seq_len):
    tq, tk = ATTN_Q_TILE, ATTN_TILE
    assert tq == 2 * tk
    b = pl.program_id(0)
    qb = pl.program_id(1)
    row0 = b * seq_len

    def fetch(j, slot):
        rows = pl.ds(pl.multiple_of(row0 + j * tq, tq), tq)
        return (pltpu.make_async_copy(k_hbm.at[rows], kbuf.at[slot], sems.at[0, slot]),
                pltpu.make_async_copy(v_hbm.at[rows], vbuf.at[slot], sems.at[1, slot]))

    def start(j, slot):
        for cp in fetch(j, slot):
            cp.start()

    def wait(j, slot):
        for cp in fetch(j, slot):
            cp.wait()

    m_ref[...] = jnp.full_like(m_ref, NEG_BIG)
    l_ref[...] = jnp.zeros_like(l_ref)
    acc_ref[...] = jnp.zeros_like(acc_ref)
    lane_lo = lax.broadcasted_iota(jnp.int32, (1, HEAD_LANES), 1) < 64

    def accumulate(slot, half, rows, diagonal_from):
        nrows = rows.stop - rows.start
        keys = slice(half * tk, (half + 1) * tk)
        if diagonal_from is not None:
            visible = (lax.broadcasted_iota(jnp.int32, (nrows, tk), 1)
                       <= lax.broadcasted_iota(jnp.int32, (nrows, tk), 0) + (rows.start - diagonal_from))
        for pair in range(MLA_HEADS // 2):
            vblk = vbuf[slot, keys, pair * 128:(pair + 1) * 128]
            alphas = []
            pvs = []
            for sub in range(2):
                hd = 2 * pair + sub
                sl = slice(hd * HEAD_LANES, (hd + 1) * HEAD_LANES)
                s = _dot_nt(q_ref[rows, sl], kbuf[slot, keys, sl])
                if diagonal_from is not None:
                    s = jnp.where(visible, s, NEG_BIG)
                m_prev = m_ref[hd, rows, :]
                m_new = jnp.maximum(m_prev, jnp.max(s, axis=-1, keepdims=True))
                alpha = jnp.exp2(m_prev - m_new)
                p = jnp.exp2(s - jnp.tile(m_new, (1, tk // HEAD_LANES)))
                l_ref[hd, rows, :] = alpha * l_ref[hd, rows, :] + jnp.sum(p, axis=-1, keepdims=True)
                m_ref[hd, rows, :] = m_new
                alphas.append(alpha)
                pvs.append(_dot(p.astype(_BF16), vblk))
            psl = slice(pair * 128, (pair + 1) * 128)
            acc_ref[rows, psl] = (acc_ref[rows, psl] * jnp.where(lane_lo, alphas[0], alphas[1])
                                  + jnp.where(lane_lo, pvs[0], pvs[1]))

    everything = slice(0, tq)
    start(0, 0)

    def earlier_group(j, carry):
        slot = j % 2
        wait(j, slot)
        start(j + 1, 1 - slot)
        accumulate(slot, 0, everything, None)
        accumulate(slot, 1, everything, None)
        return carry

    lax.fori_loop(0, qb, earlier_group, 0)

    slot = qb % 2
    wait(qb, slot)
    accumulate(slot, 0, everything, 0)
    accumulate(slot, 1, slice(tk, tq), tk)

    for pair in range(MLA_HEADS // 2):
        psl = slice(pair * 128, (pair + 1) * 128)
        denom = jnp.where(lane_lo, l_ref[2 * pair], l_ref[2 * pair + 1])
        o_ref[:, psl] = (acc_ref[:, psl] / denom).astype(o_ref.dtype)


def _mla_attn_call(qm, km, vm, batch, seq_len):
    tq, tk = ATTN_Q_TILE, ATTN_TILE
    nq = seq_len // tq
    return pl.pallas_call(
        functools.partial(_mla_attn_kernel, seq_len=seq_len),
        grid=(batch, nq),
        in_specs=[pl.BlockSpec((tq, 512), lambda b, q: (b * nq + q, 0)),
                  pl.BlockSpec(memory_space=pl.ANY), pl.BlockSpec(memory_space=pl.ANY)],
        out_specs=pl.BlockSpec((tq, 256), lambda b, q: (b * nq + q, 0)),
        out_shape=jax.ShapeDtypeStruct((batch * seq_len, 256), _BF16),
        scratch_shapes=[pltpu.VMEM((2, tq, 512), _BF16), pltpu.VMEM((2, tq, 256), _BF16),
                        pltpu.SemaphoreType.DMA((2, 2)),
                        pltpu.VMEM((MLA_HEADS, tq, HEAD_LANES), _F32),
                        pltpu.VMEM((MLA_HEADS, tq, HEAD_LANES), _F32),
                        pltpu.VMEM((tq, 256), _F32)],
        compiler_params=_params(2),
        name="mla_attention",
    )(qm, km, vm)


def _memkv_kernel(mem_ref, gmem_ref, wkv_ref, gk_ref, k_ref, v_ref):
    m = _rms_scale(mem_ref[...], D_MODEL) * gmem_ref[...]
    kv = _dot(m.astype(_BF16), wkv_ref[...])
    lane_lo = lax.broadcasted_iota(jnp.int32, (1, 128), 1) < 64
    lane = lax.broadcasted_iota(jnp.int32, (1, 256), 1)
    k = jnp.concatenate([_pair_norm64(kv[:, 0:128], lane_lo), _pair_norm64(kv[:, 128:256], lane_lo)],
                        axis=1) * gk_ref[...]
    v = kv[:, 256:512]
    for hd in range(XA_HEADS):
        own = jnp.logical_and(lane >= hd * XA_HEAD_DIM, lane < (hd + 1) * XA_HEAD_DIM)
        k_ref[hd] = jnp.where(own, k, 0.0).astype(_BF16)
        v_ref[hd] = jnp.where(own, v, 0.0).astype(_BF16)


def _memkv_call(mem2d, g_mem, w_kv, g_k4, depth, batch):
    out = jax.ShapeDtypeStruct((depth, batch, XA_HEADS, MEM_LEN, 256), _BF16)
    ospec = pl.BlockSpec((None, None, XA_HEADS, MEM_LEN, 256), lambda l, b: (l, b, 0, 0, 0))
    return pl.pallas_call(
        _memkv_kernel,
        grid=(depth, batch),
        in_specs=[pl.BlockSpec((MEM_LEN, D_MODEL), lambda l, b: (b, 0)),
                  pl.BlockSpec((None, 1, D_MODEL), lambda l, b: (l, 0, 0)),
                  pl.BlockSpec((None, D_MODEL, 512), lambda l, b: (l, 0, 0)),
                  pl.BlockSpec((None, 1, 256), lambda l, b: (l, 0, 0))],
        out_specs=[ospec, ospec],
        out_shape=[out, out],
        compiler_params=_params(2),
        name="memory_kv",
    )(mem2d, g_mem, w_kv, g_k4)


def _mix_kernel(ya_ref, yb_ref, yc_ref, yd_ref, x_ref, gmo_ref, wmo_ref, gxa_ref, wq_ref, gxq_ref,
                kx_ref, vx_ref, wo_ref, gffn_ref, wr_ref, br_ref, ltri_ref, utri_ref,
                x2_ref, h3_ref, meta_ref, cnt_ref):
    tm = MIX_TILE
    gmo = gmo_ref[...]
    lane_lo = lax.broadcasted_iota(jnp.int32, (1, 128), 1) < 64

    def rows_to_logits(rs):
        parts = []
        for g, ref in enumerate((ya_ref, yb_ref, yc_ref, yd_ref)):
            yg = ref[rs, :].astype(_F32)
            parts.append((_rms_scale(yg, GROUP_W) * gmo[:, g * 256:(g + 1) * 256]).astype(_BF16))
        y = jnp.concatenate(parts, axis=1)
        x1 = x_ref[rs, :] + _dot(y, wmo_ref[...])

        h = (_rms_scale(x1, D_MODEL) * gxa_ref[...]).astype(_BF16)
        q = _dot(h, wq_ref[...])
        qn = jnp.concatenate([_pair_norm64(q[:, 0:128], lane_lo), _pair_norm64(q[:, 128:256], lane_lo)],
                             axis=1)
        qn = (qn * gxq_ref[...] * (XA_HEAD_DIM ** -0.5)).astype(_BF16)
        o = jnp.zeros((x1.shape[0], 256), _F32)
        for hd in range(XA_HEADS):
            s = _dot_nt(qn, kx_ref[hd])
            m = jnp.max(s, axis=-1, keepdims=True)
            p = jnp.exp(s - m)
            denom = jnp.sum(p, axis=-1, keepdims=True)
            o = o + _dot(p.astype(_BF16), vx_ref[hd]) / denom
        x2 = x1 + _dot(o.astype(_BF16), wo_ref[...])
        x2_ref[rs, :] = x2

        h3 = _rms_scale(x2, D_MODEL) * gffn_ref[...]
        h3_hi = h3.astype(_BF16)
        h3_lo = (h3 - h3_hi.astype(_F32)).astype(_BF16)
        h3_ref[rs, :] = h3_hi
        by_hi = _dot(h3_hi, wr_ref[...])
        return (by_hi[:, 0:128] + (by_hi[:, 128:256] + _dot(h3_lo, wr_ref[:, 0:128]))
                + br_ref[...])

    all_logits = rows_to_logits(slice(0, tm))
    for st in range(tm // TOKEN_TILE):
        meta, counts = _route(all_logits[st * TOKEN_TILE:(st + 1) * TOKEN_TILE, :], ltri_ref, utri_ref)
        meta_ref[st * TOKEN_TILE:(st + 1) * TOKEN_TILE, :] = meta
        cnt_ref[st * 8:(st + 1) * 8, :] = jnp.broadcast_to(counts, (8, 128))


def _route(logits, ltri_ref, utri_ref):
    lane = lax.broadcasted_iota(jnp.int32, (TOKEN_TILE, 128), 1).astype(_F32)
    far = 1e9
    is_group = jnp.logical_and(lane >= N_EXPERTS, lane < N_EXPERTS + N_EXPERT_GROUPS)
    gl = jnp.where(is_group, logits, -jnp.inf)
    gmax = jnp.max(gl, axis=-1, keepdims=True)
    gidx = jnp.min(jnp.where(gl == gmax, lane, far), axis=-1, keepdims=True) - N_EXPERTS
    g_w = 1.0 / jnp.sum(jnp.where(is_group, jnp.exp(logits - gmax), 0.0), axis=-1, keepdims=True)
    in_group = jnp.logical_and(lane < N_EXPERTS, jnp.floor(lane * (1.0 / EXPERTS_PER_GROUP)) == gidx)
    el = jnp.where(in_group, logits, -jnp.inf)
    emax = jnp.max(el, axis=-1, keepdims=True)
    ep = jnp.where(in_group, jnp.exp(logits - emax), 0.0)
    prob = jnp.where(in_group, ep / jnp.sum(ep, axis=-1, keepdims=True), -1.0)
    p1 = jnp.max(prob, axis=-1, keepdims=True)
    e0 = jnp.min(jnp.where(prob == p1, lane, far), axis=-1, keepdims=True)
    prob2 = jnp.where(lane == e0, -1.0, prob)
    p2 = jnp.max(prob2, axis=-1, keepdims=True)
    e1 = jnp.min(jnp.where(prob2 == p2, lane, far), axis=-1, keepdims=True)
    w0 = g_w * (p1 / (p1 + p2))
    w1 = g_w * (p2 / (p1 + p2))

    onehot = jnp.where(jnp.logical_or(lane == e0, lane == e1), 1.0, 0.0)
    prefix = _dot(ltri_ref[...], onehot.astype(_BF16))
    counts = jnp.sum(onehot, axis=0, keepdims=True)
    units = jnp.floor((counts + (RUN_ALIGN - 1)) * (1.0 / RUN_ALIGN))
    offs = _dot(jnp.broadcast_to(units, (8, 128)).astype(_BF16), utri_ref[...])[0:1, :] * RUN_ALIGN
    where_to = prefix + offs
    pos0 = jnp.sum(jnp.where(lane == e0, where_to, 0.0), axis=-1, keepdims=True)
    pos1 = jnp.sum(jnp.where(lane == e1, where_to, 0.0), axis=-1, keepdims=True)

    meta = jnp.where(lane == 0, e0,
           jnp.where(lane == 1, e1,
           jnp.where(lane == 2, w0,
           jnp.where(lane == 3, w1,
           jnp.where(lane == 4, pos0,
           jnp.where(lane == 5, pos1, 0.0))))))
    return meta, counts


def _mix_call(ya, yb, yc, yd, x, lw, kx, vx, seq_len):
    n = x.shape[0]
    tm = MIX_TILE
    tiles_per_seq = seq_len // tm

    def full(a):
        nd = a.ndim
        return pl.BlockSpec(a.shape, lambda i, _nd=nd: (0,) * _nd)

    def rows(width):
        return pl.BlockSpec((tm, width), lambda i: (i, 0))

    kvspec = pl.BlockSpec((None, XA_HEADS, MEM_LEN, 256), lambda i: (i // tiles_per_seq, 0, 0, 0))
    consts_a = [lw["g_mo"], lw["w_mo"], lw["g_xa"], lw["xa_w_q"], lw["g_xq"]]
    consts_b = [lw["xa_w_o"], lw["g_ffn"], lw["w_router_hi_lo"], lw["b_router"],
                lw["ltri"], lw["utri"]]
    ntiles = n // tm
    return pl.pallas_call(
        _mix_kernel,
        grid=(ntiles,),
        in_specs=[rows(256)] * 4 + [rows(D_MODEL)] + [full(a) for a in consts_a]
                 + [kvspec, kvspec] + [full(a) for a in consts_b],
        out_specs=[rows(D_MODEL), rows(D_MODEL), rows(128),
                   pl.BlockSpec((8 * (tm // TOKEN_TILE), 128), lambda i: (i, 0))],
        out_shape=[jax.ShapeDtypeStruct((n, D_MODEL), _F32), jax.ShapeDtypeStruct((n, D_MODEL), _BF16),
                   jax.ShapeDtypeStruct((n, 128), _F32),
                   jax.ShapeDtypeStruct((n // TOKEN_TILE * 8, 128), _F32)],
        compiler_params=_params(1),
        name="mix_xattn_router",
    )(ya, yb, yc, yd, x, *consts_a, kx, vx, *consts_b)


def _move_runs(tab_ref, t, live, local_ref, remote_ref, sems, *, to_remote, wait):
    for bit in range(RUN_BITS):
        rows = RUN_ALIGN << bit
        base = (t * RUN_BITS + bit) * PIECE_COLS
        count = jnp.where(live, tab_ref[base], 0)

        def one(p, carry, base=base, rows=rows, bit=bit):
            loc = pl.multiple_of(tab_ref[base + 1 + p], RUN_ALIGN)
            rem = pl.multiple_of(tab_ref[base + 1 + N_EXPERTS + p], RUN_ALIGN)
            lsl = local_ref.at[pl.ds(loc, rows)]
            rsl = remote_ref.at[pl.ds(rem, rows)]
            src, dst = (lsl, rsl) if to_remote else (rsl, lsl)
            copy = pltpu.make_async_copy(src, dst, sems.at[bit, p])
            if wait:
                copy.wait()
            else:
                copy.start(priority=bit % N_DMA_PRIORITIES)
            return carry

        lax.fori_loop(0, count, one, 0)


def _selection(meta, first, rows):
    r = first + lax.broadcasted_iota(jnp.int32, (TOKEN_TILE, rows), 1)
    pos0 = meta[:, 4:5].astype(jnp.int32)
    pos1 = meta[:, 5:6].astype(jnp.int32)
    return jnp.where(r == pos0, 1.0, 0.0).astype(_BF16), jnp.where(r == pos1, 1.0, 0.0).astype(_BF16)


def _zero_rows(count, row_of, rows, zero_ref, xs_ref, sems):
    def copy(i):
        dst = xs_ref.at[pl.ds(pl.multiple_of(row_of(i), RUN_ALIGN), rows)]
        return pltpu.make_async_copy(zero_ref.at[pl.ds(0, rows)], dst, sems.at[i % ZERO_RING])

    def issue(i, carry):
        @pl.when(i >= ZERO_RING)
        def _():
            copy(i - ZERO_RING).wait()
        copy(i).start()
        return carry

    def drain(i, carry):
        copy(i).wait()
        return carry

    lax.fori_loop(0, count, issue, 0)
    lax.fori_loop(jnp.maximum(count - ZERO_RING, 0), count, drain, 0)


def _dispatch_kernel(tab_ref, used_ref, ztab_ref, meta_ref, h3_ref, xs_ref, buf0_ref, buf1_ref, zero_ref,
                     sems, zero_sems, free_sems, *, ntiles, n_blocks):
    t = pl.program_id(0)
    last = ntiles - 1
    bufs = (buf0_ref, buf1_ref)
    used = used_ref[jnp.minimum(t, last)]

    @pl.when(t == 0)
    def _():
        zero_ref[...] = jnp.zeros_like(zero_ref)
        for bit in range(TAIL_BITS):
            base = bit * (1 + N_EXPERTS)
            _zero_rows(ztab_ref[base], lambda i, base=base: ztab_ref[base + 1 + i], RUN_ALIGN << bit,
                       zero_ref, xs_ref, zero_sems.at[bit])

    live_blocks = ztab_ref[TAIL_BITS * (1 + N_EXPERTS)]
    per_step = -(-n_blocks // (ntiles + 1))

    def free_block_copy(step, j):
        blk = live_blocks + step * per_step + j
        dst = xs_ref.at[pl.ds(pl.multiple_of(blk * MOE_BLOCK, MOE_BLOCK), MOE_BLOCK)]
        return blk < n_blocks, pltpu.make_async_copy(zero_ref, dst, free_sems.at[step % 2, j])

    for j in range(per_step):
        exists, copy = free_block_copy(t - 1, j)
        pl.when(jnp.logical_and(t >= 1, exists))(copy.wait)
    for j in range(per_step):
        exists, copy = free_block_copy(t, j)
        pl.when(jnp.logical_and(t <= ntiles, exists))(copy.start)

    def step(slot):
        mine, other = bufs[slot], bufs[1 - slot]
        _move_runs(tab_ref, jnp.clip(t - 2, 0, last), t >= 2, mine, xs_ref, sems.at[slot],
                   to_remote=True, wait=True)
        _move_runs(tab_ref, jnp.clip(t - 1, 0, last), jnp.logical_and(t >= 1, t <= ntiles), other, xs_ref,
                   sems.at[1 - slot], to_remote=True, wait=False)

        meta = meta_ref[...]
        lane = lax.broadcasted_iota(jnp.int32, (TOKEN_TILE, 128), 1)
        cols = []
        for k in range(2):
            w = meta[:, 2 + k:3 + k]
            hi = w.astype(_BF16).astype(_F32)
            mid = (w - hi).astype(_BF16).astype(_F32)
            lo = (w - hi) - mid
            cols.append(jnp.where(lane == 0, hi, jnp.where(lane == 1, mid, jnp.where(lane == 2, lo, 0.0)))
                        .astype(_BF16))

        def sort_rows(first, rows):
            sel0, sel1 = _selection(meta, first, rows)
            tn = (((0,), (0,)), ((), ()))
            mine[first:first + rows, 0:D_MODEL] = lax.dot_general(
                sel0 + sel1, h3_ref[...], tn, preferred_element_type=_F32).astype(_BF16)
            mine[first:first + rows, D_MODEL:XS_COLS] = (
                lax.dot_general(sel0, cols[0], tn, preferred_element_type=_F32)
                + lax.dot_general(sel1, cols[1], tn, preferred_element_type=_F32)).astype(_BF16)

        sort_rows(0, SORT_ALWAYS)
        for first in range(SORT_ALWAYS, SORT_ROWS, SORT_CHUNK):
            pl.when(used > first)(functools.partial(sort_rows, first, SORT_CHUNK))

    for slot in range(2):
        pl.when(t % 2 == slot)(functools.partial(step, slot))


def _dispatch_call(tab, used, ztab, meta, h3):
    n = h3.shape[0]
    tm = TOKEN_TILE
    ntiles = n // tm
    xs_rows = _xs_rows(n)
    tile = lambda i, tab, used, ztab: (jnp.minimum(i, ntiles - 1), 0)
    grid_spec = pltpu.PrefetchScalarGridSpec(
        num_scalar_prefetch=3,
        grid=(ntiles + 2,),
        in_specs=[pl.BlockSpec((tm, 128), tile),
                  pl.BlockSpec((tm, D_MODEL), tile)],
        out_specs=pl.BlockSpec(memory_space=pl.ANY),
        scratch_shapes=[pltpu.VMEM((SORT_ROWS, XS_COLS), _BF16), pltpu.VMEM((SORT_ROWS, XS_COLS), _BF16),
                        pltpu.VMEM((MOE_BLOCK, XS_COLS), _BF16),
                        pltpu.SemaphoreType.DMA((2, RUN_BITS, N_EXPERTS)),
                        pltpu.SemaphoreType.DMA((TAIL_BITS, ZERO_RING)),
                        pltpu.SemaphoreType.DMA((2, -(-(xs_rows // MOE_BLOCK) // (ntiles + 1))))],
    )
    return pl.pallas_call(
        functools.partial(_dispatch_kernel, ntiles=ntiles, n_blocks=xs_rows // MOE_BLOCK),
        grid_spec=grid_spec,
        out_shape=jax.ShapeDtypeStruct((xs_rows, XS_COLS), _BF16),
        compiler_params=_params(1),
        name="moe_dispatch",
    )(tab, used, ztab, meta, h3)


def _ffn_kernel(be_ref, nv_ref, xs_ref, wg_ref, wu_ref, wd_ref, ys_ref, wgu_ref, wdn_ref):
    i = pl.program_id(0)

    @pl.when(jnp.logical_or(i == 0, be_ref[i] != be_ref[jnp.maximum(i - 1, 0)]))
    def _():
        wgu_ref[:, 0:D_EXPERT] = wg_ref[...].astype(_BF16)
        wgu_ref[:, D_EXPERT:2 * D_EXPERT] = wu_ref[...].astype(_BF16)
        wdn_ref[...] = wd_ref[...].astype(_BF16)

    @pl.when(i < nv_ref[0])
    def _():
        xb = xs_ref[:, 0:D_MODEL]
        wcols = xs_ref[:, D_MODEL:XS_COLS].astype(_F32)
        wt = wcols[:, 0:1] + wcols[:, 1:2] + wcols[:, 2:3]
        gu = _dot(xb, wgu_ref[...])
        g = gu[:, 0:D_EXPERT]
        a = (g * jax.nn.sigmoid(g)) * gu[:, D_EXPERT:2 * D_EXPERT]
        y = _dot(a.astype(_BF16), wdn_ref[...])
        ys_ref[...] = (y * wt).astype(ys_ref.dtype)

    @pl.when(i >= nv_ref[0])
    def _():
        ys_ref[...] = jnp.zeros_like(ys_ref)


def _ffn_call(blk_expert, n_valid, xs, w_gate, w_up, w_down, layer):
    n_blocks = xs.shape[0] // MOE_BLOCK
    wspec = lambda shape: pl.BlockSpec((None, None) + shape, lambda i, be, nv: (layer, be[i], 0, 0))
    last_live = lambda i, be, nv: (jnp.minimum(i, nv[0] - 1), 0)
    grid_spec = pltpu.PrefetchScalarGridSpec(
        num_scalar_prefetch=2,
        grid=(n_blocks,),
        in_specs=[pl.BlockSpec((MOE_BLOCK, XS_COLS), last_live),
                  wspec((D_MODEL, D_EXPERT)), wspec((D_MODEL, D_EXPERT)), wspec((D_EXPERT, D_MODEL))],
        out_specs=pl.BlockSpec((MOE_BLOCK, D_MODEL), lambda i, be, nv: (i, 0)),
        scratch_shapes=[pltpu.VMEM((D_MODEL, 2 * D_EXPERT), _BF16), pltpu.VMEM((D_EXPERT, D_MODEL), _BF16)],
    )
    return pl.pallas_call(
        _ffn_kernel,
        grid_spec=grid_spec,
        out_shape=jax.ShapeDtypeStruct((xs.shape[0], D_MODEL), _BF16),
        compiler_params=_params(1),
        name="moe_experts",
    )(blk_expert, n_valid, xs, w_gate, w_up, w_down)


def _combine_kernel(tab_ref, used_ref, meta_ref, x2_ref, ys_ref, out_ref, buf0_ref, buf1_ref, sems,
                    *, ntiles):
    t = pl.program_id(0)
    last = ntiles - 1
    bufs = (buf0_ref, buf1_ref)
    used = used_ref[jnp.clip(t - 1, 0, last)]

    @pl.when(t == 0)
    def _():
        buf0_ref[...] = jnp.zeros_like(buf0_ref)
        buf1_ref[...] = jnp.zeros_like(buf1_ref)

    def step(slot):
        mine, other = bufs[slot], bufs[1 - slot]
        _move_runs(tab_ref, jnp.clip(t - 1, 0, last), t >= 1, other, ys_ref, sems.at[1 - slot],
                   to_remote=False, wait=True)
        _move_runs(tab_ref, jnp.minimum(t, last), t <= last, mine, ys_ref, sems.at[slot],
                   to_remote=False, wait=False)
        meta = meta_ref[...]

        def gathered(first, rows):
            sel0, sel1 = _selection(meta, first, rows)
            return _dot(sel0 + sel1, other[first:first + rows, :])

        out_ref[...] = x2_ref[...] + gathered(0, SORT_ALWAYS)
        for first in range(SORT_ALWAYS, SORT_ROWS, SORT_CHUNK):
            @pl.when(used > first)
            def _(first=first):
                out_ref[...] += gathered(first, SORT_CHUNK)

    for slot in range(2):
        pl.when(t % 2 == slot)(functools.partial(step, slot))


def _combine_call(tab, used, meta, x2, ys):
    n = x2.shape[0]
    tm = TOKEN_TILE
    ntiles = n // tm
    tile = lambda i, tab, used: (jnp.maximum(i - 1, 0), 0)
    grid_spec = pltpu.PrefetchScalarGridSpec(
        num_scalar_prefetch=2,
        grid=(ntiles + 1,),
        in_specs=[pl.BlockSpec((tm, 128), tile),
                  pl.BlockSpec((tm, D_MODEL), tile),
                  pl.BlockSpec(memory_space=pl.ANY)],
        out_specs=pl.BlockSpec((tm, D_MODEL), tile),
        scratch_shapes=[pltpu.VMEM((SORT_ROWS, D_MODEL), _BF16), pltpu.VMEM((SORT_ROWS, D_MODEL), _BF16),
                        pltpu.SemaphoreType.DMA((2, RUN_BITS, N_EXPERTS))],
    )
    return pl.pallas_call(
        functools.partial(_combine_kernel, ntiles=ntiles),
        grid_spec=grid_spec,
        out_shape=jax.ShapeDtypeStruct((n, D_MODEL), _F32),
        compiler_params=_params(1),
        name="moe_combine",
    )(tab, used, meta, x2, ys)


def _moe_tables(cnt, n_tokens):
    ntiles = n_tokens // TOKEN_TILE
    counts = cnt.reshape(ntiles, 8, 128)[:, 0, :N_EXPERTS].astype(jnp.int32)
    units = (counts + RUN_ALIGN - 1) // RUN_ALIGN
    padded = units * RUN_ALIGN
    local = jnp.cumsum(padded, axis=1) - padded
    total = jnp.sum(padded, axis=0)
    total_blk = ((total + MOE_BLOCK - 1) // MOE_BLOCK) * MOE_BLOCK
    ends = jnp.cumsum(total_blk)
    starts = ends - total_blk
    remote = starts[None, :] + jnp.cumsum(padded, axis=0) - padded
    bits = jnp.arange(RUN_BITS, dtype=jnp.int32)[None, :, None]
    has = (units[:, None, :] >> bits) & 1
    done = (units[:, None, :] & ((1 << bits) - 1)) * RUN_ALIGN
    rank = jnp.cumsum(has, axis=2) - 1
    place = jnp.logical_and(rank[..., None] == jnp.arange(N_EXPERTS, dtype=jnp.int32), has[..., None] == 1)
    compact = lambda rows: jnp.sum(jnp.where(place, rows[..., None], 0), axis=2)
    tab = jnp.concatenate([jnp.sum(has, axis=2, keepdims=True), compact(local[:, None, :] + done),
                           compact(remote[:, None, :] + done)], axis=-1).reshape(-1).astype(jnp.int32)
    n_blocks = _xs_rows(n_tokens) // MOE_BLOCK
    blk_start = jnp.arange(n_blocks, dtype=jnp.int32) * MOE_BLOCK
    blk_expert = jnp.minimum(jnp.sum((ends[None, :] <= blk_start[:, None]).astype(jnp.int32), axis=1),
                             N_EXPERTS - 1)
    n_valid = (ends[-1] // MOE_BLOCK).astype(jnp.int32).reshape(1)
    used = jnp.sum(padded, axis=1).astype(jnp.int32)
    tail_units = (total_blk - total) // RUN_ALIGN
    tail_bits = jnp.arange(TAIL_BITS, dtype=jnp.int32)[:, None]
    tail_has = (tail_units[None, :] >> tail_bits) & 1
    tail_rows = starts + total + (tail_units[None, :] & ((1 << tail_bits) - 1)) * RUN_ALIGN
    tail_rank = jnp.cumsum(tail_has, axis=1) - 1
    tail_place = jnp.logical_and(tail_rank[..., None] == jnp.arange(N_EXPERTS, dtype=jnp.int32),
                                 tail_has[..., None] == 1)
    tail_list = jnp.sum(jnp.where(tail_place, tail_rows[..., None], 0), axis=1)
    ztab = jnp.concatenate([jnp.concatenate([jnp.sum(tail_has, axis=1, keepdims=True), tail_list],
                                            axis=1).reshape(-1), n_valid]).astype(jnp.int32)
    return tab, used, ztab, blk_expert, n_valid


def _xs_rows(n_tokens):
    ntiles = n_tokens // TOKEN_TILE
    worst = 2 * n_tokens + ntiles * N_EXPERTS * (RUN_ALIGN - 1) + N_EXPERTS * (MOE_BLOCK - 1)
    return ((worst + MOE_BLOCK - 1) // MOE_BLOCK) * MOE_BLOCK


def _pad_last(a, width):
    return jnp.pad(a, [(0, 0)] * (a.ndim - 1) + [(0, width - a.shape[-1])])


def _swap_mid_heads(a, axis):
    shape = a.shape
    a = a.reshape(shape[:axis] + (4, 64) + shape[axis + 1:])
    a = jnp.take(a, jnp.asarray([0, 2, 1, 3]), axis=axis)
    return a.reshape(shape)


def _layer_weights(l, p):
    w_in = p["w_in"][l]
    a, b = w_in[:, :A_COLS], w_in[:, A_COLS:A_COLS + B_COLS]
    c = w_in[:, A_COLS + B_COLS:A_COLS + B_COLS + C_COLS]
    d = w_in[:, A_COLS + B_COLS + C_COLS:]
    zeros = lambda w: jnp.zeros((D_MODEL, w), _F32)
    half = MLA_ROPE // 2
    k_rope = a[:, 384:416]
    k_rope_swap = jnp.concatenate([k_rope[:, half:], k_rope[:, :half]], axis=1)
    w_in_p = jnp.concatenate([
        a[:, :384], zeros(64), k_rope, zeros(32), zeros(64), k_rope_swap, zeros(32), b,
        _swap_mid_heads(c[:, :256], 1), c[:, 256:384], c[:, 384:512], d], axis=1).astype(_BF16)
    assert w_in_p.shape[1] == IN_COLS_PADDED

    w_uq = p["mla_w_uq"][l].reshape(MLA_Q_RANK, MLA_HEADS, MLA_QK_DIM)
    w_uq_swap = jnp.concatenate([jnp.zeros_like(w_uq[..., :MLA_NOPE]), w_uq[..., MLA_NOPE + half:],
                                 w_uq[..., MLA_NOPE:MLA_NOPE + half]], axis=-1)
    w_uq = jnp.concatenate([_pad_last(w_uq, HEAD_LANES).reshape(MLA_Q_RANK, -1),
                            _pad_last(w_uq_swap, HEAD_LANES).reshape(MLA_Q_RANK, -1)], axis=1).astype(_BF16)

    def swap_gain(g):
        return _pad_last(jnp.concatenate([jnp.zeros((MLA_NOPE,), _F32), g[MLA_NOPE + half:],
                                          g[MLA_NOPE:MLA_NOPE + half]]), HEAD_LANES)[None]
    w_ukv = p["mla_w_ukv"][l].reshape(MLA_KV_RANK, MLA_HEADS, MLA_NOPE + MLA_V)
    k_nope = _pad_last(w_ukv[..., :MLA_NOPE], HEAD_LANES).reshape(MLA_KV_RANK, MLA_HEADS * HEAD_LANES)
    v_part = w_ukv[..., MLA_NOPE:].reshape(MLA_KV_RANK, MLA_HEADS * MLA_V)
    w_ukv_p = jnp.concatenate([k_nope, v_part], axis=1).astype(_BF16)

    eye = jnp.eye(len(POOL_WINDOWS), dtype=_F32)
    w_pool = jnp.einsum("gcd,gh->gchd", p["pool_w"][l], eye).reshape(256, 256).astype(_BF16)

    g_mo = p["mix_out_norm"][l]
    g_mo = jnp.concatenate([g_mo[:512], _swap_mid_heads(g_mo[512:768], 0), g_mo[768:]])
    w_mo = p["w_mix_out"][l]
    w_mo = jnp.concatenate([w_mo[:512], _swap_mid_heads(w_mo[512:768], 0), w_mo[768:]], axis=0)

    w_router = _pad_last(jnp.concatenate([p["w_expert"][l], p["w_group"][l]], axis=1), 128)
    w_router_hi = w_router.astype(_BF16)
    w_router_lo = (w_router - w_router_hi.astype(_F32)).astype(_BF16)
    b_router = jnp.concatenate([p["b_expert"][l], p["b_group"][l]])
    tm = TOKEN_TILE
    ltri = (jnp.arange(tm)[None, :] < jnp.arange(tm)[:, None]).astype(_BF16)
    utri = (jnp.arange(128)[:, None] < jnp.arange(128)[None, :]).astype(_BF16)
    tile2 = lambda g: jnp.concatenate([g, g])[None]
    return dict(
        g_mix=p["norm_mix"][l][None], w_in=w_in_p,
        g_cq=p["mla_g_cq"][l][None], w_uq=w_uq, g_ckv=p["mla_g_ckv"][l][None], w_ukv=w_ukv_p,
        g_q=_pad_last(p["mla_g_q"][l], HEAD_LANES)[None], g_k=_pad_last(p["mla_g_k"][l], HEAD_LANES)[None],
        g_q_swap=swap_gain(p["mla_g_q"][l]), g_k_swap=swap_gain(p["mla_g_k"][l]),
        w_pool=w_pool, pool_scale=p["pool_scale"][l][None], conv_w=p["conv_w"][l],
        conv_b=p["conv_b"][l][None], g_sq=tile2(p["swa_g_q"][l]), g_sk=tile2(p["swa_g_k"][l]),
        sinks=jnp.take(p["swa_sinks"][l], jnp.asarray([0, 2, 1, 3])),
        g_mo=g_mo[None], w_mo=w_mo.astype(_BF16), g_xa=p["norm_xa"][l][None],
        xa_w_q=p["xa_w_q"][l].astype(_BF16), g_xq=jnp.tile(p["xa_g_q"][l], 4)[None],
        xa_w_o=p["xa_w_o"][l].astype(_BF16), g_ffn=p["norm_ffn"][l][None],
        w_router_hi_lo=jnp.concatenate([w_router_hi, w_router_lo], axis=1), b_router=_pad_last(b_router, 128)[None],
        ltri=ltri, utri=utri,
    )


def kernel(x, mem, positions, norm_mix, w_in, mla_g_cq, mla_w_uq, mla_g_ckv, mla_w_ukv, mla_g_q, mla_g_k, pool_w, pool_scale, swa_g_q, swa_g_k, swa_sinks, conv_w, conv_b, mix_out_norm, w_mix_out, norm_xa, norm_mem, xa_w_q, xa_w_kv, xa_g_q, xa_g_k, xa_w_o, norm_ffn, w_group, b_group, w_expert, b_expert, w_gate, w_up, w_down):
    p = dict(norm_mix=norm_mix, w_in=w_in, mla_g_cq=mla_g_cq, mla_w_uq=mla_w_uq, mla_g_ckv=mla_g_ckv,
             mla_w_ukv=mla_w_ukv, mla_g_q=mla_g_q, mla_g_k=mla_g_k, pool_w=pool_w, pool_scale=pool_scale,
             swa_g_q=swa_g_q, swa_g_k=swa_g_k, swa_sinks=swa_sinks, conv_w=conv_w, conv_b=conv_b,
             mix_out_norm=mix_out_norm, w_mix_out=w_mix_out, norm_xa=norm_xa, xa_w_q=xa_w_q,
             xa_g_q=xa_g_q, xa_w_o=xa_w_o, norm_ffn=norm_ffn, w_group=w_group, b_group=b_group,
             w_expert=w_expert, b_expert=b_expert)
    batch, seq_len, _ = x.shape
    depth = w_in.shape[0]
    n = batch * seq_len
    assert seq_len % MIX_TILE == 0 and MIX_TILE % TOKEN_TILE == 0
    assert seq_len % PROJ_TILE == 0 and seq_len % ATTN_Q_TILE == 0
    assert mem.shape[1] == MEM_LEN

    xf = x.reshape(n, D_MODEL)
    tabs = _rope_tables(positions)
    kx, vx = _memkv_call(mem.reshape(batch * MEM_LEN, D_MODEL), norm_mem[:, None, :],
                         xa_w_kv.astype(_BF16), jnp.tile(xa_g_k, (1, 4))[:, None, :], depth, batch)
    for l in range(depth):
        lw = _layer_weights(l, p)
        qm, km, vm, yb, yc, yd = _proj_call(xf, lw, tabs, seq_len)
        ya = _mla_attn_call(qm, km, vm, batch, seq_len)
        x2, h3, meta, cnt = _mix_call(ya, yb, yc, yd, xf, lw, kx[l], vx[l], seq_len)
        tab, used, ztab, blk_expert, n_valid = _moe_tables(cnt, n)
        xs = _dispatch_call(tab, used, ztab, meta, h3)
        ys = _ffn_call(blk_expert, n_valid, xs, w_gate, w_up, w_down, l)
        xf = _combine_call(tab, used, meta, x2, ys)
    return xf.reshape(batch, seq_len, D_MODEL)
```
